```python
import jax, jax.numpy as jnp
from jax import lax
import numpy as np

D_MODEL = 1024
BATCH = 8
SEQ = 4096
DEPTH = 1

D_MIX = D_MODEL
D_A = D_MIX // 2
D_B = D_MIX - D_A
HEAD_A = 64
G_A = D_A // HEAD_A
HEAD_B = 64
H_B = D_B // HEAD_B
CHUNK = 128
DECAY_RANK = 64
ICLR_RANK = 64
LN_EPS = 1e-5
GN_EPS = 64e-5
ALPHA = (2 * DEPTH) ** 0.25
BETA = (8 * DEPTH) ** -0.25

N_COLS_A = 3 * D_A
N_COLS_B = 4 * D_B + DECAY_RANK + ICLR_RANK
N_COLS = N_COLS_A + N_COLS_B

kernel_name = "hybrid_gmlp_rwkv7_deepnorm_adaln"


def _layer_norm(x, g, b, eps):
    xf = x.astype(jnp.float32)
    mu = jnp.mean(xf, axis=-1, keepdims=True)
    var = jnp.mean(jnp.square(xf - mu), axis=-1, keepdims=True)
    y = (xf - mu) * lax.rsqrt(var + eps)
    return (y * g.astype(jnp.float32) + b.astype(jnp.float32)).astype(x.dtype)


def _token_shift(p, mu):
    prev = jnp.pad(p, ((0, 0), (1, 0), (0, 0)))[:, :-1]
    return p + mu * (prev - p)


def _rwkv7_scan(r, w, k, v, a, b):
    Bsz, T, H, N = r.shape
    seq_first = lambda z: jnp.swapaxes(z, 0, 1)
    xs = tuple(seq_first(z) for z in (r, w, k, v, a, b))

    def step(S, inp):
        r_t, w_t, k_t, v_t, a_t, b_t = inp
        sa = jnp.einsum('bhij,bhj->bhi', S, a_t)
        S = S * w_t[:, :, None, :] + sa[..., None] * b_t[:, :, None, :] + v_t[..., None] * k_t[:, :, None, :]
        y = jnp.einsum('bhij,bhj->bhi', S, r_t)
        return S, y

    S0 = jnp.zeros((Bsz, H, N, N), jnp.float32)
    _, ys = lax.scan(step, S0, xs)
    return jnp.swapaxes(ys, 0, 1)


def _gmlp_branch(p_a, ln_v_g, ln_v_b, w_spatial, b_spatial):
    Bsz, T, _ = p_a.shape
    u = jax.nn.gelu(p_a[..., :D_A])
    v = jax.nn.gelu(p_a[..., D_A:2 * D_A])
    gate = p_a[..., 2 * D_A:]
    v = v.reshape(Bsz, T, G_A, HEAD_A)
    v = _layer_norm(v, ln_v_g, ln_v_b, LN_EPS)
    v = v.reshape(Bsz, T // CHUNK, CHUNK, G_A, HEAD_A)
    causal = jnp.tril(jnp.ones((CHUNK, CHUNK), dtype=bool))
    w_m = jnp.where(causal[None], w_spatial, jnp.zeros((), w_spatial.dtype))
    v = jnp.einsum('gts,bcsgd->bctgd', w_m, v) + jnp.swapaxes(b_spatial, 0, 1)[None, None, :, :, None]
    v = v.reshape(Bsz, T, D_A)
    return u * v * jax.nn.silu(gate)


def _rwkv7_branch(p_b, mu_b, w0, w_up, a0, a_up, k_k, k_a, r_k, gn_g, gn_b):
    Bsz, T, _ = p_b.shape
    p_b = _token_shift(p_b, mu_b)
    o = 0
    r = p_b[..., o:o + D_B]; o += D_B
    k = p_b[..., o:o + D_B]; o += D_B
    v = p_b[..., o:o + D_B]; o += D_B
    gate = p_b[..., o:o + D_B]; o += D_B
    wd = p_b[..., o:o + DECAY_RANK]; o += DECAY_RANK
    ad = p_b[..., o:o + ICLR_RANK]

    w_raw = -jax.nn.softplus(-(w0 + jnp.tanh(wd) @ w_up)) - 0.5
    a = jax.nn.sigmoid(a0 + ad @ a_up)

    heads = lambda z: z.reshape(Bsz, T, H_B, HEAD_B).astype(jnp.float32)
    kk = heads(k * k_k)
    kk = kk * lax.rsqrt(jnp.sum(kk * kk, axis=-1, keepdims=True) + 1e-12)
    k = k * (1 + (a - 1) * k_a)
    rh, kh, vh, ah = heads(r), heads(k), heads(v), heads(a)
    decay = jnp.exp(-jnp.exp(heads(w_raw)))

    y = _rwkv7_scan(rh, decay, kh, vh, -kk, kk * ah)
    y = _layer_norm(y, gn_g, gn_b, GN_EPS)
    bonus = jnp.sum(rh * kh * r_k.astype(jnp.float32), axis=-1, keepdims=True) * vh
    y = (y + bonus).reshape(Bsz, T, D_B).astype(p_b.dtype)
    return y * jax.nn.silu(gate)


def setup_inputs(seed: int = 0) -> dict:
    key = jax.random.key(seed)
    ks = jax.random.split(key, 24)
    f32 = jnp.float32
    nrm = lambda k, s, sc: jax.random.normal(k, s, f32) * sc
    return {
        "x": nrm(ks[0], (BATCH, SEQ, D_MODEL), 1.0),
        "c": nrm(ks[1], (BATCH, D_MODEL), 1.0),
        "w_ada": nrm(ks[2], (D_MODEL, 3 * D_MODEL), D_MODEL ** -0.5),
        "b_ada": nrm(ks[3], (3 * D_MODEL,), 0.02),
        "w_in": nrm(ks[4], (D_MODEL, N_COLS), D_MODEL ** -0.5),
        "mu_b": jax.random.uniform(ks[5], (N_COLS_B,), f32),
        "ln_v_g": 1.0 + nrm(ks[6], (G_A, HEAD_A), 0.05),
        "ln_v_b": nrm(ks[7], (G_A, HEAD_A), 0.02),
        "w_spatial": nrm(ks[8], (G_A, CHUNK, CHUNK), CHUNK ** -0.5),
        "b_spatial": 1.0 + nrm(ks[9], (G_A, CHUNK), 0.1),
        "w0": jax.random.uniform(ks[10], (D_B,), f32, -4.0, 1.0),
        "w_up": nrm(ks[11], (DECAY_RANK, D_B), 0.5 * DECAY_RANK ** -0.5),
        "a0": nrm(ks[12], (D_B,), 0.5),
        "a_up": nrm(ks[13], (ICLR_RANK, D_B), 0.5 * ICLR_RANK ** -0.5),
        "k_k": 0.85 + nrm(ks[14], (D_B,), 0.05),
        "k_a": 1.0 + nrm(ks[15], (D_B,), 0.05),
        "r_k": nrm(ks[16], (H_B, HEAD_B), 0.1),
        "gn_g": 1.0 + nrm(ks[17], (H_B, HEAD_B), 0.05),
        "gn_b": nrm(ks[18], (H_B, HEAD_B), 0.02),
        "w_out": nrm(ks[19], (D_MIX, D_MODEL), BETA * D_MIX ** -0.5),
        "ln_g": 1.0 + nrm(ks[20], (D_MODEL,), 0.05),
        "ln_b": nrm(ks[21], (D_MODEL,), 0.02),
    }


def reference(x, c, w_ada, b_ada, w_in, mu_b, ln_v_g, ln_v_b, w_spatial, b_spatial,
              w0, w_up, a0, a_up, k_k, k_a, r_k, gn_g, gn_b, w_out, ln_g, ln_b):
    mod = jax.nn.silu(c) @ w_ada + b_ada
    shift, scale, gate = [m[:, None, :] for m in jnp.split(mod, 3, axis=-1)]
    for _ in range(DEPTH):
        h = x * (1 + scale) + shift
        p = h @ w_in
        out_a = _gmlp_branch(p[..., :N_COLS_A], ln_v_g, ln_v_b, w_spatial, b_spatial)
        out_b = _rwkv7_branch(p[..., N_COLS_A:], mu_b, w0, w_up, a0, a_up, k_k, k_a, r_k, gn_g, gn_b)
        o = jnp.concatenate([out_a, out_b], axis=-1) @ w_out
        x = _layer_norm(ALPHA * x + gate * o, ln_g, ln_b, LN_EPS)
    return x
```

```python
import functools
import math

import jax
import jax.numpy as jnp
from jax import lax
from jax.experimental import pallas as pl
from jax.experimental.pallas import tpu as pltpu

F32 = jnp.float32
BF16 = jnp.bfloat16

D_MODEL = 1024
D_A = 512
D_B = 512
HEAD = 64
CHUNK_A = 128
LOW_RANK = 64
N_COLS_A = 3 * D_A
N_COLS_B = 4 * D_B + 2 * LOW_RANK
N_COLS = N_COLS_A + N_COLS_B
LN_EPS = 1e-5
GN_EPS = 64e-5
ALPHA = 2.0 ** 0.25

LANES = 128
SUBLANES = 8
N_PAIRS = D_B // LANES
SCAN_L = 64
PAIR_ROWS = 2 * SCAN_L
TILE_T = 512
VMEM_LIMIT_BYTES = 56 * 1024 * 1024
LOG_DECAY_SCALE = -math.exp(-0.5)


def _dot(a, b):
    return jnp.dot(a.astype(BF16), b.astype(BF16), preferred_element_type=F32)


def _dot_nt(a, b):
    return lax.dot_general(a.astype(BF16), b.astype(BF16), (((1,), (1,)), ((), ())),
                           preferred_element_type=F32)


def _dot_tn(a, b):
    return lax.dot_general(a.astype(BF16), b.astype(BF16), (((0,), (0,)), ((), ())),
                           preferred_element_type=F32)


def _split_bf16(x, terms):
    parts = []
    rem = x
    for _ in range(terms):
        hi = rem.astype(BF16)
        parts.append(hi)
        rem = rem - hi.astype(F32)
    return parts


def _dot_split(x, w_bf16, terms=2):
    parts = _split_bf16(x, terms)
    stacked = jnp.concatenate(parts, axis=0)
    res = jnp.dot(stacked, w_bf16, preferred_element_type=F32)
    rows = x.shape[0]
    out = res[0:rows]
    for i in range(1, terms):
        out = out + res[i * rows:(i + 1) * rows]
    return out


def _group_sum(x, gmat):
    rows = x.shape[0]
    n = x.shape[1] // LANES
    xs = jnp.concatenate([x[:, LANES * j:LANES * (j + 1)] for j in range(n)], axis=0)
    s = _dot_split(xs, gmat, terms=2)
    return jnp.concatenate([s[rows * j:rows * (j + 1)] for j in range(n)], axis=1)


def _sigmoid(x):
    return 1.0 / (1.0 + jnp.exp(-x))


def _silu(x):
    return x * _sigmoid(x)


def _gelu_tanh(x):
    c = math.sqrt(2.0 / math.pi)
    return 0.5 * x * (1.0 + jnp.tanh(c * (x + 0.044715 * (x * x * x))))


def _group_layer_norm(x, gavg, gamma, beta, eps):
    mean = _group_sum(x, gavg)
    xc = x - mean
    var = _group_sum(xc * xc, gavg)
    return xc * lax.rsqrt(var + eps) * gamma + beta


def _unit_lower_inverse(a_ab, eye, m8, off16, off32, off64):
    a8 = a_ab * m8
    a2 = _dot(a8, a8)
    a4 = _dot(a2, a2)
    p1 = eye + a8
    p2 = p1 + _dot(p1, a2)
    t = p2 + _dot(p2, a4)
    for off in (off16, off32, off64):
        t = t + _dot(_dot(t, a_ab * off), t)
    return t


def _scan_chunk_pair(r, k, v, kk, bb, g, lw, state, consts):
    eye, tri_strict, tri_incl, m8, off16, off32, off64 = consts
    ge = g - lw
    gl = g[SCAN_L - 1:SCAN_L, :]
    e_g = jnp.exp(g)
    e_ge = jnp.exp(ge)
    e_ng = jnp.exp(-g)
    e_gl = jnp.exp(gl - g)
    lane = lax.broadcasted_iota(jnp.int32, (SCAN_L, LANES), 1)
    first = lane < HEAD

    def stack2(z):
        return jnp.concatenate([jnp.where(first, z, 0.0), jnp.where(first, 0.0, z)], axis=0)

    xa = stack2(-(kk * e_ge))
    xr = stack2(r * e_g)
    vp = stack2(v)
    bh = stack2(bb * e_ng)
    kh = stack2(k * e_ng)
    bbar = stack2(bb * e_gl)
    kbar = stack2(k * e_gl)

    aa = _dot_nt(jnp.concatenate([xa, xr], axis=0), jnp.concatenate([bh, kh], axis=0))
    a_ab = jnp.where(tri_strict != 0.0, aa[0:PAIR_ROWS, 0:PAIR_ROWS], 0.0)
    a_ak = jnp.where(tri_strict != 0.0, aa[0:PAIR_ROWS, PAIR_ROWS:], 0.0)
    a_rb = jnp.where(tri_incl != 0.0, aa[PAIR_ROWS:, 0:PAIR_ROWS], 0.0)
    a_rk = jnp.where(tri_incl != 0.0, aa[PAIR_ROWS:, PAIR_ROWS:], 0.0)

    t_inv = _unit_lower_inverse(a_ab, eye, m8, off16, off32, off64)

    xs = _dot_nt(jnp.concatenate([xa, xr], axis=0), state)
    u = _dot(t_inv, xs[0:PAIR_ROWS] + _dot(a_ak, vp))
    uv = jnp.concatenate([u, vp], axis=0)
    y2 = xs[PAIR_ROWS:] + _dot(jnp.concatenate([a_rb, a_rk], axis=1), uv)
    y = y2[0:SCAN_L] + y2[SCAN_L:]
    new_state = state * jnp.exp(gl) + _dot_tn(uv, jnp.concatenate([bbar, kbar], axis=0))
    return y, new_state


def _ada_kernel(c_ref, w_ref, b_ref, o_ref):
    c = c_ref[...]
    sc = _silu(c)
    w = w_ref[...]
    s_hi, s_lo = _split_bf16(sc, 2)
    w_hi, w_lo = _split_bf16(w, 2)
    acc = jnp.dot(s_hi, w_hi, preferred_element_type=F32)
    acc = acc + jnp.dot(s_hi, w_lo, preferred_element_type=F32)
    acc = acc + jnp.dot(s_lo, w_hi, preferred_element_type=F32)
    o_ref[...] = acc + b_ref[...]


def _layer_kernel(x_ref, mod_ref, win_ref, mu_ref, lnv_g_ref, lnv_b_ref, ws_ref, bs_ref,
                  wlr_ref, w0_ref, a0_ref, kk_ref, ka_ref, rk_ref, gn_g_ref, gn_b_ref,
                  wout_ref, ln_g_ref, ln_b_ref, gsum_ref, gavg_ref, tri64_ref, cm_ref,
                  out_ref, p_scr, cat_scr, state_scr):
    t_idx = pl.program_id(1)
    tile = x_ref.shape[0]
    top = SUBLANES

    @pl.when(t_idx == 0)
    def _():
        state_scr[...] = jnp.zeros_like(state_scr)
        p_scr[0:top, :] = jnp.zeros((top, N_COLS), F32)

    shift = mod_ref[0:1, :]
    scale = mod_ref[1:2, :]
    gate = mod_ref[2:3, :]

    h = x_ref[...] * (1.0 + scale) + shift
    p_scr[top:top + tile, :] = jnp.dot(h.astype(BF16), win_ref[...], preferred_element_type=F32)

    gsum = gsum_ref[...]
    gavg = gavg_ref[...]

    row_a = lax.broadcasted_iota(jnp.int32, (CHUNK_A, CHUNK_A), 0)
    col_a = lax.broadcasted_iota(jnp.int32, (CHUNK_A, CHUNK_A), 1)
    causal = col_a <= row_a
    lane_a = lax.broadcasted_iota(jnp.int32, (CHUNK_A, LANES), 1)
    first_a = lane_a < HEAD

    def chunk_a(ci, carry):
        r0 = pl.multiple_of(top + ci * CHUNK_A, SUBLANES)
        rows = pl.ds(r0, CHUNK_A)
        u = _gelu_tanh(p_scr[rows, 0:D_A])
        v = _gelu_tanh(p_scr[rows, D_A:2 * D_A])
        v = _group_layer_norm(v, gavg, lnv_g_ref[...], lnv_b_ref[...], LN_EPS)
        mixed = []
        for j in range(D_A // LANES):
            vj = v[:, LANES * j:LANES * (j + 1)]
            w_even = jnp.where(causal, ws_ref[2 * j], 0.0)
            w_odd = jnp.where(causal, ws_ref[2 * j + 1], 0.0)
            mixed.append(jnp.where(first_a, _dot(w_even, vj), _dot(w_odd, vj)))
        mix = jnp.concatenate(mixed, axis=1) + bs_ref[...]
        out_a = u * mix * _silu(p_scr[rows, 2 * D_A:3 * D_A])
        o0 = pl.multiple_of(ci * CHUNK_A, CHUNK_A)
        cat_scr[pl.ds(o0, CHUNK_A), 0:D_A] = out_a.astype(BF16)
        return carry

    lax.fori_loop(0, tile // CHUNK_A, chunk_a, 0)

    consts = tuple(cm_ref[i] for i in range(7))
    row_b = lax.broadcasted_iota(jnp.int32, (SCAN_L, N_COLS_B), 0)
    lane_b = lax.broadcasted_iota(jnp.int32, (SCAN_L, LANES), 1)
    cb = N_COLS_A

    def chunk_b(ci, carry):
        r0 = pl.multiple_of(top + ci * SCAN_L, SUBLANES)
        cur = p_scr[pl.ds(r0, SCAN_L), cb:]
        before = p_scr[pl.ds(r0 - SUBLANES, SUBLANES), cb:]
        prev = jnp.where(row_b == 0, before[SUBLANES - 1:SUBLANES, :], pltpu.roll(cur, 1, axis=0))
        ps = cur + mu_ref[...] * (prev - cur)
        r = ps[:, 0:D_B]
        k = ps[:, D_B:2 * D_B]
        v = ps[:, 2 * D_B:3 * D_B]
        gate_b = ps[:, 3 * D_B:4 * D_B]
        low = ps[:, 4 * D_B:]
        low = jnp.where(lane_b < LOW_RANK, jnp.tanh(low), low)
        up = _dot(low, wlr_ref[...])
        lw = LOG_DECAY_SCALE * _sigmoid(w0_ref[...] + up[:, 0:D_B])
        ah = _sigmoid(a0_ref[...] + up[:, D_B:])
        kk = k * kk_ref[...]
        kk = kk * lax.rsqrt(_group_sum(kk * kk, gsum) + 1e-12)
        k = k * (1.0 + (ah - 1.0) * ka_ref[...])
        bb = kk * ah
        g = jnp.dot(tri64_ref[...], jnp.concatenate(_split_bf16(lw, 3), axis=1),
                    preferred_element_type=F32)
        g = g[:, 0:D_B] + g[:, D_B:2 * D_B] + g[:, 2 * D_B:]
        ys = []
        for j in range(N_PAIRS):
            sl = slice(LANES * j, LANES * (j + 1))
            y_j, s_new = _scan_chunk_pair(r[:, sl], k[:, sl], v[:, sl], kk[:, sl], bb[:, sl],
                                          g[:, sl], lw[:, sl], state_scr[j], consts)
            state_scr[j] = s_new
            ys.append(y_j)
        y = jnp.concatenate(ys, axis=1)
        y = _group_layer_norm(y, gavg, gn_g_ref[...], gn_b_ref[...], GN_EPS)
        bonus = _group_sum(r * k * rk_ref[...], gsum) * v
        out_b = (y + bonus) * _silu(gate_b)
        o0 = pl.multiple_of(ci * SCAN_L, SCAN_L)
        cat_scr[pl.ds(o0, SCAN_L), D_A:] = out_b.astype(BF16)
        return carry

    lax.fori_loop(0, tile // SCAN_L, chunk_b, 0)

    p_scr[0:top, :] = p_scr[tile:tile + top, :]

    o = jnp.dot(cat_scr[...], wout_ref[...], preferred_element_type=F32)
    res = ALPHA * x_ref[...] + gate * o
    mean = jnp.mean(res, axis=-1, keepdims=True)
    rc = res - mean
    var = jnp.mean(rc * rc, axis=-1, keepdims=True)
    out_ref[...] = rc * lax.rsqrt(var + LN_EPS) * ln_g_ref[...] + ln_b_ref[...]


def _chunk_masks():
    n = PAIR_ROWS
    i = jnp.arange(n)[:, None]
    j = jnp.arange(n)[None, :]
    same = lambda s: (i // s) == (j // s)
    eye = i == j
    masks = [eye, j < i, j <= i, same(8),
             same(16) & ~same(8), same(32) & ~same(16), same(64) & ~same(32)]
    return jnp.stack([m.astype(F32) for m in masks])


def _const_spec(shape):
    zeros = (0,) * len(shape)
    return pl.BlockSpec(shape, lambda b, t: zeros)


def kernel(x, c, w_ada, b_ada, w_in, mu_b, ln_v_g, ln_v_b, w_spatial, b_spatial, w0, w_up, a0,
           a_up, k_k, k_a, r_k, gn_g, gn_b, w_out, ln_g, ln_b):
    batch, seq, d_model = x.shape
    assert d_model == D_MODEL and w_in.shape == (D_MODEL, N_COLS)
    assert seq % TILE_T == 0 and TILE_T % CHUNK_A == 0

    mod = pl.pallas_call(
        _ada_kernel,
        out_shape=jax.ShapeDtypeStruct((batch, 3 * d_model), F32),
        name="adaln_mod",
    )(c, w_ada, b_ada.reshape(1, -1))
    mod = mod.reshape(batch, 3, d_model)

    row = lambda z: z.reshape(1, -1).astype(F32)
    lane_id = jnp.arange(LANES)
    same_head = (lane_id[:, None] // HEAD) == (lane_id[None, :] // HEAD)
    gsum = same_head.astype(BF16)
    gavg = (same_head.astype(F32) / HEAD).astype(BF16)
    tri64 = (jnp.arange(SCAN_L)[None, :] <= jnp.arange(SCAN_L)[:, None]).astype(BF16)
    zeros_lr = jnp.zeros((LOW_RANK, D_B), F32)
    w_lr = jnp.concatenate([jnp.concatenate([w_up, zeros_lr], axis=1),
                            jnp.concatenate([zeros_lr, a_up], axis=1)], axis=0).astype(BF16)
    bias_a = jnp.repeat(b_spatial.T, HEAD, axis=1)

    operands = [
        (x, pl.BlockSpec((None, TILE_T, d_model), lambda b, t: (b, t, 0))),
        (mod, pl.BlockSpec((None, 3, d_model), lambda b, t: (b, 0, 0))),
        (w_in.astype(BF16), _const_spec((d_model, N_COLS))),
        (row(mu_b), _const_spec((1, N_COLS_B))),
        (row(ln_v_g), _const_spec((1, D_A))),
        (row(ln_v_b), _const_spec((1, D_A))),
        (w_spatial, _const_spec(w_spatial.shape)),
        (bias_a, _const_spec((CHUNK_A, D_A))),
        (w_lr, _const_spec((2 * LOW_RANK, 2 * D_B))),
        (row(w0), _const_spec((1, D_B))),
        (row(a0), _const_spec((1, D_B))),
        (row(k_k), _const_spec((1, D_B))),
        (row(k_a), _const_spec((1, D_B))),
        (row(r_k), _const_spec((1, D_B))),
        (row(gn_g), _const_spec((1, D_B))),
        (row(gn_b), _const_spec((1, D_B))),
        (w_out.astype(BF16), _const_spec((d_model, d_model))),
        (row(ln_g), _const_spec((1, d_model))),
        (row(ln_b), _const_spec((1, d_model))),
        (gsum, _const_spec((LANES, LANES))),
        (gavg, _const_spec((LANES, LANES))),
        (tri64, _const_spec((SCAN_L, SCAN_L))),
        (_chunk_masks(), _const_spec((7, PAIR_ROWS, PAIR_ROWS))),
    ]
    return pl.pallas_call(
        _layer_kernel,
        grid=(batch, seq // TILE_T),
        in_specs=[spec for _, spec in operands],
        out_specs=pl.BlockSpec((None, TILE_T, d_model), lambda b, t: (b, t, 0)),
        out_shape=jax.ShapeDtypeStruct(x.shape, x.dtype),
        scratch_shapes=[
            pltpu.VMEM((TILE_T + SUBLANES, N_COLS), F32),
            pltpu.VMEM((TILE_T, d_model), BF16),
            pltpu.VMEM((N_PAIRS, LANES, LANES), F32),
        ],
        compiler_params=pltpu.CompilerParams(
            dimension_semantics=("arbitrary", "arbitrary"),
            vmem_limit_bytes=VMEM_LIMIT_BYTES),
        name="hybrid_layer",
    )(*[arr for arr, _ in operands])
```

```python
import math

import jax
import jax.numpy as jnp
from jax import lax
from jax.experimental import pallas as pl
from jax.experimental.pallas import tpu as pltpu

F32 = jnp.float32
BF16 = jnp.bfloat16

D_MODEL = 1024
D_A = 512
D_B = 512
HEAD = 64
CHUNK_A = 128
LOW_RANK = 64
N_COLS_A = 3 * D_A
N_COLS_B = 4 * D_B + 2 * LOW_RANK
N_COLS = N_COLS_A + N_COLS_B
LN_EPS = 1e-5
GN_EPS = 64e-5
ALPHA = 2.0 ** 0.25

LANES = 128
SUBLANES = 8
N_PAIRS = D_B // LANES
SCAN_L = 64
PAIR_ROWS = 2 * SCAN_L
TILE_T = 512
N_CHUNKS = TILE_T // SCAN_L
POST_ROWS = 128
VMEM_LIMIT_BYTES = 56 * 1024 * 1024
LOG_DECAY_SCALE = -math.exp(-0.5)


def _bf(x):
    return x.astype(BF16)


def _mm(a, b):
    return jnp.dot(a, b, preferred_element_type=F32)


def _mm_nt(a, b):
    return lax.dot_general(a, b, (((1,), (1,)), ((), ())), preferred_element_type=F32)


def _mm_tn(a, b):
    return lax.dot_general(a, b, (((0,), (0,)), ((), ())), preferred_element_type=F32)


def _lock(fn, *lists):
    return [fn(*args) for args in zip(*lists)]


def _split_bf16(x, terms):
    parts = []
    rem = x
    for _ in range(terms):
        hi = _bf(rem)
        parts.append(hi)
        rem = rem - hi.astype(F32)
    return parts


def _dot_split(x, w_bf16, terms=2):
    stacked = jnp.concatenate(_split_bf16(x, terms), axis=0)
    res = _mm(stacked, w_bf16)
    rows = x.shape[0]
    out = res[0:rows]
    for i in range(1, terms):
        out = out + res[i * rows:(i + 1) * rows]
    return out


def _group_sum(x, gmat):
    rows = x.shape[0]
    n = x.shape[1] // LANES
    xs = jnp.concatenate([x[:, LANES * j:LANES * (j + 1)] for j in range(n)], axis=0)
    s = _dot_split(xs, gmat, terms=2)
    return jnp.concatenate([s[rows * j:rows * (j + 1)] for j in range(n)], axis=1)


def _sigmoid(x):
    return 1.0 / (1.0 + jnp.exp(-x))


def _silu(x):
    return x * _sigmoid(x)


def _gelu_tanh(x):
    c = math.sqrt(2.0 / math.pi)
    return 0.5 * x * (1.0 + jnp.tanh(c * (x + 0.044715 * (x * x * x))))


def _group_layer_norm(x, gavg, gamma, beta, eps):
    mean = _group_sum(x, gavg)
    xc = x - mean
    var = _group_sum(xc * xc, gavg)
    return xc * lax.rsqrt(var + eps) * gamma + beta


def _unit_lower_inverse(a_abs, eye, m8, off16, off32, off64):
    a8 = _lock(lambda a: a * m8, a_abs)
    a8b = _lock(_bf, a8)
    a2b = _lock(lambda a: _bf(_mm(a, a)), a8b)
    p1 = _lock(lambda a: eye + a, a8)
    a4b = _lock(lambda a: _bf(_mm(a, a)), a2b)
    p2 = _lock(lambda p, a: p + _mm(_bf(p), a), p1, a2b)
    t = _lock(lambda p, a: p + _mm(_bf(p), a), p2, a4b)
    for off in (off16, off32, off64):
        tb = _lock(_bf, t)
        to = _lock(lambda x, a: _bf(_mm(x, _bf(a * off))), tb, a_abs)
        t = _lock(lambda x, xo, xb: x + _mm(xo, xb), t, to, tb)
    return t


def _scan_prepare(r, k, v, kk, bb, g, lw, consts):
    eye, tri_strict, tri_incl, m8, off16, off32, off64 = consts
    lane = lax.broadcasted_iota(jnp.int32, (SCAN_L, LANES), 1)
    first = lane < HEAD

    def stack2(z):
        return jnp.concatenate([jnp.where(first, z, 0.0), jnp.where(first, 0.0, z)], axis=0)

    def factors(r, k, v, kk, bb, g, lw):
        e_g = jnp.exp(g)
        e_ge = jnp.exp(g - lw)
        e_ng = jnp.exp(-g)
        e_gl = jnp.exp(g[SCAN_L - 1:SCAN_L, :] - g)
        xa = _bf(stack2(-(kk * e_ge)))
        xr = _bf(stack2(r * e_g))
        vp = _bf(stack2(v))
        bk = _bf(jnp.concatenate([stack2(bb * e_ng), stack2(k * e_ng)], axis=0))
        bbar = _bf(stack2(bb * e_gl))
        kbar = _bf(stack2(k * e_gl))
        return xa, xr, vp, bk, bbar, kbar

    xa, xr, vp, bk, bbar, kbar = zip(*_lock(factors, r, k, v, kk, bb, g, lw))
    aa = _lock(lambda a, b, c: _mm_nt(jnp.concatenate([a, b], axis=0), c), xa, xr, bk)
    a_ab = _lock(lambda m: jnp.where(tri_strict != 0.0, m[0:PAIR_ROWS, 0:PAIR_ROWS], 0.0), aa)
    a_ak = _lock(lambda m: _bf(jnp.where(tri_strict != 0.0, m[0:PAIR_ROWS, PAIR_ROWS:], 0.0)), aa)
    a_rb = _lock(lambda m: _bf(jnp.where(tri_incl != 0.0, m[PAIR_ROWS:, 0:PAIR_ROWS], 0.0)), aa)
    a_rk = _lock(lambda m: _bf(jnp.where(tri_incl != 0.0, m[PAIR_ROWS:, PAIR_ROWS:], 0.0)), aa)
    akv = _lock(lambda a, b: _bf(_mm(a, b)), a_ak, vp)
    yv = _lock(_mm, a_rk, vp)
    ktv = _lock(_mm_tn, vp, kbar)
    t_inv = _lock(_bf, _unit_lower_inverse(a_ab, eye, m8, off16, off32, off64))
    q = _lock(lambda t, x: _bf(_mm(t, x)), t_inv, xa)
    z = _lock(_mm, t_inv, akv)
    qx = _lock(lambda a, b: jnp.concatenate([a, b], axis=0), q, xr)
    return qx, z, a_rb, yv, list(bbar), ktv


def _scan_apply(states, qx, z, a_rb, yv, bbar, ktv, decay):
    xs = _lock(lambda a, s: _mm_nt(a, _bf(s)), qx, states)
    u = _lock(lambda x, zz: _bf(x[0:PAIR_ROWS] + zz), xs, z)
    new_states = _lock(lambda s, d, uu, b, kv: s * d + _mm_tn(uu, b) + kv,
                       states, decay, u, bbar, ktv)
    y2 = _lock(lambda x, a, uu, w: x[PAIR_ROWS:] + _mm(a, uu) + w, xs, a_rb, u, yv)
    ys = _lock(lambda m: m[0:SCAN_L] + m[SCAN_L:], y2)
    return ys, new_states


def _ada_kernel(c_ref, w_ref, b_ref, o_ref):
    s_hi, s_lo = _split_bf16(_silu(c_ref[...]), 2)
    w_hi, w_lo = _split_bf16(w_ref[...], 2)
    acc = _mm(s_hi, w_hi) + _mm(s_hi, w_lo) + _mm(s_lo, w_hi)
    o_ref[...] = acc + b_ref[...]


def _layer_kernel(x_ref, mod_ref, win_ref, mu_ref, lnv_g_ref, lnv_b_ref, ws_ref, bs_ref,
                  wlr_ref, w0_ref, a0_ref, kk_ref, ka_ref, rk_ref, gn_g_ref, gn_b_ref,
                  wout_ref, ln_g_ref, ln_b_ref, gsum_ref, gavg_ref, tri64_ref, cm_ref,
                  out_ref,
                  p_scr, cat_scr, state_scr, qx_scr, z_scr, arb_scr, yv_scr, bbar_scr, ktv_scr,
                  dec_scr, y_scr, bonus_scr, sgate_scr):
    t_idx = pl.program_id(1)
    tile = x_ref.shape[0]
    top = SUBLANES

    @pl.when(t_idx == 0)
    def _():
        state_scr[...] = jnp.zeros_like(state_scr)
        p_scr[0:top, :] = jnp.zeros((top, N_COLS), F32)

    shift = mod_ref[0:1, :]
    scale = mod_ref[1:2, :]
    gate = mod_ref[2:3, :]

    h = x_ref[...] * (1.0 + scale) + shift
    p_scr[top:top + tile, :] = _mm(_bf(h), win_ref[...])

    gsum = gsum_ref[...]
    gavg = gavg_ref[...]

    row_a = lax.broadcasted_iota(jnp.int32, (CHUNK_A, CHUNK_A), 0)
    col_a = lax.broadcasted_iota(jnp.int32, (CHUNK_A, CHUNK_A), 1)
    causal = col_a <= row_a
    lane_a = lax.broadcasted_iota(jnp.int32, (CHUNK_A, LANES), 1)
    first_a = lane_a < HEAD

    def chunk_a(ci, carry):
        r0 = pl.multiple_of(top + ci * CHUNK_A, SUBLANES)
        rows = pl.ds(r0, CHUNK_A)
        u = _gelu_tanh(p_scr[rows, 0:D_A])
        v = _gelu_tanh(p_scr[rows, D_A:2 * D_A])
        v = _group_layer_norm(v, gavg, lnv_g_ref[...], lnv_b_ref[...], LN_EPS)
        mixed = []
        for j in range(D_A // LANES):
            vj = _bf(v[:, LANES * j:LANES * (j + 1)])
            w_even = _bf(jnp.where(causal, ws_ref[2 * j], 0.0))
            w_odd = _bf(jnp.where(causal, ws_ref[2 * j + 1], 0.0))
            mixed.append(jnp.where(first_a, _mm(w_even, vj), _mm(w_odd, vj)))
        mix = jnp.concatenate(mixed, axis=1) + bs_ref[...]
        out_a = u * mix * _silu(p_scr[rows, 2 * D_A:3 * D_A])
        o0 = pl.multiple_of(ci * CHUNK_A, CHUNK_A)
        cat_scr[pl.ds(o0, CHUNK_A), 0:D_A] = _bf(out_a)
        return carry

    lax.fori_loop(0, tile // CHUNK_A, chunk_a, 0)

    consts = tuple(cm_ref[i] for i in range(7))
    row_b = lax.broadcasted_iota(jnp.int32, (SCAN_L, N_COLS_B), 0)
    lane_b = lax.broadcasted_iota(jnp.int32, (SCAN_L, LANES), 1)
    cb = N_COLS_A

    def prepare_chunk(ci, carry):
        r0 = pl.multiple_of(top + ci * SCAN_L, SUBLANES)
        cur = p_scr[pl.ds(r0, SCAN_L), cb:]
        before = p_scr[pl.ds(r0 - SUBLANES, SUBLANES), cb:]
        prev = jnp.where(row_b == 0, before[SUBLANES - 1:SUBLANES, :], pltpu.roll(cur, 1, axis=0))
        ps = cur + mu_ref[...] * (prev - cur)
        r = ps[:, 0:D_B]
        k = ps[:, D_B:2 * D_B]
        v = ps[:, 2 * D_B:3 * D_B]
        gate_b = ps[:, 3 * D_B:4 * D_B]
        low = ps[:, 4 * D_B:]
        low = jnp.where(lane_b < LOW_RANK, jnp.tanh(low), low)
        up = _mm(_bf(low), wlr_ref[...])
        lw = LOG_DECAY_SCALE * _sigmoid(w0_ref[...] + up[:, 0:D_B])
        ah = _sigmoid(a0_ref[...] + up[:, D_B:])
        kk = k * kk_ref[...]
        kk = kk * lax.rsqrt(_group_sum(kk * kk, gsum) + 1e-12)
        k = k * (1.0 + (ah - 1.0) * ka_ref[...])
        bb = kk * ah
        g = _mm(tri64_ref[...], jnp.concatenate(_split_bf16(lw, 3), axis=1))
        g = g[:, 0:D_B] + g[:, D_B:2 * D_B] + g[:, 2 * D_B:]

        o0 = pl.multiple_of(ci * SCAN_L, SCAN_L)
        bonus_scr[pl.ds(o0, SCAN_L), :] = _group_sum(r * k * rk_ref[...], gsum) * v
        sgate_scr[pl.ds(o0, SCAN_L), :] = _silu(gate_b)
        dec_scr[ci] = jnp.broadcast_to(jnp.exp(g[SCAN_L - 1:SCAN_L, :]), (SUBLANES, D_B))

        split = lambda z: [z[:, LANES * j:LANES * (j + 1)] for j in range(N_PAIRS)]
        prepared = _scan_prepare(split(r), split(k), split(v), split(kk), split(bb),
                                 split(g), split(lw), consts)
        for j in range(N_PAIRS):
            slot = ci * N_PAIRS + j
            for ref, vals in zip((qx_scr, z_scr, arb_scr, yv_scr, bbar_scr, ktv_scr), prepared):
                ref[slot] = vals[j]
        return carry

    lax.fori_loop(0, N_CHUNKS, prepare_chunk, 0)

    def apply_chunk(ci, carry):
        slots = [ci * N_PAIRS + j for j in range(N_PAIRS)]
        load = lambda ref: [ref[s] for s in slots]
        dec = dec_scr[ci]
        decay = [dec[0:1, LANES * j:LANES * (j + 1)] for j in range(N_PAIRS)]
        states = [state_scr[j] for j in range(N_PAIRS)]
        ys, new_states = _scan_apply(states, load(qx_scr), load(z_scr), load(arb_scr),
                                     load(yv_scr), load(bbar_scr), load(ktv_scr), decay)
        for j in range(N_PAIRS):
            state_scr[j] = new_states[j]
        o0 = pl.multiple_of(ci * SCAN_L, SCAN_L)
        y_scr[pl.ds(o0, SCAN_L), :] = jnp.concatenate(ys, axis=1)
        return carry

    lax.fori_loop(0, N_CHUNKS, apply_chunk, 0)

    def post_chunk(ci, carry):
        rows = pl.ds(pl.multiple_of(ci * POST_ROWS, POST_ROWS), POST_ROWS)
        y = _group_layer_norm(y_scr[rows, :], gavg, gn_g_ref[...], gn_b_ref[...], GN_EPS)
        cat_scr[rows, D_A:] = _bf((y + bonus_scr[rows, :]) * sgate_scr[rows, :])
        return carry

    lax.fori_loop(0, tile // POST_ROWS, post_chunk, 0)

    p_scr[0:top, :] = p_scr[tile:tile + top, :]

    o = _mm(cat_scr[...], wout_ref[...])
    res = ALPHA * x_ref[...] + gate * o
    mean = jnp.mean(res, axis=-1, keepdims=True)
    rc = res - mean
    var = jnp.mean(rc * rc, axis=-1, keepdims=True)
    out_ref[...] = rc * lax.rsqrt(var + LN_EPS) * ln_g_ref[...] + ln_b_ref[...]


def _chunk_masks():
    n = PAIR_ROWS
    i = jnp.arange(n)[:, None]
    j = jnp.arange(n)[None, :]
    same = lambda s: (i // s) == (j // s)
    eye = i == j
    masks = [eye, j < i, j <= i, same(8),
             same(16) & ~same(8), same(32) & ~same(16), same(64) & ~same(32)]
    return jnp.stack([m.astype(F32) for m in masks])


def _const_spec(shape):
    zeros = (0,) * len(shape)
    return pl.BlockSpec(shape, lambda b, t: zeros)


def kernel(x, c, w_ada, b_ada, w_in, mu_b, ln_v_g, ln_v_b, w_spatial, b_spatial, w0, w_up, a0,
           a_up, k_k, k_a, r_k, gn_g, gn_b, w_out, ln_g, ln_b):
    batch, seq, d_model = x.shape
    assert d_model == D_MODEL and w_in.shape == (D_MODEL, N_COLS)
    assert seq % TILE_T == 0 and TILE_T % CHUNK_A == 0

    mod = pl.pallas_call(
        _ada_kernel,
        out_shape=jax.ShapeDtypeStruct((batch, 3 * d_model), F32),
        name="adaln_mod",
    )(c, w_ada, b_ada.reshape(1, -1))
    mod = mod.reshape(batch, 3, d_model)

    row = lambda z: z.reshape(1, -1).astype(F32)
    lane_id = jnp.arange(LANES)
    same_head = (lane_id[:, None] // HEAD) == (lane_id[None, :] // HEAD)
    gsum = same_head.astype(BF16)
    gavg = (same_head.astype(F32) / HEAD).astype(BF16)
    tri64 = (jnp.arange(SCAN_L)[None, :] <= jnp.arange(SCAN_L)[:, None]).astype(BF16)
    zeros_lr = jnp.zeros((LOW_RANK, D_B), F32)
    w_lr = jnp.concatenate([jnp.concatenate([w_up, zeros_lr], axis=1),
                            jnp.concatenate([zeros_lr, a_up], axis=1)], axis=0).astype(BF16)
    bias_a = jnp.repeat(b_spatial.T, HEAD, axis=1)

    operands = [
        (x, pl.BlockSpec((None, TILE_T, d_model), lambda b, t: (b, t, 0))),
        (mod, pl.BlockSpec((None, 3, d_model), lambda b, t: (b, 0, 0))),
        (w_in.astype(BF16), _const_spec((d_model, N_COLS))),
        (row(mu_b), _const_spec((1, N_COLS_B))),
        (row(ln_v_g), _const_spec((1, D_A))),
        (row(ln_v_b), _const_spec((1, D_A))),
        (w_spatial, _const_spec(w_spatial.shape)),
        (bias_a, _const_spec((CHUNK_A, D_A))),
        (w_lr, _const_spec((2 * LOW_RANK, 2 * D_B))),
        (row(w0), _const_spec((1, D_B))),
        (row(a0), _const_spec((1, D_B))),
        (row(k_k), _const_spec((1, D_B))),
        (row(k_a), _const_spec((1, D_B))),
        (row(r_k), _const_spec((1, D_B))),
        (row(gn_g), _const_spec((1, D_B))),
        (row(gn_b), _const_spec((1, D_B))),
        (w_out.astype(BF16), _const_spec((d_model, d_model))),
        (row(ln_g), _const_spec((1, d_model))),
        (row(ln_b), _const_spec((1, d_model))),
        (gsum, _const_spec((LANES, LANES))),
        (gavg, _const_spec((LANES, LANES))),
        (tri64, _const_spec((SCAN_L, SCAN_L))),
        (_chunk_masks(), _const_spec((7, PAIR_ROWS, PAIR_ROWS))),
    ]
    n_slots = N_CHUNKS * N_PAIRS
    return pl.pallas_call(
        _layer_kernel,
        grid=(batch, seq // TILE_T),
        in_specs=[spec for _, spec in operands],
        out_specs=pl.BlockSpec((None, TILE_T, d_model), lambda b, t: (b, t, 0)),
        out_shape=jax.ShapeDtypeStruct(x.shape, x.dtype),
        scratch_shapes=[
            pltpu.VMEM((TILE_T + SUBLANES, N_COLS), F32),
            pltpu.VMEM((TILE_T, d_model), BF16),
            pltpu.VMEM((N_PAIRS, LANES, LANES), F32),
            pltpu.VMEM((n_slots, 2 * PAIR_ROWS, LANES), BF16),
            pltpu.VMEM((n_slots, PAIR_ROWS, LANES), F32),
            pltpu.VMEM((n_slots, PAIR_ROWS, PAIR_ROWS), BF16),
            pltpu.VMEM((n_slots, PAIR_ROWS, LANES), F32),
            pltpu.VMEM((n_slots, PAIR_ROWS, LANES), BF16),
            pltpu.VMEM((n_slots, LANES, LANES), F32),
            pltpu.VMEM((N_CHUNKS, SUBLANES, D_B), F32),
            pltpu.VMEM((TILE_T, D_B), F32),
            pltpu.VMEM((TILE_T, D_B), F32),
            pltpu.VMEM((TILE_T, D_B), F32),
        ],
        compiler_params=pltpu.CompilerParams(
            dimension_semantics=("arbitrary", "arbitrary"),
            vmem_limit_bytes=VMEM_LIMIT_BYTES),
        name="hybrid_layer",
    )(*[arr for arr, _ in operands])
```

```python
import math

import jax
import jax.numpy as jnp
from jax import lax
from jax.experimental import pallas as pl
from jax.experimental.pallas import tpu as pltpu

F32 = jnp.float32
BF16 = jnp.bfloat16

D_MODEL = 1024
D_A = 512
D_B = 512
HEAD = 64
CHUNK_A = 128
LOW_RANK = 64
N_COLS_A = 3 * D_A
N_COLS_B = 4 * D_B + 2 * LOW_RANK
N_COLS = N_COLS_A + N_COLS_B
LN_EPS = 1e-5
GN_EPS = 64e-5
ALPHA = 2.0 ** 0.25

LANES = 128
SUBLANES = 8
MXU_COLS = 256
N_PAIRS = D_B // LANES
N_GROUPS = D_B // MXU_COLS
HEADS_PER_GROUP = MXU_COLS // HEAD
SCAN_L = 64
PAIR_ROWS = 2 * SCAN_L
TILE_T = 512
N_CHUNKS = TILE_T // SCAN_L
PREP_CHUNKS = 4
PREP_ROWS = PREP_CHUNKS * SCAN_L
CHUNKS_PER_BLOCK = CHUNK_A // SCAN_L
VMEM_LIMIT_BYTES = 56 * 1024 * 1024
LOG_DECAY_SCALE = -math.exp(-0.5)


def _bf(x):
    return x.astype(BF16)


def _mm(a, b):
    return jnp.dot(a, b, preferred_element_type=F32)


def _mm_nt(a, b):
    return lax.dot_general(a, b, (((1,), (1,)), ((), ())), preferred_element_type=F32)


def _mm_tn(a, b):
    return lax.dot_general(a, b, (((0,), (0,)), ((), ())), preferred_element_type=F32)


def _aligned(index, multiple):
    return index if isinstance(index, int) else pl.multiple_of(index, multiple)


def _interleave(step_generators):
    pending = list(step_generators)
    while pending:
        for gen in list(pending):
            try:
                next(gen)
            except StopIteration:
                pending.remove(gen)


def _lock(fn, *lists):
    return [fn(*args) for args in zip(*lists)]


def _split_bf16(x, terms):
    parts = []
    rem = x
    for _ in range(terms):
        hi = _bf(rem)
        parts.append(hi)
        rem = rem - hi.astype(F32)
    return parts


def _dot_split(x, w_bf16, terms=2):
    stacked = jnp.concatenate(_split_bf16(x, terms), axis=0)
    res = _mm(stacked, w_bf16)
    rows = x.shape[0]
    out = res[0:rows]
    for i in range(1, terms):
        out = out + res[i * rows:(i + 1) * rows]
    return out


def _group_sum(x, gmat):
    rows = x.shape[0]
    n = x.shape[1] // LANES
    xs = jnp.concatenate([x[:, LANES * j:LANES * (j + 1)] for j in range(n)], axis=0)
    s = _dot_split(xs, gmat, terms=2)
    return jnp.concatenate([s[rows * j:rows * (j + 1)] for j in range(n)], axis=1)


def _sigmoid(x):
    return 1.0 / (1.0 + jnp.exp(-x))


def _silu(x):
    return x * _sigmoid(x)


def _gelu_tanh(x):
    c = math.sqrt(2.0 / math.pi)
    return 0.5 * x * (1.0 + jnp.tanh(c * (x + 0.044715 * (x * x * x))))


def _block_diag(y, head_masks):
    zero = jnp.zeros((y.shape[0], LANES), BF16)
    rows = []
    for h in range(HEADS_PER_GROUP):
        tile = h // 2
        kept = y[:, LANES * tile:LANES * (tile + 1)] * head_masks[h % 2]
        rows.append(jnp.concatenate([kept if t == tile else zero
                                     for t in range(MXU_COLS // LANES)], axis=1))
    return jnp.concatenate(rows, axis=0)


def _unit_lower_inverse(a_abs, eye, m8, off16, off32, off64, bd_mask):
    bd = lambda y: _block_diag(y, bd_mask)
    a8 = _lock(lambda a: a * m8, a_abs)
    a8b = _lock(_bf, a8)
    a2b = _lock(lambda a: _bf(_mm(a, bd(a))), a8b)
    p1 = _lock(lambda a: eye + a, a8)
    both = _lock(lambda a, p: _mm(jnp.concatenate([a, _bf(p)], axis=0), bd(a)), a2b, p1)
    a4b = _lock(lambda m: _bf(m[0:SCAN_L]), both)
    p2 = _lock(lambda p, m: p + m[SCAN_L:], p1, both)
    t = _lock(lambda p, a: p + _mm(_bf(p), bd(a)), p2, a4b)
    for off in (off16, off32, off64):
        tb = _lock(_bf, t)
        to = _lock(lambda x, a: _bf(_mm(x, bd(_bf(a * off)))), tb, a_abs)
        t = _lock(lambda x, xo, xb: x + _mm(xo, bd(xb)), t, to, tb)
    return t


def _scan_prepare(r, k, v, kk, bb, g, lw, consts, bd_mask):
    eye, tri_strict, tri_incl, m8, off16, off32, off64 = consts
    bd = lambda y: _block_diag(y, bd_mask)

    def factors(r, k, v, kk, bb, g, lw):
        e_g = jnp.exp(g)
        e_ge = jnp.exp(g - lw)
        e_ng = jnp.exp(-g)
        e_gl = jnp.exp(g[SCAN_L - 1:SCAN_L, :] - g)
        return (_bf(-(kk * e_ge)), _bf(r * e_g), _bf(v), _bf(bb * e_ng), _bf(k * e_ng),
                _bf(bb * e_gl), _bf(k * e_gl))

    xa, xr, vb, bh, kh, bbar, kbar = zip(*_lock(factors, r, k, v, kk, bb, g, lw))
    lhs = _lock(lambda a, b: jnp.concatenate([a, b], axis=0), xa, xr)
    aa_b = _lock(lambda x, y: _mm_nt(x, bd(y)), lhs, bh)
    aa_k = _lock(lambda x, y: _mm_nt(x, bd(y)), lhs, kh)
    a_ab = _lock(lambda m: jnp.where(tri_strict != 0.0, m[0:SCAN_L], 0.0), aa_b)
    a_rb = _lock(lambda m: _bf(jnp.where(tri_incl != 0.0, m[SCAN_L:], 0.0)), aa_b)
    a_ak = _lock(lambda m: _bf(jnp.where(tri_strict != 0.0, m[0:SCAN_L], 0.0)), aa_k)
    a_rk = _lock(lambda m: _bf(jnp.where(tri_incl != 0.0, m[SCAN_L:], 0.0)), aa_k)
    akv = _lock(lambda a, y: _bf(_mm(a, bd(y))), a_ak, vb)
    t_inv = _lock(_bf, _unit_lower_inverse(a_ab, eye, m8, off16, off32, off64, bd_mask))
    q = _lock(lambda t, y: _bf(_mm(t, bd(y))), t_inv, xa)
    z = _lock(lambda t, y: _mm(t, bd(y)), t_inv, akv)
    return q, list(xr), z, a_rb, a_rk, list(vb), list(bbar), list(kbar)


def _ada_kernel(c_ref, w_ref, b_ref, o_ref):
    s_hi, s_lo = _split_bf16(_silu(c_ref[...]), 2)
    w_hi, w_lo = _split_bf16(w_ref[...], 2)
    acc = _mm(s_hi, w_hi) + _mm(s_hi, w_lo) + _mm(s_lo, w_hi)
    o_ref[...] = acc + b_ref[...]


def _layer_kernel(x_ref, mod_ref, win_ref, mu_ref, lnv_g_ref, lnv_b_ref, ws_ref, bs_ref,
                  wlr_ref, w0_ref, a0_ref, kk_ref, ka_ref, rk_ref, gn_g_ref, gn_b_ref,
                  wout_ref, ln_g_ref, ln_b_ref, gsum_ref, gavg_ref, tri_ref, cm_ref,
                  out_ref,
                  p_scr, cat_scr, state_scr, qx_scr, z_scr, ar_scr, vp_scr, bk_scr,
                  dec_scr, y_scr, bonus_scr, sgate_scr):
    t_idx = pl.program_id(1)
    tile = x_ref.shape[0]
    top = SUBLANES

    @pl.when(t_idx == 0)
    def _():
        state_scr[...] = jnp.zeros_like(state_scr)
        p_scr[0:top, :] = jnp.zeros((top, N_COLS), F32)

    shift = mod_ref[0:1, :]
    scale = mod_ref[1:2, :]
    gate = mod_ref[2:3, :]

    h = x_ref[...] * (1.0 + scale) + shift
    p_scr[top:top + tile, :] = _mm(_bf(h), win_ref[...])

    gsum = gsum_ref[...]
    gavg = gavg_ref[...]

    row_a = lax.broadcasted_iota(jnp.int32, (CHUNK_A, CHUNK_A), 0)
    col_a = lax.broadcasted_iota(jnp.int32, (CHUNK_A, CHUNK_A), 1)
    causal = col_a <= row_a
    lane_a = lax.broadcasted_iota(jnp.int32, (CHUNK_A, LANES), 1)
    first_a = lane_a < HEAD

    def gmlp_steps(ci):
        r0 = _aligned(top + ci * CHUNK_A, SUBLANES)
        rows = pl.ds(r0, CHUNK_A)
        v = _gelu_tanh(p_scr[rows, D_A:2 * D_A])
        mean = _group_sum(v, gavg)
        yield
        vc = v - mean
        var = _group_sum(vc * vc, gavg)
        yield
        v = vc * lax.rsqrt(var + LN_EPS) * lnv_g_ref[...] + lnv_b_ref[...]
        mixed = []
        for j in range(D_A // LANES):
            vj = _bf(v[:, LANES * j:LANES * (j + 1)])
            w_even = _bf(jnp.where(causal, ws_ref[2 * j], 0.0))
            w_odd = _bf(jnp.where(causal, ws_ref[2 * j + 1], 0.0))
            mixed.append(jnp.where(first_a, _mm(w_even, vj), _mm(w_odd, vj)))
        yield
        mix = jnp.concatenate(mixed, axis=1) + bs_ref[...]
        u = _gelu_tanh(p_scr[rows, 0:D_A])
        out_a = u * mix * _silu(p_scr[rows, 2 * D_A:3 * D_A])
        o0 = _aligned(ci * CHUNK_A, CHUNK_A)
        cat_scr[pl.ds(o0, CHUNK_A), 0:D_A] = _bf(out_a)

    consts = tuple(cm_ref[i] for i in range(7))
    row_b = lax.broadcasted_iota(jnp.int32, (PREP_ROWS, N_COLS_B), 0)
    lane_b = lax.broadcasted_iota(jnp.int32, (PREP_ROWS, LANES), 1)
    lane_s = lax.broadcasted_iota(jnp.int32, (SCAN_L, LANES), 1)
    first_f = lane_s < HEAD
    first_b = jnp.where(first_f, 1.0, 0.0).astype(BF16)
    second_b = jnp.where(first_f, 0.0, 1.0).astype(BF16)
    bd_mask = (first_b, second_b)
    cb = N_COLS_A

    def stack2(zz):
        return jnp.concatenate([jnp.where(first_f, zz, 0.0), jnp.where(first_f, 0.0, zz)], axis=0)

    def stack2b(zz):
        return jnp.concatenate([zz * first_b, zz * second_b], axis=0)

    def prepare_block(bi, carry):
        r0 = pl.multiple_of(top + bi * PREP_ROWS, SUBLANES)
        cur = p_scr[pl.ds(r0, PREP_ROWS), cb:]
        before = p_scr[pl.ds(r0 - SUBLANES, SUBLANES), cb:]
        prev = jnp.where(row_b == 0, before[SUBLANES - 1:SUBLANES, :], pltpu.roll(cur, 1, axis=0))
        ps = cur + mu_ref[...] * (prev - cur)
        r = ps[:, 0:D_B]
        k = ps[:, D_B:2 * D_B]
        v = ps[:, 2 * D_B:3 * D_B]
        gate_b = ps[:, 3 * D_B:4 * D_B]
        low = ps[:, 4 * D_B:]
        low = jnp.where(lane_b < LOW_RANK, jnp.tanh(low), low)
        up = _mm(_bf(low), wlr_ref[...])
        lw = LOG_DECAY_SCALE * _sigmoid(w0_ref[...] + up[:, 0:D_B])
        ah = _sigmoid(a0_ref[...] + up[:, D_B:])
        kk = k * kk_ref[...]
        kk = kk * lax.rsqrt(_group_sum(kk * kk, gsum) + 1e-12)
        k = k * (1.0 + (ah - 1.0) * ka_ref[...])
        bb = kk * ah
        g = _mm(tri_ref[...], jnp.concatenate(_split_bf16(lw, 3), axis=1))
        g = g[:, 0:D_B] + g[:, D_B:2 * D_B] + g[:, 2 * D_B:]

        o0 = pl.multiple_of(bi * PREP_ROWS, PREP_ROWS)
        bonus_scr[pl.ds(o0, PREP_ROWS), :] = _group_sum(r * k * rk_ref[...], gsum) * v
        sgate_scr[pl.ds(o0, PREP_ROWS), :] = _silu(gate_b)

        problems = [(c, gi) for c in range(PREP_CHUNKS) for gi in range(N_GROUPS)]
        cut = lambda zz: [zz[SCAN_L * c:SCAN_L * (c + 1), MXU_COLS * gi:MXU_COLS * (gi + 1)]
                          for c, gi in problems]
        for c in range(PREP_CHUNKS):
            last = g[SCAN_L * (c + 1) - 1:SCAN_L * (c + 1), :]
            dec_scr[bi * PREP_CHUNKS + c] = jnp.broadcast_to(jnp.exp(last), (SUBLANES, D_B))

        q, xr, z, a_rb, a_rk, vb, bbar, kbar = _scan_prepare(
            cut(r), cut(k), cut(v), cut(kk), cut(bb), cut(g), cut(lw), consts, bd_mask)
        for i, (c, gi) in enumerate(problems):
            for pp in range(MXU_COLS // LANES):
                slot = (bi * PREP_CHUNKS + c) * N_PAIRS + gi * (MXU_COLS // LANES) + pp
                sl = slice(LANES * pp, LANES * (pp + 1))
                qx_scr[slot] = jnp.concatenate([stack2b(q[i][:, sl]), stack2b(xr[i][:, sl])], axis=0)
                z_scr[slot] = stack2(z[i][:, sl])
                ar_scr[slot] = jnp.concatenate([stack2b(a_rb[i][:, sl]), stack2b(a_rk[i][:, sl])],
                                               axis=1)
                vp_scr[slot] = stack2b(vb[i][:, sl])
                bk_scr[slot] = jnp.concatenate([stack2b(bbar[i][:, sl]), stack2b(kbar[i][:, sl])],
                                               axis=0)
        return carry

    lax.fori_loop(0, N_CHUNKS // PREP_CHUNKS, prepare_block, 0)

    def apply_steps(bi):
        states = [state_scr[j] for j in range(N_PAIRS)]
        for c in range(CHUNKS_PER_BLOCK):
            ci = bi * CHUNKS_PER_BLOCK + c
            load = lambda ref: [ref[ci * N_PAIRS + j] for j in range(N_PAIRS)]
            dec = dec_scr[ci]
            decay = [dec[0:1, LANES * j:LANES * (j + 1)] for j in range(N_PAIRS)]
            xs = _lock(lambda a, s: _mm_nt(a, _bf(s)), load(qx_scr), states)
            yield
            uv = _lock(lambda x, zz, w: jnp.concatenate([_bf(x[0:PAIR_ROWS] + zz), w], axis=0),
                       xs, load(z_scr), load(vp_scr))
            states = _lock(lambda s, d, m, b: s * d + _mm_tn(m, b),
                           states, decay, uv, load(bk_scr))
            y2 = _lock(lambda x, a, m: x[PAIR_ROWS:] + _mm(a, m), xs, load(ar_scr), uv)
            yield
            ys = _lock(lambda m: m[0:SCAN_L] + m[SCAN_L:], y2)
            y_scr[pl.ds(_aligned(ci * SCAN_L, SCAN_L), SCAN_L), :] = jnp.concatenate(ys, axis=1)
        for j in range(N_PAIRS):
            state_scr[j] = states[j]

    def post_steps(bi):
        rows = pl.ds(_aligned(bi * CHUNK_A, CHUNK_A), CHUNK_A)
        y = y_scr[rows, :]
        mean = _group_sum(y, gavg)
        yield
        yc = y - mean
        var = _group_sum(yc * yc, gavg)
        yield
        y = yc * lax.rsqrt(var + GN_EPS) * gn_g_ref[...] + gn_b_ref[...]
        cat_scr[rows, D_A:] = _bf((y + bonus_scr[rows, :]) * sgate_scr[rows, :])

    n_blocks = tile // CHUNK_A
    _interleave([apply_steps(0), gmlp_steps(0)])

    def mix_body(bi, carry):
        _interleave([apply_steps(bi), post_steps(bi - 1), gmlp_steps(bi)])
        return carry

    lax.fori_loop(1, n_blocks, mix_body, 0)
    _interleave([post_steps(n_blocks - 1)])

    p_scr[0:top, :] = p_scr[tile:tile + top, :]

    o = _mm(cat_scr[...], wout_ref[...])
    res = ALPHA * x_ref[...] + gate * o
    mean = jnp.mean(res, axis=-1, keepdims=True)
    rc = res - mean
    var = jnp.mean(rc * rc, axis=-1, keepdims=True)
    out_ref[...] = rc * lax.rsqrt(var + LN_EPS) * ln_g_ref[...] + ln_b_ref[...]


def _wide_masks():
    t = jnp.arange(SCAN_L)[:, None]
    s = (jnp.arange(MXU_COLS) % SCAN_L)[None, :]
    same = lambda n: (t // n) == (s // n)
    masks = [t == s, s < t, s <= t, same(8),
             same(16) & ~same(8), same(32) & ~same(16), same(64) & ~same(32)]
    return jnp.stack([m.astype(F32) for m in masks])


def _const_spec(shape):
    zeros = (0,) * len(shape)
    return pl.BlockSpec(shape, lambda b, t: zeros)


def kernel(x, c, w_ada, b_ada, w_in, mu_b, ln_v_g, ln_v_b, w_spatial, b_spatial, w0, w_up, a0,
           a_up, k_k, k_a, r_k, gn_g, gn_b, w_out, ln_g, ln_b):
    batch, seq, d_model = x.shape
    assert d_model == D_MODEL and w_in.shape == (D_MODEL, N_COLS)
    assert seq % TILE_T == 0 and TILE_T % CHUNK_A == 0 and N_CHUNKS % PREP_CHUNKS == 0

    mod = pl.pallas_call(
        _ada_kernel,
        out_shape=jax.ShapeDtypeStruct((batch, 3 * d_model), F32),
        name="adaln_mod",
    )(c, w_ada, b_ada.reshape(1, -1))
    mod = mod.reshape(batch, 3, d_model)

    row = lambda z: z.reshape(1, -1).astype(F32)
    lane_id = jnp.arange(LANES)
    same_head = (lane_id[:, None] // HEAD) == (lane_id[None, :] // HEAD)
    gsum = same_head.astype(BF16)
    gavg = (same_head.astype(F32) / HEAD).astype(BF16)
    tok = jnp.arange(PREP_ROWS)
    tri = ((tok[None, :] <= tok[:, None])
           & (tok[None, :] // SCAN_L == tok[:, None] // SCAN_L)).astype(BF16)
    zeros_lr = jnp.zeros((LOW_RANK, D_B), F32)
    w_lr = jnp.concatenate([jnp.concatenate([w_up, zeros_lr], axis=1),
                            jnp.concatenate([zeros_lr, a_up], axis=1)], axis=0).astype(BF16)
    bias_a = jnp.repeat(b_spatial.T, HEAD, axis=1)

    operands = [
        (x, pl.BlockSpec((None, TILE_T, d_model), lambda b, t: (b, t, 0))),
        (mod, pl.BlockSpec((None, 3, d_model), lambda b, t: (b, 0, 0))),
        (w_in.astype(BF16), _const_spec((d_model, N_COLS))),
        (row(mu_b), _const_spec((1, N_COLS_B))),
        (row(ln_v_g), _const_spec((1, D_A))),
        (row(ln_v_b), _const_spec((1, D_A))),
        (w_spatial, _const_spec(w_spatial.shape)),
        (bias_a, _const_spec((CHUNK_A, D_A))),
        (w_lr, _const_spec((2 * LOW_RANK, 2 * D_B))),
        (row(w0), _const_spec((1, D_B))),
        (row(a0), _const_spec((1, D_B))),
        (row(k_k), _const_spec((1, D_B))),
        (row(k_a), _const_spec((1, D_B))),
        (row(r_k), _const_spec((1, D_B))),
        (row(gn_g), _const_spec((1, D_B))),
        (row(gn_b), _const_spec((1, D_B))),
        (w_out.astype(BF16), _const_spec((d_model, d_model))),
        (row(ln_g), _const_spec((1, d_model))),
        (row(ln_b), _const_spec((1, d_model))),
        (gsum, _const_spec((LANES, LANES))),
        (gavg, _const_spec((LANES, LANES))),
        (tri, _const_spec((PREP_ROWS, PREP_ROWS))),
        (_wide_masks(), _const_spec((7, SCAN_L, MXU_COLS))),
    ]
    n_slots = N_CHUNKS * N_PAIRS
    return pl.pallas_call(
        _layer_kernel,
        grid=(batch, seq // TILE_T),
        in_specs=[spec for _, spec in operands],
        out_specs=pl.BlockSpec((None, TILE_T, d_model), lambda b, t: (b, t, 0)),
        out_shape=jax.ShapeDtypeStruct(x.shape, x.dtype),
        scratch_shapes=[
            pltpu.VMEM((TILE_T + SUBLANES, N_COLS), F32),
            pltpu.VMEM((TILE_T, d_model), BF16),
            pltpu.VMEM((N_PAIRS, LANES, LANES), F32),
            pltpu.VMEM((n_slots, 2 * PAIR_ROWS, LANES), BF16),
            pltpu.VMEM((n_slots, PAIR_ROWS, LANES), F32),
            pltpu.VMEM((n_slots, PAIR_ROWS, 2 * PAIR_ROWS), BF16),
            pltpu.VMEM((n_slots, PAIR_ROWS, LANES), BF16),
            pltpu.VMEM((n_slots, 2 * PAIR_ROWS, LANES), BF16),
            pltpu.VMEM((N_CHUNKS, SUBLANES, D_B), F32),
            pltpu.VMEM((TILE_T, D_B), F32),
            pltpu.VMEM((TILE_T, D_B), F32),
            pltpu.VMEM((TILE_T, D_B), F32),
        ],
        compiler_params=pltpu.CompilerParams(
            dimension_semantics=("arbitrary", "arbitrary"),
            vmem_limit_bytes=VMEM_LIMIT_BYTES),
        name="hybrid_layer",
    )(*[arr for arr, _ in operands])
```

```python
import math

import jax
import jax.numpy as jnp
from jax import lax
from jax.experimental import pallas as pl
from jax.experimental.pallas import tpu as pltpu

F32 = jnp.float32
BF16 = jnp.bfloat16

D_MODEL = 1024
D_A = 512
D_B = 512
HEAD = 64
CHUNK_A = 128
LOW_RANK = 64
N_COLS_A = 3 * D_A
N_COLS_B = 4 * D_B + 2 * LOW_RANK
N_COLS = N_COLS_A + N_COLS_B
LN_EPS = 1e-5
GN_EPS = 64e-5
ALPHA = 2.0 ** 0.25

LANES = 128
SUBLANES = 8
MXU_COLS = 256
N_PAIRS = D_B // LANES
N_GROUPS = D_B // MXU_COLS
HEADS_PER_GROUP = MXU_COLS // HEAD
SCAN_L = 64
PAIR_ROWS = 2 * SCAN_L
TILE_T = 512
N_CHUNKS = TILE_T // SCAN_L
PREP_CHUNKS = 4
PREP_ROWS = PREP_CHUNKS * SCAN_L
CHUNKS_PER_BLOCK = CHUNK_A // SCAN_L
VMEM_LIMIT_BYTES = 56 * 1024 * 1024
LOG_DECAY_SCALE = -math.exp(-0.5)


def _bf(x):
    return x.astype(BF16)


def _mm(a, b):
    return jnp.dot(a, b, preferred_element_type=F32)


def _mm_nt(a, b):
    return lax.dot_general(a, b, (((1,), (1,)), ((), ())), preferred_element_type=F32)


def _mm_tn(a, b):
    return lax.dot_general(a, b, (((0,), (0,)), ((), ())), preferred_element_type=F32)


def _aligned(index, multiple):
    return index if isinstance(index, int) else pl.multiple_of(index, multiple)


def _interleave(step_generators):
    pending = list(step_generators)
    while pending:
        for gen in list(pending):
            try:
                next(gen)
            except StopIteration:
                pending.remove(gen)


def _lock(fn, *lists):
    return [fn(*args) for args in zip(*lists)]


def _split_bf16(x, terms):
    parts = []
    rem = x
    for _ in range(terms):
        hi = _bf(rem)
        parts.append(hi)
        rem = rem - hi.astype(F32)
    return parts


def _dot_split(x, w_bf16, terms=2):
    stacked = jnp.concatenate(_split_bf16(x, terms), axis=0)
    res = _mm(stacked, w_bf16)
    rows = x.shape[0]
    out = res[0:rows]
    for i in range(1, terms):
        out = out + res[i * rows:(i + 1) * rows]
    return out


def _group_sum(x, gmat):
    rows = x.shape[0]
    n = x.shape[1] // LANES
    xs = jnp.concatenate([x[:, LANES * j:LANES * (j + 1)] for j in range(n)], axis=0)
    s = _dot_split(xs, gmat, terms=2)
    return jnp.concatenate([s[rows * j:rows * (j + 1)] for j in range(n)], axis=1)


def _sigmoid(x):
    return 1.0 / (1.0 + jnp.exp(-x))


def _silu(x):
    return x * _sigmoid(x)


def _gelu_tanh(x):
    c = math.sqrt(2.0 / math.pi)
    return 0.5 * x * (1.0 + jnp.tanh(c * (x + 0.044715 * (x * x * x))))


def _block_diag(y, head_masks):
    zero = jnp.zeros((y.shape[0], LANES), BF16)
    rows = []
    for h in range(HEADS_PER_GROUP):
        tile = h // 2
        kept = y[:, LANES * tile:LANES * (tile + 1)] * head_masks[h % 2]
        rows.append(jnp.concatenate([kept if t == tile else zero
                                     for t in range(MXU_COLS // LANES)], axis=1))
    return jnp.concatenate(rows, axis=0)


def _unit_lower_inverse_steps(a_abs, eye, m8, off16, off32, off64, bd_mask):
    bd = lambda y: _block_diag(y, bd_mask)
    a8 = _lock(lambda a: a * m8, a_abs)
    a8b = _lock(_bf, a8)
    a2b = _lock(lambda a: _bf(_mm(a, bd(a))), a8b)
    yield
    p1 = _lock(lambda a: eye + a, a8)
    both = _lock(lambda a, p: _mm(jnp.concatenate([a, _bf(p)], axis=0), bd(a)), a2b, p1)
    yield
    a4b = _lock(lambda m: _bf(m[0:SCAN_L]), both)
    p2 = _lock(lambda p, m: p + m[SCAN_L:], p1, both)
    t = _lock(lambda p, a: p + _mm(_bf(p), bd(a)), p2, a4b)
    yield
    for off in (off16, off32, off64):
        tb = _lock(_bf, t)
        to = _lock(lambda x, a: _bf(_mm(x, bd(_bf(a * off)))), tb, a_abs)
        yield
        t = _lock(lambda x, xo, xb: x + _mm(xo, bd(xb)), t, to, tb)
        yield
    return t


def _scan_prepare_steps(r, k, v, kk, bb, g, lw, consts, bd_mask):
    eye, tri_strict, tri_incl, m8, off16, off32, off64 = consts
    bd = lambda y: _block_diag(y, bd_mask)

    def factors(r, k, v, kk, bb, g, lw):
        e_g = jnp.exp(g)
        e_ge = jnp.exp(g - lw)
        e_ng = jnp.exp(-g)
        e_gl = jnp.exp(g[SCAN_L - 1:SCAN_L, :] - g)
        return (_bf(-(kk * e_ge)), _bf(r * e_g), _bf(v), _bf(bb * e_ng), _bf(k * e_ng),
                _bf(bb * e_gl), _bf(k * e_gl))

    xa, xr, vb, bh, kh, bbar, kbar = zip(*_lock(factors, r, k, v, kk, bb, g, lw))
    lhs = _lock(lambda a, b: jnp.concatenate([a, b], axis=0), xa, xr)
    aa_b = _lock(lambda x, y: _mm_nt(x, bd(y)), lhs, bh)
    yield
    aa_k = _lock(lambda x, y: _mm_nt(x, bd(y)), lhs, kh)
    yield
    a_ab = _lock(lambda m: jnp.where(tri_strict != 0.0, m[0:SCAN_L], 0.0), aa_b)
    a_rb = _lock(lambda m: _bf(jnp.where(tri_incl != 0.0, m[SCAN_L:], 0.0)), aa_b)
    a_ak = _lock(lambda m: _bf(jnp.where(tri_strict != 0.0, m[0:SCAN_L], 0.0)), aa_k)
    a_rk = _lock(lambda m: _bf(jnp.where(tri_incl != 0.0, m[SCAN_L:], 0.0)), aa_k)
    t_inv = yield from _unit_lower_inverse_steps(a_ab, eye, m8, off16, off32, off64, bd_mask)
    t_inv = _lock(_bf, t_inv)
    akv = _lock(lambda a, y: _bf(_mm(a, bd(y))), a_ak, vb)
    yield
    q = _lock(lambda t, y: _bf(_mm(t, bd(y))), t_inv, xa)
    yield
    z = _lock(lambda t, y: _mm(t, bd(y)), t_inv, akv)
    yield
    return q, list(xr), z, a_rb, a_rk, list(vb), list(bbar), list(kbar)


def _ada_kernel(c_ref, w_ref, b_ref, o_ref):
    s_hi, s_lo = _split_bf16(_silu(c_ref[...]), 2)
    w_hi, w_lo = _split_bf16(w_ref[...], 2)
    acc = _mm(s_hi, w_hi) + _mm(s_hi, w_lo) + _mm(s_lo, w_hi)
    o_ref[...] = acc + b_ref[...]


def _layer_kernel(x_ref, mod_ref, win_ref, mu_ref, lnv_g_ref, lnv_b_ref, ws_ref, bs_ref,
                  wlr_ref, w0_ref, a0_ref, kk_ref, ka_ref, rk_ref, gn_g_ref, gn_b_ref,
                  wout_ref, ln_g_ref, ln_b_ref, gsum_ref, gavg_ref, tri_ref, cm_ref,
                  out_ref,
                  h_scr, p_scr, cat_scr, state_scr, qx_scr, z_scr, ar_scr, vp_scr, bk_scr,
                  dec_scr, y_scr, bonus_scr, sgate_scr):
    t_idx = pl.program_id(1)
    tile = x_ref.shape[0]
    top = SUBLANES

    @pl.when(t_idx == 0)
    def _():
        state_scr[...] = jnp.zeros_like(state_scr)
        p_scr[0:top, :] = jnp.zeros((top, N_COLS), F32)

    shift = mod_ref[0:1, :]
    scale = mod_ref[1:2, :]
    gate = mod_ref[2:3, :]

    h_scr[...] = _bf(x_ref[...] * (1.0 + scale) + shift)

    def inproj_steps(row_blocks, col0, col1):
        for row0, nrows in row_blocks:
            hb = h_scr[row0:row0 + nrows, :]
            for c in range(col0, col1, MXU_COLS):
                ce = min(c + MXU_COLS, col1)
                p_scr[top + row0:top + row0 + nrows, c:ce] = _mm(hb, win_ref[:, c:ce])
                yield

    gsum = gsum_ref[...]
    gavg = gavg_ref[...]

    row_a = lax.broadcasted_iota(jnp.int32, (CHUNK_A, CHUNK_A), 0)
    col_a = lax.broadcasted_iota(jnp.int32, (CHUNK_A, CHUNK_A), 1)
    causal = col_a <= row_a
    lane_a = lax.broadcasted_iota(jnp.int32, (CHUNK_A, LANES), 1)
    first_a = lane_a < HEAD

    def gmlp_steps(ci):
        r0 = _aligned(top + ci * CHUNK_A, SUBLANES)
        rows = pl.ds(r0, CHUNK_A)
        v = _gelu_tanh(p_scr[rows, D_A:2 * D_A])
        mean = _group_sum(v, gavg)
        yield
        vc = v - mean
        var = _group_sum(vc * vc, gavg)
        yield
        v = vc * lax.rsqrt(var + LN_EPS) * lnv_g_ref[...] + lnv_b_ref[...]
        mixed = []
        for j in range(D_A // LANES):
            vj = _bf(v[:, LANES * j:LANES * (j + 1)])
            w_even = _bf(jnp.where(causal, ws_ref[2 * j], 0.0))
            w_odd = _bf(jnp.where(causal, ws_ref[2 * j + 1], 0.0))
            mixed.append(jnp.where(first_a, _mm(w_even, vj), _mm(w_odd, vj)))
        yield
        mix = jnp.concatenate(mixed, axis=1) + bs_ref[...]
        u = _gelu_tanh(p_scr[rows, 0:D_A])
        out_a = u * mix * _silu(p_scr[rows, 2 * D_A:3 * D_A])
        o0 = _aligned(ci * CHUNK_A, CHUNK_A)
        cat_scr[pl.ds(o0, CHUNK_A), 0:D_A] = _bf(out_a)

    consts = tuple(cm_ref[i] for i in range(7))
    row_b = lax.broadcasted_iota(jnp.int32, (PREP_ROWS, N_COLS_B), 0)
    lane_b = lax.broadcasted_iota(jnp.int32, (PREP_ROWS, LANES), 1)
    lane_s = lax.broadcasted_iota(jnp.int32, (SCAN_L, LANES), 1)
    first_f = lane_s < HEAD
    first_b = jnp.where(first_f, 1.0, 0.0).astype(BF16)
    second_b = jnp.where(first_f, 0.0, 1.0).astype(BF16)
    bd_mask = (first_b, second_b)
    cb = N_COLS_A

    def stack2(zz):
        return jnp.concatenate([jnp.where(first_f, zz, 0.0), jnp.where(first_f, 0.0, zz)], axis=0)

    def stack2b(zz):
        return jnp.concatenate([zz * first_b, zz * second_b], axis=0)

    def prepare_steps(bi):
        r0 = top + bi * PREP_ROWS
        cur = p_scr[r0:r0 + PREP_ROWS, cb:]
        before = p_scr[r0 - SUBLANES:r0, cb:]
        prev = jnp.where(row_b == 0, before[SUBLANES - 1:SUBLANES, :], pltpu.roll(cur, 1, axis=0))
        ps = cur + mu_ref[...] * (prev - cur)
        r = ps[:, 0:D_B]
        k = ps[:, D_B:2 * D_B]
        v = ps[:, 2 * D_B:3 * D_B]
        gate_b = ps[:, 3 * D_B:4 * D_B]
        low = ps[:, 4 * D_B:]
        low = jnp.where(lane_b < LOW_RANK, jnp.tanh(low), low)
        up = _mm(_bf(low), wlr_ref[...])
        yield
        lw = LOG_DECAY_SCALE * _sigmoid(w0_ref[...] + up[:, 0:D_B])
        ah = _sigmoid(a0_ref[...] + up[:, D_B:])
        g = _mm(tri_ref[...], jnp.concatenate(_split_bf16(lw, 3), axis=1))
        g = g[:, 0:D_B] + g[:, D_B:2 * D_B] + g[:, 2 * D_B:]
        kk = k * kk_ref[...]
        kk = kk * lax.rsqrt(_group_sum(kk * kk, gsum) + 1e-12)
        yield
        k = k * (1.0 + (ah - 1.0) * ka_ref[...])
        bb = kk * ah

        o0 = bi * PREP_ROWS
        bonus_scr[o0:o0 + PREP_ROWS, :] = _group_sum(r * k * rk_ref[...], gsum) * v
        sgate_scr[o0:o0 + PREP_ROWS, :] = _silu(gate_b)
        yield

        problems = [(c, gi) for c in range(PREP_CHUNKS) for gi in range(N_GROUPS)]
        cut = lambda zz: [zz[SCAN_L * c:SCAN_L * (c + 1), MXU_COLS * gi:MXU_COLS * (gi + 1)]
                          for c, gi in problems]
        for c in range(PREP_CHUNKS):
            last = g[SCAN_L * (c + 1) - 1:SCAN_L * (c + 1), :]
            dec_scr[bi * PREP_CHUNKS + c] = jnp.broadcast_to(jnp.exp(last), (SUBLANES, D_B))

        q, xr, z, a_rb, a_rk, vb, bbar, kbar = yield from _scan_prepare_steps(
            cut(r), cut(k), cut(v), cut(kk), cut(bb), cut(g), cut(lw), consts, bd_mask)
        for i, (c, gi) in enumerate(problems):
            for pp in range(MXU_COLS // LANES):
                slot = (bi * PREP_CHUNKS + c) * N_PAIRS + gi * (MXU_COLS // LANES) + pp
                sl = slice(LANES * pp, LANES * (pp + 1))
                qx_scr[slot] = jnp.concatenate([stack2b(q[i][:, sl]), stack2b(xr[i][:, sl])], axis=0)
                z_scr[slot] = stack2(z[i][:, sl])
                ar_scr[slot] = jnp.concatenate([stack2b(a_rb[i][:, sl]), stack2b(a_rk[i][:, sl])],
                                               axis=1)
                vp_scr[slot] = stack2b(vb[i][:, sl])
                bk_scr[slot] = jnp.concatenate([stack2b(bbar[i][:, sl]), stack2b(kbar[i][:, sl])],
                                               axis=0)

    n_prep = N_CHUNKS // PREP_CHUNKS
    row_block = lambda i: (i * PREP_ROWS, PREP_ROWS)
    _interleave([inproj_steps([row_block(0)], cb, N_COLS)])
    for bi in range(n_prep):
        if bi + 1 < n_prep:
            fill = inproj_steps([row_block(bi + 1)], cb, N_COLS)
        else:
            fill = inproj_steps([row_block(i) for i in range(n_prep)], 0, cb)
        _interleave([prepare_steps(bi), fill])

    def apply_steps(bi):
        states = [state_scr[j] for j in range(N_PAIRS)]
        for c in range(CHUNKS_PER_BLOCK):
            ci = bi * CHUNKS_PER_BLOCK + c
            load = lambda ref: [ref[ci * N_PAIRS + j] for j in range(N_PAIRS)]
            dec = dec_scr[ci]
            decay = [dec[0:1, LANES * j:LANES * (j + 1)] for j in range(N_PAIRS)]
            xs = _lock(lambda a, s: _mm_nt(a, _bf(s)), load(qx_scr), states)
            yield
            uv = _lock(lambda x, zz, w: jnp.concatenate([_bf(x[0:PAIR_ROWS] + zz), w], axis=0),
                       xs, load(z_scr), load(vp_scr))
            states = _lock(lambda s, d, m, b: s * d + _mm_tn(m, b),
                           states, decay, uv, load(bk_scr))
            y2 = _lock(lambda x, a, m: x[PAIR_ROWS:] + _mm(a, m), xs, load(ar_scr), uv)
            yield
            ys = _lock(lambda m: m[0:SCAN_L] + m[SCAN_L:], y2)
            y_scr[pl.ds(_aligned(ci * SCAN_L, SCAN_L), SCAN_L), :] = jnp.concatenate(ys, axis=1)
        for j in range(N_PAIRS):
            state_scr[j] = states[j]

    def post_steps(bi):
        rows = pl.ds(_aligned(bi * CHUNK_A, CHUNK_A), CHUNK_A)
        y = y_scr[rows, :]
        mean = _group_sum(y, gavg)
        yield
        yc = y - mean
        var = _group_sum(yc * yc, gavg)
        yield
        y = yc * lax.rsqrt(var + GN_EPS) * gn_g_ref[...] + gn_b_ref[...]
        cat_scr[rows, D_A:] = _bf((y + bonus_scr[rows, :]) * sgate_scr[rows, :])

    n_blocks = tile // CHUNK_A
    _interleave([apply_steps(0), gmlp_steps(0)])

    def mix_body(bi, carry):
        _interleave([apply_steps(bi), post_steps(bi - 1), gmlp_steps(bi)])
        return carry

    lax.fori_loop(1, n_blocks, mix_body, 0)
    _interleave([post_steps(n_blocks - 1)])

    p_scr[0:top, :] = p_scr[tile:tile + top, :]

    for b0 in range(0, tile, CHUNK_A):
        rows = slice(b0, b0 + CHUNK_A)
        o = _mm(cat_scr[rows, :], wout_ref[...])
        res = ALPHA * x_ref[rows, :] + gate * o
        mean = jnp.mean(res, axis=-1, keepdims=True)
        rc = res - mean
        var = jnp.mean(rc * rc, axis=-1, keepdims=True)
        out_ref[rows, :] = rc * lax.rsqrt(var + LN_EPS) * ln_g_ref[...] + ln_b_ref[...]


def _wide_masks():
    t = jnp.arange(SCAN_L)[:, None]
    s = (jnp.arange(MXU_COLS) % SCAN_L)[None, :]
    same = lambda n: (t // n) == (s // n)
    masks = [t == s, s < t, s <= t, same(8),
             same(16) & ~same(8), same(32) & ~same(16), same(64) & ~same(32)]
    return jnp.stack([m.astype(F32) for m in masks])


def _const_spec(shape):
    zeros = (0,) * len(shape)
    return pl.BlockSpec(shape, lambda b, t: zeros)


def kernel(x, c, w_ada, b_ada, w_in, mu_b, ln_v_g, ln_v_b, w_spatial, b_spatial, w0, w_up, a0,
           a_up, k_k, k_a, r_k, gn_g, gn_b, w_out, ln_g, ln_b):
    batch, seq, d_model = x.shape
    assert d_model == D_MODEL and w_in.shape == (D_MODEL, N_COLS)
    assert seq % TILE_T == 0 and TILE_T % CHUNK_A == 0 and N_CHUNKS % PREP_CHUNKS == 0

    mod = pl.pallas_call(
        _ada_kernel,
        out_shape=jax.ShapeDtypeStruct((batch, 3 * d_model), F32),
        name="adaln_mod",
    )(c, w_ada, b_ada.reshape(1, -1))
    mod = mod.reshape(batch, 3, d_model)

    row = lambda z: z.reshape(1, -1).astype(F32)
    lane_id = jnp.arange(LANES)
    same_head = (lane_id[:, None] // HEAD) == (lane_id[None, :] // HEAD)
    gsum = same_head.astype(BF16)
    gavg = (same_head.astype(F32) / HEAD).astype(BF16)
    tok = jnp.arange(PREP_ROWS)
    tri = ((tok[None, :] <= tok[:, None])
           & (tok[None, :] // SCAN_L == tok[:, None] // SCAN_L)).astype(BF16)
    zeros_lr = jnp.zeros((LOW_RANK, D_B), F32)
    w_lr = jnp.concatenate([jnp.concatenate([w_up, zeros_lr], axis=1),
                            jnp.concatenate([zeros_lr, a_up], axis=1)], axis=0).astype(BF16)
    bias_a = jnp.repeat(b_spatial.T, HEAD, axis=1)

    operands = [
        (x, pl.BlockSpec((None, TILE_T, d_model), lambda b, t: (b, t, 0))),
        (mod, pl.BlockSpec((None, 3, d_model), lambda b, t: (b, 0, 0))),
        (w_in.astype(BF16), _const_spec((d_model, N_COLS))),
        (row(mu_b), _const_spec((1, N_COLS_B))),
        (row(ln_v_g), _const_spec((1, D_A))),
        (row(ln_v_b), _const_spec((1, D_A))),
        (w_spatial, _const_spec(w_spatial.shape)),
        (bias_a, _const_spec((CHUNK_A, D_A))),
        (w_lr, _const_spec((2 * LOW_RANK, 2 * D_B))),
        (row(w0), _const_spec((1, D_B))),
        (row(a0), _const_spec((1, D_B))),
        (row(k_k), _const_spec((1, D_B))),
        (row(k_a), _const_spec((1, D_B))),
        (row(r_k), _const_spec((1, D_B))),
        (row(gn_g), _const_spec((1, D_B))),
        (row(gn_b), _const_spec((1, D_B))),
        (w_out.astype(BF16), _const_spec((d_model, d_model))),
        (row(ln_g), _const_spec((1, d_model))),
        (row(ln_b), _const_spec((1, d_model))),
        (gsum, _const_spec((LANES, LANES))),
        (gavg, _const_spec((LANES, LANES))),
        (tri, _const_spec((PREP_ROWS, PREP_ROWS))),
        (_wide_masks(), _const_spec((7, SCAN_L, MXU_COLS))),
    ]
    n_slots = N_CHUNKS * N_PAIRS
    return pl.pallas_call(
        _layer_kernel,
        grid=(batch, seq // TILE_T),
        in_specs=[spec for _, spec in operands],
        out_specs=pl.BlockSpec((None, TILE_T, d_model), lambda b, t: (b, t, 0)),
        out_shape=jax.ShapeDtypeStruct(x.shape, x.dtype),
        scratch_shapes=[
            pltpu.VMEM((TILE_T, d_model), BF16),
            pltpu.VMEM((TILE_T + SUBLANES, N_COLS), F32),
            pltpu.VMEM((TILE_T, d_model), BF16),
            pltpu.VMEM((N_PAIRS, LANES, LANES), F32),
            pltpu.VMEM((n_slots, 2 * PAIR_ROWS, LANES), BF16),
            pltpu.VMEM((n_slots, PAIR_ROWS, LANES), F32),
            pltpu.VMEM((n_slots, PAIR_ROWS, 2 * PAIR_ROWS), BF16),
            pltpu.VMEM((n_slots, PAIR_ROWS, LANES), BF16),
            pltpu.VMEM((n_slots, 2 * PAIR_ROWS, LANES), BF16),
            pltpu.VMEM((N_CHUNKS, SUBLANES, D_B), F32),
            pltpu.VMEM((TILE_T, D_B), F32),
            pltpu.VMEM((TILE_T, D_B), F32),
            pltpu.VMEM((TILE_T, D_B), F32),
        ],
        compiler_params=pltpu.CompilerParams(
            dimension_semantics=("arbitrary", "arbitrary"),
            vmem_limit_bytes=VMEM_LIMIT_BYTES),
        name="hybrid_layer",
    )(*[arr for arr, _ in operands])
```

```python
import math

import jax
import jax.numpy as jnp
from jax import lax
from jax.experimental import pallas as pl
from jax.experimental.pallas import tpu as pltpu

F32 = jnp.float32
BF16 = jnp.bfloat16

D_MODEL = 1024
D_A = 512
D_B = 512
HEAD = 64
CHUNK_A = 128
LOW_RANK = 64
N_COLS_A = 3 * D_A
N_COLS_B = 4 * D_B + 2 * LOW_RANK
N_COLS = N_COLS_A + N_COLS_B
LN_EPS = 1e-5
GN_EPS = 64e-5
ALPHA = 2.0 ** 0.25

LANES = 128
SUBLANES = 8
MXU_COLS = 256
N_PAIRS = D_B // LANES
N_GROUPS = D_B // MXU_COLS
HEADS_PER_GROUP = MXU_COLS // HEAD
SCAN_L = 64
PAIR_ROWS = 2 * SCAN_L
TILE_T = 512
N_CHUNKS = TILE_T // SCAN_L
PREP_CHUNKS = 4
PREP_ROWS = PREP_CHUNKS * SCAN_L
CHUNKS_PER_BLOCK = CHUNK_A // SCAN_L
VMEM_LIMIT_BYTES = 56 * 1024 * 1024
LOG_DECAY_SCALE = -math.exp(-0.5)


def _bf(x):
    return x.astype(BF16)


def _mm(a, b):
    return jnp.dot(a, b, preferred_element_type=F32)


def _mm_nt(a, b):
    return lax.dot_general(a, b, (((1,), (1,)), ((), ())), preferred_element_type=F32)


def _mm_tn(a, b):
    return lax.dot_general(a, b, (((0,), (0,)), ((), ())), preferred_element_type=F32)


def _aligned(index, multiple):
    return index if isinstance(index, int) else pl.multiple_of(index, multiple)


def _interleave(step_generators):
    pending = list(step_generators)
    while pending:
        for gen in list(pending):
            try:
                next(gen)
            except StopIteration:
                pending.remove(gen)


def _lock(fn, *lists):
    return [fn(*args) for args in zip(*lists)]


def _split_bf16(x, terms):
    parts = []
    rem = x
    for _ in range(terms):
        hi = _bf(rem)
        parts.append(hi)
        rem = rem - hi.astype(F32)
    return parts


def _dot_split(x, w_bf16, terms=2):
    stacked = jnp.concatenate(_split_bf16(x, terms), axis=0)
    res = _mm(stacked, w_bf16)
    rows = x.shape[0]
    out = res[0:rows]
    for i in range(1, terms):
        out = out + res[i * rows:(i + 1) * rows]
    return out


def _group_sum(x, gmat, terms):
    rows = x.shape[0]
    n = x.shape[1] // LANES
    xs = jnp.concatenate([x[:, LANES * j:LANES * (j + 1)] for j in range(n)], axis=0)
    s = _dot_split(xs, gmat, terms=terms)
    return jnp.concatenate([s[rows * j:rows * (j + 1)] for j in range(n)], axis=1)


def _sigmoid(x):
    return 1.0 / (1.0 + jnp.exp(-x))


def _silu(x):
    return x * _sigmoid(x)


def _gelu_tanh(x):
    c = math.sqrt(2.0 / math.pi)
    return 0.5 * x * (1.0 + jnp.tanh(c * (x + 0.044715 * (x * x * x))))


def _block_diag(y, head_masks):
    zero = jnp.zeros((y.shape[0], LANES), BF16)
    rows = []
    for h in range(HEADS_PER_GROUP):
        tile = h // 2
        kept = y[:, LANES * tile:LANES * (tile + 1)] * head_masks[h % 2]
        rows.append(jnp.concatenate([kept if t == tile else zero
                                     for t in range(MXU_COLS // LANES)], axis=1))
    return jnp.concatenate(rows, axis=0)


def _unit_lower_inverse_steps(a_abs, eye, m8, off16, off32, off64, bd_mask):
    bd = lambda y: _block_diag(y, bd_mask)
    a8 = _lock(lambda a: a * m8, a_abs)
    a8b = _lock(_bf, a8)
    a2b = _lock(lambda a: _bf(_mm(a, bd(a))), a8b)
    yield
    p1 = _lock(lambda a: eye + a, a8)
    both = _lock(lambda a, p: _mm(jnp.concatenate([a, _bf(p)], axis=0), bd(a)), a2b, p1)
    yield
    a4b = _lock(lambda m: _bf(m[0:SCAN_L]), both)
    p2 = _lock(lambda p, m: p + m[SCAN_L:], p1, both)
    t = _lock(lambda p, a: p + _mm(_bf(p), bd(a)), p2, a4b)
    yield
    for off in (off16, off32, off64):
        tb = _lock(_bf, t)
        to = _lock(lambda x, a: _bf(_mm(x, bd(_bf(a * off)))), tb, a_abs)
        yield
        t = _lock(lambda x, xo, xb: x + _mm(xo, bd(xb)), t, to, tb)
        yield
    return t


def _scan_prepare_steps(r, k, v, kk, bb, g, lw, consts, bd_mask):
    eye, tri_strict, tri_incl, m8, off16, off32, off64 = consts
    bd = lambda y: _block_diag(y, bd_mask)

    def factors(r, k, v, kk, bb, g, lw):
        e_g = jnp.exp(g)
        e_ge = jnp.exp(g - lw)
        e_ng = jnp.exp(-g)
        e_gl = jnp.exp(g[SCAN_L - 1:SCAN_L, :] - g)
        return (_bf(-(kk * e_ge)), _bf(r * e_g), _bf(v), _bf(bb * e_ng), _bf(k * e_ng),
                _bf(bb * e_gl), _bf(k * e_gl))

    xa, xr, vb, bh, kh, bbar, kbar = zip(*_lock(factors, r, k, v, kk, bb, g, lw))
    lhs = _lock(lambda a, b: jnp.concatenate([a, b], axis=0), xa, xr)
    aa_b = _lock(lambda x, y: _mm_nt(x, bd(y)), lhs, bh)
    yield
    aa_k = _lock(lambda x, y: _mm_nt(x, bd(y)), lhs, kh)
    yield
    a_ab = _lock(lambda m: jnp.where(tri_strict != 0.0, m[0:SCAN_L], 0.0), aa_b)
    a_rb = _lock(lambda m: _bf(jnp.where(tri_incl != 0.0, m[SCAN_L:], 0.0)), aa_b)
    a_ak = _lock(lambda m: _bf(jnp.where(tri_strict != 0.0, m[0:SCAN_L], 0.0)), aa_k)
    a_rk = _lock(lambda m: _bf(jnp.where(tri_incl != 0.0, m[SCAN_L:], 0.0)), aa_k)
    t_inv = yield from _unit_lower_inverse_steps(a_ab, eye, m8, off16, off32, off64, bd_mask)
    t_inv = _lock(_bf, t_inv)
    akv = _lock(lambda a, y: _bf(_mm(a, bd(y))), a_ak, vb)
    yield
    q = _lock(lambda t, y: _bf(_mm(t, bd(y))), t_inv, xa)
    yield
    z = _lock(lambda t, y: _mm(t, bd(y)), t_inv, akv)
    yield
    return q, list(xr), z, a_rb, a_rk, list(vb), list(bbar), list(kbar)


def _ada_kernel(c_ref, w_ref, b_ref, o_ref):
    s_hi, s_lo = _split_bf16(_silu(c_ref[...]), 2)
    w_hi, w_lo = _split_bf16(w_ref[...], 2)
    acc = _mm(s_hi, w_hi) + _mm(s_hi, w_lo) + _mm(s_lo, w_hi)
    o_ref[...] = acc + b_ref[...]


def _layer_kernel(x_ref, mod_ref, win_ref, mu_ref, lnv_g_ref, lnv_b_ref, ws_ref, bs_ref,
                  wlr_ref, w0_ref, a0_ref, kk_ref, ka_ref, rk_ref, gn_g_ref, gn_b_ref,
                  wout_ref, ln_g_ref, ln_b_ref, gsum_ref, gavg_ref, tri_ref, cm_ref,
                  out_ref,
                  h_scr, p_scr, cat_scr, state_scr, qx_scr, z_scr, ar_scr, vp_scr, bk_scr,
                  dec_scr, y_scr, bonus_scr, sgate_scr):
    t_idx = pl.program_id(1)
    tile = x_ref.shape[0]
    top = SUBLANES

    @pl.when(t_idx == 0)
    def _():
        state_scr[...] = jnp.zeros_like(state_scr)
        p_scr[0:top, :] = jnp.zeros((top, N_COLS), F32)

    shift = mod_ref[0:1, :]
    scale = mod_ref[1:2, :]
    gate = mod_ref[2:3, :]

    h_scr[...] = _bf(x_ref[...] * (1.0 + scale) + shift)

    def inproj_steps(row_blocks, col0, col1):
        for row0, nrows in row_blocks:
            hb = h_scr[row0:row0 + nrows, :]
            for c in range(col0, col1, MXU_COLS):
                ce = min(c + MXU_COLS, col1)
                p_scr[top + row0:top + row0 + nrows, c:ce] = _mm(hb, win_ref[:, c:ce])
                yield

    gsum = gsum_ref[...]
    gavg = gavg_ref[...]

    row_a = lax.broadcasted_iota(jnp.int32, (CHUNK_A, CHUNK_A), 0)
    col_a = lax.broadcasted_iota(jnp.int32, (CHUNK_A, CHUNK_A), 1)
    causal = col_a <= row_a
    lane_a = lax.broadcasted_iota(jnp.int32, (CHUNK_A, LANES), 1)
    first_a = lane_a < HEAD

    def gmlp_steps(ci):
        r0 = _aligned(top + ci * CHUNK_A, SUBLANES)
        rows = pl.ds(r0, CHUNK_A)
        v = _gelu_tanh(p_scr[rows, D_A:2 * D_A])
        mean = _group_sum(v, gavg, 2)
        yield
        vc = v - mean
        var = _group_sum(vc * vc, gavg, 1)
        yield
        v = vc * lax.rsqrt(var + LN_EPS) * lnv_g_ref[...] + lnv_b_ref[...]
        mixed = []
        for j in range(D_A // LANES):
            vj = _bf(v[:, LANES * j:LANES * (j + 1)])
            w_even = _bf(jnp.where(causal, ws_ref[2 * j], 0.0))
            w_odd = _bf(jnp.where(causal, ws_ref[2 * j + 1], 0.0))
            mixed.append(jnp.where(first_a, _mm(w_even, vj), _mm(w_odd, vj)))
        yield
        mix = jnp.concatenate(mixed, axis=1) + bs_ref[...]
        u = _gelu_tanh(p_scr[rows, 0:D_A])
        out_a = u * mix * _silu(p_scr[rows, 2 * D_A:3 * D_A])
        o0 = _aligned(ci * CHUNK_A, CHUNK_A)
        cat_scr[pl.ds(o0, CHUNK_A), 0:D_A] = _bf(out_a)

    consts = tuple(cm_ref[i] for i in range(7))
    row_b = lax.broadcasted_iota(jnp.int32, (PREP_ROWS, N_COLS_B), 0)
    lane_b = lax.broadcasted_iota(jnp.int32, (PREP_ROWS, LANES), 1)
    lane_s = lax.broadcasted_iota(jnp.int32, (SCAN_L, LANES), 1)
    first_f = lane_s < HEAD
    first_b = jnp.where(first_f, 1.0, 0.0).astype(BF16)
    second_b = jnp.where(first_f, 0.0, 1.0).astype(BF16)
    bd_mask = (first_b, second_b)
    cb = N_COLS_A

    def stack2(zz):
        return jnp.concatenate([jnp.where(first_f, zz, 0.0), jnp.where(first_f, 0.0, zz)], axis=0)

    def stack2b(zz):
        return jnp.concatenate([zz * first_b, zz * second_b], axis=0)

    def prepare_steps(bi):
        r0 = top + bi * PREP_ROWS
        cur = p_scr[r0:r0 + PREP_ROWS, cb:]
        before = p_scr[r0 - SUBLANES:r0, cb:]
        prev = jnp.where(row_b == 0, before[SUBLANES - 1:SUBLANES, :], pltpu.roll(cur, 1, axis=0))
        ps = cur + mu_ref[...] * (prev - cur)
        r = ps[:, 0:D_B]
        k = ps[:, D_B:2 * D_B]
        v = ps[:, 2 * D_B:3 * D_B]
        gate_b = ps[:, 3 * D_B:4 * D_B]
        low = ps[:, 4 * D_B:]
        low = jnp.where(lane_b < LOW_RANK, jnp.tanh(low), low)
        up = _mm(_bf(low), wlr_ref[...])
        yield
        lw = LOG_DECAY_SCALE * _sigmoid(w0_ref[...] + up[:, 0:D_B])
        ah = _sigmoid(a0_ref[...] + up[:, D_B:])
        g = _mm(tri_ref[...], jnp.concatenate(_split_bf16(lw, 2), axis=1))
        g = g[:, 0:D_B] + g[:, D_B:]
        kk = k * kk_ref[...]
        kk = kk * lax.rsqrt(_group_sum(kk * kk, gsum, 1) + 1e-12)
        yield
        k = k * (1.0 + (ah - 1.0) * ka_ref[...])
        bb = kk * ah

        o0 = bi * PREP_ROWS
        bonus_scr[o0:o0 + PREP_ROWS, :] = _group_sum(r * k * rk_ref[...], gsum, 1) * v
        sgate_scr[o0:o0 + PREP_ROWS, :] = _silu(gate_b)
        yield

        problems = [(c, gi) for c in range(PREP_CHUNKS) for gi in range(N_GROUPS)]
        cut = lambda zz: [zz[SCAN_L * c:SCAN_L * (c + 1), MXU_COLS * gi:MXU_COLS * (gi + 1)]
                          for c, gi in problems]
        for c in range(PREP_CHUNKS):
            last = g[SCAN_L * (c + 1) - 1:SCAN_L * (c + 1), :]
            dec_scr[bi * PREP_CHUNKS + c] = jnp.broadcast_to(jnp.exp(last), (SUBLANES, D_B))

        q, xr, z, a_rb, a_rk, vb, bbar, kbar = yield from _scan_prepare_steps(
            cut(r), cut(k), cut(v), cut(kk), cut(bb), cut(g), cut(lw), consts, bd_mask)
        for i, (c, gi) in enumerate(problems):
            for pp in range(MXU_COLS // LANES):
                slot = (bi * PREP_CHUNKS + c) * N_PAIRS + gi * (MXU_COLS // LANES) + pp
                sl = slice(LANES * pp, LANES * (pp + 1))
                qx_scr[slot] = jnp.concatenate([stack2b(q[i][:, sl]), stack2b(xr[i][:, sl])], axis=0)
                z_scr[slot] = stack2(z[i][:, sl])
                ar_scr[slot] = jnp.concatenate([stack2b(a_rb[i][:, sl]), stack2b(a_rk[i][:, sl])],
                                               axis=1)
                vp_scr[slot] = stack2b(vb[i][:, sl])
                bk_scr[slot] = jnp.concatenate([stack2b(bbar[i][:, sl]), stack2b(kbar[i][:, sl])],
                                               axis=0)

    n_prep = N_CHUNKS // PREP_CHUNKS
    row_block = lambda i: (i * PREP_ROWS, PREP_ROWS)
    _interleave([inproj_steps([row_block(0)], cb, N_COLS)])
    for bi in range(n_prep):
        if bi + 1 < n_prep:
            fill = inproj_steps([row_block(bi + 1)], cb, N_COLS)
        else:
            fill = inproj_steps([row_block(i) for i in range(n_prep)], 0, cb)
        _interleave([prepare_steps(bi), fill])

    def apply_steps(bi):
        states = [state_scr[j] for j in range(N_PAIRS)]
        for c in range(CHUNKS_PER_BLOCK):
            ci = bi * CHUNKS_PER_BLOCK + c
            load = lambda ref: [ref[ci * N_PAIRS + j] for j in range(N_PAIRS)]
            dec = dec_scr[ci]
            decay = [dec[0:1, LANES * j:LANES * (j + 1)] for j in range(N_PAIRS)]
            xs = _lock(lambda a, s: _mm_nt(a, _bf(s)), load(qx_scr), states)
            yield
            uv = _lock(lambda x, zz, w: jnp.concatenate([_bf(x[0:PAIR_ROWS] + zz), w], axis=0),
                       xs, load(z_scr), load(vp_scr))
            states = _lock(lambda s, d, m, b: s * d + _mm_tn(m, b),
                           states, decay, uv, load(bk_scr))
            y2 = _lock(lambda x, a, m: x[PAIR_ROWS:] + _mm(a, m), xs, load(ar_scr), uv)
            yield
            ys = _lock(lambda m: m[0:SCAN_L] + m[SCAN_L:], y2)
            y_scr[pl.ds(_aligned(ci * SCAN_L, SCAN_L), SCAN_L), :] = jnp.concatenate(ys, axis=1)
        for j in range(N_PAIRS):
            state_scr[j] = states[j]

    def post_steps(bi):
        rows = pl.ds(_aligned(bi * CHUNK_A, CHUNK_A), CHUNK_A)
        y = y_scr[rows, :]
        mean = _group_sum(y, gavg, 2)
        yield
        yc = y - mean
        var = _group_sum(yc * yc, gavg, 1)
        yield
        y = yc * lax.rsqrt(var + GN_EPS) * gn_g_ref[...] + gn_b_ref[...]
        cat_scr[rows, D_A:] = _bf((y + bonus_scr[rows, :]) * sgate_scr[rows, :])

    n_blocks = tile // CHUNK_A
    _interleave([apply_steps(0), gmlp_steps(0)])

    def mix_body(bi, carry):
        _interleave([apply_steps(bi), post_steps(bi - 1), gmlp_steps(bi)])
        return carry

    lax.fori_loop(1, n_blocks, mix_body, 0)
    _interleave([post_steps(n_blocks - 1)])

    p_scr[0:top, :] = p_scr[tile:tile + top, :]

    for b0 in range(0, tile, CHUNK_A):
        rows = slice(b0, b0 + CHUNK_A)
        o = _mm(cat_scr[rows, :], wout_ref[...])
        res = ALPHA * x_ref[rows, :] + gate * o
        mean = jnp.mean(res, axis=-1, keepdims=True)
        rc = res - mean
        var = jnp.mean(rc * rc, axis=-1, keepdims=True)
        out_ref[rows, :] = rc * lax.rsqrt(var + LN_EPS) * ln_g_ref[...] + ln_b_ref[...]


def _wide_masks():
    t = jnp.arange(SCAN_L)[:, None]
    s = (jnp.arange(MXU_COLS) % SCAN_L)[None, :]
    same = lambda n: (t // n) == (s // n)
    masks = [t == s, s < t, s <= t, same(8),
             same(16) & ~same(8), same(32) & ~same(16), same(64) & ~same(32)]
    return jnp.stack([m.astype(F32) for m in masks])


def _const_spec(shape):
    zeros = (0,) * len(shape)
    return pl.BlockSpec(shape, lambda b, t: zeros)


def kernel(x, c, w_ada, b_ada, w_in, mu_b, ln_v_g, ln_v_b, w_spatial, b_spatial, w0, w_up, a0,
           a_up, k_k, k_a, r_k, gn_g, gn_b, w_out, ln_g, ln_b):
    batch, seq, d_model = x.shape
    assert d_model == D_MODEL and w_in.shape == (D_MODEL, N_COLS)
    assert seq % TILE_T == 0 and TILE_T % CHUNK_A == 0 and N_CHUNKS % PREP_CHUNKS == 0

    mod = pl.pallas_call(
        _ada_kernel,
        out_shape=jax.ShapeDtypeStruct((batch, 3 * d_model), F32),
        name="adaln_mod",
    )(c, w_ada, b_ada.reshape(1, -1))
    mod = mod.reshape(batch, 3, d_model)

    row = lambda z: z.reshape(1, -1).astype(F32)
    lane_id = jnp.arange(LANES)
    same_head = (lane_id[:, None] // HEAD) == (lane_id[None, :] // HEAD)
    gsum = same_head.astype(BF16)
    gavg = (same_head.astype(F32) / HEAD).astype(BF16)
    tok = jnp.arange(PREP_ROWS)
    tri = ((tok[None, :] <= tok[:, None])
           & (tok[None, :] // SCAN_L == tok[:, None] // SCAN_L)).astype(BF16)
    zeros_lr = jnp.zeros((LOW_RANK, D_B), F32)
    w_lr = jnp.concatenate([jnp.concatenate([w_up, zeros_lr], axis=1),
                            jnp.concatenate([zeros_lr, a_up], axis=1)], axis=0).astype(BF16)
    bias_a = jnp.repeat(b_spatial.T, HEAD, axis=1)

    operands = [
        (x, pl.BlockSpec((None, TILE_T, d_model), lambda b, t: (b, t, 0))),
        (mod, pl.BlockSpec((None, 3, d_model), lambda b, t: (b, 0, 0))),
        (w_in.astype(BF16), _const_spec((d_model, N_COLS))),
        (row(mu_b), _const_spec((1, N_COLS_B))),
        (row(ln_v_g), _const_spec((1, D_A))),
        (row(ln_v_b), _const_spec((1, D_A))),
        (w_spatial, _const_spec(w_spatial.shape)),
        (bias_a, _const_spec((CHUNK_A, D_A))),
        (w_lr, _const_spec((2 * LOW_RANK, 2 * D_B))),
        (row(w0), _const_spec((1, D_B))),
        (row(a0), _const_spec((1, D_B))),
        (row(k_k), _const_spec((1, D_B))),
        (row(k_a), _const_spec((1, D_B))),
        (row(r_k), _const_spec((1, D_B))),
        (row(gn_g), _const_spec((1, D_B))),
        (row(gn_b), _const_spec((1, D_B))),
        (w_out.astype(BF16), _const_spec((d_model, d_model))),
        (row(ln_g), _const_spec((1, d_model))),
        (row(ln_b), _const_spec((1, d_model))),
        (gsum, _const_spec((LANES, LANES))),
        (gavg, _const_spec((LANES, LANES))),
        (tri, _const_spec((PREP_ROWS, PREP_ROWS))),
        (_wide_masks(), _const_spec((7, SCAN_L, MXU_COLS))),
    ]
    n_slots = N_CHUNKS * N_PAIRS
    return pl.pallas_call(
        _layer_kernel,
        grid=(batch, seq // TILE_T),
        in_specs=[spec for _, spec in operands],
        out_specs=pl.BlockSpec((None, TILE_T, d_model), lambda b, t: (b, t, 0)),
        out_shape=jax.ShapeDtypeStruct(x.shape, x.dtype),
        scratch_shapes=[
            pltpu.VMEM((TILE_T, d_model), BF16),
            pltpu.VMEM((TILE_T + SUBLANES, N_COLS), F32),
            pltpu.VMEM((TILE_T, d_model), BF16),
            pltpu.VMEM((N_PAIRS, LANES, LANES), F32),
            pltpu.VMEM((n_slots, 2 * PAIR_ROWS, LANES), BF16),
            pltpu.VMEM((n_slots, PAIR_ROWS, LANES), F32),
            pltpu.VMEM((n_slots, PAIR_ROWS, 2 * PAIR_ROWS), BF16),
            pltpu.VMEM((n_slots, PAIR_ROWS, LANES), BF16),
            pltpu.VMEM((n_slots, 2 * PAIR_ROWS, LANES), BF16),
            pltpu.VMEM((N_CHUNKS, SUBLANES, D_B), F32),
            pltpu.VMEM((TILE_T, D_B), F32),
            pltpu.VMEM((TILE_T, D_B), F32),
            pltpu.VMEM((TILE_T, D_B), F32),
        ],
        compiler_params=pltpu.CompilerParams(
            dimension_semantics=("arbitrary", "arbitrary"),
            vmem_limit_bytes=VMEM_LIMIT_BYTES),
        name="hybrid_layer",
    )(*[arr for arr, _ in operands])
```

```python
import math

import jax
import jax.numpy as jnp
from jax import lax
from jax.experimental import pallas as pl
from jax.experimental.pallas import tpu as pltpu

F32 = jnp.float32
BF16 = jnp.bfloat16

D_MODEL = 1024
D_A = 512
D_B = 512
HEAD = 64
CHUNK_A = 128
LOW_RANK = 64
N_COLS_A = 3 * D_A
N_COLS_B = 4 * D_B + 2 * LOW_RANK
N_COLS = N_COLS_A + N_COLS_B
LN_EPS = 1e-5
GN_EPS = 64e-5
ALPHA = 2.0 ** 0.25

LANES = 128
SUBLANES = 8
MXU_COLS = 256
N_PAIRS = D_B // LANES
N_GROUPS = D_B // MXU_COLS
HEADS_PER_GROUP = MXU_COLS // HEAD
SCAN_L = 64
PAIR_ROWS = 2 * SCAN_L
TILE_T = 512
N_CHUNKS = TILE_T // SCAN_L
PREP_CHUNKS = 4
PREP_ROWS = PREP_CHUNKS * SCAN_L
CHUNKS_PER_BLOCK = CHUNK_A // SCAN_L
VMEM_LIMIT_BYTES = 56 * 1024 * 1024
LOG_DECAY_SCALE = -math.exp(-0.5)


def _bf(x):
    return x.astype(BF16)


def _mm(a, b):
    return jnp.dot(a, b, preferred_element_type=F32)


def _mm_nt(a, b):
    return lax.dot_general(a, b, (((1,), (1,)), ((), ())), preferred_element_type=F32)


def _mm_tn(a, b):
    return lax.dot_general(a, b, (((0,), (0,)), ((), ())), preferred_element_type=F32)


def _aligned(index, multiple):
    return index if isinstance(index, int) else pl.multiple_of(index, multiple)


def _interleave(step_generators):
    pending = list(step_generators)
    while pending:
        for gen in list(pending):
            try:
                next(gen)
            except StopIteration:
                pending.remove(gen)


def _take(gen, count):
    for _ in range(count):
        try:
            next(gen)
        except StopIteration:
            return
        yield


def _lock(fn, *lists):
    return [fn(*args) for args in zip(*lists)]


def _split_bf16(x, terms):
    parts = []
    rem = x
    for _ in range(terms):
        hi = _bf(rem)
        parts.append(hi)
        rem = rem - hi.astype(F32)
    return parts


def _dot_split(x, w_bf16, terms=2):
    stacked = jnp.concatenate(_split_bf16(x, terms), axis=0)
    res = _mm(stacked, w_bf16)
    rows = x.shape[0]
    out = res[0:rows]
    for i in range(1, terms):
        out = out + res[i * rows:(i + 1) * rows]
    return out


def _group_sum(x, gmat, terms):
    rows = x.shape[0]
    n = x.shape[1] // LANES
    xs = jnp.concatenate([x[:, LANES * j:LANES * (j + 1)] for j in range(n)], axis=0)
    s = _dot_split(xs, gmat, terms=terms)
    return jnp.concatenate([s[rows * j:rows * (j + 1)] for j in range(n)], axis=1)


def _sigmoid(x):
    return 1.0 / (1.0 + jnp.exp(-x))


def _silu(x):
    return x * _sigmoid(x)


def _gelu_tanh(x):
    c = math.sqrt(2.0 / math.pi)
    return 0.5 * x * (1.0 + jnp.tanh(c * (x + 0.044715 * (x * x * x))))


def _block_diag(y, head_masks):
    zero = jnp.zeros((y.shape[0], LANES), BF16)
    rows = []
    for h in range(HEADS_PER_GROUP):
        tile = h // 2
        kept = y[:, LANES * tile:LANES * (tile + 1)] * head_masks[h % 2]
        rows.append(jnp.concatenate([kept if t == tile else zero
                                     for t in range(MXU_COLS // LANES)], axis=1))
    return jnp.concatenate(rows, axis=0)


def _unit_lower_inverse_steps(a_abs, eye, m8, off16, off32, off64, bd_mask):
    bd = lambda y: _block_diag(y, bd_mask)
    a8 = _lock(lambda a: a * m8, a_abs)
    a8b = _lock(_bf, a8)
    a2b = _lock(lambda a: _bf(_mm(a, bd(a))), a8b)
    yield
    p1 = _lock(lambda a: eye + a, a8)
    both = _lock(lambda a, p: _mm(jnp.concatenate([a, _bf(p)], axis=0), bd(a)), a2b, p1)
    yield
    a4b = _lock(lambda m: _bf(m[0:SCAN_L]), both)
    p2 = _lock(lambda p, m: p + m[SCAN_L:], p1, both)
    t = _lock(lambda p, a: p + _mm(_bf(p), bd(a)), p2, a4b)
    yield
    for off in (off16, off32, off64):
        tb = _lock(_bf, t)
        to = _lock(lambda x, a: _bf(_mm(x, bd(_bf(a * off)))), tb, a_abs)
        yield
        t = _lock(lambda x, xo, xb: x + _mm(xo, bd(xb)), t, to, tb)
        yield
    return t


def _scan_prepare_steps(r, k, v, kk, bb, g, lw, consts, bd_mask):
    eye, tri_strict, tri_incl, m8, off16, off32, off64 = consts
    bd = lambda y: _block_diag(y, bd_mask)

    def factors(r, k, v, kk, bb, g, lw):
        e_g = jnp.exp(g)
        e_ge = jnp.exp(g - lw)
        e_ng = jnp.exp(-g)
        e_gl = jnp.exp(g[SCAN_L - 1:SCAN_L, :] - g)
        return (_bf(-(kk * e_ge)), _bf(r * e_g), _bf(v), _bf(bb * e_ng), _bf(k * e_ng),
                _bf(bb * e_gl), _bf(k * e_gl))

    xa, xr, vb, bh, kh, bbar, kbar = zip(*_lock(factors, r, k, v, kk, bb, g, lw))
    lhs = _lock(lambda a, b: jnp.concatenate([a, b], axis=0), xa, xr)
    aa_b = _lock(lambda x, y: _mm_nt(x, bd(y)), lhs, bh)
    yield
    aa_k = _lock(lambda x, y: _mm_nt(x, bd(y)), lhs, kh)
    yield
    a_ab = _lock(lambda m: jnp.where(tri_strict != 0.0, m[0:SCAN_L], 0.0), aa_b)
    a_rb = _lock(lambda m: _bf(jnp.where(tri_incl != 0.0, m[SCAN_L:], 0.0)), aa_b)
    a_ak = _lock(lambda m: _bf(jnp.where(tri_strict != 0.0, m[0:SCAN_L], 0.0)), aa_k)
    a_rk = _lock(lambda m: _bf(jnp.where(tri_incl != 0.0, m[SCAN_L:], 0.0)), aa_k)
    t_inv = yield from _unit_lower_inverse_steps(a_ab, eye, m8, off16, off32, off64, bd_mask)
    t_inv = _lock(_bf, t_inv)
    akv = _lock(lambda a, y: _bf(_mm(a, bd(y))), a_ak, vb)
    yield
    q = _lock(lambda t, y: _bf(_mm(t, bd(y))), t_inv, xa)
    yield
    z = _lock(lambda t, y: _mm(t, bd(y)), t_inv, akv)
    yield
    return q, list(xr), z, a_rb, a_rk, list(vb), list(bbar), list(kbar)


def _ada_kernel(c_ref, w_ref, b_ref, o_ref):
    s_hi, s_lo = _split_bf16(_silu(c_ref[...]), 2)
    w_hi, w_lo = _split_bf16(w_ref[...], 2)
    acc = _mm(s_hi, w_hi) + _mm(s_hi, w_lo) + _mm(s_lo, w_hi)
    o_ref[...] = acc + b_ref[...]


def _layer_kernel(x_ref, mod_ref, xn_ref, modn_ref,
                  win_ref, mu_ref, lnv_g_ref, lnv_b_ref, ws_ref, bs_ref,
                  wlr_ref, w0_ref, a0_ref, kk_ref, ka_ref, rk_ref, gn_g_ref, gn_b_ref,
                  wout_ref, ln_g_ref, ln_b_ref, gsum_ref, gavg_ref, tri_ref, cm_ref,
                  out_ref,
                  h_scr, p_scr, pnext_scr, cat_scr, state_scr,
                  qx_scr, z_scr, ar_scr, vp_scr, bk_scr,
                  dec_scr, y_scr, bonus_scr, sgate_scr):
    t_idx = pl.program_id(1)
    tile = x_ref.shape[0]
    top = SUBLANES

    @pl.when(t_idx == 0)
    def _():
        state_scr[...] = jnp.zeros_like(state_scr)
        p_scr[0:top, :] = jnp.zeros((top, N_COLS), F32)

    shift = mod_ref[0:1, :]
    scale = mod_ref[1:2, :]
    gate = mod_ref[2:3, :]

    h_scr[...] = _bf(x_ref[...] * (1.0 + scale) + shift)

    @pl.when((pl.program_id(0) == 0) & (t_idx == 0))
    def _():
        pnext_scr[...] = _mm(h_scr[0:PREP_ROWS, :], win_ref[:, N_COLS_A:])

    def inproj_steps(row_blocks, col0, col1):
        for row0, nrows in row_blocks:
            hb = h_scr[row0:row0 + nrows, :]
            for c in range(col0, col1, MXU_COLS):
                ce = min(c + MXU_COLS, col1)
                p_scr[top + row0:top + row0 + nrows, c:ce] = _mm(hb, win_ref[:, c:ce])
                yield

    gsum = gsum_ref[...]
    gavg = gavg_ref[...]

    row_a = lax.broadcasted_iota(jnp.int32, (CHUNK_A, CHUNK_A), 0)
    col_a = lax.broadcasted_iota(jnp.int32, (CHUNK_A, CHUNK_A), 1)
    causal = col_a <= row_a
    lane_a = lax.broadcasted_iota(jnp.int32, (CHUNK_A, LANES), 1)
    first_a = lane_a < HEAD

    def gmlp_steps(ci):
        r0 = _aligned(top + ci * CHUNK_A, SUBLANES)
        rows = pl.ds(r0, CHUNK_A)
        v = _gelu_tanh(p_scr[rows, D_A:2 * D_A])
        mean = _group_sum(v, gavg, 2)
        yield
        vc = v - mean
        var = _group_sum(vc * vc, gavg, 1)
        yield
        v = vc * lax.rsqrt(var + LN_EPS) * lnv_g_ref[...] + lnv_b_ref[...]
        mixed = []
        for j in range(D_A // LANES):
            vj = _bf(v[:, LANES * j:LANES * (j + 1)])
            w_even = _bf(jnp.where(causal, ws_ref[2 * j], 0.0))
            w_odd = _bf(jnp.where(causal, ws_ref[2 * j + 1], 0.0))
            mixed.append(jnp.where(first_a, _mm(w_even, vj), _mm(w_odd, vj)))
        yield
        mix = jnp.concatenate(mixed, axis=1) + bs_ref[...]
        u = _gelu_tanh(p_scr[rows, 0:D_A])
        out_a = u * mix * _silu(p_scr[rows, 2 * D_A:3 * D_A])
        o0 = _aligned(ci * CHUNK_A, CHUNK_A)
        cat_scr[pl.ds(o0, CHUNK_A), 0:D_A] = _bf(out_a)

    consts = tuple(cm_ref[i] for i in range(7))
    row_b = lax.broadcasted_iota(jnp.int32, (PREP_ROWS, N_COLS_B), 0)
    lane_b = lax.broadcasted_iota(jnp.int32, (PREP_ROWS, LANES), 1)
    lane_s = lax.broadcasted_iota(jnp.int32, (SCAN_L, LANES), 1)
    first_f = lane_s < HEAD
    first_b = jnp.where(first_f, 1.0, 0.0).astype(BF16)
    second_b = jnp.where(first_f, 0.0, 1.0).astype(BF16)
    bd_mask = (first_b, second_b)
    cb = N_COLS_A

    def stack2(zz):
        return jnp.concatenate([jnp.where(first_f, zz, 0.0), jnp.where(first_f, 0.0, zz)], axis=0)

    def stack2b(zz):
        return jnp.concatenate([zz * first_b, zz * second_b], axis=0)

    def prepare_steps(bi):
        r0 = top + bi * PREP_ROWS
        if bi == 0:
            cur = pnext_scr[...]
            before = p_scr[0:top, cb:]
        else:
            cur = p_scr[r0:r0 + PREP_ROWS, cb:]
            before = (pnext_scr[PREP_ROWS - SUBLANES:, :] if bi == 1
                      else p_scr[r0 - SUBLANES:r0, cb:])
        prev = jnp.where(row_b == 0, before[SUBLANES - 1:SUBLANES, :], pltpu.roll(cur, 1, axis=0))
        ps = cur + mu_ref[...] * (prev - cur)
        r = ps[:, 0:D_B]
        k = ps[:, D_B:2 * D_B]
        v = ps[:, 2 * D_B:3 * D_B]
        gate_b = ps[:, 3 * D_B:4 * D_B]
        low = ps[:, 4 * D_B:]
        low = jnp.where(lane_b < LOW_RANK, jnp.tanh(low), low)
        up = _mm(_bf(low), wlr_ref[...])
        yield
        lw = LOG_DECAY_SCALE * _sigmoid(w0_ref[...] + up[:, 0:D_B])
        ah = _sigmoid(a0_ref[...] + up[:, D_B:])
        g = _mm(tri_ref[...], jnp.concatenate(_split_bf16(lw, 2), axis=1))
        g = g[:, 0:D_B] + g[:, D_B:]
        kk = k * kk_ref[...]
        kk = kk * lax.rsqrt(_group_sum(kk * kk, gsum, 1) + 1e-12)
        yield
        k = k * (1.0 + (ah - 1.0) * ka_ref[...])
        bb = kk * ah

        o0 = bi * PREP_ROWS
        bonus_scr[o0:o0 + PREP_ROWS, :] = _group_sum(r * k * rk_ref[...], gsum, 1) * v
        sgate_scr[o0:o0 + PREP_ROWS, :] = _silu(gate_b)
        yield

        problems = [(c, gi) for c in range(PREP_CHUNKS) for gi in range(N_GROUPS)]
        cut = lambda zz: [zz[SCAN_L * c:SCAN_L * (c + 1), MXU_COLS * gi:MXU_COLS * (gi + 1)]
                          for c, gi in problems]
        for c in range(PREP_CHUNKS):
            last = g[SCAN_L * (c + 1) - 1:SCAN_L * (c + 1), :]
            dec_scr[bi * PREP_CHUNKS + c] = jnp.broadcast_to(jnp.exp(last), (SUBLANES, D_B))

        q, xr, z, a_rb, a_rk, vb, bbar, kbar = yield from _scan_prepare_steps(
            cut(r), cut(k), cut(v), cut(kk), cut(bb), cut(g), cut(lw), consts, bd_mask)
        for i, (c, gi) in enumerate(problems):
            for pp in range(MXU_COLS // LANES):
                slot = (bi * PREP_CHUNKS + c) * N_PAIRS + gi * (MXU_COLS // LANES) + pp
                sl = slice(LANES * pp, LANES * (pp + 1))
                qx_scr[slot] = jnp.concatenate([stack2b(q[i][:, sl]), stack2b(xr[i][:, sl])], axis=0)
                z_scr[slot] = stack2(z[i][:, sl])
                ar_scr[slot] = jnp.concatenate([stack2b(a_rb[i][:, sl]), stack2b(a_rk[i][:, sl])],
                                               axis=1)
                vp_scr[slot] = stack2b(vb[i][:, sl])
                bk_scr[slot] = jnp.concatenate([stack2b(bbar[i][:, sl]), stack2b(kbar[i][:, sl])],
                                               axis=0)

    n_prep = N_CHUNKS // PREP_CHUNKS
    row_block = lambda i: (i * PREP_ROWS, PREP_ROWS)
    for bi in range(n_prep):
        if bi + 1 < n_prep:
            fill = inproj_steps([row_block(bi + 1)], cb, N_COLS)
        else:
            fill = inproj_steps([row_block(i) for i in range(n_prep)], 0, cb)
        _interleave([prepare_steps(bi), fill])

    def apply_steps(bi):
        states = [state_scr[j] for j in range(N_PAIRS)]
        for c in range(CHUNKS_PER_BLOCK):
            ci = bi * CHUNKS_PER_BLOCK + c
            load = lambda ref: [ref[ci * N_PAIRS + j] for j in range(N_PAIRS)]
            dec = dec_scr[ci]
            decay = [dec[0:1, LANES * j:LANES * (j + 1)] for j in range(N_PAIRS)]
            xs = _lock(lambda a, s: _mm_nt(a, _bf(s)), load(qx_scr), states)
            yield
            uv = _lock(lambda x, zz, w: jnp.concatenate([_bf(x[0:PAIR_ROWS] + zz), w], axis=0),
                       xs, load(z_scr), load(vp_scr))
            states = _lock(lambda s, d, m, b: s * d + _mm_tn(m, b),
                           states, decay, uv, load(bk_scr))
            y2 = _lock(lambda x, a, m: x[PAIR_ROWS:] + _mm(a, m), xs, load(ar_scr), uv)
            yield
            ys = _lock(lambda m: m[0:SCAN_L] + m[SCAN_L:], y2)
            y_scr[pl.ds(_aligned(ci * SCAN_L, SCAN_L), SCAN_L), :] = jnp.concatenate(ys, axis=1)
        for j in range(N_PAIRS):
            state_scr[j] = states[j]

    def post_steps(bi):
        rows = slice(bi * CHUNK_A, (bi + 1) * CHUNK_A)
        y = y_scr[rows, :]
        mean = _group_sum(y, gavg, 2)
        yield
        yc = y - mean
        var = _group_sum(yc * yc, gavg, 1)
        yield
        y = yc * lax.rsqrt(var + GN_EPS) * gn_g_ref[...] + gn_b_ref[...]
        cat_scr[rows, D_A:] = _bf((y + bonus_scr[rows, :]) * sgate_scr[rows, :])
        o = _mm(cat_scr[rows, :], wout_ref[...])
        yield
        res = ALPHA * x_ref[rows, :] + gate * o
        mean = jnp.mean(res, axis=-1, keepdims=True)
        rc = res - mean
        var = jnp.mean(rc * rc, axis=-1, keepdims=True)
        out_ref[rows, :] = rc * lax.rsqrt(var + LN_EPS) * ln_g_ref[...] + ln_b_ref[...]

    def next_inproj_steps():
        hn = _bf(xn_ref[...] * (1.0 + modn_ref[1:2, :]) + modn_ref[0:1, :])
        for c in range(0, N_COLS_B, MXU_COLS):
            ce = min(c + MXU_COLS, N_COLS_B)
            pnext_scr[:, c:ce] = _mm(hn, win_ref[:, cb + c:cb + ce])
            yield

    n_blocks = tile // CHUNK_A
    next_proj = next_inproj_steps()
    next_steps_per_block = -(-len(range(0, N_COLS_B, MXU_COLS)) // n_blocks)
    for bi in range(n_blocks):
        group = [apply_steps(bi)]
        if bi > 0:
            group.append(post_steps(bi - 1))
        group += [gmlp_steps(bi), _take(next_proj, next_steps_per_block)]
        _interleave(group)
    _interleave([post_steps(n_blocks - 1), next_proj])

    p_scr[0:top, :] = p_scr[tile:tile + top, :]


def _wide_masks():
    t = jnp.arange(SCAN_L)[:, None]
    s = (jnp.arange(MXU_COLS) % SCAN_L)[None, :]
    same = lambda n: (t // n) == (s // n)
    masks = [t == s, s < t, s <= t, same(8),
             same(16) & ~same(8), same(32) & ~same(16), same(64) & ~same(32)]
    return jnp.stack([m.astype(F32) for m in masks])


def _const_spec(shape):
    zeros = (0,) * len(shape)
    return pl.BlockSpec(shape, lambda b, t: zeros)


def kernel(x, c, w_ada, b_ada, w_in, mu_b, ln_v_g, ln_v_b, w_spatial, b_spatial, w0, w_up, a0,
           a_up, k_k, k_a, r_k, gn_g, gn_b, w_out, ln_g, ln_b):
    batch, seq, d_model = x.shape
    assert d_model == D_MODEL and w_in.shape == (D_MODEL, N_COLS)
    assert seq % TILE_T == 0 and TILE_T % CHUNK_A == 0 and N_CHUNKS % PREP_CHUNKS == 0

    mod = pl.pallas_call(
        _ada_kernel,
        out_shape=jax.ShapeDtypeStruct((batch, 3 * d_model), F32),
        name="adaln_mod",
    )(c, w_ada, b_ada.reshape(1, -1))
    mod = mod.reshape(batch, 3, d_model)

    row = lambda z: z.reshape(1, -1).astype(F32)
    lane_id = jnp.arange(LANES)
    same_head = (lane_id[:, None] // HEAD) == (lane_id[None, :] // HEAD)
    gsum = same_head.astype(BF16)
    gavg = (same_head.astype(F32) / HEAD).astype(BF16)
    tok = jnp.arange(PREP_ROWS)
    tri = ((tok[None, :] <= tok[:, None])
           & (tok[None, :] // SCAN_L == tok[:, None] // SCAN_L)).astype(BF16)
    zeros_lr = jnp.zeros((LOW_RANK, D_B), F32)
    w_lr = jnp.concatenate([jnp.concatenate([w_up, zeros_lr], axis=1),
                            jnp.concatenate([zeros_lr, a_up], axis=1)], axis=0).astype(BF16)
    bias_a = jnp.repeat(b_spatial.T, HEAD, axis=1)

    tiles = seq // TILE_T

    def next_block(b, t):
        s = jnp.minimum(b * tiles + t + 1, batch * tiles - 1)
        return s // tiles, (s % tiles) * (TILE_T // PREP_ROWS), 0

    operands = [
        (x, pl.BlockSpec((None, TILE_T, d_model), lambda b, t: (b, t, 0))),
        (mod, pl.BlockSpec((None, 3, d_model), lambda b, t: (b, 0, 0))),
        (x, pl.BlockSpec((None, PREP_ROWS, d_model), next_block)),
        (mod, pl.BlockSpec((None, 3, d_model), lambda b, t: (next_block(b, t)[0], 0, 0))),
        (w_in.astype(BF16), _const_spec((d_model, N_COLS))),
        (row(mu_b), _const_spec((1, N_COLS_B))),
        (row(ln_v_g), _const_spec((1, D_A))),
        (row(ln_v_b), _const_spec((1, D_A))),
        (w_spatial, _const_spec(w_spatial.shape)),
        (bias_a, _const_spec((CHUNK_A, D_A))),
        (w_lr, _const_spec((2 * LOW_RANK, 2 * D_B))),
        (row(w0), _const_spec((1, D_B))),
        (row(a0), _const_spec((1, D_B))),
        (row(k_k), _const_spec((1, D_B))),
        (row(k_a), _const_spec((1, D_B))),
        (row(r_k), _const_spec((1, D_B))),
        (row(gn_g), _const_spec((1, D_B))),
        (row(gn_b), _const_spec((1, D_B))),
        (w_out.astype(BF16), _const_spec((d_model, d_model))),
        (row(ln_g), _const_spec((1, d_model))),
        (row(ln_b), _const_spec((1, d_model))),
        (gsum, _const_spec((LANES, LANES))),
        (gavg, _const_spec((LANES, LANES))),
        (tri, _const_spec((PREP_ROWS, PREP_ROWS))),
        (_wide_masks(), _const_spec((7, SCAN_L, MXU_COLS))),
    ]
    n_slots = N_CHUNKS * N_PAIRS
    return pl.pallas_call(
        _layer_kernel,
        grid=(batch, seq // TILE_T),
        in_specs=[spec for _, spec in operands],
        out_specs=pl.BlockSpec((None, TILE_T, d_model), lambda b, t: (b, t, 0)),
        out_shape=jax.ShapeDtypeStruct(x.shape, x.dtype),
        scratch_shapes=[
            pltpu.VMEM((TILE_T, d_model), BF16),
            pltpu.VMEM((TILE_T + SUBLANES, N_COLS), F32),
            pltpu.VMEM((PREP_ROWS, N_COLS_B), F32),
            pltpu.VMEM((TILE_T, d_model), BF16),
            pltpu.VMEM((N_PAIRS, LANES, LANES), F32),
            pltpu.VMEM((n_slots, 2 * PAIR_ROWS, LANES), BF16),
            pltpu.VMEM((n_slots, PAIR_ROWS, LANES), F32),
            pltpu.VMEM((n_slots, PAIR_ROWS, 2 * PAIR_ROWS), BF16),
            pltpu.VMEM((n_slots, PAIR_ROWS, LANES), BF16),
            pltpu.VMEM((n_slots, 2 * PAIR_ROWS, LANES), BF16),
            pltpu.VMEM((N_CHUNKS, SUBLANES, D_B), F32),
            pltpu.VMEM((TILE_T, D_B), F32),
            pltpu.VMEM((TILE_T, D_B), F32),
            pltpu.VMEM((TILE_T, D_B), F32),
        ],
        compiler_params=pltpu.CompilerParams(
            dimension_semantics=("arbitrary", "arbitrary"),
            vmem_limit_bytes=VMEM_LIMIT_BYTES),
        name="hybrid_layer",
    )(*[arr for arr, _ in operands])
```

```python
import math

import jax
import jax.numpy as jnp
import numpy as np
from jax import lax
from jax.experimental import pallas as pl
from jax.experimental.pallas import tpu as pltpu

F32 = jnp.float32
BF16 = jnp.bfloat16

D_MODEL = 1024
D_A = 512
D_B = 512
HEAD = 64
CHUNK_A = 128
LOW_RANK = 64
N_COLS_A = 3 * D_A
N_COLS_B = 4 * D_B + 2 * LOW_RANK
N_COLS = N_COLS_A + N_COLS_B
LN_EPS = 1e-5
GN_EPS = 64e-5
ALPHA = 2.0 ** 0.25

LANES = 128
SUBLANES = 8
MXU_COLS = 256
N_PAIRS = D_B // LANES
N_GROUPS = D_B // MXU_COLS
HEADS_PER_GROUP = MXU_COLS // HEAD
SCAN_L = 64
PAIR_ROWS = 2 * SCAN_L
TILE_T = 512
N_CHUNKS = TILE_T // SCAN_L
PREP_CHUNKS = 4
PREP_ROWS = PREP_CHUNKS * SCAN_L
CHUNKS_PER_BLOCK = CHUNK_A // SCAN_L
VMEM_LIMIT_BYTES = 56 * 1024 * 1024

ROW_PARAMS = (("mu_b", N_COLS_B), ("ln_v_g", D_A), ("ln_v_b", D_A), ("w0", D_B), ("a0", D_B),
              ("k_k", D_B), ("k_a", D_B), ("r_k", D_B), ("gn_g", D_B), ("gn_b", D_B),
              ("ln_g", D_MODEL), ("ln_b", D_MODEL))
ROW_OFFSETS = {}
for _name, _width in ROW_PARAMS:
    ROW_OFFSETS[_name] = (sum(w for _, w in ROW_PARAMS[:len(ROW_OFFSETS)]), _width)
LOG_DECAY_SCALE = -math.exp(-0.5)


def _bf(x):
    return x.astype(BF16)


def _mm(a, b):
    return jnp.dot(a, b, preferred_element_type=F32)


def _mm_nt(a, b):
    return lax.dot_general(a, b, (((1,), (1,)), ((), ())), preferred_element_type=F32)


def _mm_tn(a, b):
    return lax.dot_general(a, b, (((0,), (0,)), ((), ())), preferred_element_type=F32)


def _aligned(index, multiple):
    return index if isinstance(index, int) else pl.multiple_of(index, multiple)


def _interleave(step_generators):
    pending = list(step_generators)
    while pending:
        for gen in list(pending):
            try:
                next(gen)
            except StopIteration:
                pending.remove(gen)


def _take(gen, count):
    for _ in range(count):
        try:
            next(gen)
        except StopIteration:
            return
        yield


def _lock(fn, *lists):
    return [fn(*args) for args in zip(*lists)]


def _split_bf16(x, terms):
    parts = []
    rem = x
    for _ in range(terms):
        hi = _bf(rem)
        parts.append(hi)
        rem = rem - hi.astype(F32)
    return parts


def _dot_split(x, w_bf16, terms=2):
    stacked = jnp.concatenate(_split_bf16(x, terms), axis=0)
    res = _mm(stacked, w_bf16)
    rows = x.shape[0]
    out = res[0:rows]
    for i in range(1, terms):
        out = out + res[i * rows:(i + 1) * rows]
    return out


def _group_sum(x, gmat, terms):
    rows = x.shape[0]
    n = x.shape[1] // LANES
    xs = jnp.concatenate([x[:, LANES * j:LANES * (j + 1)] for j in range(n)], axis=0)
    s = _dot_split(xs, gmat, terms=terms)
    return jnp.concatenate([s[rows * j:rows * (j + 1)] for j in range(n)], axis=1)


def _sigmoid(x):
    return 1.0 / (1.0 + jnp.exp(-x))


def _silu(x):
    return x * _sigmoid(x)


def _gelu_tanh(x):
    c = math.sqrt(2.0 / math.pi)
    return 0.5 * x * (1.0 + jnp.tanh(c * (x + 0.044715 * (x * x * x))))


def _block_diag(y, head_masks):
    zero = jnp.zeros((y.shape[0], LANES), BF16)
    rows = []
    for h in range(HEADS_PER_GROUP):
        tile = h // 2
        kept = y[:, LANES * tile:LANES * (tile + 1)] * head_masks[h % 2]
        rows.append(jnp.concatenate([kept if t == tile else zero
                                     for t in range(MXU_COLS // LANES)], axis=1))
    return jnp.concatenate(rows, axis=0)


def _unit_lower_inverse_steps(a_abs, eye, m8, off16, off32, off64, bd_mask):
    bd = lambda y: _block_diag(y, bd_mask)
    a8 = _lock(lambda a: a * m8, a_abs)
    a8b = _lock(_bf, a8)
    a2b = _lock(lambda a: _bf(_mm(a, bd(a))), a8b)
    yield
    p1 = _lock(lambda a: eye + a, a8)
    both = _lock(lambda a, p: _mm(jnp.concatenate([a, _bf(p)], axis=0), bd(a)), a2b, p1)
    yield
    a4b = _lock(lambda m: _bf(m[0:SCAN_L]), both)
    p2 = _lock(lambda p, m: p + m[SCAN_L:], p1, both)
    t = _lock(lambda p, a: p + _mm(_bf(p), bd(a)), p2, a4b)
    yield
    for off in (off16, off32, off64):
        tb = _lock(_bf, t)
        to = _lock(lambda x, a: _bf(_mm(x, bd(_bf(a * off)))), tb, a_abs)
        yield
        t = _lock(lambda x, xo, xb: x + _mm(xo, bd(xb)), t, to, tb)
        yield
    return t


def _scan_prepare_steps(r, k, v, kk, bb, g, lw, consts, bd_mask):
    eye, tri_strict, tri_incl, m8, off16, off32, off64 = consts
    bd = lambda y: _block_diag(y, bd_mask)

    def factors(r, k, v, kk, bb, g, lw):
        e_g = jnp.exp(g)
        e_ge = jnp.exp(g - lw)
        e_ng = jnp.exp(-g)
        e_gl = jnp.exp(g[SCAN_L - 1:SCAN_L, :] - g)
        return (_bf(-(kk * e_ge)), _bf(r * e_g), _bf(v), _bf(bb * e_ng), _bf(k * e_ng),
                _bf(bb * e_gl), _bf(k * e_gl))

    xa, xr, vb, bh, kh, bbar, kbar = zip(*_lock(factors, r, k, v, kk, bb, g, lw))
    lhs = _lock(lambda a, b: jnp.concatenate([a, b], axis=0), xa, xr)
    aa_b = _lock(lambda x, y: _mm_nt(x, bd(y)), lhs, bh)
    yield
    aa_k = _lock(lambda x, y: _mm_nt(x, bd(y)), lhs, kh)
    yield
    a_ab = _lock(lambda m: jnp.where(tri_strict != 0.0, m[0:SCAN_L], 0.0), aa_b)
    a_rb = _lock(lambda m: _bf(jnp.where(tri_incl != 0.0, m[SCAN_L:], 0.0)), aa_b)
    a_ak = _lock(lambda m: _bf(jnp.where(tri_strict != 0.0, m[0:SCAN_L], 0.0)), aa_k)
    a_rk = _lock(lambda m: _bf(jnp.where(tri_incl != 0.0, m[SCAN_L:], 0.0)), aa_k)
    t_inv = yield from _unit_lower_inverse_steps(a_ab, eye, m8, off16, off32, off64, bd_mask)
    t_inv = _lock(_bf, t_inv)
    akv = _lock(lambda a, y: _bf(_mm(a, bd(y))), a_ak, vb)
    yield
    q = _lock(lambda t, y: _bf(_mm(t, bd(y))), t_inv, xa)
    yield
    z = _lock(lambda t, y: _mm(t, bd(y)), t_inv, akv)
    yield
    return q, list(xr), z, a_rb, a_rk, list(vb), list(bbar), list(kbar)


def _ada_kernel(c_ref, w_ref, b_ref, o_ref):
    s_hi, s_lo = _split_bf16(_silu(c_ref[...]), 2)
    w_hi, w_lo = _split_bf16(w_ref[...], 2)
    acc = _mm(s_hi, w_hi) + _mm(s_hi, w_lo) + _mm(s_lo, w_hi)
    o_ref[...] = acc + b_ref[...]


def _layer_kernel(x_ref, mod_ref, xn_ref, modn_ref,
                  win_ref, rows_ref, ws_ref, bs_ref, wlr_ref, wout_ref,
                  gsum_ref, gavg_ref, tri_ref, cm_ref,
                  out_ref,
                  h_scr, p_scr, pnext_scr, cat_scr, state_scr,
                  qx_scr, z_scr, ar_scr, vp_scr, bk_scr,
                  dec_scr, y_scr, bonus_scr, sgate_scr):
    t_idx = pl.program_id(1)
    tile = x_ref.shape[0]
    top = SUBLANES

    @pl.when(t_idx == 0)
    def _():
        state_scr[...] = jnp.zeros_like(state_scr)
        p_scr[0:top, :] = jnp.zeros((top, N_COLS), F32)

    def row(name):
        offset, width = ROW_OFFSETS[name]
        return rows_ref[:, offset:offset + width]

    shift = mod_ref[0:1, :]
    scale = mod_ref[1:2, :]
    gate = mod_ref[2:3, :]

    h_scr[...] = _bf(x_ref[...] * (1.0 + scale) + shift)

    @pl.when((pl.program_id(0) == 0) & (t_idx == 0))
    def _():
        pnext_scr[...] = _mm(h_scr[0:PREP_ROWS, :], win_ref[:, N_COLS_A:])

    def inproj_steps(row_blocks, col0, col1):
        for row0, nrows in row_blocks:
            hb = h_scr[row0:row0 + nrows, :]
            for c in range(col0, col1, MXU_COLS):
                ce = min(c + MXU_COLS, col1)
                p_scr[top + row0:top + row0 + nrows, c:ce] = _mm(hb, win_ref[:, c:ce])
                yield

    gsum = gsum_ref[...]
    gavg = gavg_ref[...]

    row_a = lax.broadcasted_iota(jnp.int32, (CHUNK_A, CHUNK_A), 0)
    col_a = lax.broadcasted_iota(jnp.int32, (CHUNK_A, CHUNK_A), 1)
    causal = col_a <= row_a
    lane_a = lax.broadcasted_iota(jnp.int32, (CHUNK_A, LANES), 1)
    first_a = lane_a < HEAD

    def gmlp_steps(ci):
        r0 = _aligned(top + ci * CHUNK_A, SUBLANES)
        rows = pl.ds(r0, CHUNK_A)
        v = _gelu_tanh(p_scr[rows, D_A:2 * D_A])
        mean = _group_sum(v, gavg, 2)
        yield
        vc = v - mean
        var = _group_sum(vc * vc, gavg, 1)
        yield
        v = vc * lax.rsqrt(var + LN_EPS) * row("ln_v_g") + row("ln_v_b")
        mixed = []
        for j in range(D_A // LANES):
            vj = _bf(v[:, LANES * j:LANES * (j + 1)])
            w_even = _bf(jnp.where(causal, ws_ref[2 * j], 0.0))
            w_odd = _bf(jnp.where(causal, ws_ref[2 * j + 1], 0.0))
            mixed.append(jnp.where(first_a, _mm(w_even, vj), _mm(w_odd, vj)))
        yield
        mix = jnp.concatenate(mixed, axis=1) + bs_ref[...]
        u = _gelu_tanh(p_scr[rows, 0:D_A])
        out_a = u * mix * _silu(p_scr[rows, 2 * D_A:3 * D_A])
        o0 = _aligned(ci * CHUNK_A, CHUNK_A)
        cat_scr[pl.ds(o0, CHUNK_A), 0:D_A] = _bf(out_a)

    consts = tuple(cm_ref[i] for i in range(7))
    row_b = lax.broadcasted_iota(jnp.int32, (PREP_ROWS, N_COLS_B), 0)
    lane_b = lax.broadcasted_iota(jnp.int32, (PREP_ROWS, LANES), 1)
    lane_s = lax.broadcasted_iota(jnp.int32, (SCAN_L, LANES), 1)
    first_f = lane_s < HEAD
    first_b = jnp.where(first_f, 1.0, 0.0).astype(BF16)
    second_b = jnp.where(first_f, 0.0, 1.0).astype(BF16)
    bd_mask = (first_b, second_b)
    cb = N_COLS_A

    def stack2b(zz):
        return jnp.concatenate([zz * first_b, zz * second_b], axis=0)

    def prepare_steps(bi):
        r0 = top + bi * PREP_ROWS
        if bi == 0:
            cur = pnext_scr[...]
            before = p_scr[0:top, cb:]
        else:
            cur = p_scr[r0:r0 + PREP_ROWS, cb:]
            before = (pnext_scr[PREP_ROWS - SUBLANES:, :] if bi == 1
                      else p_scr[r0 - SUBLANES:r0, cb:])
        prev = jnp.where(row_b == 0, before[SUBLANES - 1:SUBLANES, :], pltpu.roll(cur, 1, axis=0))
        ps = cur + row("mu_b") * (prev - cur)
        r = ps[:, 0:D_B]
        k = ps[:, D_B:2 * D_B]
        v = ps[:, 2 * D_B:3 * D_B]
        gate_b = ps[:, 3 * D_B:4 * D_B]
        low = ps[:, 4 * D_B:]
        low = jnp.where(lane_b < LOW_RANK, jnp.tanh(low), low)
        up = _mm(_bf(low), wlr_ref[...])
        yield
        lw = LOG_DECAY_SCALE * _sigmoid(row("w0") + up[:, 0:D_B])
        ah = _sigmoid(row("a0") + up[:, D_B:])
        g = _mm(tri_ref[...], jnp.concatenate(_split_bf16(lw, 2), axis=1))
        g = g[:, 0:D_B] + g[:, D_B:]
        kk = k * row("k_k")
        kk = kk * lax.rsqrt(_group_sum(kk * kk, gsum, 1) + 1e-12)
        yield
        k = k * (1.0 + (ah - 1.0) * row("k_a"))
        bb = kk * ah

        o0 = bi * PREP_ROWS
        bonus_scr[o0:o0 + PREP_ROWS, :] = _group_sum(r * k * row("r_k"), gsum, 1) * v
        sgate_scr[o0:o0 + PREP_ROWS, :] = _silu(gate_b)
        yield

        problems = [(c, gi) for c in range(PREP_CHUNKS) for gi in range(N_GROUPS)]
        cut = lambda zz: [zz[SCAN_L * c:SCAN_L * (c + 1), MXU_COLS * gi:MXU_COLS * (gi + 1)]
                          for c, gi in problems]
        for c in range(PREP_CHUNKS):
            last = g[SCAN_L * (c + 1) - 1:SCAN_L * (c + 1), :]
            dec_scr[bi * PREP_CHUNKS + c] = jnp.broadcast_to(jnp.exp(last), (SUBLANES, D_B))

        q, xr, z, a_rb, a_rk, vb, bbar, kbar = yield from _scan_prepare_steps(
            cut(r), cut(k), cut(v), cut(kk), cut(bb), cut(g), cut(lw), consts, bd_mask)
        for i, (c, gi) in enumerate(problems):
            for pp in range(MXU_COLS // LANES):
                slot = (bi * PREP_CHUNKS + c) * N_PAIRS + gi * (MXU_COLS // LANES) + pp
                sl = slice(LANES * pp, LANES * (pp + 1))
                qx_scr[slot] = jnp.concatenate([q[i][:, sl], xr[i][:, sl]], axis=0)
                z_scr[slot] = z[i][:, sl]
                ar_scr[slot] = jnp.concatenate([a_rb[i][:, sl], a_rk[i][:, sl]], axis=1)
                vp_scr[slot] = stack2b(vb[i][:, sl])
                bk_scr[slot] = jnp.concatenate([stack2b(bbar[i][:, sl]), stack2b(kbar[i][:, sl])],
                                               axis=0)

    n_prep = N_CHUNKS // PREP_CHUNKS
    row_block = lambda i: (i * PREP_ROWS, PREP_ROWS)
    for bi in range(n_prep):
        if bi + 1 < n_prep:
            fill = inproj_steps([row_block(bi + 1)], cb, N_COLS)
        else:
            fill = inproj_steps([row_block(i) for i in range(n_prep)], 0, cb)
        _interleave([prepare_steps(bi), fill])

    def apply_steps(bi):
        states = [state_scr[j] for j in range(N_PAIRS)]
        for c in range(CHUNKS_PER_BLOCK):
            ci = bi * CHUNKS_PER_BLOCK + c
            load = lambda ref: [ref[ci * N_PAIRS + j] for j in range(N_PAIRS)]
            dec = dec_scr[ci]
            decay = [dec[0:1, LANES * j:LANES * (j + 1)] for j in range(N_PAIRS)]
            xs = _lock(lambda a, s: _mm_nt(a, _bf(s)), load(qx_scr), states)
            yield
            uv = _lock(lambda x, zz, w: jnp.concatenate([stack2b(_bf(x[0:SCAN_L] + zz)), w], axis=0),
                       xs, load(z_scr), load(vp_scr))
            states = _lock(lambda s, d, m, b: s * d + _mm_tn(m, b),
                           states, decay, uv, load(bk_scr))
            ys = _lock(lambda x, a, m: x[SCAN_L:] + _mm(a, m), xs, load(ar_scr), uv)
            yield
            y_scr[pl.ds(_aligned(ci * SCAN_L, SCAN_L), SCAN_L), :] = jnp.concatenate(ys, axis=1)
        for j in range(N_PAIRS):
            state_scr[j] = states[j]

    def post_steps(bi):
        rows = slice(bi * CHUNK_A, (bi + 1) * CHUNK_A)
        y = y_scr[rows, :]
        mean = _group_sum(y, gavg, 2)
        yield
        yc = y - mean
        var = _group_sum(yc * yc, gavg, 1)
        yield
        y = yc * lax.rsqrt(var + GN_EPS) * row("gn_g") + row("gn_b")
        cat_scr[rows, D_A:] = _bf((y + bonus_scr[rows, :]) * sgate_scr[rows, :])
        o = _mm(cat_scr[rows, :], wout_ref[...])
        yield
        res = ALPHA * x_ref[rows, :] + gate * o
        mean = jnp.mean(res, axis=-1, keepdims=True)
        rc = res - mean
        var = jnp.mean(rc * rc, axis=-1, keepdims=True)
        out_ref[rows, :] = rc * lax.rsqrt(var + LN_EPS) * row("ln_g") + row("ln_b")

    def next_inproj_steps():
        hn = _bf(xn_ref[...] * (1.0 + modn_ref[1:2, :]) + modn_ref[0:1, :])
        for c in range(0, N_COLS_B, MXU_COLS):
            ce = min(c + MXU_COLS, N_COLS_B)
            pnext_scr[:, c:ce] = _mm(hn, win_ref[:, cb + c:cb + ce])
            yield

    n_blocks = tile // CHUNK_A
    next_proj = next_inproj_steps()
    next_steps_per_block = -(-len(range(0, N_COLS_B, MXU_COLS)) // n_blocks)
    for bi in range(n_blocks):
        group = [apply_steps(bi)]
        if bi > 0:
            group.append(post_steps(bi - 1))
        group += [gmlp_steps(bi), _take(next_proj, next_steps_per_block)]
        _interleave(group)
    _interleave([post_steps(n_blocks - 1), next_proj])

    p_scr[0:top, :] = p_scr[tile:tile + top, :]


def _wide_masks():
    t = np.arange(SCAN_L)[:, None]
    s = (np.arange(MXU_COLS) % SCAN_L)[None, :]
    same = lambda n: (t // n) == (s // n)
    masks = [t == s, s < t, s <= t, same(8),
             same(16) & ~same(8), same(32) & ~same(16), same(64) & ~same(32)]
    return jnp.asarray(np.stack(masks), F32)


def _const_spec(shape):
    zeros = (0,) * len(shape)
    return pl.BlockSpec(shape, lambda b, t: zeros)


def kernel(x, c, w_ada, b_ada, w_in, mu_b, ln_v_g, ln_v_b, w_spatial, b_spatial, w0, w_up, a0,
           a_up, k_k, k_a, r_k, gn_g, gn_b, w_out, ln_g, ln_b):
    batch, seq, d_model = x.shape
    assert d_model == D_MODEL and w_in.shape == (D_MODEL, N_COLS)
    assert seq % TILE_T == 0 and TILE_T % CHUNK_A == 0 and N_CHUNKS % PREP_CHUNKS == 0

    mod = pl.pallas_call(
        _ada_kernel,
        out_shape=jax.ShapeDtypeStruct((batch, 3 * d_model), F32),
        name="adaln_mod",
    )(c, w_ada, b_ada.reshape(1, -1))
    mod = mod.reshape(batch, 3, d_model)

    named = dict(mu_b=mu_b, ln_v_g=ln_v_g, ln_v_b=ln_v_b, w0=w0, a0=a0, k_k=k_k, k_a=k_a, r_k=r_k,
                 gn_g=gn_g, gn_b=gn_b, ln_g=ln_g, ln_b=ln_b)
    rows = jnp.concatenate([named[name].reshape(-1).astype(F32) for name, _ in ROW_PARAMS])
    rows = rows.reshape(1, -1)
    lane_id = np.arange(LANES)
    same_head = (lane_id[:, None] // HEAD) == (lane_id[None, :] // HEAD)
    gsum = jnp.asarray(same_head, BF16)
    gavg = jnp.asarray(same_head / HEAD, BF16)
    tok = np.arange(PREP_ROWS)
    tri = jnp.asarray((tok[None, :] <= tok[:, None])
                      & (tok[None, :] // SCAN_L == tok[:, None] // SCAN_L), BF16)
    zeros_lr = jnp.zeros((LOW_RANK, D_B), F32)
    w_lr = jnp.concatenate([jnp.concatenate([w_up, zeros_lr], axis=1),
                            jnp.concatenate([zeros_lr, a_up], axis=1)], axis=0).astype(BF16)
    bias_a = jnp.repeat(b_spatial.T, HEAD, axis=1)

    tiles = seq // TILE_T

    def next_block(b, t):
        s = jnp.minimum(b * tiles + t + 1, batch * tiles - 1)
        return s // tiles, (s % tiles) * (TILE_T // PREP_ROWS), 0

    operands = [
        (x, pl.BlockSpec((None, TILE_T, d_model), lambda b, t: (b, t, 0))),
        (mod, pl.BlockSpec((None, 3, d_model), lambda b, t: (b, 0, 0))),
        (x, pl.BlockSpec((None, PREP_ROWS, d_model), next_block)),
        (mod, pl.BlockSpec((None, 3, d_model), lambda b, t: (next_block(b, t)[0], 0, 0))),
        (w_in.astype(BF16), _const_spec((d_model, N_COLS))),
        (rows, _const_spec(rows.shape)),
        (w_spatial, _const_spec(w_spatial.shape)),
        (bias_a, _const_spec((CHUNK_A, D_A))),
        (w_lr, _const_spec((2 * LOW_RANK, 2 * D_B))),
        (w_out.astype(BF16), _const_spec((d_model, d_model))),
        (gsum, _const_spec((LANES, LANES))),
        (gavg, _const_spec((LANES, LANES))),
        (tri, _const_spec((PREP_ROWS, PREP_ROWS))),
        (_wide_masks(), _const_spec((7, SCAN_L, MXU_COLS))),
    ]
    n_slots = N_CHUNKS * N_PAIRS
    return pl.pallas_call(
        _layer_kernel,
        grid=(batch, seq // TILE_T),
        in_specs=[spec for _, spec in operands],
        out_specs=pl.BlockSpec((None, TILE_T, d_model), lambda b, t: (b, t, 0)),
        out_shape=jax.ShapeDtypeStruct(x.shape, x.dtype),
        scratch_shapes=[
            pltpu.VMEM((TILE_T, d_model), BF16),
            pltpu.VMEM((TILE_T + SUBLANES, N_COLS), F32),
            pltpu.VMEM((PREP_ROWS, N_COLS_B), F32),
            pltpu.VMEM((TILE_T, d_model), BF16),
            pltpu.VMEM((N_PAIRS, LANES, LANES), F32),
            pltpu.VMEM((n_slots, 2 * SCAN_L, LANES), BF16),
            pltpu.VMEM((n_slots, SCAN_L, LANES), F32),
            pltpu.VMEM((n_slots, SCAN_L, 2 * PAIR_ROWS), BF16),
            pltpu.VMEM((n_slots, PAIR_ROWS, LANES), BF16),
            pltpu.VMEM((n_slots, 2 * PAIR_ROWS, LANES), BF16),
            pltpu.VMEM((N_CHUNKS, SUBLANES, D_B), F32),
            pltpu.VMEM((TILE_T, D_B), F32),
            pltpu.VMEM((TILE_T, D_B), F32),
            pltpu.VMEM((TILE_T, D_B), F32),
        ],
        compiler_params=pltpu.CompilerParams(
            dimension_semantics=("arbitrary", "arbitrary"),
            vmem_limit_bytes=VMEM_LIMIT_BYTES),
        name="hybrid_layer",
    )(*[arr for arr, _ in operands])
```

```python
import math

import jax
import jax.numpy as jnp
import numpy as np
from jax import lax
from jax.experimental import pallas as pl
from jax.experimental.pallas import tpu as pltpu

F32 = jnp.float32
BF16 = jnp.bfloat16

D_MODEL = 1024
D_A = 512
D_B = 512
HEAD = 64
CHUNK_A = 128
LOW_RANK = 64
N_COLS_A = 3 * D_A
N_COLS_B = 4 * D_B + 2 * LOW_RANK
N_COLS = N_COLS_A + N_COLS_B
LN_EPS = 1e-5
GN_EPS = 64e-5
ALPHA = 2.0 ** 0.25

LANES = 128
SUBLANES = 8
MXU_COLS = 256
N_PAIRS = D_B // LANES
N_GROUPS = D_B // MXU_COLS
HEADS_PER_GROUP = MXU_COLS // HEAD
SCAN_L = 64
PAIR_ROWS = 2 * SCAN_L
TILE_T = 512
N_CHUNKS = TILE_T // SCAN_L
PREP_CHUNKS = 4
PREP_ROWS = PREP_CHUNKS * SCAN_L
PREPARE_STAGES = 17
CHUNKS_PER_BLOCK = CHUNK_A // SCAN_L
VMEM_LIMIT_BYTES = 56 * 1024 * 1024

ROW_PARAMS = (("mu_b", N_COLS_B), ("ln_v_g", D_A), ("ln_v_b", D_A), ("w0", D_B), ("a0", D_B),
              ("k_k", D_B), ("k_a", D_B), ("r_k", D_B), ("gn_g", D_B), ("gn_b", D_B),
              ("ln_g", D_MODEL), ("ln_b", D_MODEL))
ROW_OFFSETS = {}
for _name, _width in ROW_PARAMS:
    ROW_OFFSETS[_name] = (sum(w for _, w in ROW_PARAMS[:len(ROW_OFFSETS)]), _width)
LOG_DECAY_SCALE = -math.exp(-0.5)


def _bf(x):
    return x.astype(BF16)


def _mm(a, b):
    return jnp.dot(a, b, preferred_element_type=F32)


def _mm_nt(a, b):
    return lax.dot_general(a, b, (((1,), (1,)), ((), ())), preferred_element_type=F32)


def _mm_tn(a, b):
    return lax.dot_general(a, b, (((0,), (0,)), ((), ())), preferred_element_type=F32)


def _aligned(index, multiple):
    return index if isinstance(index, int) else pl.multiple_of(index, multiple)


def _interleave(step_generators):
    pending = list(step_generators)
    while pending:
        for gen in list(pending):
            try:
                next(gen)
            except StopIteration:
                pending.remove(gen)


def _take(gen, count):
    for _ in range(count):
        try:
            next(gen)
        except StopIteration:
            return
        yield


def _paced(gen, n_steps, n_rounds):
    done = 0
    for r in range(n_rounds):
        while done * n_rounds < (r + 1) * n_steps:
            try:
                next(gen)
            except StopIteration:
                return
            done += 1
        yield
    yield from gen


def _lock(fn, *lists):
    return [fn(*args) for args in zip(*lists)]


def _split_bf16(x, terms):
    parts = []
    rem = x
    for _ in range(terms):
        hi = _bf(rem)
        parts.append(hi)
        rem = rem - hi.astype(F32)
    return parts


def _dot_split(x, w_bf16, terms=2):
    stacked = jnp.concatenate(_split_bf16(x, terms), axis=0)
    res = _mm(stacked, w_bf16)
    rows = x.shape[0]
    out = res[0:rows]
    for i in range(1, terms):
        out = out + res[i * rows:(i + 1) * rows]
    return out


def _group_sum(x, gmat, terms):
    rows = x.shape[0]
    n = x.shape[1] // LANES
    xs = jnp.concatenate([x[:, LANES * j:LANES * (j + 1)] for j in range(n)], axis=0)
    s = _dot_split(xs, gmat, terms=terms)
    return jnp.concatenate([s[rows * j:rows * (j + 1)] for j in range(n)], axis=1)


def _sigmoid(x):
    return 1.0 / (1.0 + jnp.exp(-x))


def _silu(x):
    return x * _sigmoid(x)


def _gelu_tanh(x):
    c = math.sqrt(2.0 / math.pi)
    return 0.5 * x * (1.0 + jnp.tanh(c * (x + 0.044715 * (x * x * x))))


def _block_diag(y, head_masks):
    zero = jnp.zeros((y.shape[0], LANES), BF16)
    rows = []
    for h in range(HEADS_PER_GROUP):
        tile = h // 2
        kept = y[:, LANES * tile:LANES * (tile + 1)] * head_masks[h % 2]
        rows.append(jnp.concatenate([kept if t == tile else zero
                                     for t in range(MXU_COLS // LANES)], axis=1))
    return jnp.concatenate(rows, axis=0)


def _unit_lower_inverse_steps(a_abs, eye, m8, off16, off32, off64, bd_mask):
    bd = lambda y: _block_diag(y, bd_mask)
    a8 = _lock(lambda a: a * m8, a_abs)
    a8b = _lock(_bf, a8)
    a2b = _lock(lambda a: _bf(_mm(a, bd(a))), a8b)
    yield
    p1 = _lock(lambda a: eye + a, a8)
    both = _lock(lambda a, p: _mm(jnp.concatenate([a, _bf(p)], axis=0), bd(a)), a2b, p1)
    yield
    a4b = _lock(lambda m: _bf(m[0:SCAN_L]), both)
    p2 = _lock(lambda p, m: p + m[SCAN_L:], p1, both)
    t = _lock(lambda p, a: p + _mm(_bf(p), bd(a)), p2, a4b)
    yield
    for off in (off16, off32, off64):
        tb = _lock(_bf, t)
        to = _lock(lambda x, a: _bf(_mm(x, bd(_bf(a * off)))), tb, a_abs)
        yield
        t = _lock(lambda x, xo, xb: x + _mm(xo, bd(xb)), t, to, tb)
        yield
    return t


def _scan_prepare_steps(r, k, v, kk, bb, g, lw, consts, bd_mask, bd_full):
    eye, tri_strict, tri_incl, m8, off16, off32, off64 = consts
    bd = lambda y: _block_diag(y, bd_mask)

    def factors(r, k, v, kk, bb, g, lw):
        e_g = jnp.exp(g)
        e_ge = jnp.exp(g - lw)
        e_ng = jnp.exp(-g)
        e_gl = jnp.exp(g[SCAN_L - 1:SCAN_L, :] - g)
        return (_bf(-(kk * e_ge)), _bf(r * e_g), _bf(v), _bf(bb * e_ng), _bf(k * e_ng),
                _bf(bb * e_gl), _bf(k * e_gl))

    xa, xr, vb, bh, kh, bbar, kbar = zip(*_lock(factors, r, k, v, kk, bb, g, lw))
    lhs = _lock(lambda a, b: jnp.concatenate([a, b], axis=0), xa, xr)
    bd_t = lambda y: jnp.concatenate([y] * HEADS_PER_GROUP, axis=0).T * bd_full
    aa_b = _lock(lambda x, y: _mm(x, bd_t(y)), lhs, bh)
    yield
    aa_k = _lock(lambda x, y: _mm(x, bd_t(y)), lhs, kh)
    yield
    a_ab = _lock(lambda m: jnp.where(tri_strict != 0.0, m[0:SCAN_L], 0.0), aa_b)
    a_rb = _lock(lambda m: _bf(jnp.where(tri_incl != 0.0, m[SCAN_L:], 0.0)), aa_b)
    a_ak = _lock(lambda m: _bf(jnp.where(tri_strict != 0.0, m[0:SCAN_L], 0.0)), aa_k)
    a_rk = _lock(lambda m: _bf(jnp.where(tri_incl != 0.0, m[SCAN_L:], 0.0)), aa_k)
    t_inv = yield from _unit_lower_inverse_steps(a_ab, eye, m8, off16, off32, off64, bd_mask)
    t_inv = _lock(_bf, t_inv)
    akv = _lock(lambda a, y: _bf(_mm(a, bd(y))), a_ak, vb)
    yield
    q = _lock(lambda t, y: _bf(_mm(t, bd(y))), t_inv, xa)
    yield
    z = _lock(lambda t, y: _mm(t, bd(y)), t_inv, akv)
    yield
    return q, list(xr), z, a_rb, a_rk, list(vb), list(bbar), list(kbar)


def _ada_kernel(c_ref, w_ref, b_ref, o_ref):
    s_hi, s_lo = _split_bf16(_silu(c_ref[...]), 2)
    w_hi, w_lo = _split_bf16(w_ref[...], 2)
    acc = _mm(s_hi, w_hi) + _mm(s_hi, w_lo) + _mm(s_lo, w_hi)
    o_ref[...] = acc + b_ref[...]


def _layer_kernel(x_ref, mod_ref, xn_ref, modn_ref,
                  win_ref, rows_ref, ws_ref, bs_ref, wlr_ref, wout_ref,
                  gsum_ref, gavg_ref, tri_ref, cm_ref, bdf_ref,
                  out_ref,
                  h_scr, p_scr, pnext_scr, cat_scr, state_scr,
                  qx_scr, z_scr, ar_scr, vp_scr, bk_scr,
                  dec_scr, y_scr, bonus_scr, sgate_scr):
    t_idx = pl.program_id(1)
    tile = x_ref.shape[0]
    top = SUBLANES

    @pl.when(t_idx == 0)
    def _():
        state_scr[...] = jnp.zeros_like(state_scr)
        p_scr[0:top, :] = jnp.zeros((top, N_COLS), F32)

    def row(name):
        offset, width = ROW_OFFSETS[name]
        return rows_ref[:, offset:offset + width]

    shift = mod_ref[0:1, :]
    scale = mod_ref[1:2, :]
    gate = mod_ref[2:3, :]

    h_scr[...] = _bf(x_ref[...] * (1.0 + scale) + shift)

    @pl.when((pl.program_id(0) == 0) & (t_idx == 0))
    def _():
        pnext_scr[...] = _mm(h_scr[0:PREP_ROWS, :], win_ref[:, N_COLS_A:])

    def inproj_steps(row_blocks, col0, col1):
        for row0, nrows in row_blocks:
            hb = h_scr[row0:row0 + nrows, :]
            for c in range(col0, col1, MXU_COLS):
                ce = min(c + MXU_COLS, col1)
                p_scr[top + row0:top + row0 + nrows, c:ce] = _mm(hb, win_ref[:, c:ce])
                yield

    gsum = gsum_ref[...]
    gavg = gavg_ref[...]

    row_a = lax.broadcasted_iota(jnp.int32, (CHUNK_A, CHUNK_A), 0)
    col_a = lax.broadcasted_iota(jnp.int32, (CHUNK_A, CHUNK_A), 1)
    causal = col_a <= row_a
    lane_a = lax.broadcasted_iota(jnp.int32, (CHUNK_A, LANES), 1)
    first_a = lane_a < HEAD

    def gmlp_steps(ci):
        r0 = _aligned(top + ci * CHUNK_A, SUBLANES)
        rows = pl.ds(r0, CHUNK_A)
        v = _gelu_tanh(p_scr[rows, D_A:2 * D_A])
        mean = _group_sum(v, gavg, 2)
        yield
        vc = v - mean
        var = _group_sum(vc * vc, gavg, 1)
        yield
        v = vc * lax.rsqrt(var + LN_EPS) * row("ln_v_g") + row("ln_v_b")
        mixed = []
        for j in range(D_A // LANES):
            vj = _bf(v[:, LANES * j:LANES * (j + 1)])
            w_even = _bf(jnp.where(causal, ws_ref[2 * j], 0.0))
            w_odd = _bf(jnp.where(causal, ws_ref[2 * j + 1], 0.0))
            mixed.append(jnp.where(first_a, _mm(w_even, vj), _mm(w_odd, vj)))
        yield
        mix = jnp.concatenate(mixed, axis=1) + bs_ref[...]
        u = _gelu_tanh(p_scr[rows, 0:D_A])
        out_a = u * mix * _silu(p_scr[rows, 2 * D_A:3 * D_A])
        o0 = _aligned(ci * CHUNK_A, CHUNK_A)
        cat_scr[pl.ds(o0, CHUNK_A), 0:D_A] = _bf(out_a)

    consts = tuple(cm_ref[i] for i in range(7))
    row_b = lax.broadcasted_iota(jnp.int32, (PREP_ROWS, N_COLS_B), 0)
    lane_b = lax.broadcasted_iota(jnp.int32, (PREP_ROWS, LANES), 1)
    lane_s = lax.broadcasted_iota(jnp.int32, (SCAN_L, LANES), 1)
    first_f = lane_s < HEAD
    first_b = jnp.where(first_f, 1.0, 0.0).astype(BF16)
    second_b = jnp.where(first_f, 0.0, 1.0).astype(BF16)
    bd_mask = (first_b, second_b)
    cb = N_COLS_A

    def stack2b(zz):
        return jnp.concatenate([zz * first_b, zz * second_b], axis=0)

    def prepare_steps(bi):
        r0 = top + bi * PREP_ROWS
        if bi == 0:
            cur = pnext_scr[...]
            before = p_scr[0:top, cb:]
        else:
            cur = p_scr[r0:r0 + PREP_ROWS, cb:]
            before = (pnext_scr[PREP_ROWS - SUBLANES:, :] if bi == 1
                      else p_scr[r0 - SUBLANES:r0, cb:])
        prev = jnp.where(row_b == 0, before[SUBLANES - 1:SUBLANES, :], pltpu.roll(cur, 1, axis=0))
        ps = cur + row("mu_b") * (prev - cur)
        r = ps[:, 0:D_B]
        k = ps[:, D_B:2 * D_B]
        v = ps[:, 2 * D_B:3 * D_B]
        gate_b = ps[:, 3 * D_B:4 * D_B]
        low = ps[:, 4 * D_B:]
        low = jnp.where(lane_b < LOW_RANK, jnp.tanh(low), low)
        up = _mm(_bf(low), wlr_ref[...])
        yield
        lw = LOG_DECAY_SCALE * _sigmoid(row("w0") + up[:, 0:D_B])
        ah = _sigmoid(row("a0") + up[:, D_B:])
        g = _mm(tri_ref[...], jnp.concatenate(_split_bf16(lw, 2), axis=1))
        g = g[:, 0:D_B] + g[:, D_B:]
        kk = k * row("k_k")
        kk = kk * lax.rsqrt(_group_sum(kk * kk, gsum, 1) + 1e-12)
        yield
        k = k * (1.0 + (ah - 1.0) * row("k_a"))
        bb = kk * ah

        o0 = bi * PREP_ROWS
        bonus_scr[o0:o0 + PREP_ROWS, :] = _group_sum(r * k * row("r_k"), gsum, 1) * v
        sgate_scr[o0:o0 + PREP_ROWS, :] = _silu(gate_b)
        yield

        problems = [(c, gi) for c in range(PREP_CHUNKS) for gi in range(N_GROUPS)]
        cut = lambda zz: [zz[SCAN_L * c:SCAN_L * (c + 1), MXU_COLS * gi:MXU_COLS * (gi + 1)]
                          for c, gi in problems]
        for c in range(PREP_CHUNKS):
            last = g[SCAN_L * (c + 1) - 1:SCAN_L * (c + 1), :]
            dec_scr[bi * PREP_CHUNKS + c] = jnp.broadcast_to(jnp.exp(last), (SUBLANES, D_B))

        q, xr, z, a_rb, a_rk, vb, bbar, kbar = yield from _scan_prepare_steps(
            cut(r), cut(k), cut(v), cut(kk), cut(bb), cut(g), cut(lw), consts, bd_mask, bdf_ref[...])
        for i, (c, gi) in enumerate(problems):
            for pp in range(MXU_COLS // LANES):
                slot = (bi * PREP_CHUNKS + c) * N_PAIRS + gi * (MXU_COLS // LANES) + pp
                sl = slice(LANES * pp, LANES * (pp + 1))
                qx_scr[slot] = jnp.concatenate([q[i][:, sl], xr[i][:, sl]], axis=0)
                z_scr[slot] = z[i][:, sl]
                ar_scr[slot] = jnp.concatenate([a_rb[i][:, sl], a_rk[i][:, sl]], axis=1)
                vp_scr[slot] = stack2b(vb[i][:, sl])
                bk_scr[slot] = jnp.concatenate([stack2b(bbar[i][:, sl]), stack2b(kbar[i][:, sl])],
                                               axis=0)

    n_prep = N_CHUNKS // PREP_CHUNKS
    row_block = lambda i: (i * PREP_ROWS, PREP_ROWS)
    tiles_of = lambda width: -(-width // MXU_COLS)
    for bi in range(n_prep):
        if bi + 1 < n_prep:
            fill = inproj_steps([row_block(bi + 1)], cb, N_COLS)
            n_fill = tiles_of(N_COLS_B)
        else:
            fill = inproj_steps([row_block(i) for i in range(n_prep)], 0, cb)
            n_fill = n_prep * tiles_of(N_COLS_A)
        _interleave([prepare_steps(bi), _paced(fill, n_fill, PREPARE_STAGES)])

    def apply_steps(bi):
        states = [state_scr[j] for j in range(N_PAIRS)]
        for c in range(CHUNKS_PER_BLOCK):
            ci = bi * CHUNKS_PER_BLOCK + c
            load = lambda ref: [ref[ci * N_PAIRS + j] for j in range(N_PAIRS)]
            dec = dec_scr[ci]
            decay = [dec[0:1, LANES * j:LANES * (j + 1)] for j in range(N_PAIRS)]
            xs = _lock(lambda a, s: _mm_nt(a, _bf(s)), load(qx_scr), states)
            yield
            uv = _lock(lambda x, zz, w: jnp.concatenate([stack2b(_bf(x[0:SCAN_L] + zz)), w], axis=0),
                       xs, load(z_scr), load(vp_scr))
            states = _lock(lambda s, d, m, b: s * d + _mm_tn(m, b),
                           states, decay, uv, load(bk_scr))
            ys = _lock(lambda x, a, m: x[SCAN_L:] + _mm(a, m), xs, load(ar_scr), uv)
            yield
            y_scr[pl.ds(_aligned(ci * SCAN_L, SCAN_L), SCAN_L), :] = jnp.concatenate(ys, axis=1)
        for j in range(N_PAIRS):
            state_scr[j] = states[j]

    def post_steps(bi):
        rows = slice(bi * CHUNK_A, (bi + 1) * CHUNK_A)
        y = y_scr[rows, :]
        mean = _group_sum(y, gavg, 2)
        yield
        yc = y - mean
        var = _group_sum(yc * yc, gavg, 1)
        yield
        y = yc * lax.rsqrt(var + GN_EPS) * row("gn_g") + row("gn_b")
        cat_scr[rows, D_A:] = _bf((y + bonus_scr[rows, :]) * sgate_scr[rows, :])
        o = _mm(cat_scr[rows, :], wout_ref[...])
        yield
        res = ALPHA * x_ref[rows, :] + gate * o
        mean = jnp.mean(res, axis=-1, keepdims=True)
        rc = res - mean
        var = jnp.mean(rc * rc, axis=-1, keepdims=True)
        out_ref[rows, :] = rc * lax.rsqrt(var + LN_EPS) * row("ln_g") + row("ln_b")

    def next_inproj_steps():
        hn = _bf(xn_ref[...] * (1.0 + modn_ref[1:2, :]) + modn_ref[0:1, :])
        for c in range(0, N_COLS_B, MXU_COLS):
            ce = min(c + MXU_COLS, N_COLS_B)
            pnext_scr[:, c:ce] = _mm(hn, win_ref[:, cb + c:cb + ce])
            yield

    n_blocks = tile // CHUNK_A
    next_proj = next_inproj_steps()
    next_steps_per_block = tiles_of(N_COLS_B) // (n_blocks + 1)
    for bi in range(n_blocks):
        group = [apply_steps(bi)]
        if bi > 0:
            group.append(post_steps(bi - 1))
        group += [gmlp_steps(bi), _take(next_proj, next_steps_per_block)]
        _interleave(group)
    _interleave([post_steps(n_blocks - 1), next_proj])

    p_scr[0:top, :] = p_scr[tile:tile + top, :]


def _wide_masks():
    t = np.arange(SCAN_L)[:, None]
    s = (np.arange(MXU_COLS) % SCAN_L)[None, :]
    same = lambda n: (t // n) == (s // n)
    masks = [t == s, s < t, s <= t, same(8),
             same(16) & ~same(8), same(32) & ~same(16), same(64) & ~same(32)]
    return jnp.asarray(np.stack(masks), F32)


def _const_spec(shape):
    zeros = (0,) * len(shape)
    return pl.BlockSpec(shape, lambda b, t: zeros)


def kernel(x, c, w_ada, b_ada, w_in, mu_b, ln_v_g, ln_v_b, w_spatial, b_spatial, w0, w_up, a0,
           a_up, k_k, k_a, r_k, gn_g, gn_b, w_out, ln_g, ln_b):
    batch, seq, d_model = x.shape
    assert d_model == D_MODEL and w_in.shape == (D_MODEL, N_COLS)
    assert seq % TILE_T == 0 and TILE_T % CHUNK_A == 0 and N_CHUNKS % PREP_CHUNKS == 0

    mod = pl.pallas_call(
        _ada_kernel,
        out_shape=jax.ShapeDtypeStruct((batch, 3 * d_model), F32),
        name="adaln_mod",
    )(c, w_ada, b_ada.reshape(1, -1))
    mod = mod.reshape(batch, 3, d_model)

    named = dict(mu_b=mu_b, ln_v_g=ln_v_g, ln_v_b=ln_v_b, w0=w0, a0=a0, k_k=k_k, k_a=k_a, r_k=r_k,
                 gn_g=gn_g, gn_b=gn_b, ln_g=ln_g, ln_b=ln_b)
    rows = jnp.concatenate([named[name].reshape(-1).astype(F32) for name, _ in ROW_PARAMS])
    rows = rows.reshape(1, -1)
    lane_id = np.arange(LANES)
    same_head = (lane_id[:, None] // HEAD) == (lane_id[None, :] // HEAD)
    gsum = jnp.asarray(same_head, BF16)
    gavg = jnp.asarray(same_head / HEAD, BF16)
    tok = np.arange(PREP_ROWS)
    tri = jnp.asarray((tok[None, :] <= tok[:, None])
                      & (tok[None, :] // SCAN_L == tok[:, None] // SCAN_L), BF16)
    col = np.arange(MXU_COLS)
    bd_full = jnp.asarray((col[:, None] // HEAD) == (col[None, :] // HEAD), BF16)
    zeros_lr = jnp.zeros((LOW_RANK, D_B), F32)
    w_lr = jnp.concatenate([jnp.concatenate([w_up, zeros_lr], axis=1),
                            jnp.concatenate([zeros_lr, a_up], axis=1)], axis=0).astype(BF16)
    bias_a = jnp.repeat(b_spatial.T, HEAD, axis=1)

    tiles = seq // TILE_T

    def next_block(b, t):
        s = jnp.minimum(b * tiles + t + 1, batch * tiles - 1)
        return s // tiles, (s % tiles) * (TILE_T // PREP_ROWS), 0

    operands = [
        (x, pl.BlockSpec((None, TILE_T, d_model), lambda b, t: (b, t, 0))),
        (mod, pl.BlockSpec((None, 3, d_model), lambda b, t: (b, 0, 0))),
        (x, pl.BlockSpec((None, PREP_ROWS, d_model), next_block)),
        (mod, pl.BlockSpec((None, 3, d_model), lambda b, t: (next_block(b, t)[0], 0, 0))),
        (w_in.astype(BF16), _const_spec((d_model, N_COLS))),
        (rows, _const_spec(rows.shape)),
        (w_spatial, _const_spec(w_spatial.shape)),
        (bias_a, _const_spec((CHUNK_A, D_A))),
        (w_lr, _const_spec((2 * LOW_RANK, 2 * D_B))),
        (w_out.astype(BF16), _const_spec((d_model, d_model))),
        (gsum, _const_spec((LANES, LANES))),
        (gavg, _const_spec((LANES, LANES))),
        (tri, _const_spec((PREP_ROWS, PREP_ROWS))),
        (_wide_masks(), _const_spec((7, SCAN_L, MXU_COLS))),
        (bd_full, _const_spec((MXU_COLS, MXU_COLS))),
    ]
    n_slots = N_CHUNKS * N_PAIRS
    return pl.pallas_call(
        _layer_kernel,
        grid=(batch, seq // TILE_T),
        in_specs=[spec for _, spec in operands],
        out_specs=pl.BlockSpec((None, TILE_T, d_model), lambda b, t: (b, t, 0)),
        out_shape=jax.ShapeDtypeStruct(x.shape, x.dtype),
        scratch_shapes=[
            pltpu.VMEM((TILE_T, d_model), BF16),
            pltpu.VMEM((TILE_T + SUBLANES, N_COLS), F32),
            pltpu.VMEM((PREP_ROWS, N_COLS_B), F32),
            pltpu.VMEM((TILE_T, d_model), BF16),
            pltpu.VMEM((N_PAIRS, LANES, LANES), F32),
            pltpu.VMEM((n_slots, 2 * SCAN_L, LANES), BF16),
            pltpu.VMEM((n_slots, SCAN_L, LANES), F32),
            pltpu.VMEM((n_slots, SCAN_L, 2 * PAIR_ROWS), BF16),
            pltpu.VMEM((n_slots, PAIR_ROWS, LANES), BF16),
            pltpu.VMEM((n_slots, 2 * PAIR_ROWS, LANES), BF16),
            pltpu.VMEM((N_CHUNKS, SUBLANES, D_B), F32),
            pltpu.VMEM((TILE_T, D_B), F32),
            pltpu.VMEM((TILE_T, D_B), F32),
            pltpu.VMEM((TILE_T, D_B), F32),
        ],
        compiler_params=pltpu.CompilerParams(
            dimension_semantics=("arbitrary", "arbitrary"),
            vmem_limit_bytes=VMEM_LIMIT_BYTES),
        name="hybrid_layer",
    )(*[arr for arr, _ in operands])
```

```python
import math

import jax
import jax.numpy as jnp
import numpy as np
from jax import lax
from jax.experimental import pallas as pl
from jax.experimental.pallas import tpu as pltpu

F32 = jnp.float32
BF16 = jnp.bfloat16

D_MODEL = 1024
D_A = 512
D_B = 512
HEAD = 64
CHUNK_A = 128
LOW_RANK = 64
N_COLS_A = 3 * D_A
N_COLS_B = 4 * D_B + 2 * LOW_RANK
N_COLS = N_COLS_A + N_COLS_B
LN_EPS = 1e-5
GN_EPS = 64e-5
ALPHA = 2.0 ** 0.25

LANES = 128
SUBLANES = 8
MXU_COLS = 256
N_PAIRS = D_B // LANES
N_GROUPS = D_B // MXU_COLS
HEADS_PER_GROUP = MXU_COLS // HEAD
SCAN_L = 64
PAIR_ROWS = 2 * SCAN_L
TILE_T = 512
N_CHUNKS = TILE_T // SCAN_L
PREP_CHUNKS = 4
PREP_ROWS = PREP_CHUNKS * SCAN_L
CHUNKS_PER_BLOCK = CHUNK_A // SCAN_L
VMEM_LIMIT_BYTES = 56 * 1024 * 1024

ROW_PARAMS = (("mu_b", N_COLS_B), ("ln_v_g", D_A), ("ln_v_b", D_A), ("w0", D_B), ("a0", D_B),
              ("k_k", D_B), ("k_a", D_B), ("r_k", D_B), ("gn_g", D_B), ("gn_b", D_B),
              ("ln_g", D_MODEL), ("ln_b", D_MODEL))
ROW_OFFSETS = {}
for _name, _width in ROW_PARAMS:
    ROW_OFFSETS[_name] = (sum(w for _, w in ROW_PARAMS[:len(ROW_OFFSETS)]), _width)
LOG_DECAY_SCALE = -math.exp(-0.5)


def _bf(x):
    return x.astype(BF16)


def _mm(a, b):
    return jnp.dot(a, b, preferred_element_type=F32)


def _mm_nt(a, b):
    return lax.dot_general(a, b, (((1,), (1,)), ((), ())), preferred_element_type=F32)


def _mm_tn(a, b):
    return lax.dot_general(a, b, (((0,), (0,)), ((), ())), preferred_element_type=F32)


def _aligned(index, multiple):
    return index if isinstance(index, int) else pl.multiple_of(index, multiple)


def _interleave(step_generators):
    pending = list(step_generators)
    while pending:
        for gen in list(pending):
            try:
                next(gen)
            except StopIteration:
                pending.remove(gen)


def _take(gen, count):
    for _ in range(count):
        try:
            next(gen)
        except StopIteration:
            return
        yield


def _lock(fn, *lists):
    return [fn(*args) for args in zip(*lists)]


def _split_bf16(x, terms):
    parts = []
    rem = x
    for _ in range(terms):
        hi = _bf(rem)
        parts.append(hi)
        rem = rem - hi.astype(F32)
    return parts


def _dot_split(x, w_bf16, terms=2):
    stacked = jnp.concatenate(_split_bf16(x, terms), axis=0)
    res = _mm(stacked, w_bf16)
    rows = x.shape[0]
    out = res[0:rows]
    for i in range(1, terms):
        out = out + res[i * rows:(i + 1) * rows]
    return out


def _group_sum(x, gmat, terms):
    rows = x.shape[0]
    n = x.shape[1] // LANES
    xs = jnp.concatenate([x[:, LANES * j:LANES * (j + 1)] for j in range(n)], axis=0)
    s = _dot_split(xs, gmat, terms=terms)
    return jnp.concatenate([s[rows * j:rows * (j + 1)] for j in range(n)], axis=1)


def _group_sum_lanes(x, scale):
    lane = lax.broadcasted_iota(jnp.int32, (x.shape[0], LANES), 1)
    first = lane < HEAD
    outs = []
    for j in range(x.shape[1] // LANES):
        t = x[:, LANES * j:LANES * (j + 1)]
        s_first = jnp.sum(jnp.where(first, t, 0.0), axis=-1, keepdims=True)
        s_second = jnp.sum(jnp.where(first, 0.0, t), axis=-1, keepdims=True)
        outs.append(jnp.where(first, s_first, s_second) * scale)
    return jnp.concatenate(outs, axis=1)


def _sigmoid(x):
    return 1.0 / (1.0 + jnp.exp(-x))


def _silu(x):
    return x * _sigmoid(x)


def _gelu_tanh(x):
    c = math.sqrt(2.0 / math.pi)
    return 0.5 * x * (1.0 + jnp.tanh(c * (x + 0.044715 * (x * x * x))))


def _block_diag(y, head_masks):
    zero = jnp.zeros((y.shape[0], LANES), BF16)
    rows = []
    for h in range(HEADS_PER_GROUP):
        tile = h // 2
        kept = y[:, LANES * tile:LANES * (tile + 1)] * head_masks[h % 2]
        rows.append(jnp.concatenate([kept if t == tile else zero
                                     for t in range(MXU_COLS // LANES)], axis=1))
    return jnp.concatenate(rows, axis=0)


def _unit_lower_inverse_steps(a_abs, eye, m8, off16, off32, off64, bd_mask):
    bd = lambda y: _block_diag(y, bd_mask)
    a8 = _lock(lambda a: a * m8, a_abs)
    a8b = _lock(_bf, a8)
    a2b = _lock(lambda a: _bf(_mm(a, bd(a))), a8b)
    yield
    p1 = _lock(lambda a: eye + a, a8)
    both = _lock(lambda a, p: _mm(jnp.concatenate([a, _bf(p)], axis=0), bd(a)), a2b, p1)
    yield
    a4b = _lock(lambda m: _bf(m[0:SCAN_L]), both)
    p2 = _lock(lambda p, m: p + m[SCAN_L:], p1, both)
    t = _lock(lambda p, a: p + _mm(_bf(p), bd(a)), p2, a4b)
    yield
    for off in (off16, off32, off64):
        tb = _lock(_bf, t)
        to = _lock(lambda x, a: _bf(_mm(x, bd(_bf(a * off)))), tb, a_abs)
        yield
        t = _lock(lambda x, xo, xb: x + _mm(xo, bd(xb)), t, to, tb)
        yield
    return t


def _scan_prepare_steps(r, k, v, kk, bb, g, lw, consts, bd_mask):
    eye, tri_strict, tri_incl, m8, off16, off32, off64 = consts
    bd = lambda y: _block_diag(y, bd_mask)

    def factors(r, k, v, kk, bb, g, lw):
        e_g = jnp.exp(g)
        e_ge = jnp.exp(g - lw)
        e_ng = jnp.exp(-g)
        e_gl = jnp.exp(g[SCAN_L - 1:SCAN_L, :] - g)
        return (_bf(-(kk * e_ge)), _bf(r * e_g), _bf(v), _bf(bb * e_ng), _bf(k * e_ng),
                _bf(bb * e_gl), _bf(k * e_gl))

    xa, xr, vb, bh, kh, bbar, kbar = zip(*_lock(factors, r, k, v, kk, bb, g, lw))
    lhs = _lock(lambda a, b: jnp.concatenate([a, b], axis=0), xa, xr)
    aa_b = _lock(lambda x, y: _mm_nt(x, bd(y)), lhs, bh)
    yield
    aa_k = _lock(lambda x, y: _mm_nt(x, bd(y)), lhs, kh)
    yield
    a_ab = _lock(lambda m: jnp.where(tri_strict != 0.0, m[0:SCAN_L], 0.0), aa_b)
    a_rb = _lock(lambda m: _bf(jnp.where(tri_incl != 0.0, m[SCAN_L:], 0.0)), aa_b)
    a_ak = _lock(lambda m: _bf(jnp.where(tri_strict != 0.0, m[0:SCAN_L], 0.0)), aa_k)
    a_rk = _lock(lambda m: _bf(jnp.where(tri_incl != 0.0, m[SCAN_L:], 0.0)), aa_k)
    t_inv = yield from _unit_lower_inverse_steps(a_ab, eye, m8, off16, off32, off64, bd_mask)
    t_inv = _lock(_bf, t_inv)
    akv = _lock(lambda a, y: _bf(_mm(a, bd(y))), a_ak, vb)
    yield
    q = _lock(lambda t, y: _bf(_mm(t, bd(y))), t_inv, xa)
    yield
    z = _lock(lambda t, y: _mm(t, bd(y)), t_inv, akv)
    yield
    return q, list(xr), z, a_rb, a_rk, list(vb), list(bbar), list(kbar)


def _ada_kernel(c_ref, w_ref, b_ref, o_ref):
    s_hi, s_lo = _split_bf16(_silu(c_ref[...]), 2)
    w_hi, w_lo = _split_bf16(w_ref[...], 2)
    acc = _mm(s_hi, w_hi) + _mm(s_hi, w_lo) + _mm(s_lo, w_hi)
    o_ref[...] = acc + b_ref[...]


def _layer_kernel(x_ref, mod_ref, xn_ref, modn_ref,
                  win_ref, rows_ref, ws_ref, bs_ref, wlr_ref, wout_ref,
                  gsum_ref, gavg_ref, tri_ref, cm_ref,
                  out_ref,
                  h_scr, p_scr, pnext_scr, cat_scr, state_scr,
                  qx_scr, z_scr, ar_scr, vp_scr, bk_scr,
                  dec_scr, y_scr, bonus_scr, sgate_scr):
    t_idx = pl.program_id(1)
    tile = x_ref.shape[0]
    top = SUBLANES

    @pl.when(t_idx == 0)
    def _():
        state_scr[...] = jnp.zeros_like(state_scr)
        p_scr[0:top, :] = jnp.zeros((top, N_COLS), F32)

    def row(name):
        offset, width = ROW_OFFSETS[name]
        return rows_ref[:, offset:offset + width]

    shift = mod_ref[0:1, :]
    scale = mod_ref[1:2, :]
    gate = mod_ref[2:3, :]

    h_scr[...] = _bf(x_ref[...] * (1.0 + scale) + shift)

    @pl.when((pl.program_id(0) == 0) & (t_idx == 0))
    def _():
        pnext_scr[...] = _mm(h_scr[0:PREP_ROWS, :], win_ref[:, N_COLS_A:])

    def inproj_steps(row_blocks, col0, col1):
        for row0, nrows in row_blocks:
            hb = h_scr[row0:row0 + nrows, :]
            for c in range(col0, col1, MXU_COLS):
                ce = min(c + MXU_COLS, col1)
                p_scr[top + row0:top + row0 + nrows, c:ce] = _mm(hb, win_ref[:, c:ce])
                yield

    gsum = gsum_ref[...]
    gavg = gavg_ref[...]

    row_a = lax.broadcasted_iota(jnp.int32, (CHUNK_A, CHUNK_A), 0)
    col_a = lax.broadcasted_iota(jnp.int32, (CHUNK_A, CHUNK_A), 1)
    causal = col_a <= row_a
    lane_a = lax.broadcasted_iota(jnp.int32, (CHUNK_A, LANES), 1)
    first_a = lane_a < HEAD

    def gmlp_steps(ci):
        r0 = _aligned(top + ci * CHUNK_A, SUBLANES)
        rows = pl.ds(r0, CHUNK_A)
        v = _gelu_tanh(p_scr[rows, D_A:2 * D_A])
        mean = _group_sum_lanes(v, 1.0 / HEAD)
        yield
        vc = v - mean
        var = _group_sum_lanes(vc * vc, 1.0 / HEAD)
        yield
        v = vc * lax.rsqrt(var + LN_EPS) * row("ln_v_g") + row("ln_v_b")
        mixed = []
        for j in range(D_A // LANES):
            vj = _bf(v[:, LANES * j:LANES * (j + 1)])
            w_even = _bf(jnp.where(causal, ws_ref[2 * j], 0.0))
            w_odd = _bf(jnp.where(causal, ws_ref[2 * j + 1], 0.0))
            mixed.append(jnp.where(first_a, _mm(w_even, vj), _mm(w_odd, vj)))
        yield
        mix = jnp.concatenate(mixed, axis=1) + bs_ref[...]
        u = _gelu_tanh(p_scr[rows, 0:D_A])
        out_a = u * mix * _silu(p_scr[rows, 2 * D_A:3 * D_A])
        o0 = _aligned(ci * CHUNK_A, CHUNK_A)
        cat_scr[pl.ds(o0, CHUNK_A), 0:D_A] = _bf(out_a)

    consts = tuple(cm_ref[i] for i in range(7))
    row_b = lax.broadcasted_iota(jnp.int32, (PREP_ROWS, N_COLS_B), 0)
    lane_b = lax.broadcasted_iota(jnp.int32, (PREP_ROWS, LANES), 1)
    lane_s = lax.broadcasted_iota(jnp.int32, (SCAN_L, LANES), 1)
    first_f = lane_s < HEAD
    first_b = jnp.where(first_f, 1.0, 0.0).astype(BF16)
    second_b = jnp.where(first_f, 0.0, 1.0).astype(BF16)
    bd_mask = (first_b, second_b)
    cb = N_COLS_A

    def stack2b(zz):
        return jnp.concatenate([zz * first_b, zz * second_b], axis=0)

    def prepare_steps(bi):
        r0 = top + bi * PREP_ROWS
        if bi == 0:
            cur = pnext_scr[...]
            before = p_scr[0:top, cb:]
        else:
            cur = p_scr[r0:r0 + PREP_ROWS, cb:]
            before = (pnext_scr[PREP_ROWS - SUBLANES:, :] if bi == 1
                      else p_scr[r0 - SUBLANES:r0, cb:])
        prev = jnp.where(row_b == 0, before[SUBLANES - 1:SUBLANES, :], pltpu.roll(cur, 1, axis=0))
        ps = cur + row("mu_b") * (prev - cur)
        r = ps[:, 0:D_B]
        k = ps[:, D_B:2 * D_B]
        v = ps[:, 2 * D_B:3 * D_B]
        gate_b = ps[:, 3 * D_B:4 * D_B]
        low = ps[:, 4 * D_B:]
        low = jnp.where(lane_b < LOW_RANK, jnp.tanh(low), low)
        up = _mm(_bf(low), wlr_ref[...])
        yield
        lw = LOG_DECAY_SCALE * _sigmoid(row("w0") + up[:, 0:D_B])
        ah = _sigmoid(row("a0") + up[:, D_B:])
        g = _mm(tri_ref[...], jnp.concatenate(_split_bf16(lw, 2), axis=1))
        g = g[:, 0:D_B] + g[:, D_B:]
        kk = k * row("k_k")
        kk = kk * lax.rsqrt(_group_sum(kk * kk, gsum, 1) + 1e-12)
        yield
        k = k * (1.0 + (ah - 1.0) * row("k_a"))
        bb = kk * ah

        o0 = bi * PREP_ROWS
        bonus_scr[o0:o0 + PREP_ROWS, :] = _group_sum(r * k * row("r_k"), gsum, 1) * v
        sgate_scr[o0:o0 + PREP_ROWS, :] = _silu(gate_b)
        yield

        problems = [(c, gi) for c in range(PREP_CHUNKS) for gi in range(N_GROUPS)]
        cut = lambda zz: [zz[SCAN_L * c:SCAN_L * (c + 1), MXU_COLS * gi:MXU_COLS * (gi + 1)]
                          for c, gi in problems]
        for c in range(PREP_CHUNKS):
            last = g[SCAN_L * (c + 1) - 1:SCAN_L * (c + 1), :]
            dec_scr[bi * PREP_CHUNKS + c] = jnp.broadcast_to(jnp.exp(last), (SUBLANES, D_B))

        q, xr, z, a_rb, a_rk, vb, bbar, kbar = yield from _scan_prepare_steps(
            cut(r), cut(k), cut(v), cut(kk), cut(bb), cut(g), cut(lw), consts, bd_mask)
        for i, (c, gi) in enumerate(problems):
            for pp in range(MXU_COLS // LANES):
                slot = (bi * PREP_CHUNKS + c) * N_PAIRS + gi * (MXU_COLS // LANES) + pp
                sl = slice(LANES * pp, LANES * (pp + 1))
                qx_scr[slot] = jnp.concatenate([q[i][:, sl], xr[i][:, sl]], axis=0)
                z_scr[slot] = z[i][:, sl]
                ar_scr[slot] = jnp.concatenate([a_rb[i][:, sl], a_rk[i][:, sl]], axis=1)
                vp_scr[slot] = stack2b(vb[i][:, sl])
                bk_scr[slot] = jnp.concatenate([stack2b(bbar[i][:, sl]), stack2b(kbar[i][:, sl])],
                                               axis=0)

    n_prep = N_CHUNKS // PREP_CHUNKS
    row_block = lambda i: (i * PREP_ROWS, PREP_ROWS)
    for bi in range(n_prep):
        if bi + 1 < n_prep:
            fill = inproj_steps([row_block(bi + 1)], cb, N_COLS)
        else:
            fill = inproj_steps([row_block(i) for i in range(n_prep)], 0, cb)
        _interleave([prepare_steps(bi), fill])

    def apply_steps(bi):
        states = [state_scr[j] for j in range(N_PAIRS)]
        for c in range(CHUNKS_PER_BLOCK):
            ci = bi * CHUNKS_PER_BLOCK + c
            load = lambda ref: [ref[ci * N_PAIRS + j] for j in range(N_PAIRS)]
            dec = dec_scr[ci]
            decay = [dec[0:1, LANES * j:LANES * (j + 1)] for j in range(N_PAIRS)]
            xs = _lock(lambda a, s: _mm_nt(a, _bf(s)), load(qx_scr), states)
            yield
            uv = _lock(lambda x, zz, w: jnp.concatenate([stack2b(_bf(x[0:SCAN_L] + zz)), w], axis=0),
                       xs, load(z_scr), load(vp_scr))
            states = _lock(lambda s, d, m, b: s * d + _mm_tn(m, b),
                           states, decay, uv, load(bk_scr))
            ys = _lock(lambda x, a, m: x[SCAN_L:] + _mm(a, m), xs, load(ar_scr), uv)
            yield
            y_scr[pl.ds(_aligned(ci * SCAN_L, SCAN_L), SCAN_L), :] = jnp.concatenate(ys, axis=1)
        for j in range(N_PAIRS):
            state_scr[j] = states[j]

    def post_steps(bi):
        rows = slice(bi * CHUNK_A, (bi + 1) * CHUNK_A)
        y = y_scr[rows, :]
        mean = _group_sum_lanes(y, 1.0 / HEAD)
        yield
        yc = y - mean
        var = _group_sum_lanes(yc * yc, 1.0 / HEAD)
        yield
        y = yc * lax.rsqrt(var + GN_EPS) * row("gn_g") + row("gn_b")
        cat_scr[rows, D_A:] = _bf((y + bonus_scr[rows, :]) * sgate_scr[rows, :])
        o = _mm(cat_scr[rows, :], wout_ref[...])
        yield
        res = ALPHA * x_ref[rows, :] + gate * o
        mean = jnp.mean(res, axis=-1, keepdims=True)
        rc = res - mean
        var = jnp.mean(rc * rc, axis=-1, keepdims=True)
        out_ref[rows, :] = rc * lax.rsqrt(var + LN_EPS) * row("ln_g") + row("ln_b")

    def next_inproj_steps():
        hn = _bf(xn_ref[...] * (1.0 + modn_ref[1:2, :]) + modn_ref[0:1, :])
        for c in range(0, N_COLS_B, MXU_COLS):
            ce = min(c + MXU_COLS, N_COLS_B)
            pnext_scr[:, c:ce] = _mm(hn, win_ref[:, cb + c:cb + ce])
            yield

    n_blocks = tile // CHUNK_A
    next_proj = next_inproj_steps()
    next_steps_per_block = -(-len(range(0, N_COLS_B, MXU_COLS)) // n_blocks)
    for bi in range(n_blocks):
        group = [apply_steps(bi)]
        if bi > 0:
            group.append(post_steps(bi - 1))
        group += [gmlp_steps(bi), _take(next_proj, next_steps_per_block)]
        _interleave(group)
    _interleave([post_steps(n_blocks - 1), next_proj])

    p_scr[0:top, :] = p_scr[tile:tile + top, :]


def _wide_masks():
    t = np.arange(SCAN_L)[:, None]
    s = (np.arange(MXU_COLS) % SCAN_L)[None, :]
    same = lambda n: (t // n) == (s // n)
    masks = [t == s, s < t, s <= t, same(8),
             same(16) & ~same(8), same(32) & ~same(16), same(64) & ~same(32)]
    return jnp.asarray(np.stack(masks), F32)


def _const_spec(shape):
    zeros = (0,) * len(shape)
    return pl.BlockSpec(shape, lambda b, t: zeros)


def kernel(x, c, w_ada, b_ada, w_in, mu_b, ln_v_g, ln_v_b, w_spatial, b_spatial, w0, w_up, a0,
           a_up, k_k, k_a, r_k, gn_g, gn_b, w_out, ln_g, ln_b):
    batch, seq, d_model = x.shape
    assert d_model == D_MODEL and w_in.shape == (D_MODEL, N_COLS)
    assert seq % TILE_T == 0 and TILE_T % CHUNK_A == 0 and N_CHUNKS % PREP_CHUNKS == 0

    mod = pl.pallas_call(
        _ada_kernel,
        out_shape=jax.ShapeDtypeStruct((batch, 3 * d_model), F32),
        name="adaln_mod",
    )(c, w_ada, b_ada.reshape(1, -1))
    mod = mod.reshape(batch, 3, d_model)

    named = dict(mu_b=mu_b, ln_v_g=ln_v_g, ln_v_b=ln_v_b, w0=w0, a0=a0, k_k=k_k, k_a=k_a, r_k=r_k,
                 gn_g=gn_g, gn_b=gn_b, ln_g=ln_g, ln_b=ln_b)
    rows = jnp.concatenate([named[name].reshape(-1).astype(F32) for name, _ in ROW_PARAMS])
    rows = rows.reshape(1, -1)
    lane_id = np.arange(LANES)
    same_head = (lane_id[:, None] // HEAD) == (lane_id[None, :] // HEAD)
    gsum = jnp.asarray(same_head, BF16)
    gavg = jnp.asarray(same_head / HEAD, BF16)
    tok = np.arange(PREP_ROWS)
    tri = jnp.asarray((tok[None, :] <= tok[:, None])
                      & (tok[None, :] // SCAN_L == tok[:, None] // SCAN_L), BF16)
    zeros_lr = jnp.zeros((LOW_RANK, D_B), F32)
    w_lr = jnp.concatenate([jnp.concatenate([w_up, zeros_lr], axis=1),
                            jnp.concatenate([zeros_lr, a_up], axis=1)], axis=0).astype(BF16)
    bias_a = jnp.repeat(b_spatial.T, HEAD, axis=1)

    tiles = seq // TILE_T

    def next_block(b, t):
        s = jnp.minimum(b * tiles + t + 1, batch * tiles - 1)
        return s // tiles, (s % tiles) * (TILE_T // PREP_ROWS), 0

    operands = [
        (x, pl.BlockSpec((None, TILE_T, d_model), lambda b, t: (b, t, 0))),
        (mod, pl.BlockSpec((None, 3, d_model), lambda b, t: (b, 0, 0))),
        (x, pl.BlockSpec((None, PREP_ROWS, d_model), next_block)),
        (mod, pl.BlockSpec((None, 3, d_model), lambda b, t: (next_block(b, t)[0], 0, 0))),
        (w_in.astype(BF16), _const_spec((d_model, N_COLS))),
        (rows, _const_spec(rows.shape)),
        (w_spatial, _const_spec(w_spatial.shape)),
        (bias_a, _const_spec((CHUNK_A, D_A))),
        (w_lr, _const_spec((2 * LOW_RANK, 2 * D_B))),
        (w_out.astype(BF16), _const_spec((d_model, d_model))),
        (gsum, _const_spec((LANES, LANES))),
        (gavg, _const_spec((LANES, LANES))),
        (tri, _const_spec((PREP_ROWS, PREP_ROWS))),
        (_wide_masks(), _const_spec((7, SCAN_L, MXU_COLS))),
    ]
    n_slots = N_CHUNKS * N_PAIRS
    return pl.pallas_call(
        _layer_kernel,
        grid=(batch, seq // TILE_T),
        in_specs=[spec for _, spec in operands],
        out_specs=pl.BlockSpec((None, TILE_T, d_model), lambda b, t: (b, t, 0)),
        out_shape=jax.ShapeDtypeStruct(x.shape, x.dtype),
        scratch_shapes=[
            pltpu.VMEM((TILE_T, d_model), BF16),
            pltpu.VMEM((TILE_T + SUBLANES, N_COLS), F32),
            pltpu.VMEM((PREP_ROWS, N_COLS_B), F32),
            pltpu.VMEM((TILE_T, d_model), BF16),
            pltpu.VMEM((N_PAIRS, LANES, LANES), F32),
            pltpu.VMEM((n_slots, 2 * SCAN_L, LANES), BF16),
            pltpu.VMEM((n_slots, SCAN_L, LANES), F32),
            pltpu.VMEM((n_slots, SCAN_L, 2 * PAIR_ROWS), BF16),
            pltpu.VMEM((n_slots, PAIR_ROWS, LANES), BF16),
            pltpu.VMEM((n_slots, 2 * PAIR_ROWS, LANES), BF16),
            pltpu.VMEM((N_CHUNKS, SUBLANES, D_B), F32),
            pltpu.VMEM((TILE_T, D_B), F32),
            pltpu.VMEM((TILE_T, D_B), F32),
            pltpu.VMEM((TILE_T, D_B), F32),
        ],
        compiler_params=pltpu.CompilerParams(
            dimension_semantics=("arbitrary", "arbitrary"),
            vmem_limit_bytes=VMEM_LIMIT_BYTES),
        name="hybrid_layer",
    )(*[arr for arr, _ in operands])
```

```python
import math

import jax
import jax.numpy as jnp
import numpy as np
from jax import lax
from jax.experimental import pallas as pl
from jax.experimental.pallas import tpu as pltpu

F32 = jnp.float32
BF16 = jnp.bfloat16

D_MODEL = 1024
D_A = 512
D_B = 512
HEAD = 64
CHUNK_A = 128
LOW_RANK = 64
N_COLS_A = 3 * D_A
N_COLS_B = 4 * D_B + 2 * LOW_RANK
N_COLS = N_COLS_A + N_COLS_B
LN_EPS = 1e-5
GN_EPS = 64e-5
ALPHA = 2.0 ** 0.25
LOG_DECAY_SCALE = -math.exp(-0.5)

LANES = 128
SUBLANES = 8
MXU_COLS = 256
N_PAIRS = D_B // LANES
N_GROUPS = D_B // MXU_COLS
HEADS_PER_GROUP = MXU_COLS // HEAD
SCAN_L = 64
PAIR_ROWS = 2 * SCAN_L
TILE_T = 512
N_CHUNKS = TILE_T // SCAN_L
PREP_CHUNKS = 4
PREP_ROWS = PREP_CHUNKS * SCAN_L
CHUNKS_PER_BLOCK = CHUNK_A // SCAN_L
VMEM_LIMIT_BYTES = 56 * 1024 * 1024

ROW_PARAMS = (("mu_b", N_COLS_B), ("ln_v_g", D_A), ("ln_v_b", D_A), ("w0", D_B), ("a0", D_B),
              ("k_k", D_B), ("k_a", D_B), ("r_k", D_B), ("gn_g", D_B), ("gn_b", D_B),
              ("ln_g", D_MODEL), ("ln_b", D_MODEL))
ROW_OFFSETS = {}
for _name, _width in ROW_PARAMS:
    ROW_OFFSETS[_name] = (sum(w for _, w in ROW_PARAMS[:len(ROW_OFFSETS)]), _width)


def _bf(x):
    return x.astype(BF16)


def _mm(a, b):
    return jnp.dot(a, b, preferred_element_type=F32)


def _mm_nt(a, b):
    return lax.dot_general(a, b, (((1,), (1,)), ((), ())), preferred_element_type=F32)


def _mm_tn(a, b):
    return lax.dot_general(a, b, (((0,), (0,)), ((), ())), preferred_element_type=F32)


def _aligned(index, multiple):
    return index if isinstance(index, int) else pl.multiple_of(index, multiple)


def _interleave(step_generators):
    pending = list(step_generators)
    while pending:
        for gen in list(pending):
            try:
                next(gen)
            except StopIteration:
                pending.remove(gen)


def _take(gen, count):
    for _ in range(count):
        try:
            next(gen)
        except StopIteration:
            return
        yield


def _lock(fn, *lists):
    return [fn(*args) for args in zip(*lists)]


def _split_bf16(x, terms):
    parts = []
    rem = x
    for _ in range(terms):
        hi = _bf(rem)
        parts.append(hi)
        rem = rem - hi.astype(F32)
    return parts


def _dot_split(x, w_bf16, terms=2):
    stacked = jnp.concatenate(_split_bf16(x, terms), axis=0)
    res = _mm(stacked, w_bf16)
    rows = x.shape[0]
    out = res[0:rows]
    for i in range(1, terms):
        out = out + res[i * rows:(i + 1) * rows]
    return out


def _group_sum(x, gmat, terms):
    rows = x.shape[0]
    n = x.shape[1] // LANES
    xs = jnp.concatenate([x[:, LANES * j:LANES * (j + 1)] for j in range(n)], axis=0)
    s = _dot_split(xs, gmat, terms=terms)
    return jnp.concatenate([s[rows * j:rows * (j + 1)] for j in range(n)], axis=1)


def _sigmoid(x):
    return 1.0 / (1.0 + jnp.exp(-x))


def _silu(x):
    return x * _sigmoid(x)


def _gelu_tanh(x):
    c = math.sqrt(2.0 / math.pi)
    return 0.5 * x * (1.0 + jnp.tanh(c * (x + 0.044715 * (x * x * x))))


def _block_diag(y, head_masks):
    zero = jnp.zeros((y.shape[0], LANES), BF16)
    rows = []
    for h in range(HEADS_PER_GROUP):
        tile = h // 2
        kept = y[:, LANES * tile:LANES * (tile + 1)] * head_masks[h % 2]
        rows.append(jnp.concatenate([kept if t == tile else zero
                                     for t in range(MXU_COLS // LANES)], axis=1))
    return jnp.concatenate(rows, axis=0)


def _unit_lower_inverse_steps(a_abs, eye, m8, off16, off32, off64, bd_mask):
    bd = lambda y: _block_diag(y, bd_mask)
    a8 = _lock(lambda a: a * m8, a_abs)
    a8b = _lock(_bf, a8)
    a2b = _lock(lambda a: _bf(_mm(a, bd(a))), a8b)
    yield
    p1 = _lock(lambda a: eye + a, a8)
    both = _lock(lambda a, p: _mm(jnp.concatenate([a, _bf(p)], axis=0), bd(a)), a2b, p1)
    yield
    a4b = _lock(lambda m: _bf(m[0:SCAN_L]), both)
    p2 = _lock(lambda p, m: p + m[SCAN_L:], p1, both)
    t = _lock(lambda p, a: p + _mm(_bf(p), bd(a)), p2, a4b)
    yield
    for off in (off16, off32, off64):
        tb = _lock(_bf, t)
        to = _lock(lambda x, a: _bf(_mm(x, bd(_bf(a * off)))), tb, a_abs)
        yield
        t = _lock(lambda x, xo, xb: x + _mm(xo, bd(xb)), t, to, tb)
        yield
    return t


def _scan_prepare_steps(r, k, v, kk, bb, g, lw, consts, bd_mask):
    eye, tri_strict, tri_incl, m8, off16, off32, off64 = consts
    bd = lambda y: _block_diag(y, bd_mask)

    def factors(r, k, v, kk, bb, g, lw):
        e_g = jnp.exp(g)
        e_ge = jnp.exp(g - lw)
        e_ng = jnp.exp(-g)
        e_gl = jnp.exp(g[SCAN_L - 1:SCAN_L, :] - g)
        return (_bf(-(kk * e_ge)), _bf(r * e_g), _bf(v), _bf(bb * e_ng), _bf(k * e_ng),
                _bf(bb * e_gl), _bf(k * e_gl))

    xa, xr, vb, bh, kh, bbar, kbar = zip(*_lock(factors, r, k, v, kk, bb, g, lw))
    lhs = _lock(lambda a, b: jnp.concatenate([a, b], axis=0), xa, xr)
    aa_b = _lock(lambda x, y: _mm_nt(x, bd(y)), lhs, bh)
    yield
    aa_k = _lock(lambda x, y: _mm_nt(x, bd(y)), lhs, kh)
    yield
    a_ab = _lock(lambda m: jnp.where(tri_strict != 0.0, m[0:SCAN_L], 0.0), aa_b)
    a_rb = _lock(lambda m: _bf(jnp.where(tri_incl != 0.0, m[SCAN_L:], 0.0)), aa_b)
    a_ak = _lock(lambda m: _bf(jnp.where(tri_strict != 0.0, m[0:SCAN_L], 0.0)), aa_k)
    a_rk = _lock(lambda m: _bf(jnp.where(tri_incl != 0.0, m[SCAN_L:], 0.0)), aa_k)
    t_inv = yield from _unit_lower_inverse_steps(a_ab, eye, m8, off16, off32, off64, bd_mask)
    t_inv = _lock(_bf, t_inv)
    akv = _lock(lambda a, y: _bf(_mm(a, bd(y))), a_ak, vb)
    yield
    q = _lock(lambda t, y: _bf(_mm(t, bd(y))), t_inv, xa)
    yield
    z = _lock(lambda t, y: _mm(t, bd(y)), t_inv, akv)
    yield
    return q, list(xr), z, a_rb, a_rk, list(vb), list(bbar), list(kbar)


def _ada_kernel(c_ref, w_ref, b_ref, o_ref):
    s_hi, s_lo = _split_bf16(_silu(c_ref[...]), 2)
    w_hi, w_lo = _split_bf16(w_ref[...], 2)
    acc = _mm(s_hi, w_hi) + _mm(s_hi, w_lo) + _mm(s_lo, w_hi)
    o_ref[...] = acc + b_ref[...]


def _layer_kernel(x_ref, mod_ref, xn_ref, modn_ref,
                  win_ref, rows_ref, ws_ref, bs_ref, wlr_ref, wout_ref,
                  gsum_ref, gavg_ref, tri_ref, cm_ref,
                  out_ref,
                  h_scr, p_scr, pnext_scr, cat_scr, state_scr,
                  qx_scr, z_scr, ar_scr, vp_scr, bk_scr,
                  dec_scr, y_scr, bonus_scr, sgate_scr):
    t_idx = pl.program_id(1)
    tile = x_ref.shape[0]
    top = SUBLANES

    @pl.when(t_idx == 0)
    def _():
        state_scr[...] = jnp.zeros_like(state_scr)
        p_scr[0:top, :] = jnp.zeros((top, N_COLS), F32)

    def row(name):
        offset, width = ROW_OFFSETS[name]
        return rows_ref[:, offset:offset + width]

    shift = mod_ref[0:1, :]
    scale = mod_ref[1:2, :]
    gate = mod_ref[2:3, :]

    h_scr[...] = _bf(x_ref[...] * (1.0 + scale) + shift)

    @pl.when((pl.program_id(0) == 0) & (t_idx == 0))
    def _():
        pnext_scr[...] = _mm(h_scr[0:PREP_ROWS, :], win_ref[:, N_COLS_A:])

    def inproj_steps(row_blocks, col0, col1):
        for row0, nrows in row_blocks:
            hb = h_scr[row0:row0 + nrows, :]
            for c in range(col0, col1, MXU_COLS):
                ce = min(c + MXU_COLS, col1)
                p_scr[top + row0:top + row0 + nrows, c:ce] = _mm(hb, win_ref[:, c:ce])
                yield

    gsum = gsum_ref[...]
    gavg = gavg_ref[...]

    row_a = lax.broadcasted_iota(jnp.int32, (CHUNK_A, CHUNK_A), 0)
    col_a = lax.broadcasted_iota(jnp.int32, (CHUNK_A, CHUNK_A), 1)
    causal = col_a <= row_a
    lane_a = lax.broadcasted_iota(jnp.int32, (CHUNK_A, LANES), 1)
    first_a = lane_a < HEAD

    def gmlp_steps(ci):
        r0 = _aligned(top + ci * CHUNK_A, SUBLANES)
        rows = pl.ds(r0, CHUNK_A)
        v = _gelu_tanh(p_scr[rows, D_A:2 * D_A])
        mean = _group_sum(v, gavg, 2)
        yield
        vc = v - mean
        var = _group_sum(vc * vc, gavg, 1)
        yield
        v = vc * lax.rsqrt(var + LN_EPS) * row("ln_v_g") + row("ln_v_b")
        mixed = []
        for j in range(D_A // LANES):
            vj = _bf(v[:, LANES * j:LANES * (j + 1)])
            w_even = _bf(jnp.where(causal, ws_ref[2 * j], 0.0))
            w_odd = _bf(jnp.where(causal, ws_ref[2 * j + 1], 0.0))
            mixed.append(jnp.where(first_a, _mm(w_even, vj), _mm(w_odd, vj)))
        yield
        mix = jnp.concatenate(mixed, axis=1) + bs_ref[...]
        u = _gelu_tanh(p_scr[rows, 0:D_A])
        out_a = u * mix * _silu(p_scr[rows, 2 * D_A:3 * D_A])
        o0 = _aligned(ci * CHUNK_A, CHUNK_A)
        cat_scr[pl.ds(o0, CHUNK_A), 0:D_A] = _bf(out_a)

    consts = tuple(cm_ref[i] for i in range(7))
    row_b = lax.broadcasted_iota(jnp.int32, (PREP_ROWS, N_COLS_B), 0)
    lane_b = lax.broadcasted_iota(jnp.int32, (PREP_ROWS, LANES), 1)
    lane_s = lax.broadcasted_iota(jnp.int32, (SCAN_L, LANES), 1)
    first_f = lane_s < HEAD
    first_b = jnp.where(first_f, 1.0, 0.0).astype(BF16)
    second_b = jnp.where(first_f, 0.0, 1.0).astype(BF16)
    bd_mask = (first_b, second_b)
    cb = N_COLS_A

    def stack2b(zz):
        return jnp.concatenate([zz * first_b, zz * second_b], axis=0)

    def prepare_steps(bi):
        r0 = top + bi * PREP_ROWS
        if bi == 0:
            cur = pnext_scr[...]
            before = p_scr[0:top, cb:]
        else:
            cur = p_scr[r0:r0 + PREP_ROWS, cb:]
            before = (pnext_scr[PREP_ROWS - SUBLANES:, :] if bi == 1
                      else p_scr[r0 - SUBLANES:r0, cb:])
        prev = jnp.where(row_b == 0, before[SUBLANES - 1:SUBLANES, :], pltpu.roll(cur, 1, axis=0))
        ps = cur + row("mu_b") * (prev - cur)
        r = ps[:, 0:D_B]
        k = ps[:, D_B:2 * D_B]
        v = ps[:, 2 * D_B:3 * D_B]
        gate_b = ps[:, 3 * D_B:4 * D_B]
        low = ps[:, 4 * D_B:]
        low = jnp.where(lane_b < LOW_RANK, jnp.tanh(low), low)
        up = _mm(_bf(low), wlr_ref[...])
        yield
        lw = LOG_DECAY_SCALE * _sigmoid(row("w0") + up[:, 0:D_B])
        ah = _sigmoid(row("a0") + up[:, D_B:])
        g = _mm(tri_ref[...], jnp.concatenate(_split_bf16(lw, 2), axis=1))
        g = g[:, 0:D_B] + g[:, D_B:]
        kk = k * row("k_k")
        kk = kk * lax.rsqrt(_group_sum(kk * kk, gsum, 1) + 1e-12)
        yield
        k = k * (1.0 + (ah - 1.0) * row("k_a"))
        bb = kk * ah

        o0 = bi * PREP_ROWS
        bonus_scr[o0:o0 + PREP_ROWS, :] = _group_sum(r * k * row("r_k"), gsum, 1) * v
        sgate_scr[o0:o0 + PREP_ROWS, :] = _silu(gate_b)
        yield

        problems = [(c, gi) for c in range(PREP_CHUNKS) for gi in range(N_GROUPS)]
        cut = lambda zz: [zz[SCAN_L * c:SCAN_L * (c + 1), MXU_COLS * gi:MXU_COLS * (gi + 1)]
                          for c, gi in problems]
        for c in range(PREP_CHUNKS):
            last = g[SCAN_L * (c + 1) - 1:SCAN_L * (c + 1), :]
            dec_scr[bi * PREP_CHUNKS + c] = jnp.broadcast_to(jnp.exp(last), (SUBLANES, D_B))

        q, xr, z, a_rb, a_rk, vb, bbar, kbar = yield from _scan_prepare_steps(
            cut(r), cut(k), cut(v), cut(kk), cut(bb), cut(g), cut(lw), consts, bd_mask)
        for i, (c, gi) in enumerate(problems):
            for pp in range(MXU_COLS // LANES):
                slot = (bi * PREP_CHUNKS + c) * N_PAIRS + gi * (MXU_COLS // LANES) + pp
                sl = slice(LANES * pp, LANES * (pp + 1))
                qx_scr[slot] = jnp.concatenate([q[i][:, sl], xr[i][:, sl]], axis=0)
                z_scr[slot] = z[i][:, sl]
                ar_scr[slot] = jnp.concatenate([a_rb[i][:, sl], a_rk[i][:, sl]], axis=1)
                vp_scr[slot] = stack2b(vb[i][:, sl])
                bk_scr[slot] = jnp.concatenate([stack2b(bbar[i][:, sl]), stack2b(kbar[i][:, sl])],
                                               axis=0)

    n_prep = N_CHUNKS // PREP_CHUNKS
    row_block = lambda i: (i * PREP_ROWS, PREP_ROWS)
    for bi in range(n_prep):
        if bi + 1 < n_prep:
            fill = inproj_steps([row_block(bi + 1)], cb, N_COLS)
        else:
            fill = inproj_steps([row_block(i) for i in range(n_prep)], 0, cb)
        _interleave([prepare_steps(bi), fill])

    def apply_steps(bi):
        states = [state_scr[j] for j in range(N_PAIRS)]
        for c in range(CHUNKS_PER_BLOCK):
            ci = bi * CHUNKS_PER_BLOCK + c
            load = lambda ref: [ref[ci * N_PAIRS + j] for j in range(N_PAIRS)]
            dec = dec_scr[ci]
            decay = [dec[0:1, LANES * j:LANES * (j + 1)] for j in range(N_PAIRS)]
            xs = _lock(lambda a, s: _mm_nt(a, _bf(s)), load(qx_scr), states)
            yield
            uv = _lock(lambda x, zz, w: jnp.concatenate([stack2b(_bf(x[0:SCAN_L] + zz)), w], axis=0),
                       xs, load(z_scr), load(vp_scr))
            states = _lock(lambda s, d, m, b: s * d + _mm_tn(m, b),
                           states, decay, uv, load(bk_scr))
            ys = _lock(lambda x, a, m: x[SCAN_L:] + _mm(a, m), xs, load(ar_scr), uv)
            yield
            y_scr[pl.ds(_aligned(ci * SCAN_L, SCAN_L), SCAN_L), :] = jnp.concatenate(ys, axis=1)
        for j in range(N_PAIRS):
            state_scr[j] = states[j]

    def post_steps(bi):
        rows = slice(bi * CHUNK_A, (bi + 1) * CHUNK_A)
        y = y_scr[rows, :]
        mean = _group_sum(y, gavg, 2)
        yield
        yc = y - mean
        var = _group_sum(yc * yc, gavg, 1)
        yield
        y = yc * lax.rsqrt(var + GN_EPS) * row("gn_g") + row("gn_b")
        cat_scr[rows, D_A:] = _bf((y + bonus_scr[rows, :]) * sgate_scr[rows, :])
        o = _mm(cat_scr[rows, :], wout_ref[...])
        yield
        res = ALPHA * x_ref[rows, :] + gate * o
        mean = jnp.mean(res, axis=-1, keepdims=True)
        rc = res - mean
        var = jnp.mean(rc * rc, axis=-1, keepdims=True)
        out_ref[rows, :] = rc * lax.rsqrt(var + LN_EPS) * row("ln_g") + row("ln_b")

    def next_inproj_steps():
        hn = _bf(xn_ref[...] * (1.0 + modn_ref[1:2, :]) + modn_ref[0:1, :])
        for c in range(0, N_COLS_B, MXU_COLS):
            ce = min(c + MXU_COLS, N_COLS_B)
            pnext_scr[:, c:ce] = _mm(hn, win_ref[:, cb + c:cb + ce])
            yield

    n_blocks = tile // CHUNK_A
    next_proj = next_inproj_steps()
    next_steps_per_block = -(-len(range(0, N_COLS_B, MXU_COLS)) // n_blocks)
    for bi in range(n_blocks):
        group = [apply_steps(bi)]
        if bi > 0:
            group.append(post_steps(bi - 1))
        group += [gmlp_steps(bi), _take(next_proj, next_steps_per_block)]
        _interleave(group)
    _interleave([post_steps(n_blocks - 1), next_proj])

    p_scr[0:top, :] = p_scr[tile:tile + top, :]


def _wide_masks():
    t = np.arange(SCAN_L)[:, None]
    s = (np.arange(MXU_COLS) % SCAN_L)[None, :]
    same = lambda n: (t // n) == (s // n)
    masks = [t == s, s < t, s <= t, same(8),
             same(16) & ~same(8), same(32) & ~same(16), same(64) & ~same(32)]
    return jnp.asarray(np.stack(masks), F32)


def _const_spec(shape):
    zeros = (0,) * len(shape)
    return pl.BlockSpec(shape, lambda b, t: zeros)


def kernel(x, c, w_ada, b_ada, w_in, mu_b, ln_v_g, ln_v_b, w_spatial, b_spatial, w0, w_up, a0,
           a_up, k_k, k_a, r_k, gn_g, gn_b, w_out, ln_g, ln_b):
    batch, seq, d_model = x.shape
    assert d_model == D_MODEL and w_in.shape == (D_MODEL, N_COLS)
    assert seq % TILE_T == 0 and TILE_T % CHUNK_A == 0 and N_CHUNKS % PREP_CHUNKS == 0
    assert N_CHUNKS // PREP_CHUNKS >= 2

    mod = pl.pallas_call(
        _ada_kernel,
        out_shape=jax.ShapeDtypeStruct((batch, 3 * d_model), F32),
        name="adaln_mod",
    )(c, w_ada, b_ada.reshape(1, -1))
    mod = mod.reshape(batch, 3, d_model)

    named = dict(mu_b=mu_b, ln_v_g=ln_v_g, ln_v_b=ln_v_b, w0=w0, a0=a0, k_k=k_k, k_a=k_a, r_k=r_k,
                 gn_g=gn_g, gn_b=gn_b, ln_g=ln_g, ln_b=ln_b)
    rows = jnp.concatenate([named[name].reshape(-1).astype(F32) for name, _ in ROW_PARAMS])
    rows = rows.reshape(1, -1)
    lane_id = np.arange(LANES)
    same_head = (lane_id[:, None] // HEAD) == (lane_id[None, :] // HEAD)
    gsum = jnp.asarray(same_head, BF16)
    gavg = jnp.asarray(same_head / HEAD, BF16)
    tok = np.arange(PREP_ROWS)
    tri = jnp.asarray((tok[None, :] <= tok[:, None])
                      & (tok[None, :] // SCAN_L == tok[:, None] // SCAN_L), BF16)
    zeros_lr = jnp.zeros((LOW_RANK, D_B), F32)
    w_lr = jnp.concatenate([jnp.concatenate([w_up, zeros_lr], axis=1),
                            jnp.concatenate([zeros_lr, a_up], axis=1)], axis=0).astype(BF16)
    bias_a = jnp.repeat(b_spatial.T, HEAD, axis=1)

    tiles = seq // TILE_T

    def next_block(b, t):
        s = jnp.minimum(b * tiles + t + 1, batch * tiles - 1)
        return s // tiles, (s % tiles) * (TILE_T // PREP_ROWS), 0

    operands = [
        (x, pl.BlockSpec((None, TILE_T, d_model), lambda b, t: (b, t, 0))),
        (mod, pl.BlockSpec((None, 3, d_model), lambda b, t: (b, 0, 0))),
        (x, pl.BlockSpec((None, PREP_ROWS, d_model), next_block)),
        (mod, pl.BlockSpec((None, 3, d_model), lambda b, t: (next_block(b, t)[0], 0, 0))),
        (w_in.astype(BF16), _const_spec((d_model, N_COLS))),
        (rows, _const_spec(rows.shape)),
        (w_spatial, _const_spec(w_spatial.shape)),
        (bias_a, _const_spec((CHUNK_A, D_A))),
        (w_lr, _const_spec((2 * LOW_RANK, 2 * D_B))),
        (w_out.astype(BF16), _const_spec((d_model, d_model))),
        (gsum, _const_spec((LANES, LANES))),
        (gavg, _const_spec((LANES, LANES))),
        (tri, _const_spec((PREP_ROWS, PREP_ROWS))),
        (_wide_masks(), _const_spec((7, SCAN_L, MXU_COLS))),
    ]
    n_slots = N_CHUNKS * N_PAIRS
    return pl.pallas_call(
        _layer_kernel,
        grid=(batch, seq // TILE_T),
        in_specs=[spec for _, spec in operands],
        out_specs=pl.BlockSpec((None, TILE_T, d_model), lambda b, t: (b, t, 0)),
        out_shape=jax.ShapeDtypeStruct(x.shape, x.dtype),
        scratch_shapes=[
            pltpu.VMEM((TILE_T, d_model), BF16),
            pltpu.VMEM((TILE_T + SUBLANES, N_COLS), F32),
            pltpu.VMEM((PREP_ROWS, N_COLS_B), F32),
            pltpu.VMEM((TILE_T, d_model), BF16),
            pltpu.VMEM((N_PAIRS, LANES, LANES), F32),
            pltpu.VMEM((n_slots, 2 * SCAN_L, LANES), BF16),
            pltpu.VMEM((n_slots, SCAN_L, LANES), F32),
            pltpu.VMEM((n_slots, SCAN_L, 2 * PAIR_ROWS), BF16),
            pltpu.VMEM((n_slots, PAIR_ROWS, LANES), BF16),
            pltpu.VMEM((n_slots, 2 * PAIR_ROWS, LANES), BF16),
            pltpu.VMEM((N_CHUNKS, SUBLANES, D_B), F32),
            pltpu.VMEM((TILE_T, D_B), F32),
            pltpu.VMEM((TILE_T, D_B), F32),
            pltpu.VMEM((TILE_T, D_B), F32),
        ],
        compiler_params=pltpu.CompilerParams(
            dimension_semantics=("arbitrary", "arbitrary"),
            vmem_limit_bytes=VMEM_LIMIT_BYTES),
        name="hybrid_layer",
    )(*[arr for arr, _ in operands])
```

```python
import math

import jax
import jax.numpy as jnp
import numpy as np
from jax import lax
from jax.experimental import pallas as pl
from jax.experimental.pallas import tpu as pltpu

F32 = jnp.float32
BF16 = jnp.bfloat16

D_MODEL = 1024
D_A = 512
D_B = 512
HEAD = 64
CHUNK_A = 128
LOW_RANK = 64
N_COLS_A = 3 * D_A
N_COLS_B = 4 * D_B + 2 * LOW_RANK
N_COLS = N_COLS_A + N_COLS_B
LN_EPS = 1e-5
GN_EPS = 64e-5
ALPHA = 2.0 ** 0.25
LOG_DECAY_SCALE = -math.exp(-0.5)

LANES = 128
SUBLANES = 8
MXU_COLS = 256
N_MXU = 2
PROJ_STEP_COLS = N_MXU * MXU_COLS
N_PAIRS = D_B // LANES
N_GROUPS = D_B // MXU_COLS
HEADS_PER_GROUP = MXU_COLS // HEAD
SCAN_L = 64
PAIR_ROWS = 2 * SCAN_L
TILE_T = 512
N_CHUNKS = TILE_T // SCAN_L
PREP_CHUNKS = 4
PREP_ROWS = PREP_CHUNKS * SCAN_L
CHUNKS_PER_BLOCK = CHUNK_A // SCAN_L
VMEM_LIMIT_BYTES = 56 * 1024 * 1024

ROW_PARAMS = (("mu_b", N_COLS_B), ("ln_v_g", D_A), ("ln_v_b", D_A), ("w0", D_B), ("a0", D_B),
              ("k_k", D_B), ("k_a", D_B), ("r_k", D_B), ("gn_g", D_B), ("gn_b", D_B),
              ("ln_g", D_MODEL), ("ln_b", D_MODEL))
ROW_OFFSETS = {}
for _name, _width in ROW_PARAMS:
    ROW_OFFSETS[_name] = (sum(w for _, w in ROW_PARAMS[:len(ROW_OFFSETS)]), _width)


def _bf(x):
    return x.astype(BF16)


def _mm(a, b):
    return jnp.dot(a, b, preferred_element_type=F32)


def _mm_nt(a, b):
    return lax.dot_general(a, b, (((1,), (1,)), ((), ())), preferred_element_type=F32)


def _mm_tn(a, b):
    return lax.dot_general(a, b, (((0,), (0,)), ((), ())), preferred_element_type=F32)


def _aligned(index, multiple):
    return index if isinstance(index, int) else pl.multiple_of(index, multiple)


def _interleave(step_generators):
    pending = list(step_generators)
    while pending:
        for gen in list(pending):
            try:
                next(gen)
            except StopIteration:
                pending.remove(gen)


def _take(gen, count):
    for _ in range(count):
        try:
            next(gen)
        except StopIteration:
            return
        yield


def _lock(fn, *lists):
    return [fn(*args) for args in zip(*lists)]


def _split_bf16(x, terms):
    parts = []
    rem = x
    for _ in range(terms):
        hi = _bf(rem)
        parts.append(hi)
        rem = rem - hi.astype(F32)
    return parts


def _dot_split(x, w_bf16, terms=2):
    stacked = jnp.concatenate(_split_bf16(x, terms), axis=0)
    res = _mm(stacked, w_bf16)
    rows = x.shape[0]
    out = res[0:rows]
    for i in range(1, terms):
        out = out + res[i * rows:(i + 1) * rows]
    return out


def _group_sum(x, gmat, terms):
    rows = x.shape[0]
    n = x.shape[1] // LANES
    xs = jnp.concatenate([x[:, LANES * j:LANES * (j + 1)] for j in range(n)], axis=0)
    s = _dot_split(xs, gmat, terms=terms)
    return jnp.concatenate([s[rows * j:rows * (j + 1)] for j in range(n)], axis=1)


def _sigmoid(x):
    return 1.0 / (1.0 + jnp.exp(-x))


def _silu(x):
    return x * _sigmoid(x)


def _gelu_tanh(x):
    c = math.sqrt(2.0 / math.pi)
    return 0.5 * x * (1.0 + jnp.tanh(c * (x + 0.044715 * (x * x * x))))


def _block_diag(y, head_masks):
    zero = jnp.zeros((y.shape[0], LANES), BF16)
    rows = []
    for h in range(HEADS_PER_GROUP):
        tile = h // 2
        kept = y[:, LANES * tile:LANES * (tile + 1)] * head_masks[h % 2]
        rows.append(jnp.concatenate([kept if t == tile else zero
                                     for t in range(MXU_COLS // LANES)], axis=1))
    return jnp.concatenate(rows, axis=0)


def _unit_lower_inverse_steps(a_abs, eye, m8, offs, bd_mask):
    bd = lambda y: _block_diag(y, bd_mask)
    a8 = _lock(lambda a: a * m8, a_abs)
    a2 = _lock(lambda a: _bf(_mm(a, bd(a))), a8)
    yield
    p1 = _lock(lambda a: a + eye, a8)
    both = _lock(lambda a, p: _mm(jnp.concatenate([a, p], axis=0), bd(a)), a2, p1)
    yield
    a4 = _lock(lambda m: _bf(m[0:SCAN_L]), both)
    p2 = _lock(lambda p, m: p.astype(F32) + m[SCAN_L:], p1, both)
    t = _lock(lambda p, a: _bf(p + _mm(_bf(p), bd(a))), p2, a4)
    yield
    for off in offs:
        to = _lock(lambda x, a: _bf(_mm(x, bd(a * off))), t, a_abs)
        yield
        t = _lock(lambda x, xo: x + _bf(_mm(xo, bd(x))), t, to)
        yield
    return t


def _scan_prepare_steps(r, k, v, kk, bb, g, lw, consts, bd_mask):
    tri_strict, tri_incl, eye, m8, *offs = consts
    bd = lambda y: _block_diag(y, bd_mask)

    def factors(r, k, v, kk, bb, g, lw):
        e_g = jnp.exp(g)
        e_ge = jnp.exp(g - lw)
        e_ng = jnp.exp(-g)
        e_gl = jnp.exp(g[SCAN_L - 1:SCAN_L, :] - g)
        return (_bf(-(kk * e_ge)), _bf(r * e_g), _bf(v), _bf(bb * e_ng), _bf(k * e_ng),
                _bf(bb * e_gl), _bf(k * e_gl))

    xa, xr, vb, bh, kh, bbar, kbar = zip(*_lock(factors, r, k, v, kk, bb, g, lw))
    lhs = _lock(lambda a, b: jnp.concatenate([a, b], axis=0), xa, xr)
    aa_b = _lock(lambda x, y: _mm_nt(x, bd(y)), lhs, bh)
    yield
    aa_k = _lock(lambda x, y: _mm_nt(x, bd(y)), lhs, kh)
    yield
    a_ab = _lock(lambda m: _bf(jnp.where(tri_strict != 0.0, m[0:SCAN_L], 0.0)), aa_b)
    a_rb = _lock(lambda m: _bf(jnp.where(tri_incl != 0.0, m[SCAN_L:], 0.0)), aa_b)
    a_ak = _lock(lambda m: _bf(jnp.where(tri_strict != 0.0, m[0:SCAN_L], 0.0)), aa_k)
    a_rk = _lock(lambda m: _bf(jnp.where(tri_incl != 0.0, m[SCAN_L:], 0.0)), aa_k)
    t_inv = yield from _unit_lower_inverse_steps(a_ab, eye, m8, offs, bd_mask)
    akv = _lock(lambda a, y: _bf(_mm(a, bd(y))), a_ak, vb)
    yield
    q = _lock(lambda t, y: _bf(_mm(t, bd(y))), t_inv, xa)
    yield
    z = _lock(lambda t, y: _mm(t, bd(y)), t_inv, akv)
    yield
    return q, list(xr), z, a_rb, a_rk, list(vb), list(bbar), list(kbar)


def _ada_kernel(c_ref, w_ref, b_ref, o_ref):
    s_hi, s_lo = _split_bf16(_silu(c_ref[...]), 2)
    w_hi, w_lo = _split_bf16(w_ref[...], 2)
    acc = _mm(s_hi, w_hi) + _mm(s_hi, w_lo) + _mm(s_lo, w_hi)
    o_ref[...] = acc + b_ref[...]


def _layer_kernel(x_ref, mod_ref, xn_ref, modn_ref,
                  win_ref, rows_ref, ws_ref, bs_ref, wlr_ref, wout_ref,
                  gsum_ref, gavg_ref, tri_ref, cm_ref, cmb_ref,
                  out_ref,
                  h_scr, p_scr, pnext_scr, cat_scr, state_scr,
                  qx_scr, z_scr, ar_scr, vp_scr, bk_scr,
                  dec_scr, y_scr, bonus_scr, sgate_scr):
    t_idx = pl.program_id(1)
    tile = x_ref.shape[0]
    top = SUBLANES

    @pl.when(t_idx == 0)
    def _():
        state_scr[...] = jnp.zeros_like(state_scr)
        p_scr[0:top, :] = jnp.zeros((top, N_COLS), F32)

    def row(name):
        offset, width = ROW_OFFSETS[name]
        return rows_ref[:, offset:offset + width]

    shift = mod_ref[0:1, :]
    scale = mod_ref[1:2, :]
    gate = mod_ref[2:3, :]

    h_scr[...] = _bf(x_ref[...] * (1.0 + scale) + shift)

    @pl.when((pl.program_id(0) == 0) & (t_idx == 0))
    def _():
        pnext_scr[...] = _mm(h_scr[0:PREP_ROWS, :], win_ref[:, N_COLS_A:])

    def inproj_steps(row_blocks, col0, col1):
        for row0, nrows in row_blocks:
            hb = h_scr[row0:row0 + nrows, :]
            for c in range(col0, col1, PROJ_STEP_COLS):
                ce = min(c + PROJ_STEP_COLS, col1)
                p_scr[top + row0:top + row0 + nrows, c:ce] = _mm(hb, win_ref[:, c:ce])
                yield

    gsum = gsum_ref[...]
    gavg = gavg_ref[...]

    row_a = lax.broadcasted_iota(jnp.int32, (CHUNK_A, CHUNK_A), 0)
    col_a = lax.broadcasted_iota(jnp.int32, (CHUNK_A, CHUNK_A), 1)
    causal = col_a <= row_a
    lane_a = lax.broadcasted_iota(jnp.int32, (CHUNK_A, LANES), 1)
    first_a = lane_a < HEAD

    def gmlp_steps(ci):
        r0 = _aligned(top + ci * CHUNK_A, SUBLANES)
        rows = pl.ds(r0, CHUNK_A)
        v = _gelu_tanh(p_scr[rows, D_A:2 * D_A])
        mean = _group_sum(v, gavg, 2)
        yield
        vc = v - mean
        var = _group_sum(vc * vc, gavg, 1)
        yield
        v = vc * lax.rsqrt(var + LN_EPS) * row("ln_v_g") + row("ln_v_b")
        mixed = []
        for j in range(D_A // LANES):
            vj = _bf(v[:, LANES * j:LANES * (j + 1)])
            w_even = _bf(jnp.where(causal, ws_ref[2 * j], 0.0))
            w_odd = _bf(jnp.where(causal, ws_ref[2 * j + 1], 0.0))
            mixed.append(jnp.where(first_a, _mm(w_even, vj), _mm(w_odd, vj)))
        yield
        mix = jnp.concatenate(mixed, axis=1) + bs_ref[...]
        u = _gelu_tanh(p_scr[rows, 0:D_A])
        out_a = u * mix * _silu(p_scr[rows, 2 * D_A:3 * D_A])
        o0 = _aligned(ci * CHUNK_A, CHUNK_A)
        cat_scr[pl.ds(o0, CHUNK_A), 0:D_A] = _bf(out_a)

    consts = tuple(cm_ref[i] for i in range(2)) + tuple(cmb_ref[i] for i in range(5))
    row_b = lax.broadcasted_iota(jnp.int32, (PREP_ROWS, N_COLS_B), 0)
    lane_b = lax.broadcasted_iota(jnp.int32, (PREP_ROWS, LANES), 1)
    lane_s = lax.broadcasted_iota(jnp.int32, (SCAN_L, LANES), 1)
    first_f = lane_s < HEAD
    first_b = jnp.where(first_f, 1.0, 0.0).astype(BF16)
    second_b = jnp.where(first_f, 0.0, 1.0).astype(BF16)
    bd_mask = (first_b, second_b)
    cb = N_COLS_A

    def stack2b(zz):
        return jnp.concatenate([zz * first_b, zz * second_b], axis=0)

    def prepare_steps(bi):
        r0 = top + bi * PREP_ROWS
        if bi == 0:
            cur = pnext_scr[...]
            before = p_scr[0:top, cb:]
        else:
            cur = p_scr[r0:r0 + PREP_ROWS, cb:]
            before = (pnext_scr[PREP_ROWS - SUBLANES:, :] if bi == 1
                      else p_scr[r0 - SUBLANES:r0, cb:])
        prev = jnp.where(row_b == 0, before[SUBLANES - 1:SUBLANES, :], pltpu.roll(cur, 1, axis=0))
        ps = cur + row("mu_b") * (prev - cur)
        r = ps[:, 0:D_B]
        k = ps[:, D_B:2 * D_B]
        v = ps[:, 2 * D_B:3 * D_B]
        gate_b = ps[:, 3 * D_B:4 * D_B]
        low = ps[:, 4 * D_B:]
        low = jnp.where(lane_b < LOW_RANK, jnp.tanh(low), low)
        up = _mm(_bf(low), wlr_ref[...])
        yield
        lw = LOG_DECAY_SCALE * _sigmoid(row("w0") + up[:, 0:D_B])
        ah = _sigmoid(row("a0") + up[:, D_B:])
        g = _mm(tri_ref[...], jnp.concatenate(_split_bf16(lw, 2), axis=1))
        g = g[:, 0:D_B] + g[:, D_B:]
        kk = k * row("k_k")
        kk = kk * lax.rsqrt(_group_sum(kk * kk, gsum, 1) + 1e-12)
        yield
        k = k * (1.0 + (ah - 1.0) * row("k_a"))
        bb = kk * ah

        o0 = bi * PREP_ROWS
        bonus_scr[o0:o0 + PREP_ROWS, :] = _group_sum(r * k * row("r_k"), gsum, 1) * v
        sgate_scr[o0:o0 + PREP_ROWS, :] = _silu(gate_b)
        yield

        problems = [(c, gi) for c in range(PREP_CHUNKS) for gi in range(N_GROUPS)]
        cut = lambda zz: [zz[SCAN_L * c:SCAN_L * (c + 1), MXU_COLS * gi:MXU_COLS * (gi + 1)]
                          for c, gi in problems]
        for c in range(PREP_CHUNKS):
            last = g[SCAN_L * (c + 1) - 1:SCAN_L * (c + 1), :]
            dec_scr[bi * PREP_CHUNKS + c] = jnp.broadcast_to(jnp.exp(last), (SUBLANES, D_B))

        q, xr, z, a_rb, a_rk, vb, bbar, kbar = yield from _scan_prepare_steps(
            cut(r), cut(k), cut(v), cut(kk), cut(bb), cut(g), cut(lw), consts, bd_mask)
        for i, (c, gi) in enumerate(problems):
            for pp in range(MXU_COLS // LANES):
                slot = (bi * PREP_CHUNKS + c) * N_PAIRS + gi * (MXU_COLS // LANES) + pp
                sl = slice(LANES * pp, LANES * (pp + 1))
                qx_scr[slot] = jnp.concatenate([q[i][:, sl], xr[i][:, sl]], axis=0)
                z_scr[slot] = z[i][:, sl]
                ar_scr[slot] = jnp.concatenate([a_rb[i][:, sl], a_rk[i][:, sl]], axis=1)
                vp_scr[slot] = stack2b(vb[i][:, sl])
                bk_scr[slot] = jnp.concatenate([stack2b(bbar[i][:, sl]), stack2b(kbar[i][:, sl])],
                                               axis=0)

    def apply_steps(bi):
        states = [state_scr[j] for j in range(N_PAIRS)]
        for c in range(CHUNKS_PER_BLOCK):
            ci = bi * CHUNKS_PER_BLOCK + c
            load = lambda ref: [ref[ci * N_PAIRS + j] for j in range(N_PAIRS)]
            dec = dec_scr[ci]
            decay = [dec[0:1, LANES * j:LANES * (j + 1)] for j in range(N_PAIRS)]
            xs = _lock(lambda a, s: _mm_nt(a, _bf(s)), load(qx_scr), states)
            yield
            uv = _lock(lambda x, zz, w: jnp.concatenate([stack2b(_bf(x[0:SCAN_L] + zz)), w], axis=0),
                       xs, load(z_scr), load(vp_scr))
            states = _lock(lambda s, d, m, b: s * d + _mm_tn(m, b),
                           states, decay, uv, load(bk_scr))
            ys = _lock(lambda x, a, m: x[SCAN_L:] + _mm(a, m), xs, load(ar_scr), uv)
            yield
            y_scr[pl.ds(_aligned(ci * SCAN_L, SCAN_L), SCAN_L), :] = jnp.concatenate(ys, axis=1)
        for j in range(N_PAIRS):
            state_scr[j] = states[j]

    def post_steps(bi):
        rows = slice(bi * CHUNK_A, (bi + 1) * CHUNK_A)
        y = y_scr[rows, :]
        mean = _group_sum(y, gavg, 2)
        yield
        yc = y - mean
        var = _group_sum(yc * yc, gavg, 1)
        yield
        y = yc * lax.rsqrt(var + GN_EPS) * row("gn_g") + row("gn_b")
        cat_scr[rows, D_A:] = _bf((y + bonus_scr[rows, :]) * sgate_scr[rows, :])
        o = _mm(cat_scr[rows, :], wout_ref[...])
        yield
        res = ALPHA * x_ref[rows, :] + gate * o
        mean = jnp.mean(res, axis=-1, keepdims=True)
        rc = res - mean
        var = jnp.mean(rc * rc, axis=-1, keepdims=True)
        out_ref[rows, :] = rc * lax.rsqrt(var + LN_EPS) * row("ln_g") + row("ln_b")

    def next_inproj_steps():
        hn = _bf(xn_ref[...] * (1.0 + modn_ref[1:2, :]) + modn_ref[0:1, :])
        for c in range(0, N_COLS_B, PROJ_STEP_COLS):
            ce = min(c + PROJ_STEP_COLS, N_COLS_B)
            pnext_scr[:, c:ce] = _mm(hn, win_ref[:, cb + c:cb + ce])
            yield

    n_prep = N_CHUNKS // PREP_CHUNKS
    n_blocks = tile // CHUNK_A
    row_block = lambda i: (i * PREP_ROWS, PREP_ROWS)
    for bi in range(n_prep):
        if bi + 1 < n_prep:
            fill = inproj_steps([row_block(bi + 1)], cb, N_COLS)
        else:
            fill = inproj_steps([row_block(i) for i in range(n_prep)], 0, cb)
        _interleave([prepare_steps(bi), fill])
    next_proj = next_inproj_steps()
    next_steps_per_block = -(-len(range(0, N_COLS_B, PROJ_STEP_COLS)) // n_blocks)
    for bi in range(n_blocks):
        group = [apply_steps(bi)]
        if bi > 0:
            group.append(post_steps(bi - 1))
        group += [gmlp_steps(bi), _take(next_proj, next_steps_per_block)]
        _interleave(group)
    _interleave([post_steps(n_blocks - 1), next_proj])

    p_scr[0:top, :] = p_scr[tile:tile + top, :]


def _wide_masks():
    t = np.arange(SCAN_L)[:, None]
    s = (np.arange(MXU_COLS) % SCAN_L)[None, :]
    same = lambda n: (t // n) == (s // n)
    tri = [s < t, s <= t]
    levels = [t == s, same(8), same(16) & ~same(8), same(32) & ~same(16), same(64) & ~same(32)]
    return jnp.asarray(np.stack(tri), F32), jnp.asarray(np.stack(levels), BF16)


def _const_spec(shape):
    zeros = (0,) * len(shape)
    return pl.BlockSpec(shape, lambda b, t: zeros)


def kernel(x, c, w_ada, b_ada, w_in, mu_b, ln_v_g, ln_v_b, w_spatial, b_spatial, w0, w_up, a0,
           a_up, k_k, k_a, r_k, gn_g, gn_b, w_out, ln_g, ln_b):
    batch, seq, d_model = x.shape
    assert d_model == D_MODEL and w_in.shape == (D_MODEL, N_COLS)
    assert seq % TILE_T == 0 and TILE_T % CHUNK_A == 0 and N_CHUNKS % PREP_CHUNKS == 0
    assert N_CHUNKS // PREP_CHUNKS >= 2

    mod = pl.pallas_call(
        _ada_kernel,
        out_shape=jax.ShapeDtypeStruct((batch, 3 * d_model), F32),
        name="adaln_mod",
    )(c, w_ada, b_ada.reshape(1, -1))
    mod = mod.reshape(batch, 3, d_model)

    named = dict(mu_b=mu_b, ln_v_g=ln_v_g, ln_v_b=ln_v_b, w0=w0, a0=a0, k_k=k_k, k_a=k_a, r_k=r_k,
                 gn_g=gn_g, gn_b=gn_b, ln_g=ln_g, ln_b=ln_b)
    rows = jnp.concatenate([named[name].reshape(-1).astype(F32) for name, _ in ROW_PARAMS])
    rows = rows.reshape(1, -1)
    lane_id = np.arange(LANES)
    same_head = (lane_id[:, None] // HEAD) == (lane_id[None, :] // HEAD)
    gsum = jnp.asarray(same_head, BF16)
    gavg = jnp.asarray(same_head / HEAD, BF16)
    tok = np.arange(PREP_ROWS)
    tri = jnp.asarray((tok[None, :] <= tok[:, None])
                      & (tok[None, :] // SCAN_L == tok[:, None] // SCAN_L), BF16)
    zeros_lr = jnp.zeros((LOW_RANK, D_B), F32)
    w_lr = jnp.concatenate([jnp.concatenate([w_up, zeros_lr], axis=1),
                            jnp.concatenate([zeros_lr, a_up], axis=1)], axis=0).astype(BF16)
    bias_a = jnp.repeat(b_spatial.T, HEAD, axis=1)

    tri_masks, level_masks = _wide_masks()
    tiles = seq // TILE_T

    def next_block(b, t):
        s = jnp.minimum(b * tiles + t + 1, batch * tiles - 1)
        return s // tiles, (s % tiles) * (TILE_T // PREP_ROWS), 0

    operands = [
        (x, pl.BlockSpec((None, TILE_T, d_model), lambda b, t: (b, t, 0))),
        (mod, pl.BlockSpec((None, 3, d_model), lambda b, t: (b, 0, 0))),
        (x, pl.BlockSpec((None, PREP_ROWS, d_model), next_block)),
        (mod, pl.BlockSpec((None, 3, d_model), lambda b, t: (next_block(b, t)[0], 0, 0))),
        (w_in.astype(BF16), _const_spec((d_model, N_COLS))),
        (rows, _const_spec(rows.shape)),
        (w_spatial, _const_spec(w_spatial.shape)),
        (bias_a, _const_spec((CHUNK_A, D_A))),
        (w_lr, _const_spec((2 * LOW_RANK, 2 * D_B))),
        (w_out.astype(BF16), _const_spec((d_model, d_model))),
        (gsum, _const_spec((LANES, LANES))),
        (gavg, _const_spec((LANES, LANES))),
        (tri, _const_spec((PREP_ROWS, PREP_ROWS))),
        (tri_masks, _const_spec(tri_masks.shape)),
        (level_masks, _const_spec(level_masks.shape)),
    ]
    n_slots = N_CHUNKS * N_PAIRS
    return pl.pallas_call(
        _layer_kernel,
        grid=(batch, seq // TILE_T),
        in_specs=[spec for _, spec in operands],
        out_specs=pl.BlockSpec((None, TILE_T, d_model), lambda b, t: (b, t, 0)),
        out_shape=jax.ShapeDtypeStruct(x.shape, x.dtype),
        scratch_shapes=[
            pltpu.VMEM((TILE_T, d_model), BF16),
            pltpu.VMEM((TILE_T + SUBLANES, N_COLS), F32),
            pltpu.VMEM((PREP_ROWS, N_COLS_B), F32),
            pltpu.VMEM((TILE_T, d_model), BF16),
            pltpu.VMEM((N_PAIRS, LANES, LANES), F32),
            pltpu.VMEM((n_slots, 2 * SCAN_L, LANES), BF16),
            pltpu.VMEM((n_slots, SCAN_L, LANES), F32),
            pltpu.VMEM((n_slots, SCAN_L, 2 * PAIR_ROWS), BF16),
            pltpu.VMEM((n_slots, PAIR_ROWS, LANES), BF16),
            pltpu.VMEM((n_slots, 2 * PAIR_ROWS, LANES), BF16),
            pltpu.VMEM((N_CHUNKS, SUBLANES, D_B), F32),
            pltpu.VMEM((TILE_T, D_B), F32),
            pltpu.VMEM((TILE_T, D_B), F32),
            pltpu.VMEM((TILE_T, D_B), F32),
        ],
        compiler_params=pltpu.CompilerParams(
            dimension_semantics=("arbitrary", "arbitrary"),
            vmem_limit_bytes=VMEM_LIMIT_BYTES),
        name="hybrid_layer",
    )(*[arr for arr, _ in operands])
```

```python
import math

import jax
import jax.numpy as jnp
import numpy as np
from jax import lax
from jax.experimental import pallas as pl
from jax.experimental.pallas import tpu as pltpu

F32 = jnp.float32
BF16 = jnp.bfloat16

D_MODEL = 1024
D_A = 512
D_B = 512
HEAD = 64
CHUNK_A = 128
LOW_RANK = 64
N_COLS_A = 3 * D_A
N_COLS_B = 4 * D_B + 2 * LOW_RANK
N_COLS = N_COLS_A + N_COLS_B
LN_EPS = 1e-5
GN_EPS = 64e-5
ALPHA = 2.0 ** 0.25
LOG_DECAY_SCALE = -math.exp(-0.5)

LANES = 128
SUBLANES = 8
MXU_COLS = 256
PROJ_STEP_COLS = MXU_COLS
N_PAIRS = D_B // LANES
N_GROUPS = D_B // MXU_COLS
HEADS_PER_GROUP = MXU_COLS // HEAD
SCAN_L = 64
PAIR_ROWS = 2 * SCAN_L
TILE_T = 512
N_CHUNKS = TILE_T // SCAN_L
PREP_CHUNKS = 4
PREP_ROWS = PREP_CHUNKS * SCAN_L
CHUNKS_PER_BLOCK = CHUNK_A // SCAN_L
VMEM_LIMIT_BYTES = 56 * 1024 * 1024

ROW_PARAMS = (("mu_b", N_COLS_B), ("ln_v_g", D_A), ("ln_v_b", D_A), ("w0", D_B), ("a0", D_B),
              ("k_k", D_B), ("k_a", D_B), ("r_k", D_B), ("gn_g", D_B), ("gn_b", D_B),
              ("ln_g", D_MODEL), ("ln_b", D_MODEL))
ROW_OFFSETS = {}
for _name, _width in ROW_PARAMS:
    ROW_OFFSETS[_name] = (sum(w for _, w in ROW_PARAMS[:len(ROW_OFFSETS)]), _width)


def _bf(x):
    return x.astype(BF16)


def _mm(a, b):
    return jnp.dot(a, b, preferred_element_type=F32)


def _mm_nt(a, b):
    return lax.dot_general(a, b, (((1,), (1,)), ((), ())), preferred_element_type=F32)


def _mm_tn(a, b):
    return lax.dot_general(a, b, (((0,), (0,)), ((), ())), preferred_element_type=F32)


def _aligned(index, multiple):
    return index if isinstance(index, int) else pl.multiple_of(index, multiple)


def _interleave(step_generators):
    pending = list(step_generators)
    while pending:
        for gen in list(pending):
            try:
                next(gen)
            except StopIteration:
                pending.remove(gen)


def _take(gen, count):
    for _ in range(count):
        try:
            next(gen)
        except StopIteration:
            return
        yield


def _lock(fn, *lists):
    return [fn(*args) for args in zip(*lists)]


def _split_bf16(x, terms):
    parts = []
    rem = x
    for _ in range(terms):
        hi = _bf(rem)
        parts.append(hi)
        rem = rem - hi.astype(F32)
    return parts


def _dot_split(x, w_bf16, terms=2):
    stacked = jnp.concatenate(_split_bf16(x, terms), axis=0)
    res = _mm(stacked, w_bf16)
    rows = x.shape[0]
    out = res[0:rows]
    for i in range(1, terms):
        out = out + res[i * rows:(i + 1) * rows]
    return out


def _group_sum(x, gmat, terms):
    rows = x.shape[0]
    n = x.shape[1] // LANES
    xs = jnp.concatenate([x[:, LANES * j:LANES * (j + 1)] for j in range(n)], axis=0)
    s = _dot_split(xs, gmat, terms=terms)
    return jnp.concatenate([s[rows * j:rows * (j + 1)] for j in range(n)], axis=1)


def _sigmoid(x):
    return 1.0 / (1.0 + jnp.exp(-x))


def _silu(x):
    return x * _sigmoid(x)


def _gelu_tanh(x):
    c = math.sqrt(2.0 / math.pi)
    return 0.5 * x * (1.0 + jnp.tanh(c * (x + 0.044715 * (x * x * x))))


def _block_diag(y, head_masks):
    zero = jnp.zeros((y.shape[0], LANES), BF16)
    rows = []
    for h in range(HEADS_PER_GROUP):
        tile = h // 2
        kept = y[:, LANES * tile:LANES * (tile + 1)] * head_masks[h % 2]
        rows.append(jnp.concatenate([kept if t == tile else zero
                                     for t in range(MXU_COLS // LANES)], axis=1))
    return jnp.concatenate(rows, axis=0)


def _unit_lower_inverse_steps(a_abs, eye, m8, offs, bd_mask):
    bd = lambda y: _block_diag(y, bd_mask)
    a8 = _lock(lambda a: a * m8, a_abs)
    a2 = _lock(lambda a: _bf(_mm(a, bd(a))), a8)
    yield
    p1 = _lock(lambda a: a + eye, a8)
    both = _lock(lambda a, p: _mm(jnp.concatenate([a, p], axis=0), bd(a)), a2, p1)
    yield
    a4 = _lock(lambda m: _bf(m[0:SCAN_L]), both)
    p2 = _lock(lambda p, m: p.astype(F32) + m[SCAN_L:], p1, both)
    t = _lock(lambda p, a: _bf(p + _mm(_bf(p), bd(a))), p2, a4)
    yield
    for off in offs:
        to = _lock(lambda x, a: _bf(_mm(x, bd(a * off))), t, a_abs)
        yield
        t = _lock(lambda x, xo: x + _bf(_mm(xo, bd(x))), t, to)
        yield
    return t


def _scan_prepare_steps(r, k, v, kk, bb, g, lw, consts, bd_mask):
    tri_strict, tri_incl, eye, m8, *offs = consts
    bd = lambda y: _block_diag(y, bd_mask)

    def factors(r, k, v, kk, bb, g, lw):
        e_g = jnp.exp(g)
        e_ge = jnp.exp(g - lw)
        e_ng = jnp.exp(-g)
        e_gl = jnp.exp(g[SCAN_L - 1:SCAN_L, :] - g)
        return (_bf(-(kk * e_ge)), _bf(r * e_g), _bf(v), _bf(bb * e_ng), _bf(k * e_ng),
                _bf(bb * e_gl), _bf(k * e_gl))

    xa, xr, vb, bh, kh, bbar, kbar = zip(*_lock(factors, r, k, v, kk, bb, g, lw))
    lhs = _lock(lambda a, b: jnp.concatenate([a, b], axis=0), xa, xr)
    aa_b = _lock(lambda x, y: _mm_nt(x, bd(y)), lhs, bh)
    yield
    aa_k = _lock(lambda x, y: _mm_nt(x, bd(y)), lhs, kh)
    yield
    a_ab = _lock(lambda m: _bf(jnp.where(tri_strict != 0.0, m[0:SCAN_L], 0.0)), aa_b)
    a_rb = _lock(lambda m: _bf(jnp.where(tri_incl != 0.0, m[SCAN_L:], 0.0)), aa_b)
    a_ak = _lock(lambda m: _bf(jnp.where(tri_strict != 0.0, m[0:SCAN_L], 0.0)), aa_k)
    a_rk = _lock(lambda m: _bf(jnp.where(tri_incl != 0.0, m[SCAN_L:], 0.0)), aa_k)
    t_inv = yield from _unit_lower_inverse_steps(a_ab, eye, m8, offs, bd_mask)
    akv = _lock(lambda a, y: _bf(_mm(a, bd(y))), a_ak, vb)
    yield
    q = _lock(lambda t, y: _bf(_mm(t, bd(y))), t_inv, xa)
    yield
    z = _lock(lambda t, y: _mm(t, bd(y)), t_inv, akv)
    yield
    return q, list(xr), z, a_rb, a_rk, list(vb), list(bbar), list(kbar)


def _ada_kernel(c_ref, w_ref, b_ref, o_ref):
    s_hi, s_lo = _split_bf16(_silu(c_ref[...]), 2)
    w_hi, w_lo = _split_bf16(w_ref[...], 2)
    acc = _mm(s_hi, w_hi) + _mm(s_hi, w_lo) + _mm(s_lo, w_hi)
    o_ref[...] = acc + b_ref[...]


def _layer_kernel(x_ref, mod_ref, xn_ref, modn_ref,
                  win_ref, rows_ref, ws_ref, bs_ref, wlr_ref, wout_ref,
                  gsum_ref, gavg_ref, tri_ref, cm_ref, cmb_ref,
                  out_ref,
                  h_scr, p_scr, pnext_scr, cat_scr, state_scr,
                  qx_scr, z_scr, ar_scr, vp_scr, bk_scr,
                  dec_scr, y_scr, bonus_scr, sgate_scr):
    t_idx = pl.program_id(1)
    tile = x_ref.shape[0]
    top = SUBLANES

    @pl.when(t_idx == 0)
    def _():
        state_scr[...] = jnp.zeros_like(state_scr)
        p_scr[0:top, :] = jnp.zeros((top, N_COLS), F32)

    def row(name):
        offset, width = ROW_OFFSETS[name]
        return rows_ref[:, offset:offset + width]

    shift = mod_ref[0:1, :]
    scale = mod_ref[1:2, :]
    gate = mod_ref[2:3, :]

    h_scr[...] = _bf(x_ref[...] * (1.0 + scale) + shift)

    @pl.when((pl.program_id(0) == 0) & (t_idx == 0))
    def _():
        pnext_scr[...] = _mm(h_scr[0:PREP_ROWS, :], win_ref[:, N_COLS_A:])

    def inproj_steps(row_blocks, col0, col1):
        for row0, nrows in row_blocks:
            hb = h_scr[row0:row0 + nrows, :]
            for c in range(col0, col1, PROJ_STEP_COLS):
                ce = min(c + PROJ_STEP_COLS, col1)
                p_scr[top + row0:top + row0 + nrows, c:ce] = _mm(hb, win_ref[:, c:ce])
                yield

    gsum = gsum_ref[...]
    gavg = gavg_ref[...]

    row_a = lax.broadcasted_iota(jnp.int32, (CHUNK_A, CHUNK_A), 0)
    col_a = lax.broadcasted_iota(jnp.int32, (CHUNK_A, CHUNK_A), 1)
    causal = col_a <= row_a
    lane_a = lax.broadcasted_iota(jnp.int32, (CHUNK_A, LANES), 1)
    first_a = lane_a < HEAD

    def gmlp_steps(ci):
        r0 = _aligned(top + ci * CHUNK_A, SUBLANES)
        rows = pl.ds(r0, CHUNK_A)
        v = _gelu_tanh(p_scr[rows, D_A:2 * D_A])
        mean = _group_sum(v, gavg, 2)
        yield
        vc = v - mean
        var = _group_sum(vc * vc, gavg, 1)
        yield
        v = vc * lax.rsqrt(var + LN_EPS) * row("ln_v_g") + row("ln_v_b")
        mixed = []
        for j in range(D_A // LANES):
            vj = _bf(v[:, LANES * j:LANES * (j + 1)])
            w_even = _bf(jnp.where(causal, ws_ref[2 * j], 0.0))
            w_odd = _bf(jnp.where(causal, ws_ref[2 * j + 1], 0.0))
            mixed.append(jnp.where(first_a, _mm(w_even, vj), _mm(w_odd, vj)))
        yield
        mix = jnp.concatenate(mixed, axis=1) + bs_ref[...]
        u = _gelu_tanh(p_scr[rows, 0:D_A])
        out_a = u * mix * _silu(p_scr[rows, 2 * D_A:3 * D_A])
        o0 = _aligned(ci * CHUNK_A, CHUNK_A)
        cat_scr[pl.ds(o0, CHUNK_A), 0:D_A] = _bf(out_a)

    consts = tuple(cm_ref[i] for i in range(2)) + tuple(cmb_ref[i] for i in range(5))
    row_b = lax.broadcasted_iota(jnp.int32, (PREP_ROWS, N_COLS_B), 0)
    lane_b = lax.broadcasted_iota(jnp.int32, (PREP_ROWS, LANES), 1)
    lane_s = lax.broadcasted_iota(jnp.int32, (SCAN_L, LANES), 1)
    first_f = lane_s < HEAD
    first_b = jnp.where(first_f, 1.0, 0.0).astype(BF16)
    second_b = jnp.where(first_f, 0.0, 1.0).astype(BF16)
    bd_mask = (first_b, second_b)
    cb = N_COLS_A

    def stack2b(zz):
        return jnp.concatenate([zz * first_b, zz * second_b], axis=0)

    def prepare_steps(bi):
        r0 = top + bi * PREP_ROWS
        if bi == 0:
            cur = pnext_scr[...]
            before = p_scr[0:top, cb:]
        else:
            cur = p_scr[r0:r0 + PREP_ROWS, cb:]
            before = (pnext_scr[PREP_ROWS - SUBLANES:, :] if bi == 1
                      else p_scr[r0 - SUBLANES:r0, cb:])
        prev = jnp.where(row_b == 0, before[SUBLANES - 1:SUBLANES, :], pltpu.roll(cur, 1, axis=0))
        ps = cur + row("mu_b") * (prev - cur)
        r = ps[:, 0:D_B]
        k = ps[:, D_B:2 * D_B]
        v = ps[:, 2 * D_B:3 * D_B]
        gate_b = ps[:, 3 * D_B:4 * D_B]
        low = ps[:, 4 * D_B:]
        low = jnp.where(lane_b < LOW_RANK, jnp.tanh(low), low)
        up = _mm(_bf(low), wlr_ref[...])
        yield
        lw = LOG_DECAY_SCALE * _sigmoid(row("w0") + up[:, 0:D_B])
        ah = _sigmoid(row("a0") + up[:, D_B:])
        g = _mm(tri_ref[...], jnp.concatenate(_split_bf16(lw, 2), axis=1))
        g = g[:, 0:D_B] + g[:, D_B:]
        kk = k * row("k_k")
        kk = kk * lax.rsqrt(_group_sum(kk * kk, gsum, 1) + 1e-12)
        yield
        k = k * (1.0 + (ah - 1.0) * row("k_a"))
        bb = kk * ah

        o0 = bi * PREP_ROWS
        bonus_scr[o0:o0 + PREP_ROWS, :] = _group_sum(r * k * row("r_k"), gsum, 1) * v
        sgate_scr[o0:o0 + PREP_ROWS, :] = _silu(gate_b)
        yield

        problems = [(c, gi) for c in range(PREP_CHUNKS) for gi in range(N_GROUPS)]
        cut = lambda zz: [zz[SCAN_L * c:SCAN_L * (c + 1), MXU_COLS * gi:MXU_COLS * (gi + 1)]
                          for c, gi in problems]
        for c in range(PREP_CHUNKS):
            last = g[SCAN_L * (c + 1) - 1:SCAN_L * (c + 1), :]
            dec_scr[bi * PREP_CHUNKS + c] = jnp.broadcast_to(jnp.exp(last), (SUBLANES, D_B))

        q, xr, z, a_rb, a_rk, vb, bbar, kbar = yield from _scan_prepare_steps(
            cut(r), cut(k), cut(v), cut(kk), cut(bb), cut(g), cut(lw), consts, bd_mask)
        for i, (c, gi) in enumerate(problems):
            for pp in range(MXU_COLS // LANES):
                slot = (bi * PREP_CHUNKS + c) * N_PAIRS + gi * (MXU_COLS // LANES) + pp
                sl = slice(LANES * pp, LANES * (pp + 1))
                qx_scr[slot] = jnp.concatenate([q[i][:, sl], xr[i][:, sl]], axis=0)
                z_scr[slot] = z[i][:, sl]
                ar_scr[slot] = jnp.concatenate([a_rb[i][:, sl], a_rk[i][:, sl]], axis=1)
                vp_scr[slot] = stack2b(vb[i][:, sl])
                bk_scr[slot] = jnp.concatenate([stack2b(bbar[i][:, sl]), stack2b(kbar[i][:, sl])],
                                               axis=0)

    def apply_steps(bi):
        states = [state_scr[j] for j in range(N_PAIRS)]
        for c in range(CHUNKS_PER_BLOCK):
            ci = bi * CHUNKS_PER_BLOCK + c
            load = lambda ref: [ref[ci * N_PAIRS + j] for j in range(N_PAIRS)]
            dec = dec_scr[ci]
            decay = [dec[0:1, LANES * j:LANES * (j + 1)] for j in range(N_PAIRS)]
            xs = _lock(lambda a, s: _mm_nt(a, _bf(s)), load(qx_scr), states)
            yield
            uv = _lock(lambda x, zz, w: jnp.concatenate([stack2b(_bf(x[0:SCAN_L] + zz)), w], axis=0),
                       xs, load(z_scr), load(vp_scr))
            states = _lock(lambda s, d, m, b: s * d + _mm_tn(m, b),
                           states, decay, uv, load(bk_scr))
            ys = _lock(lambda x, a, m: x[SCAN_L:] + _mm(a, m), xs, load(ar_scr), uv)
            yield
            y_scr[pl.ds(_aligned(ci * SCAN_L, SCAN_L), SCAN_L), :] = jnp.concatenate(ys, axis=1)
        for j in range(N_PAIRS):
            state_scr[j] = states[j]

    def post_steps(bi):
        rows = slice(bi * CHUNK_A, (bi + 1) * CHUNK_A)
        y = y_scr[rows, :]
        mean = _group_sum(y, gavg, 2)
        yield
        yc = y - mean
        var = _group_sum(yc * yc, gavg, 1)
        yield
        y = yc * lax.rsqrt(var + GN_EPS) * row("gn_g") + row("gn_b")
        cat_scr[rows, D_A:] = _bf((y + bonus_scr[rows, :]) * sgate_scr[rows, :])
        o = _mm(cat_scr[rows, :], wout_ref[...])
        yield
        res = ALPHA * x_ref[rows, :] + gate * o
        mean = jnp.mean(res, axis=-1, keepdims=True)
        rc = res - mean
        var = jnp.mean(rc * rc, axis=-1, keepdims=True)
        out_ref[rows, :] = rc * lax.rsqrt(var + LN_EPS) * row("ln_g") + row("ln_b")

    def next_inproj_steps():
        hn = _bf(xn_ref[...] * (1.0 + modn_ref[1:2, :]) + modn_ref[0:1, :])
        for c in range(0, N_COLS_B, PROJ_STEP_COLS):
            ce = min(c + PROJ_STEP_COLS, N_COLS_B)
            pnext_scr[:, c:ce] = _mm(hn, win_ref[:, cb + c:cb + ce])
            yield

    n_prep = N_CHUNKS // PREP_CHUNKS
    n_blocks = tile // CHUNK_A
    row_block = lambda i: (i * PREP_ROWS, PREP_ROWS)
    for bi in range(n_prep):
        if bi + 1 < n_prep:
            fill = inproj_steps([row_block(bi + 1)], cb, N_COLS)
        else:
            fill = inproj_steps([row_block(i) for i in range(n_prep)], 0, cb)
        _interleave([prepare_steps(bi), fill])
    next_proj = next_inproj_steps()
    next_steps_per_block = -(-len(range(0, N_COLS_B, PROJ_STEP_COLS)) // n_blocks)
    for bi in range(n_blocks):
        group = [apply_steps(bi)]
        if bi > 0:
            group.append(post_steps(bi - 1))
        group += [gmlp_steps(bi), _take(next_proj, next_steps_per_block)]
        _interleave(group)
    _interleave([post_steps(n_blocks - 1), next_proj])

    p_scr[0:top, :] = p_scr[tile:tile + top, :]


def _wide_masks():
    t = np.arange(SCAN_L)[:, None]
    s = (np.arange(MXU_COLS) % SCAN_L)[None, :]
    same = lambda n: (t // n) == (s // n)
    tri = [s < t, s <= t]
    levels = [t == s, same(8), same(16) & ~same(8), same(32) & ~same(16), same(64) & ~same(32)]
    return jnp.asarray(np.stack(tri), F32), jnp.asarray(np.stack(levels), BF16)


def _const_spec(shape):
    zeros = (0,) * len(shape)
    return pl.BlockSpec(shape, lambda b, t: zeros)


def kernel(x, c, w_ada, b_ada, w_in, mu_b, ln_v_g, ln_v_b, w_spatial, b_spatial, w0, w_up, a0,
           a_up, k_k, k_a, r_k, gn_g, gn_b, w_out, ln_g, ln_b):
    batch, seq, d_model = x.shape
    assert d_model == D_MODEL and w_in.shape == (D_MODEL, N_COLS)
    assert seq % TILE_T == 0 and TILE_T % CHUNK_A == 0 and N_CHUNKS % PREP_CHUNKS == 0
    assert N_CHUNKS // PREP_CHUNKS >= 2

    mod = pl.pallas_call(
        _ada_kernel,
        out_shape=jax.ShapeDtypeStruct((batch, 3 * d_model), F32),
        name="adaln_mod",
    )(c, w_ada, b_ada.reshape(1, -1))
    mod = mod.reshape(batch, 3, d_model)

    named = dict(mu_b=mu_b, ln_v_g=ln_v_g, ln_v_b=ln_v_b, w0=w0, a0=a0, k_k=k_k, k_a=k_a, r_k=r_k,
                 gn_g=gn_g, gn_b=gn_b, ln_g=ln_g, ln_b=ln_b)
    rows = jnp.concatenate([named[name].reshape(-1).astype(F32) for name, _ in ROW_PARAMS])
    rows = rows.reshape(1, -1)
    lane_id = np.arange(LANES)
    same_head = (lane_id[:, None] // HEAD) == (lane_id[None, :] // HEAD)
    gsum = jnp.asarray(same_head, BF16)
    gavg = jnp.asarray(same_head / HEAD, BF16)
    tok = np.arange(PREP_ROWS)
    tri = jnp.asarray((tok[None, :] <= tok[:, None])
                      & (tok[None, :] // SCAN_L == tok[:, None] // SCAN_L), BF16)
    zeros_lr = jnp.zeros((LOW_RANK, D_B), F32)
    w_lr = jnp.concatenate([jnp.concatenate([w_up, zeros_lr], axis=1),
                            jnp.concatenate([zeros_lr, a_up], axis=1)], axis=0).astype(BF16)
    bias_a = jnp.repeat(b_spatial.T, HEAD, axis=1)

    tri_masks, level_masks = _wide_masks()
    tiles = seq // TILE_T

    def next_block(b, t):
        s = jnp.minimum(b * tiles + t + 1, batch * tiles - 1)
        return s // tiles, (s % tiles) * (TILE_T // PREP_ROWS), 0

    operands = [
        (x, pl.BlockSpec((None, TILE_T, d_model), lambda b, t: (b, t, 0))),
        (mod, pl.BlockSpec((None, 3, d_model), lambda b, t: (b, 0, 0))),
        (x, pl.BlockSpec((None, PREP_ROWS, d_model), next_block)),
        (mod, pl.BlockSpec((None, 3, d_model), lambda b, t: (next_block(b, t)[0], 0, 0))),
        (w_in.astype(BF16), _const_spec((d_model, N_COLS))),
        (rows, _const_spec(rows.shape)),
        (w_spatial, _const_spec(w_spatial.shape)),
        (bias_a, _const_spec((CHUNK_A, D_A))),
        (w_lr, _const_spec((2 * LOW_RANK, 2 * D_B))),
        (w_out.astype(BF16), _const_spec((d_model, d_model))),
        (gsum, _const_spec((LANES, LANES))),
        (gavg, _const_spec((LANES, LANES))),
        (tri, _const_spec((PREP_ROWS, PREP_ROWS))),
        (tri_masks, _const_spec(tri_masks.shape)),
        (level_masks, _const_spec(level_masks.shape)),
    ]
    n_slots = N_CHUNKS * N_PAIRS
    return pl.pallas_call(
        _layer_kernel,
        grid=(batch, seq // TILE_T),
        in_specs=[spec for _, spec in operands],
        out_specs=pl.BlockSpec((None, TILE_T, d_model), lambda b, t: (b, t, 0)),
        out_shape=jax.ShapeDtypeStruct(x.shape, x.dtype),
        scratch_shapes=[
            pltpu.VMEM((TILE_T, d_model), BF16),
            pltpu.VMEM((TILE_T + SUBLANES, N_COLS), F32),
            pltpu.VMEM((PREP_ROWS, N_COLS_B), F32),
            pltpu.VMEM((TILE_T, d_model), BF16),
            pltpu.VMEM((N_PAIRS, LANES, LANES), F32),
            pltpu.VMEM((n_slots, 2 * SCAN_L, LANES), BF16),
            pltpu.VMEM((n_slots, SCAN_L, LANES), F32),
            pltpu.VMEM((n_slots, SCAN_L, 2 * PAIR_ROWS), BF16),
            pltpu.VMEM((n_slots, PAIR_ROWS, LANES), BF16),
            pltpu.VMEM((n_slots, 2 * PAIR_ROWS, LANES), BF16),
            pltpu.VMEM((N_CHUNKS, SUBLANES, D_B), F32),
            pltpu.VMEM((TILE_T, D_B), F32),
            pltpu.VMEM((TILE_T, D_B), F32),
            pltpu.VMEM((TILE_T, D_B), F32),
        ],
        compiler_params=pltpu.CompilerParams(
            dimension_semantics=("arbitrary", "arbitrary"),
            vmem_limit_bytes=VMEM_LIMIT_BYTES),
        name="hybrid_layer",
    )(*[arr for arr, _ in operands])
```

```python
import math

import jax
import jax.numpy as jnp
import numpy as np
from jax import lax
from jax.experimental import pallas as pl
from jax.experimental.pallas import tpu as pltpu

F32 = jnp.float32
BF16 = jnp.bfloat16

D_MODEL = 1024
D_A = 512
D_B = 512
HEAD = 64
CHUNK_A = 128
LOW_RANK = 64
N_COLS_A = 3 * D_A
N_COLS_B = 4 * D_B + 2 * LOW_RANK
N_COLS = N_COLS_A + N_COLS_B
LN_EPS = 1e-5
GN_EPS = 64e-5
ALPHA = 2.0 ** 0.25
LOG_DECAY_SCALE = -math.exp(-0.5)

LANES = 128
SUBLANES = 8
MXU_COLS = 256
N_PAIRS = D_B // LANES
N_GROUPS = D_B // MXU_COLS
HEADS_PER_GROUP = MXU_COLS // HEAD
SCAN_L = 64
PAIR_ROWS = 2 * SCAN_L
TILE_T = 512
N_CHUNKS = TILE_T // SCAN_L
PREP_CHUNKS = 4
PREP_ROWS = PREP_CHUNKS * SCAN_L
CHUNKS_PER_BLOCK = CHUNK_A // SCAN_L
VMEM_LIMIT_BYTES = 56 * 1024 * 1024

ROW_PARAMS = (("mu_b", N_COLS_B), ("ln_v_g", D_A), ("ln_v_b", D_A), ("w0", D_B), ("a0", D_B),
              ("k_k", D_B), ("k_a", D_B), ("r_k", D_B), ("gn_g", D_B), ("gn_b", D_B),
              ("ln_g", D_MODEL), ("ln_b", D_MODEL))
ROW_OFFSETS = {}
for _name, _width in ROW_PARAMS:
    ROW_OFFSETS[_name] = (sum(w for _, w in ROW_PARAMS[:len(ROW_OFFSETS)]), _width)


def _bf(x):
    return x.astype(BF16)


def _mm(a, b):
    return jnp.dot(a, b, preferred_element_type=F32)


def _mm_nt(a, b):
    return lax.dot_general(a, b, (((1,), (1,)), ((), ())), preferred_element_type=F32)


def _mm_tn(a, b):
    return lax.dot_general(a, b, (((0,), (0,)), ((), ())), preferred_element_type=F32)


def _aligned(index, multiple):
    return index if isinstance(index, int) else pl.multiple_of(index, multiple)


def _interleave(step_generators):
    pending = list(step_generators)
    while pending:
        for gen in list(pending):
            try:
                next(gen)
            except StopIteration:
                pending.remove(gen)


def _take(gen, count):
    for _ in range(count):
        try:
            next(gen)
        except StopIteration:
            return
        yield


def _lock(fn, *lists):
    return [fn(*args) for args in zip(*lists)]


def _split_bf16(x, terms):
    parts = []
    rem = x
    for _ in range(terms):
        hi = _bf(rem)
        parts.append(hi)
        rem = rem - hi.astype(F32)
    return parts


def _dot_split(x, w_bf16, terms=2):
    stacked = jnp.concatenate(_split_bf16(x, terms), axis=0)
    res = _mm(stacked, w_bf16)
    rows = x.shape[0]
    out = res[0:rows]
    for i in range(1, terms):
        out = out + res[i * rows:(i + 1) * rows]
    return out


def _group_sum(x, gmat, terms):
    rows = x.shape[0]
    n = x.shape[1] // LANES
    xs = jnp.concatenate([x[:, LANES * j:LANES * (j + 1)] for j in range(n)], axis=0)
    s = _dot_split(xs, gmat, terms=terms)
    return jnp.concatenate([s[rows * j:rows * (j + 1)] for j in range(n)], axis=1)


def _sigmoid(x, scale=1.0):
    return 0.5 * scale + (0.5 * scale) * jnp.tanh(0.5 * x)


def _silu(x):
    h = 0.5 * x
    return h + h * jnp.tanh(h)


def _gelu_tanh(x):
    c = math.sqrt(2.0 / math.pi)
    h = 0.5 * x
    return h + h * jnp.tanh(x * (c + (c * 0.044715) * (x * x)))


def _block_diag(y, head_masks):
    zero = jnp.zeros((y.shape[0], LANES), BF16)
    rows = []
    for h in range(HEADS_PER_GROUP):
        tile = h // 2
        kept = y[:, LANES * tile:LANES * (tile + 1)] * head_masks[h % 2]
        rows.append(jnp.concatenate([kept if t == tile else zero
                                     for t in range(MXU_COLS // LANES)], axis=1))
    return jnp.concatenate(rows, axis=0)


def _unit_lower_inverse_steps(a_abs, eye, m8, off16, off32, off64, bd_mask):
    bd = lambda y: _block_diag(y, bd_mask)
    a8 = _lock(lambda a: a * m8, a_abs)
    a8b = _lock(_bf, a8)
    a2b = _lock(lambda a: _bf(_mm(a, bd(a))), a8b)
    yield
    p1 = _lock(lambda a: eye + a, a8)
    both = _lock(lambda a, p: _mm(jnp.concatenate([a, _bf(p)], axis=0), bd(a)), a2b, p1)
    yield
    a4b = _lock(lambda m: _bf(m[0:SCAN_L]), both)
    p2 = _lock(lambda p, m: p + m[SCAN_L:], p1, both)
    t = _lock(lambda p, a: p + _mm(_bf(p), bd(a)), p2, a4b)
    yield
    for off in (off16, off32, off64):
        tb = _lock(_bf, t)
        to = _lock(lambda x, a: _bf(_mm(x, bd(_bf(a * off)))), tb, a_abs)
        yield
        t = _lock(lambda x, xo, xb: x + _mm(xo, bd(xb)), t, to, tb)
        yield
    return t


def _scan_prepare_steps(r, k, v, kk, bb, g, lw, consts, bd_mask):
    eye, tri_strict, tri_incl, m8, off16, off32, off64 = consts
    bd = lambda y: _block_diag(y, bd_mask)

    def factors(r, k, v, kk, bb, g, lw):
        e_g = jnp.exp(g)
        e_ge = jnp.exp(g - lw)
        e_ng = jnp.exp(-g)
        e_gl = jnp.exp(g[SCAN_L - 1:SCAN_L, :] - g)
        return (_bf(-(kk * e_ge)), _bf(r * e_g), _bf(v), _bf(bb * e_ng), _bf(k * e_ng),
                _bf(bb * e_gl), _bf(k * e_gl))

    xa, xr, vb, bh, kh, bbar, kbar = zip(*_lock(factors, r, k, v, kk, bb, g, lw))
    lhs = _lock(lambda a, b: jnp.concatenate([a, b], axis=0), xa, xr)
    aa_b = _lock(lambda x, y: _mm_nt(x, bd(y)), lhs, bh)
    yield
    aa_k = _lock(lambda x, y: _mm_nt(x, bd(y)), lhs, kh)
    yield
    a_ab = _lock(lambda m: jnp.where(tri_strict != 0.0, m[0:SCAN_L], 0.0), aa_b)
    a_rb = _lock(lambda m: _bf(jnp.where(tri_incl != 0.0, m[SCAN_L:], 0.0)), aa_b)
    a_ak = _lock(lambda m: _bf(jnp.where(tri_strict != 0.0, m[0:SCAN_L], 0.0)), aa_k)
    a_rk = _lock(lambda m: _bf(jnp.where(tri_incl != 0.0, m[SCAN_L:], 0.0)), aa_k)
    t_inv = yield from _unit_lower_inverse_steps(a_ab, eye, m8, off16, off32, off64, bd_mask)
    t_inv = _lock(_bf, t_inv)
    akv = _lock(lambda a, y: _bf(_mm(a, bd(y))), a_ak, vb)
    yield
    q = _lock(lambda t, y: _bf(_mm(t, bd(y))), t_inv, xa)
    yield
    z = _lock(lambda t, y: _mm(t, bd(y)), t_inv, akv)
    yield
    return q, list(xr), z, a_rb, a_rk, list(vb), list(bbar), list(kbar)


def _ada_kernel(c_ref, w_ref, b_ref, o_ref):
    s_hi, s_lo = _split_bf16(_silu(c_ref[...]), 2)
    w_hi, w_lo = _split_bf16(w_ref[...], 2)
    acc = _mm(s_hi, w_hi) + _mm(s_hi, w_lo) + _mm(s_lo, w_hi)
    o_ref[...] = acc + b_ref[...]


def _layer_kernel(x_ref, mod_ref, xn_ref, modn_ref,
                  win_ref, rows_ref, ws_ref, bs_ref, wlr_ref, wout_ref,
                  gsum_ref, gavg_ref, tri_ref, cm_ref,
                  out_ref,
                  h_scr, p_scr, pnext_scr, cat_scr, state_scr,
                  qx_scr, z_scr, ar_scr, vp_scr, bk_scr,
                  dec_scr, y_scr, bonus_scr, sgate_scr):
    t_idx = pl.program_id(1)
    tile = x_ref.shape[0]
    top = SUBLANES

    @pl.when(t_idx == 0)
    def _():
        state_scr[...] = jnp.zeros_like(state_scr)
        p_scr[0:top, :] = jnp.zeros((top, N_COLS), F32)

    def row(name):
        offset, width = ROW_OFFSETS[name]
        return rows_ref[:, offset:offset + width]

    shift = mod_ref[0:1, :]
    scale = mod_ref[1:2, :]
    gate_over_alpha = mod_ref[2:3, :] * (1.0 / ALPHA)

    h_scr[...] = _bf(x_ref[...] * (1.0 + scale) + shift)

    @pl.when((pl.program_id(0) == 0) & (t_idx == 0))
    def _():
        pnext_scr[...] = _mm(h_scr[0:PREP_ROWS, :], win_ref[:, N_COLS_A:])

    def inproj_steps(row_blocks, col0, col1):
        for row0, nrows in row_blocks:
            hb = h_scr[row0:row0 + nrows, :]
            for c in range(col0, col1, MXU_COLS):
                ce = min(c + MXU_COLS, col1)
                p_scr[top + row0:top + row0 + nrows, c:ce] = _mm(hb, win_ref[:, c:ce])
                yield

    gsum = gsum_ref[...]
    gavg = gavg_ref[...]

    row_a = lax.broadcasted_iota(jnp.int32, (CHUNK_A, CHUNK_A), 0)
    col_a = lax.broadcasted_iota(jnp.int32, (CHUNK_A, CHUNK_A), 1)
    causal = col_a <= row_a
    lane_a = lax.broadcasted_iota(jnp.int32, (CHUNK_A, LANES), 1)
    first_a = lane_a < HEAD

    def gmlp_steps(ci):
        r0 = _aligned(top + ci * CHUNK_A, SUBLANES)
        rows = pl.ds(r0, CHUNK_A)
        v = _gelu_tanh(p_scr[rows, D_A:2 * D_A])
        mean = _group_sum(v, gavg, 2)
        yield
        vc = v - mean
        var = _group_sum(vc * vc, gavg, 1)
        yield
        v = vc * lax.rsqrt(var + LN_EPS) * row("ln_v_g") + row("ln_v_b")
        mixed = []
        for j in range(D_A // LANES):
            vj = _bf(v[:, LANES * j:LANES * (j + 1)])
            w_even = _bf(jnp.where(causal, ws_ref[2 * j], 0.0))
            w_odd = _bf(jnp.where(causal, ws_ref[2 * j + 1], 0.0))
            mixed.append(jnp.where(first_a, _mm(w_even, vj), _mm(w_odd, vj)))
        yield
        mix = jnp.concatenate(mixed, axis=1) + bs_ref[...]
        u = _gelu_tanh(p_scr[rows, 0:D_A])
        out_a = u * mix * _silu(p_scr[rows, 2 * D_A:3 * D_A])
        o0 = _aligned(ci * CHUNK_A, CHUNK_A)
        cat_scr[pl.ds(o0, CHUNK_A), 0:D_A] = _bf(out_a)

    consts = tuple(cm_ref[i] for i in range(7))
    row_b = lax.broadcasted_iota(jnp.int32, (PREP_ROWS, N_COLS_B), 0)
    lane_b = lax.broadcasted_iota(jnp.int32, (PREP_ROWS, LANES), 1)
    lane_s = lax.broadcasted_iota(jnp.int32, (SCAN_L, LANES), 1)
    first_f = lane_s < HEAD
    first_b = jnp.where(first_f, 1.0, 0.0).astype(BF16)
    second_b = jnp.where(first_f, 0.0, 1.0).astype(BF16)
    bd_mask = (first_b, second_b)
    cb = N_COLS_A

    def stack2b(zz):
        return jnp.concatenate([zz * first_b, zz * second_b], axis=0)

    def prepare_steps(bi):
        r0 = top + bi * PREP_ROWS
        if bi == 0:
            cur = pnext_scr[...]
            before = p_scr[0:top, cb:]
        else:
            cur = p_scr[r0:r0 + PREP_ROWS, cb:]
            before = (pnext_scr[PREP_ROWS - SUBLANES:, :] if bi == 1
                      else p_scr[r0 - SUBLANES:r0, cb:])
        prev = jnp.where(row_b == 0, before[SUBLANES - 1:SUBLANES, :], pltpu.roll(cur, 1, axis=0))
        ps = cur + row("mu_b") * (prev - cur)
        r = ps[:, 0:D_B]
        k = ps[:, D_B:2 * D_B]
        v = ps[:, 2 * D_B:3 * D_B]
        gate_b = ps[:, 3 * D_B:4 * D_B]
        low = ps[:, 4 * D_B:]
        low = jnp.where(lane_b < LOW_RANK, jnp.tanh(low), low)
        up = _mm(_bf(low), wlr_ref[...])
        yield
        lw = _sigmoid(row("w0") + up[:, 0:D_B], LOG_DECAY_SCALE)
        ah = _sigmoid(row("a0") + up[:, D_B:])
        g = _mm(tri_ref[...], jnp.concatenate(_split_bf16(lw, 2), axis=1))
        g = g[:, 0:D_B] + g[:, D_B:]
        kk = k * row("k_k")
        kk = kk * lax.rsqrt(_group_sum(kk * kk, gsum, 1) + 1e-12)
        yield
        k = k * (1.0 + (ah - 1.0) * row("k_a"))
        bb = kk * ah

        o0 = bi * PREP_ROWS
        bonus_scr[o0:o0 + PREP_ROWS, :] = _group_sum(r * k * row("r_k"), gsum, 1) * v
        sgate_scr[o0:o0 + PREP_ROWS, :] = _silu(gate_b)
        yield

        problems = [(c, gi) for c in range(PREP_CHUNKS) for gi in range(N_GROUPS)]
        cut = lambda zz: [zz[SCAN_L * c:SCAN_L * (c + 1), MXU_COLS * gi:MXU_COLS * (gi + 1)]
                          for c, gi in problems]
        for c in range(PREP_CHUNKS):
            last = g[SCAN_L * (c + 1) - 1:SCAN_L * (c + 1), :]
            dec_scr[bi * PREP_CHUNKS + c] = jnp.broadcast_to(jnp.exp(last), (SUBLANES, D_B))

        q, xr, z, a_rb, a_rk, vb, bbar, kbar = yield from _scan_prepare_steps(
            cut(r), cut(k), cut(v), cut(kk), cut(bb), cut(g), cut(lw), consts, bd_mask)
        for i, (c, gi) in enumerate(problems):
            for pp in range(MXU_COLS // LANES):
                slot = (bi * PREP_CHUNKS + c) * N_PAIRS + gi * (MXU_COLS // LANES) + pp
                sl = slice(LANES * pp, LANES * (pp + 1))
                qx_scr[slot] = jnp.concatenate([q[i][:, sl], xr[i][:, sl]], axis=0)
                z_scr[slot] = z[i][:, sl]
                ar_scr[slot] = jnp.concatenate([a_rb[i][:, sl], a_rk[i][:, sl]], axis=1)
                vp_scr[slot] = stack2b(vb[i][:, sl])
                bk_scr[slot] = jnp.concatenate([stack2b(bbar[i][:, sl]), stack2b(kbar[i][:, sl])],
                                               axis=0)

    n_prep = N_CHUNKS // PREP_CHUNKS
    row_block = lambda i: (i * PREP_ROWS, PREP_ROWS)
    for bi in range(n_prep):
        if bi + 1 < n_prep:
            fill = inproj_steps([row_block(bi + 1)], cb, N_COLS)
        else:
            fill = inproj_steps([row_block(i) for i in range(n_prep)], 0, cb)
        _interleave([prepare_steps(bi), fill])

    def apply_steps(bi):
        states = [state_scr[j] for j in range(N_PAIRS)]
        for c in range(CHUNKS_PER_BLOCK):
            ci = bi * CHUNKS_PER_BLOCK + c
            load = lambda ref: [ref[ci * N_PAIRS + j] for j in range(N_PAIRS)]
            dec = dec_scr[ci]
            decay = [dec[0:1, LANES * j:LANES * (j + 1)] for j in range(N_PAIRS)]
            xs = _lock(lambda a, s: _mm_nt(a, _bf(s)), load(qx_scr), states)
            yield
            uv = _lock(lambda x, zz, w: jnp.concatenate([stack2b(_bf(x[0:SCAN_L] + zz)), w], axis=0),
                       xs, load(z_scr), load(vp_scr))
            states = _lock(lambda s, d, m, b: s * d + _mm_tn(m, b),
                           states, decay, uv, load(bk_scr))
            ys = _lock(lambda x, a, m: x[SCAN_L:] + _mm(a, m), xs, load(ar_scr), uv)
            yield
            y_scr[pl.ds(_aligned(ci * SCAN_L, SCAN_L), SCAN_L), :] = jnp.concatenate(ys, axis=1)
        for j in range(N_PAIRS):
            state_scr[j] = states[j]

    def post_steps(bi):
        rows = slice(bi * CHUNK_A, (bi + 1) * CHUNK_A)
        y = y_scr[rows, :]
        mean = _group_sum(y, gavg, 2)
        yield
        yc = y - mean
        var = _group_sum(yc * yc, gavg, 1)
        yield
        y = yc * lax.rsqrt(var + GN_EPS) * row("gn_g") + row("gn_b")
        cat_scr[rows, D_A:] = _bf((y + bonus_scr[rows, :]) * sgate_scr[rows, :])
        o = _mm(cat_scr[rows, :], wout_ref[...])
        yield
        res = x_ref[rows, :] + gate_over_alpha * o
        mean = jnp.mean(res, axis=-1, keepdims=True)
        rc = res - mean
        var = jnp.mean(rc * rc, axis=-1, keepdims=True)
        out_ref[rows, :] = rc * lax.rsqrt(var + LN_EPS / ALPHA ** 2) * row("ln_g") + row("ln_b")

    def next_inproj_steps():
        hn = _bf(xn_ref[...] * (1.0 + modn_ref[1:2, :]) + modn_ref[0:1, :])
        for c in range(0, N_COLS_B, MXU_COLS):
            ce = min(c + MXU_COLS, N_COLS_B)
            pnext_scr[:, c:ce] = _mm(hn, win_ref[:, cb + c:cb + ce])
            yield

    n_blocks = tile // CHUNK_A
    next_proj = next_inproj_steps()
    next_steps_per_block = -(-len(range(0, N_COLS_B, MXU_COLS)) // n_blocks)
    for bi in range(n_blocks):
        group = [apply_steps(bi)]
        if bi > 0:
            group.append(post_steps(bi - 1))
        group += [gmlp_steps(bi), _take(next_proj, next_steps_per_block)]
        _interleave(group)
    _interleave([post_steps(n_blocks - 1), next_proj])

    p_scr[0:top, :] = p_scr[tile:tile + top, :]


def _wide_masks():
    t = np.arange(SCAN_L)[:, None]
    s = (np.arange(MXU_COLS) % SCAN_L)[None, :]
    same = lambda n: (t // n) == (s // n)
    masks = [t == s, s < t, s <= t, same(8),
             same(16) & ~same(8), same(32) & ~same(16), same(64) & ~same(32)]
    return jnp.asarray(np.stack(masks), F32)


def _const_spec(shape):
    zeros = (0,) * len(shape)
    return pl.BlockSpec(shape, lambda b, t: zeros)


def kernel(x, c, w_ada, b_ada, w_in, mu_b, ln_v_g, ln_v_b, w_spatial, b_spatial, w0, w_up, a0,
           a_up, k_k, k_a, r_k, gn_g, gn_b, w_out, ln_g, ln_b):
    batch, seq, d_model = x.shape
    assert d_model == D_MODEL and w_in.shape == (D_MODEL, N_COLS)
    assert seq % TILE_T == 0 and TILE_T % CHUNK_A == 0 and N_CHUNKS % PREP_CHUNKS == 0
    assert N_CHUNKS // PREP_CHUNKS >= 2

    mod = pl.pallas_call(
        _ada_kernel,
        out_shape=jax.ShapeDtypeStruct((batch, 3 * d_model), F32),
        name="adaln_mod",
    )(c, w_ada, b_ada.reshape(1, -1))
    mod = mod.reshape(batch, 3, d_model)

    named = dict(mu_b=mu_b, ln_v_g=ln_v_g, ln_v_b=ln_v_b, w0=w0, a0=a0, k_k=k_k, k_a=k_a, r_k=r_k,
                 gn_g=gn_g, gn_b=gn_b, ln_g=ln_g, ln_b=ln_b)
    rows = jnp.concatenate([named[name].reshape(-1).astype(F32) for name, _ in ROW_PARAMS])
    rows = rows.reshape(1, -1)
    lane_id = np.arange(LANES)
    same_head = (lane_id[:, None] // HEAD) == (lane_id[None, :] // HEAD)
    gsum = jnp.asarray(same_head, BF16)
    gavg = jnp.asarray(same_head / HEAD, BF16)
    tok = np.arange(PREP_ROWS)
    tri = jnp.asarray((tok[None, :] <= tok[:, None])
                      & (tok[None, :] // SCAN_L == tok[:, None] // SCAN_L), BF16)
    zeros_lr = jnp.zeros((LOW_RANK, D_B), F32)
    w_lr = jnp.concatenate([jnp.concatenate([w_up, zeros_lr], axis=1),
                            jnp.concatenate([zeros_lr, a_up], axis=1)], axis=0).astype(BF16)
    bias_a = jnp.repeat(b_spatial.T, HEAD, axis=1)

    tiles = seq // TILE_T

    def next_block(b, t):
        s = jnp.minimum(b * tiles + t + 1, batch * tiles - 1)
        return s // tiles, (s % tiles) * (TILE_T // PREP_ROWS), 0

    operands = [
        (x, pl.BlockSpec((None, TILE_T, d_model), lambda b, t: (b, t, 0))),
        (mod, pl.BlockSpec((None, 3, d_model), lambda b, t: (b, 0, 0))),
        (x, pl.BlockSpec((None, PREP_ROWS, d_model), next_block)),
        (mod, pl.BlockSpec((None, 3, d_model), lambda b, t: (next_block(b, t)[0], 0, 0))),
        (w_in.astype(BF16), _const_spec((d_model, N_COLS))),
        (rows, _const_spec(rows.shape)),
        (w_spatial, _const_spec(w_spatial.shape)),
        (bias_a, _const_spec((CHUNK_A, D_A))),
        (w_lr, _const_spec((2 * LOW_RANK, 2 * D_B))),
        (w_out.astype(BF16), _const_spec((d_model, d_model))),
        (gsum, _const_spec((LANES, LANES))),
        (gavg, _const_spec((LANES, LANES))),
        (tri, _const_spec((PREP_ROWS, PREP_ROWS))),
        (_wide_masks(), _const_spec((7, SCAN_L, MXU_COLS))),
    ]
    n_slots = N_CHUNKS * N_PAIRS
    return pl.pallas_call(
        _layer_kernel,
        grid=(batch, seq // TILE_T),
        in_specs=[spec for _, spec in operands],
        out_specs=pl.BlockSpec((None, TILE_T, d_model), lambda b, t: (b, t, 0)),
        out_shape=jax.ShapeDtypeStruct(x.shape, x.dtype),
        scratch_shapes=[
            pltpu.VMEM((TILE_T, d_model), BF16),
            pltpu.VMEM((TILE_T + SUBLANES, N_COLS), F32),
            pltpu.VMEM((PREP_ROWS, N_COLS_B), F32),
            pltpu.VMEM((TILE_T, d_model), BF16),
            pltpu.VMEM((N_PAIRS, LANES, LANES), F32),
            pltpu.VMEM((n_slots, 2 * SCAN_L, LANES), BF16),
            pltpu.VMEM((n_slots, SCAN_L, LANES), F32),
            pltpu.VMEM((n_slots, SCAN_L, 2 * PAIR_ROWS), BF16),
            pltpu.VMEM((n_slots, PAIR_ROWS, LANES), BF16),
            pltpu.VMEM((n_slots, 2 * PAIR_ROWS, LANES), BF16),
            pltpu.VMEM((N_CHUNKS, SUBLANES, D_B), F32),
            pltpu.VMEM((TILE_T, D_B), F32),
            pltpu.VMEM((TILE_T, D_B), F32),
            pltpu.VMEM((TILE_T, D_B), F32),
        ],
        compiler_params=pltpu.CompilerParams(
            dimension_semantics=("arbitrary", "arbitrary"),
            vmem_limit_bytes=VMEM_LIMIT_BYTES),
        name="hybrid_layer",
    )(*[arr for arr, _ in operands])
```

```python
import math

import jax
import jax.numpy as jnp
import numpy as np
from jax import lax
from jax.experimental import pallas as pl
from jax.experimental.pallas import tpu as pltpu

F32 = jnp.float32
BF16 = jnp.bfloat16

D_MODEL = 1024
D_A = 512
D_B = 512
HEAD = 64
CHUNK_A = 128
LOW_RANK = 64
N_COLS_A = 3 * D_A
N_COLS_B = 4 * D_B + 2 * LOW_RANK
N_COLS = N_COLS_A + N_COLS_B
LN_EPS = 1e-5
GN_EPS = 64e-5
ALPHA = 2.0 ** 0.25
LOG_DECAY_SCALE = -math.exp(-0.5)

LANES = 128
SUBLANES = 8
MXU_COLS = 256
N_PAIRS = D_B // LANES
N_GROUPS = D_B // MXU_COLS
HEADS_PER_GROUP = MXU_COLS // HEAD
SCAN_L = 64
PAIR_ROWS = 2 * SCAN_L
TILE_T = 512
N_CHUNKS = TILE_T // SCAN_L
PREP_CHUNKS = 4
PREP_ROWS = PREP_CHUNKS * SCAN_L
NEXT_TILES_PER_PREPARE = 5
NEXT_TILES_PER_BLOCK = 1
CHUNKS_PER_BLOCK = CHUNK_A // SCAN_L
VMEM_LIMIT_BYTES = 56 * 1024 * 1024

ROW_PARAMS = (("mu_b", N_COLS_B), ("ln_v_g", D_A), ("ln_v_b", D_A), ("w0", D_B), ("a0", D_B),
              ("k_k", D_B), ("k_a", D_B), ("r_k", D_B), ("gn_g", D_B), ("gn_b", D_B),
              ("ln_g", D_MODEL), ("ln_b", D_MODEL))
ROW_OFFSETS = {}
for _name, _width in ROW_PARAMS:
    ROW_OFFSETS[_name] = (sum(w for _, w in ROW_PARAMS[:len(ROW_OFFSETS)]), _width)


def _bf(x):
    return x.astype(BF16)


def _mm(a, b):
    return jnp.dot(a, b, preferred_element_type=F32)


def _mm_nt(a, b):
    return lax.dot_general(a, b, (((1,), (1,)), ((), ())), preferred_element_type=F32)


def _mm_tn(a, b):
    return lax.dot_general(a, b, (((0,), (0,)), ((), ())), preferred_element_type=F32)


def _aligned(index, multiple):
    return index if isinstance(index, int) else pl.multiple_of(index, multiple)


def _interleave(step_generators):
    pending = list(step_generators)
    while pending:
        for gen in list(pending):
            try:
                next(gen)
            except StopIteration:
                pending.remove(gen)


def _take(gen, count):
    for _ in range(count):
        try:
            next(gen)
        except StopIteration:
            return
        yield


def _lock(fn, *lists):
    return [fn(*args) for args in zip(*lists)]


def _split_bf16(x, terms):
    parts = []
    rem = x
    for _ in range(terms):
        hi = _bf(rem)
        parts.append(hi)
        rem = rem - hi.astype(F32)
    return parts


def _dot_split(x, w_bf16, terms=2):
    stacked = jnp.concatenate(_split_bf16(x, terms), axis=0)
    res = _mm(stacked, w_bf16)
    rows = x.shape[0]
    out = res[0:rows]
    for i in range(1, terms):
        out = out + res[i * rows:(i + 1) * rows]
    return out


def _group_sum(x, gmat, terms):
    rows = x.shape[0]
    n = x.shape[1] // LANES
    xs = jnp.concatenate([x[:, LANES * j:LANES * (j + 1)] for j in range(n)], axis=0)
    s = _dot_split(xs, gmat, terms=terms)
    return jnp.concatenate([s[rows * j:rows * (j + 1)] for j in range(n)], axis=1)


def _sigmoid(x, scale=1.0):
    return 0.5 * scale + (0.5 * scale) * jnp.tanh(0.5 * x)


def _silu(x):
    h = 0.5 * x
    return h + h * jnp.tanh(h)


def _gelu_tanh(x):
    c = math.sqrt(2.0 / math.pi)
    h = 0.5 * x
    return h + h * jnp.tanh(x * (c + (c * 0.044715) * (x * x)))


def _block_diag(y, head_masks):
    zero = jnp.zeros((y.shape[0], LANES), BF16)
    rows = []
    for h in range(HEADS_PER_GROUP):
        tile = h // 2
        kept = y[:, LANES * tile:LANES * (tile + 1)] * head_masks[h % 2]
        rows.append(jnp.concatenate([kept if t == tile else zero
                                     for t in range(MXU_COLS // LANES)], axis=1))
    return jnp.concatenate(rows, axis=0)


def _unit_lower_inverse_steps(a_abs, eye, m8, off16, off32, off64, bd_mask):
    bd = lambda y: _block_diag(y, bd_mask)
    a8 = _lock(lambda a: a * m8, a_abs)
    a8b = _lock(_bf, a8)
    a2b = _lock(lambda a: _bf(_mm(a, bd(a))), a8b)
    yield
    p1 = _lock(lambda a: eye + a, a8)
    both = _lock(lambda a, p: _mm(jnp.concatenate([a, _bf(p)], axis=0), bd(a)), a2b, p1)
    yield
    a4b = _lock(lambda m: _bf(m[0:SCAN_L]), both)
    p2 = _lock(lambda p, m: p + m[SCAN_L:], p1, both)
    t = _lock(lambda p, a: p + _mm(_bf(p), bd(a)), p2, a4b)
    yield
    for off in (off16, off32, off64):
        tb = _lock(_bf, t)
        to = _lock(lambda x, a: _bf(_mm(x, bd(_bf(a * off)))), tb, a_abs)
        yield
        t = _lock(lambda x, xo, xb: x + _mm(xo, bd(xb)), t, to, tb)
        yield
    return t


def _scan_prepare_steps(r, k, v, kk, bb, g, lw, consts, bd_mask):
    eye, tri_strict, tri_incl, m8, off16, off32, off64 = consts
    bd = lambda y: _block_diag(y, bd_mask)

    def factors(r, k, v, kk, bb, g, lw):
        e_g = jnp.exp(g)
        e_ge = jnp.exp(g - lw)
        e_ng = jnp.exp(-g)
        e_gl = jnp.exp(g[SCAN_L - 1:SCAN_L, :] - g)
        return (_bf(-(kk * e_ge)), _bf(r * e_g), _bf(v), _bf(bb * e_ng), _bf(k * e_ng),
                _bf(bb * e_gl), _bf(k * e_gl))

    xa, xr, vb, bh, kh, bbar, kbar = zip(*_lock(factors, r, k, v, kk, bb, g, lw))
    lhs = _lock(lambda a, b: jnp.concatenate([a, b], axis=0), xa, xr)
    aa_b = _lock(lambda x, y: _mm_nt(x, bd(y)), lhs, bh)
    yield
    aa_k = _lock(lambda x, y: _mm_nt(x, bd(y)), lhs, kh)
    yield
    a_ab = _lock(lambda m: jnp.where(tri_strict != 0.0, m[0:SCAN_L], 0.0), aa_b)
    a_rb = _lock(lambda m: _bf(jnp.where(tri_incl != 0.0, m[SCAN_L:], 0.0)), aa_b)
    a_ak = _lock(lambda m: _bf(jnp.where(tri_strict != 0.0, m[0:SCAN_L], 0.0)), aa_k)
    a_rk = _lock(lambda m: _bf(jnp.where(tri_incl != 0.0, m[SCAN_L:], 0.0)), aa_k)
    t_inv = yield from _unit_lower_inverse_steps(a_ab, eye, m8, off16, off32, off64, bd_mask)
    t_inv = _lock(_bf, t_inv)
    akv = _lock(lambda a, y: _bf(_mm(a, bd(y))), a_ak, vb)
    yield
    q = _lock(lambda t, y: _bf(_mm(t, bd(y))), t_inv, xa)
    yield
    z = _lock(lambda t, y: _mm(t, bd(y)), t_inv, akv)
    yield
    return q, list(xr), z, a_rb, a_rk, list(vb), list(bbar), list(kbar)


def _ada_kernel(c_ref, w_ref, b_ref, o_ref):
    s_hi, s_lo = _split_bf16(_silu(c_ref[...]), 2)
    w_hi, w_lo = _split_bf16(w_ref[...], 2)
    acc = _mm(s_hi, w_hi) + _mm(s_hi, w_lo) + _mm(s_lo, w_hi)
    o_ref[...] = acc + b_ref[...]


def _layer_kernel(x_ref, mod_ref, xn_ref, modn_ref,
                  win_ref, rows_ref, ws_ref, bs_ref, wlr_ref, wout_ref,
                  gsum_ref, gavg_ref, tri_ref, cm_ref,
                  out_ref,
                  h_scr, p_scr, carry_scr, cat_scr, state_scr,
                  qx_scr, z_scr, ar_scr, vp_scr, bk_scr,
                  dec_scr, y_scr, bonus_scr, sgate_scr):
    t_idx = pl.program_id(1)
    tile = x_ref.shape[0]
    cur = t_idx % 2
    nxt = 1 - cur

    @pl.when(t_idx == 0)
    def _():
        state_scr[...] = jnp.zeros_like(state_scr)
        carry_scr[...] = jnp.zeros_like(carry_scr)

    def row(name):
        offset, width = ROW_OFFSETS[name]
        return rows_ref[:, offset:offset + width]

    shift = mod_ref[0:1, :]
    scale = mod_ref[1:2, :]
    gate_over_alpha = mod_ref[2:3, :] * (1.0 / ALPHA)

    @pl.when((pl.program_id(0) == 0) & (t_idx == 0))
    def _():
        h_own = _bf(x_ref[...] * (1.0 + scale) + shift)
        p_scr[cur] = _mm(h_own, win_ref[...])

    h_scr[...] = _bf(xn_ref[...] * (1.0 + modn_ref[1:2, :]) + modn_ref[0:1, :])

    def next_inproj_steps():
        for c in range(0, N_COLS, MXU_COLS):
            ce = min(c + MXU_COLS, N_COLS)
            p_scr[nxt, :, c:ce] = _mm(h_scr[...], win_ref[:, c:ce])
            yield

    gsum = gsum_ref[...]
    gavg = gavg_ref[...]

    row_a = lax.broadcasted_iota(jnp.int32, (CHUNK_A, CHUNK_A), 0)
    col_a = lax.broadcasted_iota(jnp.int32, (CHUNK_A, CHUNK_A), 1)
    causal = col_a <= row_a
    lane_a = lax.broadcasted_iota(jnp.int32, (CHUNK_A, LANES), 1)
    first_a = lane_a < HEAD

    def gmlp_steps(ci):
        rows = slice(ci * CHUNK_A, (ci + 1) * CHUNK_A)
        v = _gelu_tanh(p_scr[cur, rows, D_A:2 * D_A])
        mean = _group_sum(v, gavg, 2)
        yield
        vc = v - mean
        var = _group_sum(vc * vc, gavg, 1)
        yield
        v = vc * lax.rsqrt(var + LN_EPS) * row("ln_v_g") + row("ln_v_b")
        mixed = []
        for j in range(D_A // LANES):
            vj = _bf(v[:, LANES * j:LANES * (j + 1)])
            w_even = _bf(jnp.where(causal, ws_ref[2 * j], 0.0))
            w_odd = _bf(jnp.where(causal, ws_ref[2 * j + 1], 0.0))
            mixed.append(jnp.where(first_a, _mm(w_even, vj), _mm(w_odd, vj)))
        yield
        mix = jnp.concatenate(mixed, axis=1) + bs_ref[...]
        u = _gelu_tanh(p_scr[cur, rows, 0:D_A])
        out_a = u * mix * _silu(p_scr[cur, rows, 2 * D_A:3 * D_A])
        cat_scr[rows, 0:D_A] = _bf(out_a)

    consts = tuple(cm_ref[i] for i in range(7))
    row_b = lax.broadcasted_iota(jnp.int32, (PREP_ROWS, N_COLS_B), 0)
    lane_b = lax.broadcasted_iota(jnp.int32, (PREP_ROWS, LANES), 1)
    lane_s = lax.broadcasted_iota(jnp.int32, (SCAN_L, LANES), 1)
    first_f = lane_s < HEAD
    first_b = jnp.where(first_f, 1.0, 0.0).astype(BF16)
    second_b = jnp.where(first_f, 0.0, 1.0).astype(BF16)
    bd_mask = (first_b, second_b)
    cb = N_COLS_A

    def stack2b(zz):
        return jnp.concatenate([zz * first_b, zz * second_b], axis=0)

    def prepare_steps(bi):
        r0 = bi * PREP_ROWS
        here = p_scr[cur, r0:r0 + PREP_ROWS, cb:]
        before = carry_scr[...] if bi == 0 else p_scr[cur, r0 - SUBLANES:r0, cb:]
        prev = jnp.where(row_b == 0, before[SUBLANES - 1:SUBLANES, :], pltpu.roll(here, 1, axis=0))
        ps = here + row("mu_b") * (prev - here)
        r = ps[:, 0:D_B]
        k = ps[:, D_B:2 * D_B]
        v = ps[:, 2 * D_B:3 * D_B]
        gate_b = ps[:, 3 * D_B:4 * D_B]
        low = ps[:, 4 * D_B:]
        low = jnp.where(lane_b < LOW_RANK, jnp.tanh(low), low)
        up = _mm(_bf(low), wlr_ref[...])
        yield
        lw = _sigmoid(row("w0") + up[:, 0:D_B], LOG_DECAY_SCALE)
        ah = _sigmoid(row("a0") + up[:, D_B:])
        g = _mm(tri_ref[...], jnp.concatenate(_split_bf16(lw, 2), axis=1))
        g = g[:, 0:D_B] + g[:, D_B:]
        kk = k * row("k_k")
        kk = kk * lax.rsqrt(_group_sum(kk * kk, gsum, 1) + 1e-12)
        yield
        k = k * (1.0 + (ah - 1.0) * row("k_a"))
        bb = kk * ah

        o0 = bi * PREP_ROWS
        bonus_scr[o0:o0 + PREP_ROWS, :] = _group_sum(r * k * row("r_k"), gsum, 1) * v
        sgate_scr[o0:o0 + PREP_ROWS, :] = _silu(gate_b)
        yield

        problems = [(c, gi) for c in range(PREP_CHUNKS) for gi in range(N_GROUPS)]
        cut = lambda zz: [zz[SCAN_L * c:SCAN_L * (c + 1), MXU_COLS * gi:MXU_COLS * (gi + 1)]
                          for c, gi in problems]
        for c in range(PREP_CHUNKS):
            last = g[SCAN_L * (c + 1) - 1:SCAN_L * (c + 1), :]
            dec_scr[bi * PREP_CHUNKS + c] = jnp.broadcast_to(jnp.exp(last), (SUBLANES, D_B))

        q, xr, z, a_rb, a_rk, vb, bbar, kbar = yield from _scan_prepare_steps(
            cut(r), cut(k), cut(v), cut(kk), cut(bb), cut(g), cut(lw), consts, bd_mask)
        for i, (c, gi) in enumerate(problems):
            for pp in range(MXU_COLS // LANES):
                slot = (bi * PREP_CHUNKS + c) * N_PAIRS + gi * (MXU_COLS // LANES) + pp
                sl = slice(LANES * pp, LANES * (pp + 1))
                qx_scr[slot] = jnp.concatenate([q[i][:, sl], xr[i][:, sl]], axis=0)
                z_scr[slot] = z[i][:, sl]
                ar_scr[slot] = jnp.concatenate([a_rb[i][:, sl], a_rk[i][:, sl]], axis=1)
                vp_scr[slot] = stack2b(vb[i][:, sl])
                bk_scr[slot] = jnp.concatenate([stack2b(bbar[i][:, sl]), stack2b(kbar[i][:, sl])],
                                               axis=0)

    def apply_steps(bi):
        states = [state_scr[j] for j in range(N_PAIRS)]
        for c in range(CHUNKS_PER_BLOCK):
            ci = bi * CHUNKS_PER_BLOCK + c
            load = lambda ref: [ref[ci * N_PAIRS + j] for j in range(N_PAIRS)]
            dec = dec_scr[ci]
            decay = [dec[0:1, LANES * j:LANES * (j + 1)] for j in range(N_PAIRS)]
            xs = _lock(lambda a, s: _mm_nt(a, _bf(s)), load(qx_scr), states)
            yield
            uv = _lock(lambda x, zz, w: jnp.concatenate([stack2b(_bf(x[0:SCAN_L] + zz)), w], axis=0),
                       xs, load(z_scr), load(vp_scr))
            states = _lock(lambda s, d, m, b: s * d + _mm_tn(m, b),
                           states, decay, uv, load(bk_scr))
            ys = _lock(lambda x, a, m: x[SCAN_L:] + _mm(a, m), xs, load(ar_scr), uv)
            yield
            y_scr[pl.ds(_aligned(ci * SCAN_L, SCAN_L), SCAN_L), :] = jnp.concatenate(ys, axis=1)
        for j in range(N_PAIRS):
            state_scr[j] = states[j]

    def post_steps(bi):
        rows = slice(bi * CHUNK_A, (bi + 1) * CHUNK_A)
        y = y_scr[rows, :]
        mean = _group_sum(y, gavg, 2)
        yield
        yc = y - mean
        var = _group_sum(yc * yc, gavg, 1)
        yield
        y = yc * lax.rsqrt(var + GN_EPS) * row("gn_g") + row("gn_b")
        cat_scr[rows, D_A:] = _bf((y + bonus_scr[rows, :]) * sgate_scr[rows, :])
        o = _mm(cat_scr[rows, :], wout_ref[...])
        yield
        res = x_ref[rows, :] + gate_over_alpha * o
        mean = jnp.mean(res, axis=-1, keepdims=True)
        rc = res - mean
        var = jnp.mean(rc * rc, axis=-1, keepdims=True)
        out_ref[rows, :] = rc * lax.rsqrt(var + LN_EPS / ALPHA ** 2) * row("ln_g") + row("ln_b")

    n_prep = N_CHUNKS // PREP_CHUNKS
    n_blocks = tile // CHUNK_A
    next_proj = next_inproj_steps()
    for bi in range(n_prep):
        _interleave([prepare_steps(bi), _take(next_proj, NEXT_TILES_PER_PREPARE)])
    for bi in range(n_blocks):
        group = [apply_steps(bi)]
        if bi > 0:
            group.append(post_steps(bi - 1))
        group += [gmlp_steps(bi), _take(next_proj, NEXT_TILES_PER_BLOCK)]
        _interleave(group)
    _interleave([post_steps(n_blocks - 1), next_proj])

    carry_scr[...] = p_scr[cur, tile - SUBLANES:tile, cb:]


def _wide_masks():
    t = np.arange(SCAN_L)[:, None]
    s = (np.arange(MXU_COLS) % SCAN_L)[None, :]
    same = lambda n: (t // n) == (s // n)
    masks = [t == s, s < t, s <= t, same(8),
             same(16) & ~same(8), same(32) & ~same(16), same(64) & ~same(32)]
    return jnp.asarray(np.stack(masks), F32)


def _const_spec(shape):
    zeros = (0,) * len(shape)
    return pl.BlockSpec(shape, lambda b, t: zeros)


def kernel(x, c, w_ada, b_ada, w_in, mu_b, ln_v_g, ln_v_b, w_spatial, b_spatial, w0, w_up, a0,
           a_up, k_k, k_a, r_k, gn_g, gn_b, w_out, ln_g, ln_b):
    batch, seq, d_model = x.shape
    assert d_model == D_MODEL and w_in.shape == (D_MODEL, N_COLS)
    assert seq % TILE_T == 0 and TILE_T % CHUNK_A == 0 and N_CHUNKS % PREP_CHUNKS == 0
    assert (seq // TILE_T) % 2 == 0

    mod = pl.pallas_call(
        _ada_kernel,
        out_shape=jax.ShapeDtypeStruct((batch, 3 * d_model), F32),
        name="adaln_mod",
    )(c, w_ada, b_ada.reshape(1, -1))
    mod = mod.reshape(batch, 3, d_model)

    named = dict(mu_b=mu_b, ln_v_g=ln_v_g, ln_v_b=ln_v_b, w0=w0, a0=a0, k_k=k_k, k_a=k_a, r_k=r_k,
                 gn_g=gn_g, gn_b=gn_b, ln_g=ln_g, ln_b=ln_b)
    rows = jnp.concatenate([named[name].reshape(-1).astype(F32) for name, _ in ROW_PARAMS])
    rows = rows.reshape(1, -1)
    lane_id = np.arange(LANES)
    same_head = (lane_id[:, None] // HEAD) == (lane_id[None, :] // HEAD)
    gsum = jnp.asarray(same_head, BF16)
    gavg = jnp.asarray(same_head / HEAD, BF16)
    tok = np.arange(PREP_ROWS)
    tri = jnp.asarray((tok[None, :] <= tok[:, None])
                      & (tok[None, :] // SCAN_L == tok[:, None] // SCAN_L), BF16)
    zeros_lr = jnp.zeros((LOW_RANK, D_B), F32)
    w_lr = jnp.concatenate([jnp.concatenate([w_up, zeros_lr], axis=1),
                            jnp.concatenate([zeros_lr, a_up], axis=1)], axis=0).astype(BF16)
    bias_a = jnp.repeat(b_spatial.T, HEAD, axis=1)

    tiles = seq // TILE_T

    def next_block(b, t):
        s = jnp.minimum(b * tiles + t + 1, batch * tiles - 1)
        return s // tiles, s % tiles, 0

    operands = [
        (x, pl.BlockSpec((None, TILE_T, d_model), lambda b, t: (b, t, 0))),
        (mod, pl.BlockSpec((None, 3, d_model), lambda b, t: (b, 0, 0))),
        (x, pl.BlockSpec((None, TILE_T, d_model), next_block)),
        (mod, pl.BlockSpec((None, 3, d_model), lambda b, t: (next_block(b, t)[0], 0, 0))),
        (w_in.astype(BF16), _const_spec((d_model, N_COLS))),
        (rows, _const_spec(rows.shape)),
        (w_spatial, _const_spec(w_spatial.shape)),
        (bias_a, _const_spec((CHUNK_A, D_A))),
        (w_lr, _const_spec((2 * LOW_RANK, 2 * D_B))),
        (w_out.astype(BF16), _const_spec((d_model, d_model))),
        (gsum, _const_spec((LANES, LANES))),
        (gavg, _const_spec((LANES, LANES))),
        (tri, _const_spec((PREP_ROWS, PREP_ROWS))),
        (_wide_masks(), _const_spec((7, SCAN_L, MXU_COLS))),
    ]
    n_slots = N_CHUNKS * N_PAIRS
    return pl.pallas_call(
        _layer_kernel,
        grid=(batch, seq // TILE_T),
        in_specs=[spec for _, spec in operands],
        out_specs=pl.BlockSpec((None, TILE_T, d_model), lambda b, t: (b, t, 0)),
        out_shape=jax.ShapeDtypeStruct(x.shape, x.dtype),
        scratch_shapes=[
            pltpu.VMEM((TILE_T, d_model), BF16),
            pltpu.VMEM((2, TILE_T, N_COLS), F32),
            pltpu.VMEM((SUBLANES, N_COLS_B), F32),
            pltpu.VMEM((TILE_T, d_model), BF16),
            pltpu.VMEM((N_PAIRS, LANES, LANES), F32),
            pltpu.VMEM((n_slots, 2 * SCAN_L, LANES), BF16),
            pltpu.VMEM((n_slots, SCAN_L, LANES), F32),
            pltpu.VMEM((n_slots, SCAN_L, 2 * PAIR_ROWS), BF16),
            pltpu.VMEM((n_slots, PAIR_ROWS, LANES), BF16),
            pltpu.VMEM((n_slots, 2 * PAIR_ROWS, LANES), BF16),
            pltpu.VMEM((N_CHUNKS, SUBLANES, D_B), F32),
            pltpu.VMEM((TILE_T, D_B), F32),
            pltpu.VMEM((TILE_T, D_B), F32),
            pltpu.VMEM((TILE_T, D_B), F32),
        ],
        compiler_params=pltpu.CompilerParams(
            dimension_semantics=("arbitrary", "arbitrary"),
            vmem_limit_bytes=VMEM_LIMIT_BYTES),
        name="hybrid_layer",
    )(*[arr for arr, _ in operands])
```

```python
import math

import jax
import jax.numpy as jnp
import numpy as np
from jax import lax
from jax.experimental import pallas as pl
from jax.experimental.pallas import tpu as pltpu

F32 = jnp.float32
BF16 = jnp.bfloat16

D_MODEL = 1024
D_A = 512
D_B = 512
HEAD = 64
CHUNK_A = 128
LOW_RANK = 64
N_COLS_A = 3 * D_A
N_COLS_B = 4 * D_B + 2 * LOW_RANK
N_COLS = N_COLS_A + N_COLS_B
LN_EPS = 1e-5
GN_EPS = 64e-5
ALPHA = 2.0 ** 0.25
LOG_DECAY_SCALE = -math.exp(-0.5)

LANES = 128
SUBLANES = 8
MXU_COLS = 256
N_PAIRS = D_B // LANES
N_GROUPS = D_B // MXU_COLS
HEADS_PER_GROUP = MXU_COLS // HEAD
SCAN_L = 64
PAIR_ROWS = 2 * SCAN_L
TILE_T = 512
N_CHUNKS = TILE_T // SCAN_L
PREP_CHUNKS = 4
PREP_ROWS = PREP_CHUNKS * SCAN_L
CHUNKS_PER_BLOCK = CHUNK_A // SCAN_L
VMEM_LIMIT_BYTES = 56 * 1024 * 1024

ROW_PARAMS = (("mu_b", N_COLS_B), ("ln_v_g", D_A), ("ln_v_b", D_A), ("w0", D_B), ("a0", D_B),
              ("k_k", D_B), ("k_a", D_B), ("r_k", D_B), ("gn_g", D_B), ("gn_b", D_B),
              ("ln_g", D_MODEL), ("ln_b", D_MODEL))
ROW_OFFSETS = {}
for _name, _width in ROW_PARAMS:
    ROW_OFFSETS[_name] = (sum(w for _, w in ROW_PARAMS[:len(ROW_OFFSETS)]), _width)


def _bf(x):
    return x.astype(BF16)


def _mm(a, b):
    return jnp.dot(a, b, preferred_element_type=F32)


def _mm_nt(a, b):
    return lax.dot_general(a, b, (((1,), (1,)), ((), ())), preferred_element_type=F32)


def _mm_tn(a, b):
    return lax.dot_general(a, b, (((0,), (0,)), ((), ())), preferred_element_type=F32)


def _aligned(index, multiple):
    return index if isinstance(index, int) else pl.multiple_of(index, multiple)


def _interleave(step_generators):
    pending = list(step_generators)
    while pending:
        for gen in list(pending):
            try:
                next(gen)
            except StopIteration:
                pending.remove(gen)


def _take(gen, count):
    for _ in range(count):
        try:
            next(gen)
        except StopIteration:
            return
        yield


def _lock(fn, *lists):
    return [fn(*args) for args in zip(*lists)]


def _split_bf16(x, terms):
    parts = []
    rem = x
    for _ in range(terms):
        hi = _bf(rem)
        parts.append(hi)
        rem = rem - hi.astype(F32)
    return parts


def _dot_split(x, w_bf16, terms=2):
    stacked = jnp.concatenate(_split_bf16(x, terms), axis=0)
    res = _mm(stacked, w_bf16)
    rows = x.shape[0]
    out = res[0:rows]
    for i in range(1, terms):
        out = out + res[i * rows:(i + 1) * rows]
    return out


def _group_sum(x, gmat, terms):
    rows = x.shape[0]
    n = x.shape[1] // LANES
    xs = jnp.concatenate([x[:, LANES * j:LANES * (j + 1)] for j in range(n)], axis=0)
    s = _dot_split(xs, gmat, terms=terms)
    return jnp.concatenate([s[rows * j:rows * (j + 1)] for j in range(n)], axis=1)


def _group_sum_lanes(x):
    lane = lax.broadcasted_iota(jnp.int32, (x.shape[0], LANES), 1)
    first = lane < HEAD
    outs = []
    for j in range(x.shape[1] // LANES):
        t = x[:, LANES * j:LANES * (j + 1)]
        s_first = jnp.sum(jnp.where(first, t, 0.0), axis=-1, keepdims=True)
        s_second = jnp.sum(jnp.where(first, 0.0, t), axis=-1, keepdims=True)
        outs.append(jnp.where(first, s_first, s_second))
    return jnp.concatenate(outs, axis=1)


def _sigmoid(x):
    return 1.0 / (1.0 + jnp.exp(-x))


def _silu(x):
    return x * _sigmoid(x)


def _gelu_tanh(x):
    c = math.sqrt(2.0 / math.pi)
    return 0.5 * x * (1.0 + jnp.tanh(c * (x + 0.044715 * (x * x * x))))


def _block_diag(y, head_masks):
    zero = jnp.zeros((y.shape[0], LANES), BF16)
    rows = []
    for h in range(HEADS_PER_GROUP):
        tile = h // 2
        kept = y[:, LANES * tile:LANES * (tile + 1)] * head_masks[h % 2]
        rows.append(jnp.concatenate([kept if t == tile else zero
                                     for t in range(MXU_COLS // LANES)], axis=1))
    return jnp.concatenate(rows, axis=0)


def _unit_lower_inverse_steps(a_abs, eye, m8, off16, off32, off64, bd_mask):
    bd = lambda y: _block_diag(y, bd_mask)
    a8 = _lock(lambda a: a * m8, a_abs)
    a8b = _lock(_bf, a8)
    a2b = _lock(lambda a: _bf(_mm(a, bd(a))), a8b)
    yield
    p1 = _lock(lambda a: eye + a, a8)
    both = _lock(lambda a, p: _mm(jnp.concatenate([a, _bf(p)], axis=0), bd(a)), a2b, p1)
    yield
    a4b = _lock(lambda m: _bf(m[0:SCAN_L]), both)
    p2 = _lock(lambda p, m: p + m[SCAN_L:], p1, both)
    t = _lock(lambda p, a: p + _mm(_bf(p), bd(a)), p2, a4b)
    yield
    for off in (off16, off32, off64):
        tb = _lock(_bf, t)
        to = _lock(lambda x, a: _bf(_mm(x, bd(_bf(a * off)))), tb, a_abs)
        yield
        t = _lock(lambda x, xo, xb: x + _mm(xo, bd(xb)), t, to, tb)
        yield
    return t


def _scan_prepare_steps(r, k, v, kk, bb, g, lw, consts, bd_mask):
    eye, tri_strict, tri_incl, m8, off16, off32, off64 = consts
    bd = lambda y: _block_diag(y, bd_mask)

    def factors(r, k, v, kk, bb, g, lw):
        e_g = jnp.exp(g)
        e_ge = jnp.exp(g - lw)
        e_ng = jnp.exp(-g)
        e_gl = jnp.exp(g[SCAN_L - 1:SCAN_L, :] - g)
        return (_bf(-(kk * e_ge)), _bf(r * e_g), _bf(v), _bf(bb * e_ng), _bf(k * e_ng),
                _bf(bb * e_gl), _bf(k * e_gl))

    xa, xr, vb, bh, kh, bbar, kbar = zip(*_lock(factors, r, k, v, kk, bb, g, lw))
    lhs = _lock(lambda a, b: jnp.concatenate([a, b], axis=0), xa, xr)
    aa_b = _lock(lambda x, y: _mm_nt(x, bd(y)), lhs, bh)
    yield
    aa_k = _lock(lambda x, y: _mm_nt(x, bd(y)), lhs, kh)
    yield
    a_ab = _lock(lambda m: jnp.where(tri_strict != 0.0, m[0:SCAN_L], 0.0), aa_b)
    a_rb = _lock(lambda m: _bf(jnp.where(tri_incl != 0.0, m[SCAN_L:], 0.0)), aa_b)
    a_ak = _lock(lambda m: _bf(jnp.where(tri_strict != 0.0, m[0:SCAN_L], 0.0)), aa_k)
    a_rk = _lock(lambda m: _bf(jnp.where(tri_incl != 0.0, m[SCAN_L:], 0.0)), aa_k)
    t_inv = yield from _unit_lower_inverse_steps(a_ab, eye, m8, off16, off32, off64, bd_mask)
    t_inv = _lock(_bf, t_inv)
    akv = _lock(lambda a, y: _bf(_mm(a, bd(y))), a_ak, vb)
    yield
    q = _lock(lambda t, y: _bf(_mm(t, bd(y))), t_inv, xa)
    yield
    z = _lock(lambda t, y: _mm(t, bd(y)), t_inv, akv)
    yield
    return q, list(xr), z, a_rb, a_rk, list(vb), list(bbar), list(kbar)


def _ada_kernel(c_ref, w_ref, b_ref, o_ref):
    s_hi, s_lo = _split_bf16(_silu(c_ref[...]), 2)
    w_hi, w_lo = _split_bf16(w_ref[...], 2)
    acc = _mm(s_hi, w_hi) + _mm(s_hi, w_lo) + _mm(s_lo, w_hi)
    o_ref[...] = acc + b_ref[...]


def _layer_kernel(x_ref, mod_ref, xn_ref, modn_ref,
                  win_ref, rows_ref, ws_ref, bs_ref, wlr_ref, wout_ref,
                  gsum_ref, gavg_ref, tri_ref, cm_ref,
                  out_ref,
                  h_scr, p_scr, pnext_scr, cat_scr, state_scr,
                  qx_scr, z_scr, ar_scr, vp_scr, bkt_scr,
                  dec_scr, y_scr, bonus_scr, sgate_scr):
    t_idx = pl.program_id(1)
    tile = x_ref.shape[0]
    top = SUBLANES

    @pl.when(t_idx == 0)
    def _():
        state_scr[...] = jnp.zeros_like(state_scr)
        p_scr[0:top, :] = jnp.zeros((top, N_COLS), F32)

    def row(name):
        offset, width = ROW_OFFSETS[name]
        return rows_ref[:, offset:offset + width]

    shift = mod_ref[0:1, :]
    scale = mod_ref[1:2, :]
    gate = mod_ref[2:3, :]

    h_scr[...] = _bf(x_ref[...] * (1.0 + scale) + shift)

    @pl.when((pl.program_id(0) == 0) & (t_idx == 0))
    def _():
        pnext_scr[...] = _mm(h_scr[0:PREP_ROWS, :], win_ref[:, N_COLS_A:])

    def inproj_steps(row_blocks, col0, col1):
        for row0, nrows in row_blocks:
            hb = h_scr[row0:row0 + nrows, :]
            for c in range(col0, col1, MXU_COLS):
                ce = min(c + MXU_COLS, col1)
                p_scr[top + row0:top + row0 + nrows, c:ce] = _mm(hb, win_ref[:, c:ce])
                yield

    gsum = gsum_ref[...]
    gavg = gavg_ref[...]

    row_a = lax.broadcasted_iota(jnp.int32, (CHUNK_A, CHUNK_A), 0)
    col_a = lax.broadcasted_iota(jnp.int32, (CHUNK_A, CHUNK_A), 1)
    causal = col_a <= row_a
    lane_a = lax.broadcasted_iota(jnp.int32, (CHUNK_A, LANES), 1)
    first_a = lane_a < HEAD

    def gmlp_steps(ci):
        r0 = _aligned(top + ci * CHUNK_A, SUBLANES)
        rows = pl.ds(r0, CHUNK_A)
        v = _gelu_tanh(p_scr[rows, D_A:2 * D_A])
        mean = _group_sum_lanes(v) * (1.0 / HEAD)
        yield
        vc = v - mean
        var = _group_sum_lanes(vc * vc) * (1.0 / HEAD)
        yield
        v = vc * lax.rsqrt(var + LN_EPS) * row("ln_v_g") + row("ln_v_b")
        mixed = []
        for j in range(D_A // LANES):
            vj = _bf(v[:, LANES * j:LANES * (j + 1)])
            w_even = _bf(jnp.where(causal, ws_ref[2 * j], 0.0))
            w_odd = _bf(jnp.where(causal, ws_ref[2 * j + 1], 0.0))
            mixed.append(jnp.where(first_a, _mm(w_even, vj), _mm(w_odd, vj)))
        yield
        mix = jnp.concatenate(mixed, axis=1) + bs_ref[...]
        u = _gelu_tanh(p_scr[rows, 0:D_A])
        out_a = u * mix * _silu(p_scr[rows, 2 * D_A:3 * D_A])
        o0 = _aligned(ci * CHUNK_A, CHUNK_A)
        cat_scr[pl.ds(o0, CHUNK_A), 0:D_A] = _bf(out_a)

    consts = tuple(cm_ref[i] for i in range(7))
    row_b = lax.broadcasted_iota(jnp.int32, (PREP_ROWS, N_COLS_B), 0)
    lane_b = lax.broadcasted_iota(jnp.int32, (PREP_ROWS, LANES), 1)
    lane_s = lax.broadcasted_iota(jnp.int32, (SCAN_L, LANES), 1)
    first_f = lane_s < HEAD
    first_b = jnp.where(first_f, 1.0, 0.0).astype(BF16)
    second_b = jnp.where(first_f, 0.0, 1.0).astype(BF16)
    bd_mask = (first_b, second_b)
    cb = N_COLS_A

    def stack2b(zz):
        return jnp.concatenate([zz * first_b, zz * second_b], axis=0)

    def prepare_steps(bi):
        r0 = top + bi * PREP_ROWS
        if bi == 0:
            cur = pnext_scr[...]
            before = p_scr[0:top, cb:]
        else:
            cur = p_scr[r0:r0 + PREP_ROWS, cb:]
            before = (pnext_scr[PREP_ROWS - SUBLANES:, :] if bi == 1
                      else p_scr[r0 - SUBLANES:r0, cb:])
        prev = jnp.where(row_b == 0, before[SUBLANES - 1:SUBLANES, :], pltpu.roll(cur, 1, axis=0))
        ps = cur + row("mu_b") * (prev - cur)
        r = ps[:, 0:D_B]
        k = ps[:, D_B:2 * D_B]
        v = ps[:, 2 * D_B:3 * D_B]
        gate_b = ps[:, 3 * D_B:4 * D_B]
        low = ps[:, 4 * D_B:]
        low = jnp.where(lane_b < LOW_RANK, jnp.tanh(low), low)
        up = _mm(_bf(low), wlr_ref[...])
        yield
        lw = LOG_DECAY_SCALE * _sigmoid(row("w0") + up[:, 0:D_B])
        ah = _sigmoid(row("a0") + up[:, D_B:])
        g = _mm(tri_ref[...], jnp.concatenate(_split_bf16(lw, 2), axis=1))
        g = g[:, 0:D_B] + g[:, D_B:]
        kk = k * row("k_k")
        kk = kk * lax.rsqrt(_group_sum_lanes(kk * kk) + 1e-12)
        yield
        k = k * (1.0 + (ah - 1.0) * row("k_a"))
        bb = kk * ah

        o0 = bi * PREP_ROWS
        bonus_scr[o0:o0 + PREP_ROWS, :] = _group_sum_lanes(r * k * row("r_k")) * v
        sgate_scr[o0:o0 + PREP_ROWS, :] = _silu(gate_b)
        yield

        problems = [(c, gi) for c in range(PREP_CHUNKS) for gi in range(N_GROUPS)]
        cut = lambda zz: [zz[SCAN_L * c:SCAN_L * (c + 1), MXU_COLS * gi:MXU_COLS * (gi + 1)]
                          for c, gi in problems]
        for c in range(PREP_CHUNKS):
            last = jnp.exp(g[SCAN_L * (c + 1) - 1:SCAN_L * (c + 1), :])
            for j in range(N_PAIRS):
                rowwise = jnp.broadcast_to(last[:, LANES * j:LANES * (j + 1)], (LANES, LANES))
                dec_scr[(bi * PREP_CHUNKS + c) * N_PAIRS + j] = rowwise.T

        q, xr, z, a_rb, a_rk, vb, bbar, kbar = yield from _scan_prepare_steps(
            cut(r), cut(k), cut(v), cut(kk), cut(bb), cut(g), cut(lw), consts, bd_mask)
        for i, (c, gi) in enumerate(problems):
            for pp in range(MXU_COLS // LANES):
                slot = (bi * PREP_CHUNKS + c) * N_PAIRS + gi * (MXU_COLS // LANES) + pp
                sl = slice(LANES * pp, LANES * (pp + 1))
                qx_scr[slot] = jnp.concatenate([q[i][:, sl], xr[i][:, sl]], axis=0)
                z_scr[slot] = z[i][:, sl]
                ar_scr[slot] = jnp.concatenate([a_rb[i][:, sl], a_rk[i][:, sl]], axis=1)
                vp_scr[slot] = stack2b(vb[i][:, sl])
                bkt_scr[slot] = jnp.concatenate([stack2b(bbar[i][:, sl]), stack2b(kbar[i][:, sl])],
                                                axis=0).T

    n_prep = N_CHUNKS // PREP_CHUNKS
    row_block = lambda i: (i * PREP_ROWS, PREP_ROWS)
    for bi in range(n_prep):
        if bi + 1 < n_prep:
            fill = inproj_steps([row_block(bi + 1)], cb, N_COLS)
        else:
            fill = inproj_steps([row_block(i) for i in range(n_prep)], 0, cb)
        _interleave([prepare_steps(bi), fill])

    def apply_steps(bi):
        states = [state_scr[j] for j in range(N_PAIRS)]
        for c in range(CHUNKS_PER_BLOCK):
            ci = bi * CHUNKS_PER_BLOCK + c
            load = lambda ref: [ref[ci * N_PAIRS + j] for j in range(N_PAIRS)]
            xs = _lock(lambda a, s: _mm(a, _bf(s)), load(qx_scr), states)
            yield
            uv = _lock(lambda x, zz, w: jnp.concatenate([stack2b(_bf(x[0:SCAN_L] + zz)), w], axis=0),
                       xs, load(z_scr), load(vp_scr))
            both = _lock(lambda b, a, m: _mm(jnp.concatenate([b, a], axis=0), m),
                         load(bkt_scr), load(ar_scr), uv)
            yield
            states = _lock(lambda s, d, m: s * d + m[0:LANES], states, load(dec_scr), both)
            ys = _lock(lambda x, m: x[SCAN_L:] + m[LANES:], xs, both)
            y_scr[pl.ds(_aligned(ci * SCAN_L, SCAN_L), SCAN_L), :] = jnp.concatenate(ys, axis=1)
        for j in range(N_PAIRS):
            state_scr[j] = states[j]

    def post_steps(bi):
        rows = slice(bi * CHUNK_A, (bi + 1) * CHUNK_A)
        y = y_scr[rows, :]
        mean = _group_sum_lanes(y) * (1.0 / HEAD)
        yield
        yc = y - mean
        var = _group_sum_lanes(yc * yc) * (1.0 / HEAD)
        yield
        y = yc * lax.rsqrt(var + GN_EPS) * row("gn_g") + row("gn_b")
        cat_scr[rows, D_A:] = _bf((y + bonus_scr[rows, :]) * sgate_scr[rows, :])
        o = _mm(cat_scr[rows, :], wout_ref[...])
        yield
        res = ALPHA * x_ref[rows, :] + gate * o
        mean = jnp.mean(res, axis=-1, keepdims=True)
        rc = res - mean
        var = jnp.mean(rc * rc, axis=-1, keepdims=True)
        out_ref[rows, :] = rc * lax.rsqrt(var + LN_EPS) * row("ln_g") + row("ln_b")

    def next_inproj_steps():
        hn = _bf(xn_ref[...] * (1.0 + modn_ref[1:2, :]) + modn_ref[0:1, :])
        for c in range(0, N_COLS_B, MXU_COLS):
            ce = min(c + MXU_COLS, N_COLS_B)
            pnext_scr[:, c:ce] = _mm(hn, win_ref[:, cb + c:cb + ce])
            yield

    n_blocks = tile // CHUNK_A
    next_proj = next_inproj_steps()
    next_steps_per_block = -(-len(range(0, N_COLS_B, MXU_COLS)) // n_blocks)
    for bi in range(n_blocks):
        group = [apply_steps(bi)]
        if bi > 0:
            group.append(post_steps(bi - 1))
        group += [gmlp_steps(bi), _take(next_proj, next_steps_per_block)]
        _interleave(group)
    _interleave([post_steps(n_blocks - 1), next_proj])

    p_scr[0:top, :] = p_scr[tile:tile + top, :]


def _wide_masks():
    t = np.arange(SCAN_L)[:, None]
    s = (np.arange(MXU_COLS) % SCAN_L)[None, :]
    same = lambda n: (t // n) == (s // n)
    masks = [t == s, s < t, s <= t, same(8),
             same(16) & ~same(8), same(32) & ~same(16), same(64) & ~same(32)]
    return jnp.asarray(np.stack(masks), F32)


def _const_spec(shape):
    zeros = (0,) * len(shape)
    return pl.BlockSpec(shape, lambda b, t: zeros)


def kernel(x, c, w_ada, b_ada, w_in, mu_b, ln_v_g, ln_v_b, w_spatial, b_spatial, w0, w_up, a0,
           a_up, k_k, k_a, r_k, gn_g, gn_b, w_out, ln_g, ln_b):
    batch, seq, d_model = x.shape
    assert d_model == D_MODEL and w_in.shape == (D_MODEL, N_COLS)
    assert seq % TILE_T == 0 and TILE_T % CHUNK_A == 0 and N_CHUNKS % PREP_CHUNKS == 0
    assert N_CHUNKS // PREP_CHUNKS >= 2

    mod = pl.pallas_call(
        _ada_kernel,
        out_shape=jax.ShapeDtypeStruct((batch, 3 * d_model), F32),
        name="adaln_mod",
    )(c, w_ada, b_ada.reshape(1, -1))
    mod = mod.reshape(batch, 3, d_model)

    named = dict(mu_b=mu_b, ln_v_g=ln_v_g, ln_v_b=ln_v_b, w0=w0, a0=a0, k_k=k_k, k_a=k_a, r_k=r_k,
                 gn_g=gn_g, gn_b=gn_b, ln_g=ln_g, ln_b=ln_b)
    rows = jnp.concatenate([named[name].reshape(-1).astype(F32) for name, _ in ROW_PARAMS])
    rows = rows.reshape(1, -1)
    lane_id = np.arange(LANES)
    same_head = (lane_id[:, None] // HEAD) == (lane_id[None, :] // HEAD)
    gsum = jnp.asarray(same_head, BF16)
    gavg = jnp.asarray(same_head / HEAD, BF16)
    tok = np.arange(PREP_ROWS)
    tri = jnp.asarray((tok[None, :] <= tok[:, None])
                      & (tok[None, :] // SCAN_L == tok[:, None] // SCAN_L), BF16)
    zeros_lr = jnp.zeros((LOW_RANK, D_B), F32)
    w_lr = jnp.concatenate([jnp.concatenate([w_up, zeros_lr], axis=1),
                            jnp.concatenate([zeros_lr, a_up], axis=1)], axis=0).astype(BF16)
    bias_a = jnp.repeat(b_spatial.T, HEAD, axis=1)

    tiles = seq // TILE_T

    def next_block(b, t):
        s = jnp.minimum(b * tiles + t + 1, batch * tiles - 1)
        return s // tiles, (s % tiles) * (TILE_T // PREP_ROWS), 0

    operands = [
        (x, pl.BlockSpec((None, TILE_T, d_model), lambda b, t: (b, t, 0))),
        (mod, pl.BlockSpec((None, 3, d_model), lambda b, t: (b, 0, 0))),
        (x, pl.BlockSpec((None, PREP_ROWS, d_model), next_block)),
        (mod, pl.BlockSpec((None, 3, d_model), lambda b, t: (next_block(b, t)[0], 0, 0))),
        (w_in.astype(BF16), _const_spec((d_model, N_COLS))),
        (rows, _const_spec(rows.shape)),
        (w_spatial, _const_spec(w_spatial.shape)),
        (bias_a, _const_spec((CHUNK_A, D_A))),
        (w_lr, _const_spec((2 * LOW_RANK, 2 * D_B))),
        (w_out.astype(BF16), _const_spec((d_model, d_model))),
        (gsum, _const_spec((LANES, LANES))),
        (gavg, _const_spec((LANES, LANES))),
        (tri, _const_spec((PREP_ROWS, PREP_ROWS))),
        (_wide_masks(), _const_spec((7, SCAN_L, MXU_COLS))),
    ]
    n_slots = N_CHUNKS * N_PAIRS
    return pl.pallas_call(
        _layer_kernel,
        grid=(batch, seq // TILE_T),
        in_specs=[spec for _, spec in operands],
        out_specs=pl.BlockSpec((None, TILE_T, d_model), lambda b, t: (b, t, 0)),
        out_shape=jax.ShapeDtypeStruct(x.shape, x.dtype),
        scratch_shapes=[
            pltpu.VMEM((TILE_T, d_model), BF16),
            pltpu.VMEM((TILE_T + SUBLANES, N_COLS), F32),
            pltpu.VMEM((PREP_ROWS, N_COLS_B), F32),
            pltpu.VMEM((TILE_T, d_model), BF16),
            pltpu.VMEM((N_PAIRS, LANES, LANES), F32),
            pltpu.VMEM((n_slots, 2 * SCAN_L, LANES), BF16),
            pltpu.VMEM((n_slots, SCAN_L, LANES), F32),
            pltpu.VMEM((n_slots, SCAN_L, 2 * PAIR_ROWS), BF16),
            pltpu.VMEM((n_slots, PAIR_ROWS, LANES), BF16),
            pltpu.VMEM((n_slots, LANES, 2 * PAIR_ROWS), BF16),
            pltpu.VMEM((n_slots, LANES, LANES), F32),
            pltpu.VMEM((TILE_T, D_B), F32),
            pltpu.VMEM((TILE_T, D_B), F32),
            pltpu.VMEM((TILE_T, D_B), F32),
        ],
        compiler_params=pltpu.CompilerParams(
            dimension_semantics=("arbitrary", "arbitrary"),
            vmem_limit_bytes=VMEM_LIMIT_BYTES),
        name="hybrid_layer",
    )(*[arr for arr, _ in operands])
```

```python
import math

import jax
import jax.numpy as jnp
import numpy as np
from jax import lax
from jax.experimental import pallas as pl
from jax.experimental.pallas import tpu as pltpu

F32 = jnp.float32
BF16 = jnp.bfloat16

D_MODEL = 1024
D_A = 512
D_B = 512
HEAD = 64
CHUNK_A = 128
LOW_RANK = 64
N_COLS_A = 3 * D_A
N_COLS_B = 4 * D_B + 2 * LOW_RANK
N_COLS = N_COLS_A + N_COLS_B
LN_EPS = 1e-5
GN_EPS = 64e-5
ALPHA = 2.0 ** 0.25
LOG_DECAY_SCALE = -math.exp(-0.5)

LANES = 128
SUBLANES = 8
MXU_COLS = 256
N_PAIRS = D_B // LANES
N_GROUPS = D_B // MXU_COLS
HEADS_PER_GROUP = MXU_COLS // HEAD
SCAN_L = 64
PAIR_ROWS = 2 * SCAN_L
TILE_T = 512
N_CHUNKS = TILE_T // SCAN_L
PREP_CHUNKS = 4
PREP_ROWS = PREP_CHUNKS * SCAN_L
CHUNKS_PER_BLOCK = CHUNK_A // SCAN_L
VMEM_LIMIT_BYTES = 56 * 1024 * 1024

ROW_PARAMS = (("mu_b", N_COLS_B), ("ln_v_g", D_A), ("ln_v_b", D_A), ("w0", D_B), ("a0", D_B),
              ("k_k", D_B), ("k_a", D_B), ("r_k", D_B), ("gn_g", D_B), ("gn_b", D_B),
              ("ln_g", D_MODEL), ("ln_b", D_MODEL))
ROW_OFFSETS = {}
for _name, _width in ROW_PARAMS:
    ROW_OFFSETS[_name] = (sum(w for _, w in ROW_PARAMS[:len(ROW_OFFSETS)]), _width)


def _bf(x):
    return x.astype(BF16)


def _mm(a, b):
    return jnp.dot(a, b, preferred_element_type=F32)


def _mm_nt(a, b):
    return lax.dot_general(a, b, (((1,), (1,)), ((), ())), preferred_element_type=F32)


def _mm_tn(a, b):
    return lax.dot_general(a, b, (((0,), (0,)), ((), ())), preferred_element_type=F32)


def _aligned(index, multiple):
    return index if isinstance(index, int) else pl.multiple_of(index, multiple)


def _interleave(step_generators):
    pending = list(step_generators)
    while pending:
        for gen in list(pending):
            try:
                next(gen)
            except StopIteration:
                pending.remove(gen)


def _take(gen, count):
    for _ in range(count):
        try:
            next(gen)
        except StopIteration:
            return
        yield


def _lock(fn, *lists):
    return [fn(*args) for args in zip(*lists)]


def _split_bf16(x, terms):
    parts = []
    rem = x
    for _ in range(terms):
        hi = _bf(rem)
        parts.append(hi)
        rem = rem - hi.astype(F32)
    return parts


def _dot_split(x, w_bf16, terms=2):
    stacked = jnp.concatenate(_split_bf16(x, terms), axis=0)
    res = _mm(stacked, w_bf16)
    rows = x.shape[0]
    out = res[0:rows]
    for i in range(1, terms):
        out = out + res[i * rows:(i + 1) * rows]
    return out


def _group_sum(x, gmat, terms):
    rows = x.shape[0]
    n = x.shape[1] // LANES
    xs = jnp.concatenate([x[:, LANES * j:LANES * (j + 1)] for j in range(n)], axis=0)
    s = _dot_split(xs, gmat, terms=terms)
    return jnp.concatenate([s[rows * j:rows * (j + 1)] for j in range(n)], axis=1)


def _group_sum_lanes(x):
    lane = lax.broadcasted_iota(jnp.int32, (x.shape[0], LANES), 1)
    first = lane < HEAD
    outs = []
    for j in range(x.shape[1] // LANES):
        t = x[:, LANES * j:LANES * (j + 1)]
        s_first = jnp.sum(jnp.where(first, t, 0.0), axis=-1, keepdims=True)
        s_second = jnp.sum(jnp.where(first, 0.0, t), axis=-1, keepdims=True)
        outs.append(jnp.where(first, s_first, s_second))
    return jnp.concatenate(outs, axis=1)


def _sigmoid(x):
    return 1.0 / (1.0 + jnp.exp(-x))


def _silu(x):
    return x * _sigmoid(x)


def _gelu_tanh(x):
    c = math.sqrt(2.0 / math.pi)
    return 0.5 * x * (1.0 + jnp.tanh(c * (x + 0.044715 * (x * x * x))))


def _block_diag(y, head_masks):
    zero = jnp.zeros((y.shape[0], LANES), BF16)
    rows = []
    for h in range(HEADS_PER_GROUP):
        tile = h // 2
        kept = y[:, LANES * tile:LANES * (tile + 1)] * head_masks[h % 2]
        rows.append(jnp.concatenate([kept if t == tile else zero
                                     for t in range(MXU_COLS // LANES)], axis=1))
    return jnp.concatenate(rows, axis=0)


def _unit_lower_inverse_steps(a_abs, eye, m8, off16, off32, off64, bd_mask):
    bd = lambda y: _block_diag(y, bd_mask)
    a8 = _lock(lambda a: a * m8, a_abs)
    a8b = _lock(_bf, a8)
    a2b = _lock(lambda a: _bf(_mm(a, bd(a))), a8b)
    yield
    p1 = _lock(lambda a: eye + a, a8)
    both = _lock(lambda a, p: _mm(jnp.concatenate([a, _bf(p)], axis=0), bd(a)), a2b, p1)
    yield
    a4b = _lock(lambda m: _bf(m[0:SCAN_L]), both)
    p2 = _lock(lambda p, m: p + m[SCAN_L:], p1, both)
    t = _lock(lambda p, a: p + _mm(_bf(p), bd(a)), p2, a4b)
    yield
    for off in (off16, off32, off64):
        tb = _lock(_bf, t)
        to = _lock(lambda x, a: _bf(_mm(x, bd(_bf(a * off)))), tb, a_abs)
        yield
        t = _lock(lambda x, xo, xb: x + _mm(xo, bd(xb)), t, to, tb)
        yield
    return t


def _scan_prepare_steps(r, k, v, kk, bb, g, lw, consts, bd_mask):
    eye, tri_strict, tri_incl, m8, off16, off32, off64 = consts
    bd = lambda y: _block_diag(y, bd_mask)

    def factors(r, k, v, kk, bb, g, lw):
        e_g = jnp.exp(g)
        e_ge = jnp.exp(g - lw)
        e_ng = jnp.exp(-g)
        e_gl = jnp.exp(g[SCAN_L - 1:SCAN_L, :] - g)
        return (_bf(-(kk * e_ge)), _bf(r * e_g), _bf(v), _bf(bb * e_ng), _bf(k * e_ng),
                _bf(bb * e_gl), _bf(k * e_gl))

    xa, xr, vb, bh, kh, bbar, kbar = zip(*_lock(factors, r, k, v, kk, bb, g, lw))
    lhs = _lock(lambda a, b: jnp.concatenate([a, b], axis=0), xa, xr)
    def pair_rows(y, sl):
        yp = y[:, sl]
        return jnp.concatenate([yp * bd_mask[0], yp * bd_mask[1]], axis=0)

    pairs = [slice(LANES * p, LANES * (p + 1)) for p in range(MXU_COLS // LANES)]
    aa = _lock(lambda x, yb, yk: [_mm_nt(x[:, sl], jnp.concatenate([pair_rows(yb, sl),
                                                                     pair_rows(yk, sl)], axis=0))
                                  for sl in pairs], lhs, bh, kh)
    yield
    aa_b = _lock(lambda parts: jnp.concatenate([m[:, 0:LANES] for m in parts], axis=1), aa)
    aa_k = _lock(lambda parts: jnp.concatenate([m[:, LANES:] for m in parts], axis=1), aa)
    a_ab = _lock(lambda m: jnp.where(tri_strict != 0.0, m[0:SCAN_L], 0.0), aa_b)
    a_rb = _lock(lambda m: _bf(jnp.where(tri_incl != 0.0, m[SCAN_L:], 0.0)), aa_b)
    a_ak = _lock(lambda m: _bf(jnp.where(tri_strict != 0.0, m[0:SCAN_L], 0.0)), aa_k)
    a_rk = _lock(lambda m: _bf(jnp.where(tri_incl != 0.0, m[SCAN_L:], 0.0)), aa_k)
    t_inv = yield from _unit_lower_inverse_steps(a_ab, eye, m8, off16, off32, off64, bd_mask)
    t_inv = _lock(_bf, t_inv)
    akv = _lock(lambda a, y: _bf(_mm(a, bd(y))), a_ak, vb)
    yield
    def pair_weights(y, sl):
        yp = y[:, sl]
        return jnp.concatenate([yp * bd_mask[0], yp * bd_mask[1]], axis=0)

    pair_slices = [slice(LANES * p, LANES * (p + 1)) for p in range(MXU_COLS // LANES)]
    qz = _lock(lambda t, a, b: [_mm(t[:, sl], jnp.concatenate([pair_weights(a, sl),
                                                                pair_weights(b, sl)], axis=1))
                                for sl in pair_slices], t_inv, xa, akv)
    yield
    q = _lock(lambda parts: _bf(jnp.concatenate([m[:, 0:LANES] for m in parts], axis=1)), qz)
    z = _lock(lambda parts: jnp.concatenate([m[:, LANES:] for m in parts], axis=1), qz)
    return q, list(xr), z, a_rb, a_rk, list(vb), list(bbar), list(kbar)


def _ada_kernel(c_ref, w_ref, b_ref, o_ref):
    s_hi, s_lo = _split_bf16(_silu(c_ref[...]), 2)
    w_hi, w_lo = _split_bf16(w_ref[...], 2)
    acc = _mm(s_hi, w_hi) + _mm(s_hi, w_lo) + _mm(s_lo, w_hi)
    o_ref[...] = acc + b_ref[...]


def _layer_kernel(x_ref, mod_ref, xn_ref, modn_ref,
                  win_ref, rows_ref, ws_ref, bs_ref, wlr_ref, wout_ref,
                  gsum_ref, gavg_ref, tri_ref, cm_ref,
                  out_ref,
                  h_scr, p_scr, pnext_scr, cat_scr, state_scr,
                  qx_scr, z_scr, ar_scr, vp_scr, bkt_scr,
                  dec_scr, y_scr, bonus_scr, sgate_scr):
    t_idx = pl.program_id(1)
    tile = x_ref.shape[0]
    top = SUBLANES

    @pl.when(t_idx == 0)
    def _():
        state_scr[...] = jnp.zeros_like(state_scr)
        p_scr[0:top, :] = jnp.zeros((top, N_COLS), F32)

    def row(name):
        offset, width = ROW_OFFSETS[name]
        return rows_ref[:, offset:offset + width]

    shift = mod_ref[0:1, :]
    scale = mod_ref[1:2, :]
    gate = mod_ref[2:3, :]

    h_scr[...] = _bf(x_ref[...] * (1.0 + scale) + shift)

    @pl.when((pl.program_id(0) == 0) & (t_idx == 0))
    def _():
        pnext_scr[...] = _mm(h_scr[0:PREP_ROWS, :], win_ref[:, N_COLS_A:])

    def inproj_steps(row_blocks, col0, col1):
        for row0, nrows in row_blocks:
            hb = h_scr[row0:row0 + nrows, :]
            for c in range(col0, col1, MXU_COLS):
                ce = min(c + MXU_COLS, col1)
                p_scr[top + row0:top + row0 + nrows, c:ce] = _mm(hb, win_ref[:, c:ce])
                yield

    gsum = gsum_ref[...]
    gavg = gavg_ref[...]

    row_a = lax.broadcasted_iota(jnp.int32, (CHUNK_A, CHUNK_A), 0)
    col_a = lax.broadcasted_iota(jnp.int32, (CHUNK_A, CHUNK_A), 1)
    causal = col_a <= row_a
    lane_a = lax.broadcasted_iota(jnp.int32, (CHUNK_A, LANES), 1)
    first_a = lane_a < HEAD

    def gmlp_steps(ci):
        r0 = _aligned(top + ci * CHUNK_A, SUBLANES)
        rows = pl.ds(r0, CHUNK_A)
        v = _gelu_tanh(p_scr[rows, D_A:2 * D_A])
        mean = _group_sum_lanes(v) * (1.0 / HEAD)
        yield
        vc = v - mean
        var = _group_sum_lanes(vc * vc) * (1.0 / HEAD)
        yield
        v = vc * lax.rsqrt(var + LN_EPS) * row("ln_v_g") + row("ln_v_b")
        mixed = []
        for j in range(D_A // LANES):
            vj = _bf(v[:, LANES * j:LANES * (j + 1)])
            w_even = _bf(jnp.where(causal, ws_ref[2 * j], 0.0))
            w_odd = _bf(jnp.where(causal, ws_ref[2 * j + 1], 0.0))
            mixed.append(jnp.where(first_a, _mm(w_even, vj), _mm(w_odd, vj)))
        yield
        mix = jnp.concatenate(mixed, axis=1) + bs_ref[...]
        u = _gelu_tanh(p_scr[rows, 0:D_A])
        out_a = u * mix * _silu(p_scr[rows, 2 * D_A:3 * D_A])
        o0 = _aligned(ci * CHUNK_A, CHUNK_A)
        cat_scr[pl.ds(o0, CHUNK_A), 0:D_A] = _bf(out_a)

    consts = tuple(cm_ref[i] for i in range(7))
    row_b = lax.broadcasted_iota(jnp.int32, (PREP_ROWS, N_COLS_B), 0)
    lane_b = lax.broadcasted_iota(jnp.int32, (PREP_ROWS, LANES), 1)
    lane_s = lax.broadcasted_iota(jnp.int32, (SCAN_L, LANES), 1)
    first_f = lane_s < HEAD
    first_b = jnp.where(first_f, 1.0, 0.0).astype(BF16)
    second_b = jnp.where(first_f, 0.0, 1.0).astype(BF16)
    bd_mask = (first_b, second_b)
    cb = N_COLS_A

    def stack2b(zz):
        return jnp.concatenate([zz * first_b, zz * second_b], axis=0)

    def prepare_steps(bi):
        r0 = top + bi * PREP_ROWS
        if bi == 0:
            cur = pnext_scr[...]
            before = p_scr[0:top, cb:]
        else:
            cur = p_scr[r0:r0 + PREP_ROWS, cb:]
            before = (pnext_scr[PREP_ROWS - SUBLANES:, :] if bi == 1
                      else p_scr[r0 - SUBLANES:r0, cb:])
        prev = jnp.where(row_b == 0, before[SUBLANES - 1:SUBLANES, :], pltpu.roll(cur, 1, axis=0))
        ps = cur + row("mu_b") * (prev - cur)
        r = ps[:, 0:D_B]
        k = ps[:, D_B:2 * D_B]
        v = ps[:, 2 * D_B:3 * D_B]
        gate_b = ps[:, 3 * D_B:4 * D_B]
        low = ps[:, 4 * D_B:]
        low = jnp.where(lane_b < LOW_RANK, jnp.tanh(low), low)
        up = _mm(_bf(low), wlr_ref[...])
        yield
        lw = LOG_DECAY_SCALE * _sigmoid(row("w0") + up[:, 0:D_B])
        ah = _sigmoid(row("a0") + up[:, D_B:])
        g = _mm(tri_ref[...], jnp.concatenate(_split_bf16(lw, 2), axis=1))
        g = g[:, 0:D_B] + g[:, D_B:]
        kk = k * row("k_k")
        kk = kk * lax.rsqrt(_group_sum_lanes(kk * kk) + 1e-12)
        yield
        k = k * (1.0 + (ah - 1.0) * row("k_a"))
        bb = kk * ah

        o0 = bi * PREP_ROWS
        bonus_scr[o0:o0 + PREP_ROWS, :] = _group_sum_lanes(r * k * row("r_k")) * v
        sgate_scr[o0:o0 + PREP_ROWS, :] = _silu(gate_b)
        yield

        problems = [(c, gi) for c in range(PREP_CHUNKS) for gi in range(N_GROUPS)]
        cut = lambda zz: [zz[SCAN_L * c:SCAN_L * (c + 1), MXU_COLS * gi:MXU_COLS * (gi + 1)]
                          for c, gi in problems]
        for c in range(PREP_CHUNKS):
            last = jnp.exp(g[SCAN_L * (c + 1) - 1:SCAN_L * (c + 1), :])
            for j in range(N_PAIRS):
                rowwise = jnp.broadcast_to(last[:, LANES * j:LANES * (j + 1)], (LANES, LANES))
                dec_scr[(bi * PREP_CHUNKS + c) * N_PAIRS + j] = rowwise.T

        q, xr, z, a_rb, a_rk, vb, bbar, kbar = yield from _scan_prepare_steps(
            cut(r), cut(k), cut(v), cut(kk), cut(bb), cut(g), cut(lw), consts, bd_mask)
        for i, (c, gi) in enumerate(problems):
            for pp in range(MXU_COLS // LANES):
                slot = (bi * PREP_CHUNKS + c) * N_PAIRS + gi * (MXU_COLS // LANES) + pp
                sl = slice(LANES * pp, LANES * (pp + 1))
                qx_scr[slot] = jnp.concatenate([q[i][:, sl], xr[i][:, sl]], axis=0)
                z_scr[slot] = z[i][:, sl]
                ar_scr[slot] = jnp.concatenate([a_rb[i][:, sl], a_rk[i][:, sl]], axis=1)
                vp_scr[slot] = stack2b(vb[i][:, sl])
                bkt_scr[slot] = jnp.concatenate([stack2b(bbar[i][:, sl]), stack2b(kbar[i][:, sl])],
                                                axis=0).T

    n_prep = N_CHUNKS // PREP_CHUNKS
    row_block = lambda i: (i * PREP_ROWS, PREP_ROWS)
    for bi in range(n_prep):
        if bi + 1 < n_prep:
            fill = inproj_steps([row_block(bi + 1)], cb, N_COLS)
        else:
            fill = inproj_steps([row_block(i) for i in range(n_prep)], 0, cb)
        _interleave([prepare_steps(bi), fill])

    def apply_steps(bi):
        states = [state_scr[j] for j in range(N_PAIRS)]
        for c in range(CHUNKS_PER_BLOCK):
            ci = bi * CHUNKS_PER_BLOCK + c
            load = lambda ref: [ref[ci * N_PAIRS + j] for j in range(N_PAIRS)]
            xs = _lock(lambda a, s: _mm(a, _bf(s)), load(qx_scr), states)
            yield
            uv = _lock(lambda x, zz, w: jnp.concatenate([stack2b(_bf(x[0:SCAN_L] + zz)), w], axis=0),
                       xs, load(z_scr), load(vp_scr))
            both = _lock(lambda b, a, m: _mm(jnp.concatenate([b, a], axis=0), m),
                         load(bkt_scr), load(ar_scr), uv)
            yield
            states = _lock(lambda s, d, m: s * d + m[0:LANES], states, load(dec_scr), both)
            ys = _lock(lambda x, m: x[SCAN_L:] + m[LANES:], xs, both)
            y_scr[pl.ds(_aligned(ci * SCAN_L, SCAN_L), SCAN_L), :] = jnp.concatenate(ys, axis=1)
        for j in range(N_PAIRS):
            state_scr[j] = states[j]

    def post_steps(bi):
        rows = slice(bi * CHUNK_A, (bi + 1) * CHUNK_A)
        y = y_scr[rows, :]
        mean = _group_sum_lanes(y) * (1.0 / HEAD)
        yield
        yc = y - mean
        var = _group_sum_lanes(yc * yc) * (1.0 / HEAD)
        yield
        y = yc * lax.rsqrt(var + GN_EPS) * row("gn_g") + row("gn_b")
        cat_scr[rows, D_A:] = _bf((y + bonus_scr[rows, :]) * sgate_scr[rows, :])
        o = _mm(cat_scr[rows, :], wout_ref[...])
        yield
        res = ALPHA * x_ref[rows, :] + gate * o
        mean = jnp.mean(res, axis=-1, keepdims=True)
        rc = res - mean
        var = jnp.mean(rc * rc, axis=-1, keepdims=True)
        out_ref[rows, :] = rc * lax.rsqrt(var + LN_EPS) * row("ln_g") + row("ln_b")

    def next_inproj_steps():
        hn = _bf(xn_ref[...] * (1.0 + modn_ref[1:2, :]) + modn_ref[0:1, :])
        for c in range(0, N_COLS_B, MXU_COLS):
            ce = min(c + MXU_COLS, N_COLS_B)
            pnext_scr[:, c:ce] = _mm(hn, win_ref[:, cb + c:cb + ce])
            yield

    n_blocks = tile // CHUNK_A
    next_proj = next_inproj_steps()
    next_steps_per_block = -(-len(range(0, N_COLS_B, MXU_COLS)) // n_blocks)
    for bi in range(n_blocks):
        group = [apply_steps(bi)]
        if bi > 0:
            group.append(post_steps(bi - 1))
        group += [gmlp_steps(bi), _take(next_proj, next_steps_per_block)]
        _interleave(group)
    _interleave([post_steps(n_blocks - 1), next_proj])

    p_scr[0:top, :] = p_scr[tile:tile + top, :]


def _wide_masks():
    t = np.arange(SCAN_L)[:, None]
    s = (np.arange(MXU_COLS) % SCAN_L)[None, :]
    same = lambda n: (t // n) == (s // n)
    masks = [t == s, s < t, s <= t, same(8),
             same(16) & ~same(8), same(32) & ~same(16), same(64) & ~same(32)]
    return jnp.asarray(np.stack(masks), F32)


def _const_spec(shape):
    zeros = (0,) * len(shape)
    return pl.BlockSpec(shape, lambda b, t: zeros)


def kernel(x, c, w_ada, b_ada, w_in, mu_b, ln_v_g, ln_v_b, w_spatial, b_spatial, w0, w_up, a0,
           a_up, k_k, k_a, r_k, gn_g, gn_b, w_out, ln_g, ln_b):
    batch, seq, d_model = x.shape
    assert d_model == D_MODEL and w_in.shape == (D_MODEL, N_COLS)
    assert seq % TILE_T == 0 and TILE_T % CHUNK_A == 0 and N_CHUNKS % PREP_CHUNKS == 0
    assert N_CHUNKS // PREP_CHUNKS >= 2

    mod = pl.pallas_call(
        _ada_kernel,
        out_shape=jax.ShapeDtypeStruct((batch, 3 * d_model), F32),
        name="adaln_mod",
    )(c, w_ada, b_ada.reshape(1, -1))
    mod = mod.reshape(batch, 3, d_model)

    named = dict(mu_b=mu_b, ln_v_g=ln_v_g, ln_v_b=ln_v_b, w0=w0, a0=a0, k_k=k_k, k_a=k_a, r_k=r_k,
                 gn_g=gn_g, gn_b=gn_b, ln_g=ln_g, ln_b=ln_b)
    rows = jnp.concatenate([named[name].reshape(-1).astype(F32) for name, _ in ROW_PARAMS])
    rows = rows.reshape(1, -1)
    lane_id = np.arange(LANES)
    same_head = (lane_id[:, None] // HEAD) == (lane_id[None, :] // HEAD)
    gsum = jnp.asarray(same_head, BF16)
    gavg = jnp.asarray(same_head / HEAD, BF16)
    tok = np.arange(PREP_ROWS)
    tri = jnp.asarray((tok[None, :] <= tok[:, None])
                      & (tok[None, :] // SCAN_L == tok[:, None] // SCAN_L), BF16)
    zeros_lr = jnp.zeros((LOW_RANK, D_B), F32)
    w_lr = jnp.concatenate([jnp.concatenate([w_up, zeros_lr], axis=1),
                            jnp.concatenate([zeros_lr, a_up], axis=1)], axis=0).astype(BF16)
    bias_a = jnp.repeat(b_spatial.T, HEAD, axis=1)

    tiles = seq // TILE_T

    def next_block(b, t):
        s = jnp.minimum(b * tiles + t + 1, batch * tiles - 1)
        return s // tiles, (s % tiles) * (TILE_T // PREP_ROWS), 0

    operands = [
        (x, pl.BlockSpec((None, TILE_T, d_model), lambda b, t: (b, t, 0))),
        (mod, pl.BlockSpec((None, 3, d_model), lambda b, t: (b, 0, 0))),
        (x, pl.BlockSpec((None, PREP_ROWS, d_model), next_block)),
        (mod, pl.BlockSpec((None, 3, d_model), lambda b, t: (next_block(b, t)[0], 0, 0))),
        (w_in.astype(BF16), _const_spec((d_model, N_COLS))),
        (rows, _const_spec(rows.shape)),
        (w_spatial, _const_spec(w_spatial.shape)),
        (bias_a, _const_spec((CHUNK_A, D_A))),
        (w_lr, _const_spec((2 * LOW_RANK, 2 * D_B))),
        (w_out.astype(BF16), _const_spec((d_model, d_model))),
        (gsum, _const_spec((LANES, LANES))),
        (gavg, _const_spec((LANES, LANES))),
        (tri, _const_spec((PREP_ROWS, PREP_ROWS))),
        (_wide_masks(), _const_spec((7, SCAN_L, MXU_COLS))),
    ]
    n_slots = N_CHUNKS * N_PAIRS
    return pl.pallas_call(
        _layer_kernel,
        grid=(batch, seq // TILE_T),
        in_specs=[spec for _, spec in operands],
        out_specs=pl.BlockSpec((None, TILE_T, d_model), lambda b, t: (b, t, 0)),
        out_shape=jax.ShapeDtypeStruct(x.shape, x.dtype),
        scratch_shapes=[
            pltpu.VMEM((TILE_T, d_model), BF16),
            pltpu.VMEM((TILE_T + SUBLANES, N_COLS), F32),
            pltpu.VMEM((PREP_ROWS, N_COLS_B), F32),
            pltpu.VMEM((TILE_T, d_model), BF16),
            pltpu.VMEM((N_PAIRS, LANES, LANES), F32),
            pltpu.VMEM((n_slots, 2 * SCAN_L, LANES), BF16),
            pltpu.VMEM((n_slots, SCAN_L, LANES), F32),
            pltpu.VMEM((n_slots, SCAN_L, 2 * PAIR_ROWS), BF16),
            pltpu.VMEM((n_slots, PAIR_ROWS, LANES), BF16),
            pltpu.VMEM((n_slots, LANES, 2 * PAIR_ROWS), BF16),
            pltpu.VMEM((n_slots, LANES, LANES), F32),
            pltpu.VMEM((TILE_T, D_B), F32),
            pltpu.VMEM((TILE_T, D_B), F32),
            pltpu.VMEM((TILE_T, D_B), F32),
        ],
        compiler_params=pltpu.CompilerParams(
            dimension_semantics=("arbitrary", "arbitrary"),
            vmem_limit_bytes=VMEM_LIMIT_BYTES),
        name="hybrid_layer",
    )(*[arr for arr, _ in operands])
```

```python
import math

import jax
import jax.numpy as jnp
import numpy as np
from jax import lax
from jax.experimental import pallas as pl
from jax.experimental.pallas import tpu as pltpu

F32 = jnp.float32
BF16 = jnp.bfloat16

D_MODEL = 1024
D_A = 512
D_B = 512
HEAD = 64
CHUNK_A = 128
LOW_RANK = 64
N_COLS_A = 3 * D_A
N_COLS_B = 4 * D_B + 2 * LOW_RANK
N_COLS = N_COLS_A + N_COLS_B
LN_EPS = 1e-5
GN_EPS = 64e-5
ALPHA = 2.0 ** 0.25
LOG_DECAY_SCALE = -math.exp(-0.5)

LANES = 128
SUBLANES = 8
MXU_COLS = 256
N_PAIRS = D_B // LANES
N_GROUPS = D_B // MXU_COLS
HEADS_PER_GROUP = MXU_COLS // HEAD
SCAN_L = 64
PAIR_ROWS = 2 * SCAN_L
TILE_T = 512
N_CHUNKS = TILE_T // SCAN_L
PREP_CHUNKS = 4
PREP_ROWS = PREP_CHUNKS * SCAN_L
CHUNKS_PER_BLOCK = CHUNK_A // SCAN_L
VMEM_LIMIT_BYTES = 56 * 1024 * 1024

ROW_PARAMS = (("mu_b", N_COLS_B), ("ln_v_g", D_A), ("ln_v_b", D_A), ("w0", D_B), ("a0", D_B),
              ("k_k", D_B), ("k_a", D_B), ("r_k", D_B), ("gn_g", D_B), ("gn_b", D_B),
              ("ln_g", D_MODEL), ("ln_b", D_MODEL))
ROW_OFFSETS = {}
for _name, _width in ROW_PARAMS:
    ROW_OFFSETS[_name] = (sum(w for _, w in ROW_PARAMS[:len(ROW_OFFSETS)]), _width)


def _bf(x):
    return x.astype(BF16)


def _mm(a, b):
    return jnp.dot(a, b, preferred_element_type=F32)


def _mm_nt(a, b):
    return lax.dot_general(a, b, (((1,), (1,)), ((), ())), preferred_element_type=F32)


def _interleave(step_generators):
    pending = list(step_generators)
    while pending:
        for gen in list(pending):
            try:
                next(gen)
            except StopIteration:
                pending.remove(gen)


def _take(gen, count):
    for _ in range(count):
        try:
            next(gen)
        except StopIteration:
            return
        yield


def _lock(fn, *lists):
    return [fn(*args) for args in zip(*lists)]


def _split_bf16(x, terms):
    parts = []
    rem = x
    for _ in range(terms):
        hi = _bf(rem)
        parts.append(hi)
        rem = rem - hi.astype(F32)
    return parts


def _group_sum_lanes(x):
    lane = lax.broadcasted_iota(jnp.int32, (x.shape[0], LANES), 1)
    first = lane < HEAD
    outs = []
    for j in range(x.shape[1] // LANES):
        t = x[:, LANES * j:LANES * (j + 1)]
        s_first = jnp.sum(jnp.where(first, t, 0.0), axis=-1, keepdims=True)
        s_second = jnp.sum(jnp.where(first, 0.0, t), axis=-1, keepdims=True)
        outs.append(jnp.where(first, s_first, s_second))
    return jnp.concatenate(outs, axis=1)


def _sigmoid(x):
    return 1.0 / (1.0 + jnp.exp(-x))


def _silu(x):
    return x * _sigmoid(x)


def _gelu_tanh(x):
    c = math.sqrt(2.0 / math.pi)
    return 0.5 * x * (1.0 + jnp.tanh(c * (x + 0.044715 * (x * x * x))))


def _block_diag(y, head_masks):
    zero = jnp.zeros((y.shape[0], LANES), BF16)
    rows = []
    for h in range(HEADS_PER_GROUP):
        tile = h // 2
        kept = y[:, LANES * tile:LANES * (tile + 1)] * head_masks[h % 2]
        rows.append(jnp.concatenate([kept if t == tile else zero
                                     for t in range(MXU_COLS // LANES)], axis=1))
    return jnp.concatenate(rows, axis=0)


def _unit_lower_inverse_steps(a_abs, eye, m8, off16, off32, off64, bd_mask):
    bd = lambda y: _block_diag(y, bd_mask)
    a8 = _lock(lambda a: a * m8, a_abs)
    a8b = _lock(_bf, a8)
    a2b = _lock(lambda a: _bf(_mm(a, bd(a))), a8b)
    yield
    p1 = _lock(lambda a: eye + a, a8)
    both = _lock(lambda a, p: _mm(jnp.concatenate([a, _bf(p)], axis=0), bd(a)), a2b, p1)
    yield
    a4b = _lock(lambda m: _bf(m[0:SCAN_L]), both)
    p2 = _lock(lambda p, m: p + m[SCAN_L:], p1, both)
    t = _lock(lambda p, a: p + _mm(_bf(p), bd(a)), p2, a4b)
    yield
    for off in (off16, off32, off64):
        tb = _lock(_bf, t)
        to = _lock(lambda x, a: _bf(_mm(x, bd(_bf(a * off)))), tb, a_abs)
        yield
        t = _lock(lambda x, xo, xb: x + _mm(xo, bd(xb)), t, to, tb)
        yield
    return t


def _scan_prepare_steps(r, k, v, kk, bb, g, lw, consts, bd_mask):
    eye, tri_strict, tri_incl, m8, off16, off32, off64 = consts
    bd = lambda y: _block_diag(y, bd_mask)

    def factors(r, k, v, kk, bb, g, lw):
        e_g = jnp.exp(g)
        e_ge = jnp.exp(g - lw)
        e_ng = jnp.exp(-g)
        e_gl = jnp.exp(g[SCAN_L - 1:SCAN_L, :] - g)
        return (_bf(-(kk * e_ge)), _bf(r * e_g), _bf(v), _bf(bb * e_ng), _bf(k * e_ng),
                _bf(bb * e_gl), _bf(k * e_gl))

    xa, xr, vb, bh, kh, bbar, kbar = zip(*_lock(factors, r, k, v, kk, bb, g, lw))
    lhs = _lock(lambda a, b: jnp.concatenate([a, b], axis=0), xa, xr)
    def pair_rows(y, sl):
        yp = y[:, sl]
        return jnp.concatenate([yp * bd_mask[0], yp * bd_mask[1]], axis=0)

    pairs = [slice(LANES * p, LANES * (p + 1)) for p in range(MXU_COLS // LANES)]
    aa = _lock(lambda x, yb, yk: [_mm_nt(x[:, sl], jnp.concatenate([pair_rows(yb, sl),
                                                                     pair_rows(yk, sl)], axis=0))
                                  for sl in pairs], lhs, bh, kh)
    yield
    aa_b = _lock(lambda parts: jnp.concatenate([m[:, 0:LANES] for m in parts], axis=1), aa)
    aa_k = _lock(lambda parts: jnp.concatenate([m[:, LANES:] for m in parts], axis=1), aa)
    a_ab = _lock(lambda m: jnp.where(tri_strict != 0.0, m[0:SCAN_L], 0.0), aa_b)
    a_rb = _lock(lambda m: _bf(jnp.where(tri_incl != 0.0, m[SCAN_L:], 0.0)), aa_b)
    a_ak = _lock(lambda m: _bf(jnp.where(tri_strict != 0.0, m[0:SCAN_L], 0.0)), aa_k)
    a_rk = _lock(lambda m: _bf(jnp.where(tri_incl != 0.0, m[SCAN_L:], 0.0)), aa_k)
    t_inv = yield from _unit_lower_inverse_steps(a_ab, eye, m8, off16, off32, off64, bd_mask)
    t_inv = _lock(_bf, t_inv)
    akv = _lock(lambda a, y: _bf(_mm(a, bd(y))), a_ak, vb)
    yield
    def pair_weights(y, sl):
        yp = y[:, sl]
        return jnp.concatenate([yp * bd_mask[0], yp * bd_mask[1]], axis=0)

    pair_slices = [slice(LANES * p, LANES * (p + 1)) for p in range(MXU_COLS // LANES)]
    qz = _lock(lambda t, a, b: [_mm(t[:, sl], jnp.concatenate([pair_weights(a, sl),
                                                                pair_weights(b, sl)], axis=1))
                                for sl in pair_slices], t_inv, xa, akv)
    yield
    q = _lock(lambda parts: _bf(jnp.concatenate([m[:, 0:LANES] for m in parts], axis=1)), qz)
    z = _lock(lambda parts: jnp.concatenate([m[:, LANES:] for m in parts], axis=1), qz)
    return q, list(xr), z, a_rb, a_rk, list(vb), list(bbar), list(kbar)


def _ada_kernel(c_ref, w_ref, b_ref, o_ref):
    s_hi, s_lo = _split_bf16(_silu(c_ref[...]), 2)
    w_hi, w_lo = _split_bf16(w_ref[...], 2)
    acc = _mm(s_hi, w_hi) + _mm(s_hi, w_lo) + _mm(s_lo, w_hi)
    o_ref[...] = acc + b_ref[...]


def _layer_kernel(x_ref, mod_ref, xn_ref, modn_ref,
                  win_ref, rows_ref, ws_ref, bs_ref, wlr_ref, wout_ref,
                  tri_ref, cm_ref,
                  out_ref,
                  h_scr, p_scr, pnext_scr, cat_scr, state_scr,
                  qx_scr, z_scr, ar_scr, vp_scr, bkt_scr,
                  dec_scr, y_scr, bonus_scr, sgate_scr):
    t_idx = pl.program_id(1)
    tile = x_ref.shape[0]
    top = SUBLANES

    @pl.when(t_idx == 0)
    def _():
        state_scr[...] = jnp.zeros_like(state_scr)
        p_scr[0:top, :] = jnp.zeros((top, N_COLS), F32)

    def row(name):
        offset, width = ROW_OFFSETS[name]
        return rows_ref[:, offset:offset + width]

    shift = mod_ref[0:1, :]
    scale = mod_ref[1:2, :]
    gate = mod_ref[2:3, :]

    h_scr[...] = _bf(x_ref[...] * (1.0 + scale) + shift)

    @pl.when((pl.program_id(0) == 0) & (t_idx == 0))
    def _():
        pnext_scr[...] = _mm(h_scr[0:PREP_ROWS, :], win_ref[:, N_COLS_A:])

    def inproj_steps(row_blocks, col0, col1):
        for row0, nrows in row_blocks:
            hb = h_scr[row0:row0 + nrows, :]
            for c in range(col0, col1, MXU_COLS):
                ce = min(c + MXU_COLS, col1)
                p_scr[top + row0:top + row0 + nrows, c:ce] = _mm(hb, win_ref[:, c:ce])
                yield

    row_a = lax.broadcasted_iota(jnp.int32, (CHUNK_A, CHUNK_A), 0)
    col_a = lax.broadcasted_iota(jnp.int32, (CHUNK_A, CHUNK_A), 1)
    causal = col_a <= row_a
    lane_a = lax.broadcasted_iota(jnp.int32, (CHUNK_A, LANES), 1)
    first_a = lane_a < HEAD

    def gmlp_steps(ci):
        rows = slice(top + ci * CHUNK_A, top + (ci + 1) * CHUNK_A)
        v = _gelu_tanh(p_scr[rows, D_A:2 * D_A])
        mean = _group_sum_lanes(v) * (1.0 / HEAD)
        yield
        vc = v - mean
        var = _group_sum_lanes(vc * vc) * (1.0 / HEAD)
        yield
        v = vc * lax.rsqrt(var + LN_EPS) * row("ln_v_g") + row("ln_v_b")
        mixed = []
        for j in range(D_A // LANES):
            vj = _bf(v[:, LANES * j:LANES * (j + 1)])
            w_even = _bf(jnp.where(causal, ws_ref[2 * j], 0.0))
            w_odd = _bf(jnp.where(causal, ws_ref[2 * j + 1], 0.0))
            mixed.append(jnp.where(first_a, _mm(w_even, vj), _mm(w_odd, vj)))
        yield
        mix = jnp.concatenate(mixed, axis=1) + bs_ref[...]
        u = _gelu_tanh(p_scr[rows, 0:D_A])
        out_a = u * mix * _silu(p_scr[rows, 2 * D_A:3 * D_A])
        cat_scr[ci * CHUNK_A:(ci + 1) * CHUNK_A, 0:D_A] = _bf(out_a)

    consts = tuple(cm_ref[i] for i in range(7))
    row_b = lax.broadcasted_iota(jnp.int32, (PREP_ROWS, N_COLS_B), 0)
    lane_b = lax.broadcasted_iota(jnp.int32, (PREP_ROWS, LANES), 1)
    lane_s = lax.broadcasted_iota(jnp.int32, (SCAN_L, LANES), 1)
    first_f = lane_s < HEAD
    first_b = jnp.where(first_f, 1.0, 0.0).astype(BF16)
    second_b = jnp.where(first_f, 0.0, 1.0).astype(BF16)
    bd_mask = (first_b, second_b)
    cb = N_COLS_A

    def stack2b(zz):
        return jnp.concatenate([zz * first_b, zz * second_b], axis=0)

    def prepare_steps(bi):
        r0 = top + bi * PREP_ROWS
        if bi == 0:
            cur = pnext_scr[...]
            before = p_scr[0:top, cb:]
        else:
            cur = p_scr[r0:r0 + PREP_ROWS, cb:]
            before = (pnext_scr[PREP_ROWS - SUBLANES:, :] if bi == 1
                      else p_scr[r0 - SUBLANES:r0, cb:])
        prev = jnp.where(row_b == 0, before[SUBLANES - 1:SUBLANES, :], pltpu.roll(cur, 1, axis=0))
        ps = cur + row("mu_b") * (prev - cur)
        r = ps[:, 0:D_B]
        k = ps[:, D_B:2 * D_B]
        v = ps[:, 2 * D_B:3 * D_B]
        gate_b = ps[:, 3 * D_B:4 * D_B]
        low = ps[:, 4 * D_B:]
        low = jnp.where(lane_b < LOW_RANK, jnp.tanh(low), low)
        up = _mm(_bf(low), wlr_ref[...])
        yield
        lw = LOG_DECAY_SCALE * _sigmoid(row("w0") + up[:, 0:D_B])
        ah = _sigmoid(row("a0") + up[:, D_B:])
        lw_terms = jnp.concatenate(_split_bf16(lw, 2), axis=1)
        g = jnp.concatenate([_mm(tri_ref[...], lw_terms[SCAN_L * c:SCAN_L * (c + 1), :])
                             for c in range(PREP_CHUNKS)], axis=0)
        g = g[:, 0:D_B] + g[:, D_B:]
        kk = k * row("k_k")
        kk = kk * lax.rsqrt(_group_sum_lanes(kk * kk) + 1e-12)
        yield
        k = k * (1.0 + (ah - 1.0) * row("k_a"))
        bb = kk * ah

        o0 = bi * PREP_ROWS
        bonus_scr[o0:o0 + PREP_ROWS, :] = _group_sum_lanes(r * k * row("r_k")) * v
        sgate_scr[o0:o0 + PREP_ROWS, :] = _silu(gate_b)
        yield

        problems = [(c, gi) for c in range(PREP_CHUNKS) for gi in range(N_GROUPS)]
        cut = lambda zz: [zz[SCAN_L * c:SCAN_L * (c + 1), MXU_COLS * gi:MXU_COLS * (gi + 1)]
                          for c, gi in problems]
        for c in range(PREP_CHUNKS):
            last = jnp.exp(g[SCAN_L * (c + 1) - 1:SCAN_L * (c + 1), :])
            for j in range(N_PAIRS):
                rowwise = jnp.broadcast_to(last[:, LANES * j:LANES * (j + 1)], (LANES, LANES))
                dec_scr[(bi * PREP_CHUNKS + c) * N_PAIRS + j] = rowwise.T

        q, xr, z, a_rb, a_rk, vb, bbar, kbar = yield from _scan_prepare_steps(
            cut(r), cut(k), cut(v), cut(kk), cut(bb), cut(g), cut(lw), consts, bd_mask)
        for i, (c, gi) in enumerate(problems):
            for pp in range(MXU_COLS // LANES):
                slot = (bi * PREP_CHUNKS + c) * N_PAIRS + gi * (MXU_COLS // LANES) + pp
                sl = slice(LANES * pp, LANES * (pp + 1))
                qx_scr[slot] = jnp.concatenate([q[i][:, sl], xr[i][:, sl]], axis=0)
                z_scr[slot] = z[i][:, sl]
                ar_scr[slot] = jnp.concatenate([a_rb[i][:, sl], a_rk[i][:, sl]], axis=1)
                vp_scr[slot] = stack2b(vb[i][:, sl])
                bkt_scr[slot] = jnp.concatenate([stack2b(bbar[i][:, sl]), stack2b(kbar[i][:, sl])],
                                                axis=0).T

    n_prep = N_CHUNKS // PREP_CHUNKS
    row_block = lambda i: (i * PREP_ROWS, PREP_ROWS)
    for bi in range(n_prep):
        if bi + 1 < n_prep:
            fill = inproj_steps([row_block(bi + 1)], cb, N_COLS)
        else:
            fill = inproj_steps([row_block(i) for i in range(n_prep)], 0, cb)
        _interleave([prepare_steps(bi), fill])

    def apply_steps(bi):
        states = [state_scr[j] for j in range(N_PAIRS)]
        for c in range(CHUNKS_PER_BLOCK):
            ci = bi * CHUNKS_PER_BLOCK + c
            load = lambda ref: [ref[ci * N_PAIRS + j] for j in range(N_PAIRS)]
            xs = _lock(lambda a, s: _mm(a, _bf(s)), load(qx_scr), states)
            yield
            uv = _lock(lambda x, zz, w: jnp.concatenate([stack2b(_bf(x[0:SCAN_L] + zz)), w], axis=0),
                       xs, load(z_scr), load(vp_scr))
            both = _lock(lambda b, a, m: _mm(jnp.concatenate([b, a], axis=0), m),
                         load(bkt_scr), load(ar_scr), uv)
            yield
            states = _lock(lambda s, d, m: s * d + m[0:LANES], states, load(dec_scr), both)
            ys = _lock(lambda x, m: x[SCAN_L:] + m[LANES:], xs, both)
            y_scr[ci * SCAN_L:(ci + 1) * SCAN_L, :] = jnp.concatenate(ys, axis=1)
        for j in range(N_PAIRS):
            state_scr[j] = states[j]

    def post_steps(bi):
        rows = slice(bi * CHUNK_A, (bi + 1) * CHUNK_A)
        y = y_scr[rows, :]
        mean = _group_sum_lanes(y) * (1.0 / HEAD)
        yield
        yc = y - mean
        var = _group_sum_lanes(yc * yc) * (1.0 / HEAD)
        yield
        y = yc * lax.rsqrt(var + GN_EPS) * row("gn_g") + row("gn_b")
        cat_scr[rows, D_A:] = _bf((y + bonus_scr[rows, :]) * sgate_scr[rows, :])
        o = _mm(cat_scr[rows, :], wout_ref[...])
        yield
        res = ALPHA * x_ref[rows, :] + gate * o
        mean = jnp.mean(res, axis=-1, keepdims=True)
        rc = res - mean
        var = jnp.mean(rc * rc, axis=-1, keepdims=True)
        out_ref[rows, :] = rc * lax.rsqrt(var + LN_EPS) * row("ln_g") + row("ln_b")

    def next_inproj_steps():
        hn = _bf(xn_ref[...] * (1.0 + modn_ref[1:2, :]) + modn_ref[0:1, :])
        for c in range(0, N_COLS_B, MXU_COLS):
            ce = min(c + MXU_COLS, N_COLS_B)
            pnext_scr[:, c:ce] = _mm(hn, win_ref[:, cb + c:cb + ce])
            yield

    n_blocks = tile // CHUNK_A
    next_proj = next_inproj_steps()
    next_steps_per_block = -(-len(range(0, N_COLS_B, MXU_COLS)) // n_blocks)
    for bi in range(n_blocks):
        group = [apply_steps(bi)]
        if bi > 0:
            group.append(post_steps(bi - 1))
        group += [gmlp_steps(bi), _take(next_proj, next_steps_per_block)]
        _interleave(group)
    _interleave([post_steps(n_blocks - 1), next_proj])

    p_scr[0:top, :] = p_scr[tile:tile + top, :]


def _wide_masks():
    t = np.arange(SCAN_L)[:, None]
    s = (np.arange(MXU_COLS) % SCAN_L)[None, :]
    same = lambda n: (t // n) == (s // n)
    masks = [t == s, s < t, s <= t, same(8),
             same(16) & ~same(8), same(32) & ~same(16), same(64) & ~same(32)]
    return jnp.asarray(np.stack(masks), F32)


def _const_spec(shape):
    zeros = (0,) * len(shape)
    return pl.BlockSpec(shape, lambda b, t: zeros)


def kernel(x, c, w_ada, b_ada, w_in, mu_b, ln_v_g, ln_v_b, w_spatial, b_spatial, w0, w_up, a0,
           a_up, k_k, k_a, r_k, gn_g, gn_b, w_out, ln_g, ln_b):
    batch, seq, d_model = x.shape
    assert d_model == D_MODEL and w_in.shape == (D_MODEL, N_COLS)
    assert seq % TILE_T == 0 and TILE_T % CHUNK_A == 0 and N_CHUNKS % PREP_CHUNKS == 0
    assert N_CHUNKS // PREP_CHUNKS >= 2

    mod = pl.pallas_call(
        _ada_kernel,
        out_shape=jax.ShapeDtypeStruct((batch, 3 * d_model), F32),
        name="adaln_mod",
    )(c, w_ada, b_ada.reshape(1, -1))
    mod = mod.reshape(batch, 3, d_model)

    named = dict(mu_b=mu_b, ln_v_g=ln_v_g, ln_v_b=ln_v_b, w0=w0, a0=a0, k_k=k_k, k_a=k_a, r_k=r_k,
                 gn_g=gn_g, gn_b=gn_b, ln_g=ln_g, ln_b=ln_b)
    rows = jnp.concatenate([named[name].reshape(-1).astype(F32) for name, _ in ROW_PARAMS])
    rows = rows.reshape(1, -1)
    tok = np.arange(SCAN_L)
    tri = jnp.asarray(tok[None, :] <= tok[:, None], BF16)
    zeros_lr = jnp.zeros((LOW_RANK, D_B), F32)
    w_lr = jnp.concatenate([jnp.concatenate([w_up, zeros_lr], axis=1),
                            jnp.concatenate([zeros_lr, a_up], axis=1)], axis=0).astype(BF16)
    bias_a = jnp.repeat(b_spatial.T, HEAD, axis=1)

    tiles = seq // TILE_T

    def next_block(b, t):
        s = jnp.minimum(b * tiles + t + 1, batch * tiles - 1)
        return s // tiles, (s % tiles) * (TILE_T // PREP_ROWS), 0

    operands = [
        (x, pl.BlockSpec((None, TILE_T, d_model), lambda b, t: (b, t, 0))),
        (mod, pl.BlockSpec((None, 3, d_model), lambda b, t: (b, 0, 0))),
        (x, pl.BlockSpec((None, PREP_ROWS, d_model), next_block)),
        (mod, pl.BlockSpec((None, 3, d_model), lambda b, t: (next_block(b, t)[0], 0, 0))),
        (w_in.astype(BF16), _const_spec((d_model, N_COLS))),
        (rows, _const_spec(rows.shape)),
        (w_spatial, _const_spec(w_spatial.shape)),
        (bias_a, _const_spec((CHUNK_A, D_A))),
        (w_lr, _const_spec((2 * LOW_RANK, 2 * D_B))),
        (w_out.astype(BF16), _const_spec((d_model, d_model))),
        (tri, _const_spec((SCAN_L, SCAN_L))),
        (_wide_masks(), _const_spec((7, SCAN_L, MXU_COLS))),
    ]
    n_slots = N_CHUNKS * N_PAIRS
    return pl.pallas_call(
        _layer_kernel,
        grid=(batch, seq // TILE_T),
        in_specs=[spec for _, spec in operands],
        out_specs=pl.BlockSpec((None, TILE_T, d_model), lambda b, t: (b, t, 0)),
        out_shape=jax.ShapeDtypeStruct(x.shape, x.dtype),
        scratch_shapes=[
            pltpu.VMEM((TILE_T, d_model), BF16),
            pltpu.VMEM((TILE_T + SUBLANES, N_COLS), F32),
            pltpu.VMEM((PREP_ROWS, N_COLS_B), F32),
            pltpu.VMEM((TILE_T, d_model), BF16),
            pltpu.VMEM((N_PAIRS, LANES, LANES), F32),
            pltpu.VMEM((n_slots, 2 * SCAN_L, LANES), BF16),
            pltpu.VMEM((n_slots, SCAN_L, LANES), F32),
            pltpu.VMEM((n_slots, SCAN_L, 2 * PAIR_ROWS), BF16),
            pltpu.VMEM((n_slots, PAIR_ROWS, LANES), BF16),
            pltpu.VMEM((n_slots, LANES, 2 * PAIR_ROWS), BF16),
            pltpu.VMEM((n_slots, LANES, LANES), F32),
            pltpu.VMEM((TILE_T, D_B), F32),
            pltpu.VMEM((TILE_T, D_B), F32),
            pltpu.VMEM((TILE_T, D_B), F32),
        ],
        compiler_params=pltpu.CompilerParams(
            dimension_semantics=("arbitrary", "arbitrary"),
            vmem_limit_bytes=VMEM_LIMIT_BYTES),
        name="hybrid_layer",
    )(*[arr for arr, _ in operands])
```

```python
import math

import jax
import jax.numpy as jnp
import numpy as np
from jax import lax
from jax.experimental import pallas as pl
from jax.experimental.pallas import tpu as pltpu

F32 = jnp.float32
BF16 = jnp.bfloat16

D_MODEL = 1024
D_A = 512
D_B = 512
HEAD = 64
CHUNK_A = 128
LOW_RANK = 64
N_COLS_A = 3 * D_A
N_COLS_B = 4 * D_B + 2 * LOW_RANK
N_COLS = N_COLS_A + N_COLS_B
LN_EPS = 1e-5
GN_EPS = 64e-5
ALPHA = 2.0 ** 0.25
LOG_DECAY_SCALE = -math.exp(-0.5)

LANES = 128
SUBLANES = 8
MXU_COLS = 256
N_PAIRS = D_B // LANES
N_GROUPS = D_B // MXU_COLS
HEADS_PER_GROUP = MXU_COLS // HEAD
SCAN_L = 64
PAIR_ROWS = 2 * SCAN_L
TILE_T = 512
N_CHUNKS = TILE_T // SCAN_L
PREP_CHUNKS = 4
PREP_ROWS = PREP_CHUNKS * SCAN_L
CHUNKS_PER_BLOCK = CHUNK_A // SCAN_L
VMEM_LIMIT_BYTES = 56 * 1024 * 1024

ROW_PARAMS = (("mu_b", N_COLS_B), ("ln_v_g", D_A), ("ln_v_b", D_A), ("w0", D_B), ("a0", D_B),
              ("k_k", D_B), ("k_a", D_B), ("r_k", D_B), ("gn_g", D_B), ("gn_b", D_B),
              ("ln_g", D_MODEL), ("ln_b", D_MODEL))
ROW_OFFSETS = {}
for _name, _width in ROW_PARAMS:
    ROW_OFFSETS[_name] = (sum(w for _, w in ROW_PARAMS[:len(ROW_OFFSETS)]), _width)


def _bf(x):
    return x.astype(BF16)


def _mm(a, b):
    return jnp.dot(a, b, preferred_element_type=F32)


def _mm_nt(a, b):
    return lax.dot_general(a, b, (((1,), (1,)), ((), ())), preferred_element_type=F32)


def _interleave(step_generators):
    pending = list(step_generators)
    while pending:
        for gen in list(pending):
            try:
                next(gen)
            except StopIteration:
                pending.remove(gen)


def _take(gen, count):
    for _ in range(count):
        try:
            next(gen)
        except StopIteration:
            return
        yield


def _lock(fn, *lists):
    return [fn(*args) for args in zip(*lists)]


def _split_bf16(x, terms):
    parts = []
    rem = x
    for _ in range(terms):
        hi = _bf(rem)
        parts.append(hi)
        rem = rem - hi.astype(F32)
    return parts


def _group_sum_lanes(x):
    lane = lax.broadcasted_iota(jnp.int32, (x.shape[0], LANES), 1)
    first = lane < HEAD
    outs = []
    for j in range(x.shape[1] // LANES):
        t = x[:, LANES * j:LANES * (j + 1)]
        s_first = jnp.sum(jnp.where(first, t, 0.0), axis=-1, keepdims=True)
        s_second = jnp.sum(jnp.where(first, 0.0, t), axis=-1, keepdims=True)
        outs.append(jnp.where(first, s_first, s_second))
    return jnp.concatenate(outs, axis=1)


def _sigmoid(x, scale=1.0):
    return 0.5 * scale + (0.5 * scale) * jnp.tanh(0.5 * x)


def _silu(x):
    h = 0.5 * x
    return h + h * jnp.tanh(h)


def _gelu_tanh(x):
    c = math.sqrt(2.0 / math.pi)
    h = 0.5 * x
    return h + h * jnp.tanh(x * (c + (c * 0.044715) * (x * x)))


def _block_diag(y, head_masks):
    zero = jnp.zeros((y.shape[0], LANES), BF16)
    rows = []
    for h in range(HEADS_PER_GROUP):
        tile = h // 2
        kept = y[:, LANES * tile:LANES * (tile + 1)] * head_masks[h % 2]
        rows.append(jnp.concatenate([kept if t == tile else zero
                                     for t in range(MXU_COLS // LANES)], axis=1))
    return jnp.concatenate(rows, axis=0)


def _unit_lower_inverse_steps(a_abs, eye, m8, off16, off32, off64, bd_mask):
    bd = lambda y: _block_diag(y, bd_mask)
    a8 = _lock(lambda a: a * m8, a_abs)
    a8b = _lock(_bf, a8)
    a2b = _lock(lambda a: _bf(_mm(a, bd(a))), a8b)
    yield
    p1 = _lock(lambda a: eye + a, a8)
    both = _lock(lambda a, p: _mm(jnp.concatenate([a, _bf(p)], axis=0), bd(a)), a2b, p1)
    yield
    a4b = _lock(lambda m: _bf(m[0:SCAN_L]), both)
    p2 = _lock(lambda p, m: p + m[SCAN_L:], p1, both)
    t = _lock(lambda p, a: p + _mm(_bf(p), bd(a)), p2, a4b)
    yield
    for off in (off16, off32, off64):
        tb = _lock(_bf, t)
        to = _lock(lambda x, a: _bf(_mm(x, bd(_bf(a * off)))), tb, a_abs)
        yield
        t = _lock(lambda x, xo, xb: x + _mm(xo, bd(xb)), t, to, tb)
        yield
    return t


def _scan_prepare_steps(r, k, v, kk, bb, g, lw, consts, bd_mask):
    eye, tri_strict, tri_incl, m8, off16, off32, off64 = consts
    bd = lambda y: _block_diag(y, bd_mask)

    def factors(r, k, v, kk, bb, g, lw):
        e_g = jnp.exp(g)
        e_ge = jnp.exp(g - lw)
        e_ng = jnp.exp(-g)
        e_gl = jnp.exp(g[SCAN_L - 1:SCAN_L, :] - g)
        return (_bf(-(kk * e_ge)), _bf(r * e_g), _bf(v), _bf(bb * e_ng), _bf(k * e_ng),
                _bf(bb * e_gl), _bf(k * e_gl))

    xa, xr, vb, bh, kh, bbar, kbar = zip(*_lock(factors, r, k, v, kk, bb, g, lw))
    lhs = _lock(lambda a, b: jnp.concatenate([a, b], axis=0), xa, xr)
    def pair_rows(y, sl):
        yp = y[:, sl]
        return jnp.concatenate([yp * bd_mask[0], yp * bd_mask[1]], axis=0)

    pairs = [slice(LANES * p, LANES * (p + 1)) for p in range(MXU_COLS // LANES)]
    aa = _lock(lambda x, yb, yk: [_mm_nt(x[:, sl], jnp.concatenate([pair_rows(yb, sl),
                                                                     pair_rows(yk, sl)], axis=0))
                                  for sl in pairs], lhs, bh, kh)
    yield
    aa_b = _lock(lambda parts: jnp.concatenate([m[:, 0:LANES] for m in parts], axis=1), aa)
    aa_k = _lock(lambda parts: jnp.concatenate([m[:, LANES:] for m in parts], axis=1), aa)
    a_ab = _lock(lambda m: jnp.where(tri_strict != 0.0, m[0:SCAN_L], 0.0), aa_b)
    a_rb = _lock(lambda m: _bf(jnp.where(tri_incl != 0.0, m[SCAN_L:], 0.0)), aa_b)
    a_ak = _lock(lambda m: _bf(jnp.where(tri_strict != 0.0, m[0:SCAN_L], 0.0)), aa_k)
    a_rk = _lock(lambda m: _bf(jnp.where(tri_incl != 0.0, m[SCAN_L:], 0.0)), aa_k)
    t_inv = yield from _unit_lower_inverse_steps(a_ab, eye, m8, off16, off32, off64, bd_mask)
    t_inv = _lock(_bf, t_inv)
    akv = _lock(lambda a, y: _bf(_mm(a, bd(y))), a_ak, vb)
    yield
    def pair_weights(y, sl):
        yp = y[:, sl]
        return jnp.concatenate([yp * bd_mask[0], yp * bd_mask[1]], axis=0)

    pair_slices = [slice(LANES * p, LANES * (p + 1)) for p in range(MXU_COLS // LANES)]
    qz = _lock(lambda t, a, b: [_mm(t[:, sl], jnp.concatenate([pair_weights(a, sl),
                                                                pair_weights(b, sl)], axis=1))
                                for sl in pair_slices], t_inv, xa, akv)
    yield
    q = _lock(lambda parts: _bf(jnp.concatenate([m[:, 0:LANES] for m in parts], axis=1)), qz)
    z = _lock(lambda parts: jnp.concatenate([m[:, LANES:] for m in parts], axis=1), qz)
    return q, list(xr), z, a_rb, a_rk, list(vb), list(bbar), list(kbar)


def _ada_kernel(c_ref, w_ref, b_ref, o_ref):
    s_hi, s_lo = _split_bf16(_silu(c_ref[...]), 2)
    w_hi, w_lo = _split_bf16(w_ref[...], 2)
    acc = _mm(s_hi, w_hi) + _mm(s_hi, w_lo) + _mm(s_lo, w_hi)
    o_ref[...] = acc + b_ref[...]


def _layer_kernel(x_ref, mod_ref, xn_ref, modn_ref,
                  win_ref, rows_ref, ws_ref, bs_ref, wlr_ref, wout_ref,
                  tri_ref, cm_ref,
                  out_ref,
                  h_scr, p_scr, pnext_scr, cat_scr, state_scr,
                  qx_scr, z_scr, ar_scr, vp_scr, bkt_scr,
                  dec_scr, y_scr, bonus_scr, sgate_scr, wm_scr):
    t_idx = pl.program_id(1)
    tile = x_ref.shape[0]
    top = SUBLANES

    @pl.when(t_idx == 0)
    def _():
        state_scr[...] = jnp.zeros_like(state_scr)
        p_scr[0:top, :] = jnp.zeros((top, N_COLS), F32)

    def row(name):
        offset, width = ROW_OFFSETS[name]
        return rows_ref[:, offset:offset + width]

    shift = mod_ref[0:1, :]
    scale = mod_ref[1:2, :]
    gate_over_alpha = mod_ref[2:3, :] * (1.0 / ALPHA)

    h_scr[...] = _bf(x_ref[...] * (1.0 + scale) + shift)

    @pl.when((pl.program_id(0) == 0) & (t_idx == 0))
    def _():
        pnext_scr[...] = _mm(h_scr[0:PREP_ROWS, :], win_ref[:, N_COLS_A:])

    def inproj_steps(row_blocks, col0, col1):
        for row0, nrows in row_blocks:
            hb = h_scr[row0:row0 + nrows, :]
            for c in range(col0, col1, MXU_COLS):
                ce = min(c + MXU_COLS, col1)
                p_scr[top + row0:top + row0 + nrows, c:ce] = _mm(hb, win_ref[:, c:ce])
                yield

    row_a = lax.broadcasted_iota(jnp.int32, (CHUNK_A, CHUNK_A), 0)
    col_a = lax.broadcasted_iota(jnp.int32, (CHUNK_A, CHUNK_A), 1)
    causal = col_a <= row_a
    lane_a = lax.broadcasted_iota(jnp.int32, (CHUNK_A, LANES), 1)
    first_a = lane_a < HEAD

    for gidx in range(ws_ref.shape[0]):
        wm_scr[gidx] = _bf(jnp.where(causal, ws_ref[gidx], 0.0))

    def gmlp_steps(ci):
        rows = slice(top + ci * CHUNK_A, top + (ci + 1) * CHUNK_A)
        v = _gelu_tanh(p_scr[rows, D_A:2 * D_A])
        mean = _group_sum_lanes(v) * (1.0 / HEAD)
        yield
        vc = v - mean
        var = _group_sum_lanes(vc * vc) * (1.0 / HEAD)
        yield
        v = vc * lax.rsqrt(var + LN_EPS) * row("ln_v_g") + row("ln_v_b")
        mixed = []
        for j in range(D_A // LANES):
            vj = _bf(v[:, LANES * j:LANES * (j + 1)])
            mixed.append(jnp.where(first_a, _mm(wm_scr[2 * j], vj), _mm(wm_scr[2 * j + 1], vj)))
        yield
        mix = jnp.concatenate(mixed, axis=1) + bs_ref[...]
        u = _gelu_tanh(p_scr[rows, 0:D_A])
        out_a = u * mix * _silu(p_scr[rows, 2 * D_A:3 * D_A])
        cat_scr[ci * CHUNK_A:(ci + 1) * CHUNK_A, 0:D_A] = _bf(out_a)

    consts = tuple(cm_ref[i] for i in range(7))
    row_b = lax.broadcasted_iota(jnp.int32, (PREP_ROWS, N_COLS_B), 0)
    lane_b = lax.broadcasted_iota(jnp.int32, (PREP_ROWS, LANES), 1)
    lane_s = lax.broadcasted_iota(jnp.int32, (SCAN_L, LANES), 1)
    first_f = lane_s < HEAD
    first_b = jnp.where(first_f, 1.0, 0.0).astype(BF16)
    second_b = jnp.where(first_f, 0.0, 1.0).astype(BF16)
    bd_mask = (first_b, second_b)
    cb = N_COLS_A

    def stack2b(zz):
        return jnp.concatenate([zz * first_b, zz * second_b], axis=0)

    def prepare_steps(bi):
        r0 = top + bi * PREP_ROWS
        if bi == 0:
            cur = pnext_scr[...]
            before = p_scr[0:top, cb:]
        else:
            cur = p_scr[r0:r0 + PREP_ROWS, cb:]
            before = (pnext_scr[PREP_ROWS - SUBLANES:, :] if bi == 1
                      else p_scr[r0 - SUBLANES:r0, cb:])
        prev = jnp.where(row_b == 0, before[SUBLANES - 1:SUBLANES, :], pltpu.roll(cur, 1, axis=0))
        ps = cur + row("mu_b") * (prev - cur)
        r = ps[:, 0:D_B]
        k = ps[:, D_B:2 * D_B]
        v = ps[:, 2 * D_B:3 * D_B]
        gate_b = ps[:, 3 * D_B:4 * D_B]
        low = ps[:, 4 * D_B:]
        low = jnp.where(lane_b < LOW_RANK, jnp.tanh(low), low)
        up = _mm(_bf(low), wlr_ref[...])
        yield
        lw = _sigmoid(row("w0") + up[:, 0:D_B], LOG_DECAY_SCALE)
        ah = _sigmoid(row("a0") + up[:, D_B:])
        lw_terms = jnp.concatenate(_split_bf16(lw, 2), axis=1)
        g = jnp.concatenate([_mm(tri_ref[...], lw_terms[SCAN_L * c:SCAN_L * (c + 1), :])
                             for c in range(PREP_CHUNKS)], axis=0)
        g = g[:, 0:D_B] + g[:, D_B:]
        kk = k * row("k_k")
        kk = kk * lax.rsqrt(_group_sum_lanes(kk * kk) + 1e-12)
        yield
        k = k * (1.0 + (ah - 1.0) * row("k_a"))
        bb = kk * ah

        o0 = bi * PREP_ROWS
        bonus_scr[o0:o0 + PREP_ROWS, :] = _group_sum_lanes(r * k * row("r_k")) * v
        sgate_scr[o0:o0 + PREP_ROWS, :] = _silu(gate_b)
        yield

        problems = [(c, gi) for c in range(PREP_CHUNKS) for gi in range(N_GROUPS)]
        cut = lambda zz: [zz[SCAN_L * c:SCAN_L * (c + 1), MXU_COLS * gi:MXU_COLS * (gi + 1)]
                          for c, gi in problems]
        for c in range(PREP_CHUNKS):
            last = jnp.exp(g[SCAN_L * (c + 1) - 1:SCAN_L * (c + 1), :])
            for j in range(N_PAIRS):
                rowwise = jnp.broadcast_to(last[:, LANES * j:LANES * (j + 1)], (LANES, LANES))
                dec_scr[(bi * PREP_CHUNKS + c) * N_PAIRS + j] = rowwise.T

        q, xr, z, a_rb, a_rk, vb, bbar, kbar = yield from _scan_prepare_steps(
            cut(r), cut(k), cut(v), cut(kk), cut(bb), cut(g), cut(lw), consts, bd_mask)
        for i, (c, gi) in enumerate(problems):
            for pp in range(MXU_COLS // LANES):
                slot = (bi * PREP_CHUNKS + c) * N_PAIRS + gi * (MXU_COLS // LANES) + pp
                sl = slice(LANES * pp, LANES * (pp + 1))
                qx_scr[slot] = jnp.concatenate([q[i][:, sl], xr[i][:, sl]], axis=0)
                z_scr[slot] = z[i][:, sl]
                ar_scr[slot] = jnp.concatenate([a_rb[i][:, sl], a_rk[i][:, sl]], axis=1)
                vp_scr[slot] = stack2b(vb[i][:, sl])
                bkt_scr[slot] = jnp.concatenate([stack2b(bbar[i][:, sl]), stack2b(kbar[i][:, sl])],
                                                axis=0).T

    n_prep = N_CHUNKS // PREP_CHUNKS
    row_block = lambda i: (i * PREP_ROWS, PREP_ROWS)
    for bi in range(n_prep):
        if bi + 1 < n_prep:
            fill = inproj_steps([row_block(bi + 1)], cb, N_COLS)
        else:
            fill = inproj_steps([row_block(i) for i in range(n_prep)], 0, cb)
        _interleave([prepare_steps(bi), fill])

    def apply_steps(bi):
        states = [state_scr[j] for j in range(N_PAIRS)]
        for c in range(CHUNKS_PER_BLOCK):
            ci = bi * CHUNKS_PER_BLOCK + c
            load = lambda ref: [ref[ci * N_PAIRS + j] for j in range(N_PAIRS)]
            xs = _lock(lambda a, s: _mm(a, _bf(s)), load(qx_scr), states)
            yield
            uv = _lock(lambda x, zz, w: jnp.concatenate([stack2b(_bf(x[0:SCAN_L] + zz)), w], axis=0),
                       xs, load(z_scr), load(vp_scr))
            both = _lock(lambda b, a, m: _mm(jnp.concatenate([b, a], axis=0), m),
                         load(bkt_scr), load(ar_scr), uv)
            yield
            states = _lock(lambda s, d, m: s * d + m[0:LANES], states, load(dec_scr), both)
            ys = _lock(lambda x, m: x[SCAN_L:] + m[LANES:], xs, both)
            y_scr[ci * SCAN_L:(ci + 1) * SCAN_L, :] = jnp.concatenate(ys, axis=1)
        for j in range(N_PAIRS):
            state_scr[j] = states[j]

    def post_steps(bi):
        rows = slice(bi * CHUNK_A, (bi + 1) * CHUNK_A)
        y = y_scr[rows, :]
        mean = _group_sum_lanes(y) * (1.0 / HEAD)
        yield
        yc = y - mean
        var = _group_sum_lanes(yc * yc) * (1.0 / HEAD)
        yield
        y = yc * lax.rsqrt(var + GN_EPS) * row("gn_g") + row("gn_b")
        cat_scr[rows, D_A:] = _bf((y + bonus_scr[rows, :]) * sgate_scr[rows, :])
        o = _mm(cat_scr[rows, :], wout_ref[...])
        yield
        res = x_ref[rows, :] + gate_over_alpha * o
        mean = jnp.mean(res, axis=-1, keepdims=True)
        rc = res - mean
        var = jnp.mean(rc * rc, axis=-1, keepdims=True)
        out_ref[rows, :] = rc * lax.rsqrt(var + LN_EPS / ALPHA ** 2) * row("ln_g") + row("ln_b")

    def next_inproj_steps():
        hn = _bf(xn_ref[...] * (1.0 + modn_ref[1:2, :]) + modn_ref[0:1, :])
        for c in range(0, N_COLS_B, MXU_COLS):
            ce = min(c + MXU_COLS, N_COLS_B)
            pnext_scr[:, c:ce] = _mm(hn, win_ref[:, cb + c:cb + ce])
            yield

    n_blocks = tile // CHUNK_A
    next_proj = next_inproj_steps()
    next_steps_per_block = -(-len(range(0, N_COLS_B, MXU_COLS)) // n_blocks)
    for bi in range(n_blocks):
        group = [apply_steps(bi)]
        if bi > 0:
            group.append(post_steps(bi - 1))
        group += [gmlp_steps(bi), _take(next_proj, next_steps_per_block)]
        _interleave(group)
    _interleave([post_steps(n_blocks - 1), next_proj])

    p_scr[0:top, :] = p_scr[tile:tile + top, :]


def _wide_masks():
    t = np.arange(SCAN_L)[:, None]
    s = (np.arange(MXU_COLS) % SCAN_L)[None, :]
    same = lambda n: (t // n) == (s // n)
    masks = [t == s, s < t, s <= t, same(8),
             same(16) & ~same(8), same(32) & ~same(16), same(64) & ~same(32)]
    return jnp.asarray(np.stack(masks), F32)


def _const_spec(shape):
    zeros = (0,) * len(shape)
    return pl.BlockSpec(shape, lambda b, t: zeros)


def kernel(x, c, w_ada, b_ada, w_in, mu_b, ln_v_g, ln_v_b, w_spatial, b_spatial, w0, w_up, a0,
           a_up, k_k, k_a, r_k, gn_g, gn_b, w_out, ln_g, ln_b):
    batch, seq, d_model = x.shape
    assert d_model == D_MODEL and w_in.shape == (D_MODEL, N_COLS)
    assert seq % TILE_T == 0 and TILE_T % CHUNK_A == 0 and N_CHUNKS % PREP_CHUNKS == 0
    assert N_CHUNKS // PREP_CHUNKS >= 2

    mod = pl.pallas_call(
        _ada_kernel,
        out_shape=jax.ShapeDtypeStruct((batch, 3 * d_model), F32),
        name="adaln_mod",
    )(c, w_ada, b_ada.reshape(1, -1))
    mod = mod.reshape(batch, 3, d_model)

    named = dict(mu_b=mu_b, ln_v_g=ln_v_g, ln_v_b=ln_v_b, w0=w0, a0=a0, k_k=k_k, k_a=k_a, r_k=r_k,
                 gn_g=gn_g, gn_b=gn_b, ln_g=ln_g, ln_b=ln_b)
    rows = jnp.concatenate([named[name].reshape(-1).astype(F32) for name, _ in ROW_PARAMS])
    rows = rows.reshape(1, -1)
    tok = np.arange(SCAN_L)
    tri = jnp.asarray(tok[None, :] <= tok[:, None], BF16)
    zeros_lr = jnp.zeros((LOW_RANK, D_B), F32)
    w_lr = jnp.concatenate([jnp.concatenate([w_up, zeros_lr], axis=1),
                            jnp.concatenate([zeros_lr, a_up], axis=1)], axis=0).astype(BF16)
    bias_a = jnp.repeat(b_spatial.T, HEAD, axis=1)

    tiles = seq // TILE_T

    def next_block(b, t):
        s = jnp.minimum(b * tiles + t + 1, batch * tiles - 1)
        return s // tiles, (s % tiles) * (TILE_T // PREP_ROWS), 0

    operands = [
        (x, pl.BlockSpec((None, TILE_T, d_model), lambda b, t: (b, t, 0))),
        (mod, pl.BlockSpec((None, 3, d_model), lambda b, t: (b, 0, 0))),
        (x, pl.BlockSpec((None, PREP_ROWS, d_model), next_block)),
        (mod, pl.BlockSpec((None, 3, d_model), lambda b, t: (next_block(b, t)[0], 0, 0))),
        (w_in.astype(BF16), _const_spec((d_model, N_COLS))),
        (rows, _const_spec(rows.shape)),
        (w_spatial, _const_spec(w_spatial.shape)),
        (bias_a, _const_spec((CHUNK_A, D_A))),
        (w_lr, _const_spec((2 * LOW_RANK, 2 * D_B))),
        (w_out.astype(BF16), _const_spec((d_model, d_model))),
        (tri, _const_spec((SCAN_L, SCAN_L))),
        (_wide_masks(), _const_spec((7, SCAN_L, MXU_COLS))),
    ]
    n_slots = N_CHUNKS * N_PAIRS
    return pl.pallas_call(
        _layer_kernel,
        grid=(batch, seq // TILE_T),
        in_specs=[spec for _, spec in operands],
        out_specs=pl.BlockSpec((None, TILE_T, d_model), lambda b, t: (b, t, 0)),
        out_shape=jax.ShapeDtypeStruct(x.shape, x.dtype),
        scratch_shapes=[
            pltpu.VMEM((TILE_T, d_model), BF16),
            pltpu.VMEM((TILE_T + SUBLANES, N_COLS), F32),
            pltpu.VMEM((PREP_ROWS, N_COLS_B), F32),
            pltpu.VMEM((TILE_T, d_model), BF16),
            pltpu.VMEM((N_PAIRS, LANES, LANES), F32),
            pltpu.VMEM((n_slots, 2 * SCAN_L, LANES), BF16),
            pltpu.VMEM((n_slots, SCAN_L, LANES), F32),
            pltpu.VMEM((n_slots, SCAN_L, 2 * PAIR_ROWS), BF16),
            pltpu.VMEM((n_slots, PAIR_ROWS, LANES), BF16),
            pltpu.VMEM((n_slots, LANES, 2 * PAIR_ROWS), BF16),
            pltpu.VMEM((n_slots, LANES, LANES), F32),
            pltpu.VMEM((TILE_T, D_B), F32),
            pltpu.VMEM((TILE_T, D_B), F32),
            pltpu.VMEM((TILE_T, D_B), F32),
            pltpu.VMEM((D_A // HEAD, CHUNK_A, CHUNK_A), BF16),
        ],
        compiler_params=pltpu.CompilerParams(
            dimension_semantics=("arbitrary", "arbitrary"),
            vmem_limit_bytes=VMEM_LIMIT_BYTES),
        name="hybrid_layer",
    )(*[arr for arr, _ in operands])
```

```python
import math

import jax
import jax.numpy as jnp
import numpy as np
from jax import lax
from jax.experimental import pallas as pl
from jax.experimental.pallas import tpu as pltpu

F32 = jnp.float32
BF16 = jnp.bfloat16

D_MODEL = 1024
D_A = 512
D_B = 512
HEAD = 64
CHUNK_A = 128
LOW_RANK = 64
N_COLS_A = 3 * D_A
N_COLS_B = 4 * D_B + 2 * LOW_RANK
N_COLS = N_COLS_A + N_COLS_B
LN_EPS = 1e-5
GN_EPS = 64e-5
ALPHA = 2.0 ** 0.25
LOG_DECAY_SCALE = -math.exp(-0.5)

LANES = 128
SUBLANES = 8
MXU_COLS = 256
N_PAIRS = D_B // LANES
N_GROUPS = D_B // MXU_COLS
HEADS_PER_GROUP = MXU_COLS // HEAD
SCAN_L = 64
PAIR_ROWS = 2 * SCAN_L
TILE_T = 512
N_CHUNKS = TILE_T // SCAN_L
PREP_CHUNKS = 4
PREP_ROWS = PREP_CHUNKS * SCAN_L
CHUNKS_PER_BLOCK = CHUNK_A // SCAN_L
VMEM_LIMIT_BYTES = 56 * 1024 * 1024

ROW_PARAMS = (("mu_b", N_COLS_B), ("ln_v_g", D_A), ("ln_v_b", D_A), ("w0", D_B), ("a0", D_B),
              ("k_k", D_B), ("k_a", D_B), ("r_k", D_B), ("gn_g", D_B), ("gn_b", D_B),
              ("ln_g", D_MODEL), ("ln_b", D_MODEL))
ROW_OFFSETS = {}
for _name, _width in ROW_PARAMS:
    ROW_OFFSETS[_name] = (sum(w for _, w in ROW_PARAMS[:len(ROW_OFFSETS)]), _width)


def _bf(x):
    return x.astype(BF16)


def _mm(a, b):
    return jnp.dot(a, b, preferred_element_type=F32)


def _mm_nt(a, b):
    return lax.dot_general(a, b, (((1,), (1,)), ((), ())), preferred_element_type=F32)


def _interleave(step_generators):
    pending = list(step_generators)
    while pending:
        for gen in list(pending):
            try:
                next(gen)
            except StopIteration:
                pending.remove(gen)


def _take(gen, count):
    for _ in range(count):
        try:
            next(gen)
        except StopIteration:
            return
        yield


def _lock(fn, *lists):
    return [fn(*args) for args in zip(*lists)]


def _split_bf16(x, terms):
    parts = []
    rem = x
    for _ in range(terms):
        hi = _bf(rem)
        parts.append(hi)
        rem = rem - hi.astype(F32)
    return parts


def _group_sum_lanes(x):
    lane = lax.broadcasted_iota(jnp.int32, (x.shape[0], LANES), 1)
    first = lane < HEAD
    outs = []
    for j in range(x.shape[1] // LANES):
        t = x[:, LANES * j:LANES * (j + 1)]
        s_first = jnp.sum(jnp.where(first, t, 0.0), axis=-1, keepdims=True)
        s_second = jnp.sum(jnp.where(first, 0.0, t), axis=-1, keepdims=True)
        outs.append(jnp.where(first, s_first, s_second))
    return jnp.concatenate(outs, axis=1)


def _sigmoid(x, scale=1.0):
    return 0.5 * scale + (0.5 * scale) * jnp.tanh(0.5 * x)


def _silu(x):
    h = 0.5 * x
    return h + h * jnp.tanh(h)


def _gelu_tanh(x):
    c = math.sqrt(2.0 / math.pi)
    h = 0.5 * x
    return h + h * jnp.tanh(x * (c + (c * 0.044715) * (x * x)))


def _block_diag(y, head_masks):
    zero = jnp.zeros((y.shape[0], LANES), BF16)
    rows = []
    for h in range(HEADS_PER_GROUP):
        tile = h // 2
        kept = y[:, LANES * tile:LANES * (tile + 1)] * head_masks[h % 2]
        rows.append(jnp.concatenate([kept if t == tile else zero
                                     for t in range(MXU_COLS // LANES)], axis=1))
    return jnp.concatenate(rows, axis=0)


def _unit_lower_inverse_steps(a_abs, eye, m8, off16, off32, off64, bd_mask):
    bd = lambda y: _block_diag(y, bd_mask)
    a8 = _lock(lambda a: a * m8, a_abs)
    a8b = _lock(_bf, a8)
    a2b = _lock(lambda a: _bf(_mm(a, bd(a))), a8b)
    yield
    p1 = _lock(lambda a: eye + a, a8)
    both = _lock(lambda a, p: _mm(jnp.concatenate([a, _bf(p)], axis=0), bd(a)), a2b, p1)
    yield
    a4b = _lock(lambda m: _bf(m[0:SCAN_L]), both)
    p2 = _lock(lambda p, m: p + m[SCAN_L:], p1, both)
    t = _lock(lambda p, a: p + _mm(_bf(p), bd(a)), p2, a4b)
    yield
    for off in (off16, off32, off64):
        tb = _lock(_bf, t)
        to = _lock(lambda x, a: _bf(_mm(x, bd(_bf(a * off)))), tb, a_abs)
        yield
        t = _lock(lambda x, xo, xb: x + _mm(xo, bd(xb)), t, to, tb)
        yield
    return t


def _scan_prepare_steps(r, k, v, kk, bb, g, lw, consts, bd_mask):
    eye, tri_strict, tri_incl, m8, off16, off32, off64 = consts
    bd = lambda y: _block_diag(y, bd_mask)

    def factors(r, k, v, kk, bb, g, lw):
        e_g = jnp.exp(g)
        e_ge = jnp.exp(g - lw)
        e_ng = jnp.exp(-g)
        e_gl = jnp.exp(g[SCAN_L - 1:SCAN_L, :] - g)
        return (_bf(-(kk * e_ge)), _bf(r * e_g), _bf(v), _bf(bb * e_ng), _bf(k * e_ng),
                _bf(bb * e_gl), _bf(k * e_gl))

    xa, xr, vb, bh, kh, bbar, kbar = zip(*_lock(factors, r, k, v, kk, bb, g, lw))
    lhs = _lock(lambda a, b: jnp.concatenate([a, b], axis=0), xa, xr)
    def pair_rows(y, sl):
        yp = y[:, sl]
        return jnp.concatenate([yp * bd_mask[0], yp * bd_mask[1]], axis=0)

    pairs = [slice(LANES * p, LANES * (p + 1)) for p in range(MXU_COLS // LANES)]
    aa = _lock(lambda x, yb, yk: [_mm_nt(x[:, sl], jnp.concatenate([pair_rows(yb, sl),
                                                                     pair_rows(yk, sl)], axis=0))
                                  for sl in pairs], lhs, bh, kh)
    yield
    aa_b = _lock(lambda parts: jnp.concatenate([m[:, 0:LANES] for m in parts], axis=1), aa)
    aa_k = _lock(lambda parts: jnp.concatenate([m[:, LANES:] for m in parts], axis=1), aa)
    a_ab = _lock(lambda m: jnp.where(tri_strict != 0.0, m[0:SCAN_L], 0.0), aa_b)
    a_rb = _lock(lambda m: _bf(jnp.where(tri_incl != 0.0, m[SCAN_L:], 0.0)), aa_b)
    a_ak = _lock(lambda m: _bf(jnp.where(tri_strict != 0.0, m[0:SCAN_L], 0.0)), aa_k)
    a_rk = _lock(lambda m: _bf(jnp.where(tri_incl != 0.0, m[SCAN_L:], 0.0)), aa_k)
    t_inv = yield from _unit_lower_inverse_steps(a_ab, eye, m8, off16, off32, off64, bd_mask)
    t_inv = _lock(_bf, t_inv)
    akv = _lock(lambda a, y: _bf(_mm(a, bd(y))), a_ak, vb)
    yield
    def pair_weights(y, sl):
        yp = y[:, sl]
        return jnp.concatenate([yp * bd_mask[0], yp * bd_mask[1]], axis=0)

    pair_slices = [slice(LANES * p, LANES * (p + 1)) for p in range(MXU_COLS // LANES)]
    qz = _lock(lambda t, a, b: [_mm(t[:, sl], jnp.concatenate([pair_weights(a, sl),
                                                                pair_weights(b, sl)], axis=1))
                                for sl in pair_slices], t_inv, xa, akv)
    yield
    q = _lock(lambda parts: _bf(jnp.concatenate([m[:, 0:LANES] for m in parts], axis=1)), qz)
    z = _lock(lambda parts: jnp.concatenate([m[:, LANES:] for m in parts], axis=1), qz)
    return q, list(xr), z, a_rb, a_rk, list(vb), list(bbar), list(kbar)


def _ada_kernel(c_ref, w_ref, b_ref, o_ref):
    s_hi, s_lo = _split_bf16(_silu(c_ref[...]), 2)
    w_hi, w_lo = _split_bf16(w_ref[...], 2)
    acc = _mm(s_hi, w_hi) + _mm(s_hi, w_lo) + _mm(s_lo, w_hi)
    o_ref[...] = acc + b_ref[...]


def _layer_kernel(x_ref, mod_ref, xn_ref, modn_ref,
                  win_ref, rows_ref, ws_ref, bs_ref, wlr_ref, wout_ref,
                  tri_ref, cm_ref,
                  out_ref,
                  h_scr, p_scr, pnext_scr, cat_scr, state_scr,
                  qx_scr, z_scr, ar_scr, vp_scr, bkt_scr,
                  dec_scr, y_scr, bonus_scr, sgate_scr, wm_scr):
    t_idx = pl.program_id(1)
    tile = x_ref.shape[0]
    top = SUBLANES

    @pl.when(t_idx == 0)
    def _():
        state_scr[...] = jnp.zeros_like(state_scr)
        p_scr[0:top, :] = jnp.zeros((top, N_COLS), F32)

    def row(name):
        offset, width = ROW_OFFSETS[name]
        return rows_ref[:, offset:offset + width]

    shift = mod_ref[0:1, :]
    scale = mod_ref[1:2, :]
    gate_over_alpha = mod_ref[2:3, :] * (1.0 / ALPHA)

    h_scr[...] = _bf(x_ref[...] * (1.0 + scale) + shift)

    @pl.when((pl.program_id(0) == 0) & (t_idx == 0))
    def _():
        pnext_scr[...] = _mm(h_scr[0:PREP_ROWS, :], win_ref[:, N_COLS_A:])

    def inproj_steps(row_blocks, col0, col1):
        for row0, nrows in row_blocks:
            hb = h_scr[row0:row0 + nrows, :]
            for i, c in enumerate(range(col0, col1, MXU_COLS)):
                ce = min(c + MXU_COLS, col1)
                p_scr[top + row0:top + row0 + nrows, c:ce] = _mm(hb, win_ref[:, c:ce])
                if i % 2 == 1:
                    yield
            yield

    row_a = lax.broadcasted_iota(jnp.int32, (CHUNK_A, CHUNK_A), 0)
    col_a = lax.broadcasted_iota(jnp.int32, (CHUNK_A, CHUNK_A), 1)
    causal = col_a <= row_a
    lane_a = lax.broadcasted_iota(jnp.int32, (CHUNK_A, LANES), 1)
    first_a = lane_a < HEAD

    for gidx in range(ws_ref.shape[0]):
        wm_scr[gidx] = _bf(jnp.where(causal, ws_ref[gidx], 0.0))

    def gmlp_steps(ci):
        rows = slice(top + ci * CHUNK_A, top + (ci + 1) * CHUNK_A)
        v = _gelu_tanh(p_scr[rows, D_A:2 * D_A])
        mean = _group_sum_lanes(v) * (1.0 / HEAD)
        yield
        vc = v - mean
        var = _group_sum_lanes(vc * vc) * (1.0 / HEAD)
        yield
        v = vc * lax.rsqrt(var + LN_EPS) * row("ln_v_g") + row("ln_v_b")
        mixed = []
        for j in range(D_A // LANES):
            vj = _bf(v[:, LANES * j:LANES * (j + 1)])
            mixed.append(jnp.where(first_a, _mm(wm_scr[2 * j], vj), _mm(wm_scr[2 * j + 1], vj)))
        yield
        mix = jnp.concatenate(mixed, axis=1) + bs_ref[...]
        u = _gelu_tanh(p_scr[rows, 0:D_A])
        out_a = u * mix * _silu(p_scr[rows, 2 * D_A:3 * D_A])
        cat_scr[ci * CHUNK_A:(ci + 1) * CHUNK_A, 0:D_A] = _bf(out_a)

    consts = tuple(cm_ref[i] for i in range(7))
    row_b = lax.broadcasted_iota(jnp.int32, (PREP_ROWS, N_COLS_B), 0)
    lane_b = lax.broadcasted_iota(jnp.int32, (PREP_ROWS, LANES), 1)
    lane_s = lax.broadcasted_iota(jnp.int32, (SCAN_L, LANES), 1)
    first_f = lane_s < HEAD
    first_b = jnp.where(first_f, 1.0, 0.0).astype(BF16)
    second_b = jnp.where(first_f, 0.0, 1.0).astype(BF16)
    bd_mask = (first_b, second_b)
    cb = N_COLS_A

    def stack2b(zz):
        return jnp.concatenate([zz * first_b, zz * second_b], axis=0)

    def prepare_steps(bi):
        r0 = top + bi * PREP_ROWS
        if bi == 0:
            cur = pnext_scr[...]
            before = p_scr[0:top, cb:]
        else:
            cur = p_scr[r0:r0 + PREP_ROWS, cb:]
            before = (pnext_scr[PREP_ROWS - SUBLANES:, :] if bi == 1
                      else p_scr[r0 - SUBLANES:r0, cb:])
        prev = jnp.where(row_b == 0, before[SUBLANES - 1:SUBLANES, :], pltpu.roll(cur, 1, axis=0))
        ps = cur + row("mu_b") * (prev - cur)
        r = ps[:, 0:D_B]
        k = ps[:, D_B:2 * D_B]
        v = ps[:, 2 * D_B:3 * D_B]
        gate_b = ps[:, 3 * D_B:4 * D_B]
        low = ps[:, 4 * D_B:]
        low = jnp.where(lane_b < LOW_RANK, jnp.tanh(low), low)
        up = _mm(_bf(low), wlr_ref[...])
        yield
        lw = _sigmoid(row("w0") + up[:, 0:D_B], LOG_DECAY_SCALE)
        ah = _sigmoid(row("a0") + up[:, D_B:])
        lw_terms = jnp.concatenate(_split_bf16(lw, 2), axis=1)
        g = jnp.concatenate([_mm(tri_ref[...], lw_terms[SCAN_L * c:SCAN_L * (c + 1), :])
                             for c in range(PREP_CHUNKS)], axis=0)
        g = g[:, 0:D_B] + g[:, D_B:]
        kk = k * row("k_k")
        kk = kk * lax.rsqrt(_group_sum_lanes(kk * kk) + 1e-12)
        yield
        k = k * (1.0 + (ah - 1.0) * row("k_a"))
        bb = kk * ah

        o0 = bi * PREP_ROWS
        bonus_scr[o0:o0 + PREP_ROWS, :] = _group_sum_lanes(r * k * row("r_k")) * v
        sgate_scr[o0:o0 + PREP_ROWS, :] = _silu(gate_b)
        yield

        problems = [(c, gi) for c in range(PREP_CHUNKS) for gi in range(N_GROUPS)]
        cut = lambda zz: [zz[SCAN_L * c:SCAN_L * (c + 1), MXU_COLS * gi:MXU_COLS * (gi + 1)]
                          for c, gi in problems]
        for c in range(PREP_CHUNKS):
            last = jnp.exp(g[SCAN_L * (c + 1) - 1:SCAN_L * (c + 1), :])
            for j in range(N_PAIRS):
                rowwise = jnp.broadcast_to(last[:, LANES * j:LANES * (j + 1)], (LANES, LANES))
                dec_scr[(bi * PREP_CHUNKS + c) * N_PAIRS + j] = rowwise.T

        q, xr, z, a_rb, a_rk, vb, bbar, kbar = yield from _scan_prepare_steps(
            cut(r), cut(k), cut(v), cut(kk), cut(bb), cut(g), cut(lw), consts, bd_mask)
        for i, (c, gi) in enumerate(problems):
            for pp in range(MXU_COLS // LANES):
                slot = (bi * PREP_CHUNKS + c) * N_PAIRS + gi * (MXU_COLS // LANES) + pp
                sl = slice(LANES * pp, LANES * (pp + 1))
                qx_scr[slot] = jnp.concatenate([q[i][:, sl], xr[i][:, sl]], axis=0)
                z_scr[slot] = z[i][:, sl]
                ar_scr[slot] = jnp.concatenate([a_rb[i][:, sl], a_rk[i][:, sl]], axis=1)
                vp_scr[slot] = stack2b(vb[i][:, sl])
                bkt_scr[slot] = jnp.concatenate([stack2b(bbar[i][:, sl]), stack2b(kbar[i][:, sl])],
                                                axis=0).T

    n_prep = N_CHUNKS // PREP_CHUNKS
    row_block = lambda i: (i * PREP_ROWS, PREP_ROWS)
    for bi in range(n_prep):
        if bi + 1 < n_prep:
            fill = inproj_steps([row_block(bi + 1)], cb, N_COLS)
        else:
            fill = inproj_steps([row_block(i) for i in range(n_prep)], 0, cb)
        _interleave([prepare_steps(bi), fill])

    def apply_steps(bi):
        states = [state_scr[j] for j in range(N_PAIRS)]
        for c in range(CHUNKS_PER_BLOCK):
            ci = bi * CHUNKS_PER_BLOCK + c
            load = lambda ref: [ref[ci * N_PAIRS + j] for j in range(N_PAIRS)]
            xs = _lock(lambda a, s: _mm(a, _bf(s)), load(qx_scr), states)
            yield
            uv = _lock(lambda x, zz, w: jnp.concatenate([stack2b(_bf(x[0:SCAN_L] + zz)), w], axis=0),
                       xs, load(z_scr), load(vp_scr))
            both = _lock(lambda b, a, m: _mm(jnp.concatenate([b, a], axis=0), m),
                         load(bkt_scr), load(ar_scr), uv)
            yield
            states = _lock(lambda s, d, m: s * d + m[0:LANES], states, load(dec_scr), both)
            ys = _lock(lambda x, m: x[SCAN_L:] + m[LANES:], xs, both)
            y_scr[ci * SCAN_L:(ci + 1) * SCAN_L, :] = jnp.concatenate(ys, axis=1)
        for j in range(N_PAIRS):
            state_scr[j] = states[j]

    def post_steps(bi):
        rows = slice(bi * CHUNK_A, (bi + 1) * CHUNK_A)
        y = y_scr[rows, :]
        mean = _group_sum_lanes(y) * (1.0 / HEAD)
        yield
        yc = y - mean
        var = _group_sum_lanes(yc * yc) * (1.0 / HEAD)
        yield
        y = yc * lax.rsqrt(var + GN_EPS) * row("gn_g") + row("gn_b")
        cat_scr[rows, D_A:] = _bf((y + bonus_scr[rows, :]) * sgate_scr[rows, :])
        o = _mm(cat_scr[rows, :], wout_ref[...])
        yield
        res = x_ref[rows, :] + gate_over_alpha * o
        mean = jnp.mean(res, axis=-1, keepdims=True)
        rc = res - mean
        var = jnp.mean(rc * rc, axis=-1, keepdims=True)
        out_ref[rows, :] = rc * lax.rsqrt(var + LN_EPS / ALPHA ** 2) * row("ln_g") + row("ln_b")

    def next_inproj_steps():
        hn = _bf(xn_ref[...] * (1.0 + modn_ref[1:2, :]) + modn_ref[0:1, :])
        for c in range(0, N_COLS_B, MXU_COLS):
            ce = min(c + MXU_COLS, N_COLS_B)
            pnext_scr[:, c:ce] = _mm(hn, win_ref[:, cb + c:cb + ce])
            yield

    n_blocks = tile // CHUNK_A
    next_proj = next_inproj_steps()
    next_steps_per_block = -(-len(range(0, N_COLS_B, MXU_COLS)) // n_blocks)
    for bi in range(n_blocks):
        group = [apply_steps(bi)]
        if bi > 0:
            group.append(post_steps(bi - 1))
        group += [gmlp_steps(bi), _take(next_proj, next_steps_per_block)]
        _interleave(group)
    _interleave([post_steps(n_blocks - 1), next_proj])

    p_scr[0:top, :] = p_scr[tile:tile + top, :]


def _wide_masks():
    t = np.arange(SCAN_L)[:, None]
    s = (np.arange(MXU_COLS) % SCAN_L)[None, :]
    same = lambda n: (t // n) == (s // n)
    masks = [t == s, s < t, s <= t, same(8),
             same(16) & ~same(8), same(32) & ~same(16), same(64) & ~same(32)]
    return jnp.asarray(np.stack(masks), F32)


def _const_spec(shape):
    zeros = (0,) * len(shape)
    return pl.BlockSpec(shape, lambda b, t: zeros)


def kernel(x, c, w_ada, b_ada, w_in, mu_b, ln_v_g, ln_v_b, w_spatial, b_spatial, w0, w_up, a0,
           a_up, k_k, k_a, r_k, gn_g, gn_b, w_out, ln_g, ln_b):
    batch, seq, d_model = x.shape
    assert d_model == D_MODEL and w_in.shape == (D_MODEL, N_COLS)
    assert seq % TILE_T == 0 and TILE_T % CHUNK_A == 0 and N_CHUNKS % PREP_CHUNKS == 0
    assert N_CHUNKS // PREP_CHUNKS >= 2

    mod = pl.pallas_call(
        _ada_kernel,
        out_shape=jax.ShapeDtypeStruct((batch, 3 * d_model), F32),
        name="adaln_mod",
    )(c, w_ada, b_ada.reshape(1, -1))
    mod = mod.reshape(batch, 3, d_model)

    named = dict(mu_b=mu_b, ln_v_g=ln_v_g, ln_v_b=ln_v_b, w0=w0, a0=a0, k_k=k_k, k_a=k_a, r_k=r_k,
                 gn_g=gn_g, gn_b=gn_b, ln_g=ln_g, ln_b=ln_b)
    rows = jnp.concatenate([named[name].reshape(-1).astype(F32) for name, _ in ROW_PARAMS])
    rows = rows.reshape(1, -1)
    tok = np.arange(SCAN_L)
    tri = jnp.asarray(tok[None, :] <= tok[:, None], BF16)
    zeros_lr = jnp.zeros((LOW_RANK, D_B), F32)
    w_lr = jnp.concatenate([jnp.concatenate([w_up, zeros_lr], axis=1),
                            jnp.concatenate([zeros_lr, a_up], axis=1)], axis=0).astype(BF16)
    bias_a = jnp.repeat(b_spatial.T, HEAD, axis=1)

    tiles = seq // TILE_T

    def next_block(b, t):
        s = jnp.minimum(b * tiles + t + 1, batch * tiles - 1)
        return s // tiles, (s % tiles) * (TILE_T // PREP_ROWS), 0

    operands = [
        (x, pl.BlockSpec((None, TILE_T, d_model), lambda b, t: (b, t, 0))),
        (mod, pl.BlockSpec((None, 3, d_model), lambda b, t: (b, 0, 0))),
        (x, pl.BlockSpec((None, PREP_ROWS, d_model), next_block)),
        (mod, pl.BlockSpec((None, 3, d_model), lambda b, t: (next_block(b, t)[0], 0, 0))),
        (w_in.astype(BF16), _const_spec((d_model, N_COLS))),
        (rows, _const_spec(rows.shape)),
        (w_spatial, _const_spec(w_spatial.shape)),
        (bias_a, _const_spec((CHUNK_A, D_A))),
        (w_lr, _const_spec((2 * LOW_RANK, 2 * D_B))),
        (w_out.astype(BF16), _const_spec((d_model, d_model))),
        (tri, _const_spec((SCAN_L, SCAN_L))),
        (_wide_masks(), _const_spec((7, SCAN_L, MXU_COLS))),
    ]
    n_slots = N_CHUNKS * N_PAIRS
    return pl.pallas_call(
        _layer_kernel,
        grid=(batch, seq // TILE_T),
        in_specs=[spec for _, spec in operands],
        out_specs=pl.BlockSpec((None, TILE_T, d_model), lambda b, t: (b, t, 0)),
        out_shape=jax.ShapeDtypeStruct(x.shape, x.dtype),
        scratch_shapes=[
            pltpu.VMEM((TILE_T, d_model), BF16),
            pltpu.VMEM((TILE_T + SUBLANES, N_COLS), F32),
            pltpu.VMEM((PREP_ROWS, N_COLS_B), F32),
            pltpu.VMEM((TILE_T, d_model), BF16),
            pltpu.VMEM((N_PAIRS, LANES, LANES), F32),
            pltpu.VMEM((n_slots, 2 * SCAN_L, LANES), BF16),
            pltpu.VMEM((n_slots, SCAN_L, LANES), F32),
            pltpu.VMEM((n_slots, SCAN_L, 2 * PAIR_ROWS), BF16),
            pltpu.VMEM((n_slots, PAIR_ROWS, LANES), BF16),
            pltpu.VMEM((n_slots, LANES, 2 * PAIR_ROWS), BF16),
            pltpu.VMEM((n_slots, LANES, LANES), F32),
            pltpu.VMEM((TILE_T, D_B), F32),
            pltpu.VMEM((TILE_T, D_B), F32),
            pltpu.VMEM((TILE_T, D_B), F32),
            pltpu.VMEM((D_A // HEAD, CHUNK_A, CHUNK_A), BF16),
        ],
        compiler_params=pltpu.CompilerParams(
            dimension_semantics=("arbitrary", "arbitrary"),
            vmem_limit_bytes=VMEM_LIMIT_BYTES),
        name="hybrid_layer",
    )(*[arr for arr, _ in operands])
```

```python
import math

import jax
import jax.numpy as jnp
import numpy as np
from jax import lax
from jax.experimental import pallas as pl
from jax.experimental.pallas import tpu as pltpu

F32 = jnp.float32
BF16 = jnp.bfloat16

D_MODEL = 1024
D_A = 512
D_B = 512
HEAD = 64
CHUNK_A = 128
LOW_RANK = 64
N_COLS_A = 3 * D_A
N_COLS_B = 4 * D_B + 2 * LOW_RANK
N_COLS = N_COLS_A + N_COLS_B
LN_EPS = 1e-5
GN_EPS = 64e-5
ALPHA = 2.0 ** 0.25
LOG_DECAY_SCALE = -math.exp(-0.5)

LANES = 128
SUBLANES = 8
MXU_COLS = 256
N_PAIRS = D_B // LANES
N_GROUPS = D_B // MXU_COLS
HEADS_PER_GROUP = MXU_COLS // HEAD
SCAN_L = 64
PAIR_ROWS = 2 * SCAN_L
TILE_T = 512
N_CHUNKS = TILE_T // SCAN_L
PREP_CHUNKS = 4
PREP_ROWS = PREP_CHUNKS * SCAN_L
CHUNKS_PER_BLOCK = CHUNK_A // SCAN_L
VMEM_LIMIT_BYTES = 56 * 1024 * 1024

ROW_PARAMS = (("mu_b", N_COLS_B), ("ln_v_g", D_A), ("ln_v_b", D_A), ("w0", D_B), ("a0", D_B),
              ("k_k", D_B), ("k_a", D_B), ("r_k", D_B), ("gn_g", D_B), ("gn_b", D_B),
              ("ln_g", D_MODEL), ("ln_b", D_MODEL))
ROW_OFFSETS = {}
for _name, _width in ROW_PARAMS:
    ROW_OFFSETS[_name] = (sum(w for _, w in ROW_PARAMS[:len(ROW_OFFSETS)]), _width)


def _bf(x):
    return x.astype(BF16)


def _mm(a, b):
    return jnp.dot(a, b, preferred_element_type=F32)


def _mm_nt(a, b):
    return lax.dot_general(a, b, (((1,), (1,)), ((), ())), preferred_element_type=F32)


def _interleave(step_generators):
    pending = list(step_generators)
    while pending:
        for gen in list(pending):
            try:
                next(gen)
            except StopIteration:
                pending.remove(gen)


def _take(gen, count):
    for _ in range(count):
        try:
            next(gen)
        except StopIteration:
            return
        yield


def _lock(fn, *lists):
    return [fn(*args) for args in zip(*lists)]


def _split_bf16(x, terms):
    parts = []
    rem = x
    for _ in range(terms):
        hi = _bf(rem)
        parts.append(hi)
        rem = rem - hi.astype(F32)
    return parts


def _group_sum_lanes(x):
    lane = lax.broadcasted_iota(jnp.int32, (x.shape[0], LANES), 1)
    first = lane < HEAD
    outs = []
    for j in range(x.shape[1] // LANES):
        t = x[:, LANES * j:LANES * (j + 1)]
        s_first = jnp.sum(jnp.where(first, t, 0.0), axis=-1, keepdims=True)
        s_second = jnp.sum(jnp.where(first, 0.0, t), axis=-1, keepdims=True)
        outs.append(jnp.where(first, s_first, s_second))
    return jnp.concatenate(outs, axis=1)


def _sigmoid(x, scale=1.0):
    return 0.5 * scale + (0.5 * scale) * jnp.tanh(0.5 * x)


def _silu(x):
    h = 0.5 * x
    return h + h * jnp.tanh(h)


def _gelu_tanh(x):
    c = math.sqrt(2.0 / math.pi)
    h = 0.5 * x
    return h + h * jnp.tanh(x * (c + (c * 0.044715) * (x * x)))


def _block_diag(y, head_masks):
    zero = jnp.zeros((y.shape[0], LANES), BF16)
    rows = []
    for h in range(HEADS_PER_GROUP):
        tile = h // 2
        kept = y[:, LANES * tile:LANES * (tile + 1)] * head_masks[h % 2]
        rows.append(jnp.concatenate([kept if t == tile else zero
                                     for t in range(MXU_COLS // LANES)], axis=1))
    return jnp.concatenate(rows, axis=0)


def _unit_lower_inverse_steps(a_abs, eye, m8, off16, off32, off64, bd_mask):
    bd = lambda y: _block_diag(y, bd_mask)
    a8 = _lock(lambda a: a * m8, a_abs)
    a8b = _lock(_bf, a8)
    a2b = _lock(lambda a: _bf(_mm(a, bd(a))), a8b)
    yield
    p1 = _lock(lambda a: eye + a, a8)
    both = _lock(lambda a, p: _mm(jnp.concatenate([a, _bf(p)], axis=0), bd(a)), a2b, p1)
    yield
    a4b = _lock(lambda m: _bf(m[0:SCAN_L]), both)
    p2 = _lock(lambda p, m: p + m[SCAN_L:], p1, both)
    t = _lock(lambda p, a: p + _mm(_bf(p), bd(a)), p2, a4b)
    yield
    for off in (off16, off32, off64):
        tb = _lock(_bf, t)
        to = _lock(lambda x, a: _bf(_mm(x, bd(_bf(a * off)))), tb, a_abs)
        yield
        t = _lock(lambda x, xo, xb: x + _mm(xo, bd(xb)), t, to, tb)
        yield
    return t


def _scan_prepare_steps(r, k, v, kk, bb, g, lw, consts, bd_mask):
    eye, tri_strict, tri_incl, m8, off16, off32, off64 = consts
    bd = lambda y: _block_diag(y, bd_mask)

    def factors(r, k, v, kk, bb, g, lw):
        e_g = jnp.exp(g)
        e_ge = jnp.exp(g - lw)
        e_ng = jnp.exp(-g)
        e_gl = jnp.exp(g[SCAN_L - 1:SCAN_L, :] - g)
        return (_bf(-(kk * e_ge)), _bf(r * e_g), _bf(v), _bf(bb * e_ng), _bf(k * e_ng),
                _bf(bb * e_gl), _bf(k * e_gl))

    xa, xr, vb, bh, kh, bbar, kbar = zip(*_lock(factors, r, k, v, kk, bb, g, lw))
    lhs = _lock(lambda a, b: jnp.concatenate([a, b], axis=0), xa, xr)
    def pair_rows(y, sl):
        yp = y[:, sl]
        return jnp.concatenate([yp * bd_mask[0], yp * bd_mask[1]], axis=0)

    pairs = [slice(LANES * p, LANES * (p + 1)) for p in range(MXU_COLS // LANES)]
    aa = _lock(lambda x, yb, yk: [_mm_nt(x[:, sl], jnp.concatenate([pair_rows(yb, sl),
                                                                     pair_rows(yk, sl)], axis=0))
                                  for sl in pairs], lhs, bh, kh)
    yield
    aa_b = _lock(lambda parts: jnp.concatenate([m[:, 0:LANES] for m in parts], axis=1), aa)
    aa_k = _lock(lambda parts: jnp.concatenate([m[:, LANES:] for m in parts], axis=1), aa)
    a_ab = _lock(lambda m: jnp.where(tri_strict != 0.0, m[0:SCAN_L], 0.0), aa_b)
    a_rb = _lock(lambda m: _bf(jnp.where(tri_incl != 0.0, m[SCAN_L:], 0.0)), aa_b)
    a_ak = _lock(lambda m: _bf(jnp.where(tri_strict != 0.0, m[0:SCAN_L], 0.0)), aa_k)
    a_rk = _lock(lambda m: _bf(jnp.where(tri_incl != 0.0, m[SCAN_L:], 0.0)), aa_k)
    t_inv = yield from _unit_lower_inverse_steps(a_ab, eye, m8, off16, off32, off64, bd_mask)
    t_inv = _lock(_bf, t_inv)
    akv = _lock(lambda a, y: _bf(_mm(a, bd(y))), a_ak, vb)
    yield
    def pair_weights(y, sl):
        yp = y[:, sl]
        return jnp.concatenate([yp * bd_mask[0], yp * bd_mask[1]], axis=0)

    pair_slices = [slice(LANES * p, LANES * (p + 1)) for p in range(MXU_COLS // LANES)]
    qz = _lock(lambda t, a, b: [_mm(t[:, sl], jnp.concatenate([pair_weights(a, sl),
                                                                pair_weights(b, sl)], axis=1))
                                for sl in pair_slices], t_inv, xa, akv)
    yield
    q = _lock(lambda parts: _bf(jnp.concatenate([m[:, 0:LANES] for m in parts], axis=1)), qz)
    z = _lock(lambda parts: jnp.concatenate([m[:, LANES:] for m in parts], axis=1), qz)
    return q, list(xr), z, a_rb, a_rk, list(vb), list(bbar), list(kbar)


def _ada_kernel(c_ref, w_ref, b_ref, o_ref):
    s_hi, s_lo = _split_bf16(_silu(c_ref[...]), 2)
    w_hi, w_lo = _split_bf16(w_ref[...], 2)
    acc = _mm(s_hi, w_hi) + _mm(s_hi, w_lo) + _mm(s_lo, w_hi)
    o_ref[...] = acc + b_ref[...]


def _layer_kernel(x_ref, mod_ref, xn_ref, modn_ref,
                  win_ref, rows_ref, ws_ref, bs_ref, wlr_ref, wout_ref,
                  tri_ref, cm_ref,
                  out_ref,
                  h_scr, p_scr, pnext_scr, cat_scr, state_scr,
                  qx_scr, z_scr, ar_scr, vp_scr, bkt_scr,
                  dec_scr, y_scr, bonus_scr, sgate_scr, wm_scr):
    t_idx = pl.program_id(1)
    tile = x_ref.shape[0]
    top = SUBLANES

    @pl.when(t_idx == 0)
    def _():
        state_scr[...] = jnp.zeros_like(state_scr)
        p_scr[0:top, :] = jnp.zeros((top, N_COLS), F32)

    def row(name):
        offset, width = ROW_OFFSETS[name]
        return rows_ref[:, offset:offset + width]

    shift = mod_ref[0:1, :]
    scale = mod_ref[1:2, :]
    gate_over_alpha = mod_ref[2:3, :] * (1.0 / ALPHA)

    h_scr[...] = _bf(x_ref[...] * (1.0 + scale) + shift)

    @pl.when((pl.program_id(0) == 0) & (t_idx == 0))
    def _():
        pnext_scr[...] = _mm(h_scr[0:PREP_ROWS, :], win_ref[:, N_COLS_A:])

    def inproj_steps(row_blocks, col0, col1):
        for row0, nrows in row_blocks:
            hb = h_scr[row0:row0 + nrows, :]
            for i, c in enumerate(range(col0, col1, MXU_COLS)):
                ce = min(c + MXU_COLS, col1)
                p_scr[top + row0:top + row0 + nrows, c:ce] = _mm(hb, win_ref[:, c:ce])
                if i % 2 == 1:
                    yield
            yield

    row_a = lax.broadcasted_iota(jnp.int32, (CHUNK_A, CHUNK_A), 0)
    col_a = lax.broadcasted_iota(jnp.int32, (CHUNK_A, CHUNK_A), 1)
    causal = col_a <= row_a
    lane_a = lax.broadcasted_iota(jnp.int32, (CHUNK_A, LANES), 1)
    first_a = lane_a < HEAD

    for gidx in range(ws_ref.shape[0]):
        wm_scr[gidx] = _bf(jnp.where(causal, ws_ref[gidx], 0.0))

    def gmlp_steps(ci):
        rows = slice(top + ci * CHUNK_A, top + (ci + 1) * CHUNK_A)
        v = _gelu_tanh(p_scr[rows, D_A:2 * D_A])
        mean = _group_sum_lanes(v) * (1.0 / HEAD)
        yield
        vc = v - mean
        var = _group_sum_lanes(vc * vc) * (1.0 / HEAD)
        yield
        v = vc * lax.rsqrt(var + LN_EPS) * row("ln_v_g") + row("ln_v_b")
        mixed = []
        for j in range(D_A // LANES):
            vj = _bf(v[:, LANES * j:LANES * (j + 1)])
            mixed.append(jnp.where(first_a, _mm(wm_scr[2 * j], vj), _mm(wm_scr[2 * j + 1], vj)))
        yield
        mix = jnp.concatenate(mixed, axis=1) + bs_ref[...]
        u = _gelu_tanh(p_scr[rows, 0:D_A])
        out_a = u * mix * _silu(p_scr[rows, 2 * D_A:3 * D_A])
        cat_scr[ci * CHUNK_A:(ci + 1) * CHUNK_A, 0:D_A] = _bf(out_a)

    consts = tuple(cm_ref[i] for i in range(7))
    row_b = lax.broadcasted_iota(jnp.int32, (PREP_ROWS, N_COLS_B), 0)
    lane_b = lax.broadcasted_iota(jnp.int32, (PREP_ROWS, LANES), 1)
    lane_s = lax.broadcasted_iota(jnp.int32, (SCAN_L, LANES), 1)
    first_f = lane_s < HEAD
    first_b = jnp.where(first_f, 1.0, 0.0).astype(BF16)
    second_b = jnp.where(first_f, 0.0, 1.0).astype(BF16)
    bd_mask = (first_b, second_b)
    cb = N_COLS_A

    def stack2b(zz):
        return jnp.concatenate([zz * first_b, zz * second_b], axis=0)

    def prepare_steps(bi):
        r0 = top + bi * PREP_ROWS
        if bi == 0:
            cur = pnext_scr[...]
            before = p_scr[0:top, cb:]
        else:
            cur = p_scr[r0:r0 + PREP_ROWS, cb:]
            before = (pnext_scr[PREP_ROWS - SUBLANES:, :] if bi == 1
                      else p_scr[r0 - SUBLANES:r0, cb:])
        prev = jnp.where(row_b == 0, before[SUBLANES - 1:SUBLANES, :], pltpu.roll(cur, 1, axis=0))
        ps = cur + row("mu_b") * (prev - cur)
        r = ps[:, 0:D_B]
        k = ps[:, D_B:2 * D_B]
        v = ps[:, 2 * D_B:3 * D_B]
        gate_b = ps[:, 3 * D_B:4 * D_B]
        low = ps[:, 4 * D_B:]
        low = jnp.where(lane_b < LOW_RANK, jnp.tanh(low), low)
        up = _mm(_bf(low), wlr_ref[...])
        yield
        lw = _sigmoid(row("w0") + up[:, 0:D_B], LOG_DECAY_SCALE)
        ah = _sigmoid(row("a0") + up[:, D_B:])
        lw_terms = jnp.concatenate(_split_bf16(lw, 2), axis=1)
        g = jnp.concatenate([_mm(tri_ref[...], lw_terms[SCAN_L * c:SCAN_L * (c + 1), :])
                             for c in range(PREP_CHUNKS)], axis=0)
        g = g[:, 0:D_B] + g[:, D_B:]
        kk = k * row("k_k")
        kk = kk * lax.rsqrt(_group_sum_lanes(kk * kk) + 1e-12)
        yield
        k = k * (1.0 + (ah - 1.0) * row("k_a"))
        bb = kk * ah

        o0 = bi * PREP_ROWS
        bonus_scr[o0:o0 + PREP_ROWS, :] = _group_sum_lanes(r * k * row("r_k")) * v
        sgate_scr[o0:o0 + PREP_ROWS, :] = _silu(gate_b)
        yield

        problems = [(c, gi) for c in range(PREP_CHUNKS) for gi in range(N_GROUPS)]
        cut = lambda zz: [zz[SCAN_L * c:SCAN_L * (c + 1), MXU_COLS * gi:MXU_COLS * (gi + 1)]
                          for c, gi in problems]
        for c in range(PREP_CHUNKS):
            last = jnp.exp(g[SCAN_L * (c + 1) - 1:SCAN_L * (c + 1), :])
            for j in range(N_PAIRS):
                rowwise = jnp.broadcast_to(last[:, LANES * j:LANES * (j + 1)], (LANES, LANES))
                dec_scr[(bi * PREP_CHUNKS + c) * N_PAIRS + j] = rowwise.T

        q, xr, z, a_rb, a_rk, vb, bbar, kbar = yield from _scan_prepare_steps(
            cut(r), cut(k), cut(v), cut(kk), cut(bb), cut(g), cut(lw), consts, bd_mask)
        for i, (c, gi) in enumerate(problems):
            for pp in range(MXU_COLS // LANES):
                slot = (bi * PREP_CHUNKS + c) * N_PAIRS + gi * (MXU_COLS // LANES) + pp
                sl = slice(LANES * pp, LANES * (pp + 1))
                qx_scr[slot] = jnp.concatenate([q[i][:, sl], xr[i][:, sl]], axis=0)
                z_scr[slot] = z[i][:, sl]
                ar_scr[slot] = jnp.concatenate([a_rb[i][:, sl], a_rk[i][:, sl]], axis=1)
                vp_scr[slot] = stack2b(vb[i][:, sl])
                bkt_scr[slot] = jnp.concatenate([stack2b(bbar[i][:, sl]), stack2b(kbar[i][:, sl])],
                                                axis=0).T

    n_prep = N_CHUNKS // PREP_CHUNKS
    row_block = lambda i: (i * PREP_ROWS, PREP_ROWS)
    for bi in range(n_prep):
        if bi + 1 < n_prep:
            fill = inproj_steps([row_block(bi + 1)], cb, N_COLS)
        else:
            fill = inproj_steps([row_block(i) for i in range(n_prep)], 0, cb)
        _interleave([prepare_steps(bi), fill])

    def apply_steps(bi):
        states = [state_scr[j] for j in range(N_PAIRS)]
        for c in range(CHUNKS_PER_BLOCK):
            ci = bi * CHUNKS_PER_BLOCK + c
            load = lambda ref: [ref[ci * N_PAIRS + j] for j in range(N_PAIRS)]
            xs = _lock(lambda a, s: _mm(a, _bf(s)), load(qx_scr), states)
            yield
            uv = _lock(lambda x, zz, w: jnp.concatenate([stack2b(_bf(x[0:SCAN_L] + zz)), w], axis=0),
                       xs, load(z_scr), load(vp_scr))
            both = _lock(lambda b, a, m: _mm(jnp.concatenate([b, a], axis=0), m),
                         load(bkt_scr), load(ar_scr), uv)
            yield
            states = _lock(lambda s, d, m: s * d + m[0:LANES], states, load(dec_scr), both)
            ys = _lock(lambda x, m: x[SCAN_L:] + m[LANES:], xs, both)
            y_scr[ci * SCAN_L:(ci + 1) * SCAN_L, :] = jnp.concatenate(ys, axis=1)
        for j in range(N_PAIRS):
            state_scr[j] = states[j]

    def post_steps(bi):
        rows = slice(bi * CHUNK_A, (bi + 1) * CHUNK_A)
        y = y_scr[rows, :]
        mean = _group_sum_lanes(y) * (1.0 / HEAD)
        yield
        yc = y - mean
        var = _group_sum_lanes(yc * yc) * (1.0 / HEAD)
        yield
        y = yc * lax.rsqrt(var + GN_EPS) * row("gn_g") + row("gn_b")
        cat_scr[rows, D_A:] = _bf((y + bonus_scr[rows, :]) * sgate_scr[rows, :])
        o = _mm(cat_scr[rows, :], wout_ref[...])
        yield
        res = x_ref[rows, :] + gate_over_alpha * o
        mean = jnp.mean(res, axis=-1, keepdims=True)
        rc = res - mean
        var = jnp.mean(rc * rc, axis=-1, keepdims=True)
        out_ref[rows, :] = rc * lax.rsqrt(var + LN_EPS / ALPHA ** 2) * row("ln_g") + row("ln_b")

    def next_inproj_steps():
        hn = _bf(xn_ref[...] * (1.0 + modn_ref[1:2, :]) + modn_ref[0:1, :])
        for i, c in enumerate(range(0, N_COLS_B, MXU_COLS)):
            ce = min(c + MXU_COLS, N_COLS_B)
            pnext_scr[:, c:ce] = _mm(hn, win_ref[:, cb + c:cb + ce])
            if i % 2 == 1:
                yield
        yield

    n_blocks = tile // CHUNK_A
    next_proj = next_inproj_steps()
    next_steps_per_block = -(-len(range(0, N_COLS_B, 2 * MXU_COLS)) // n_blocks)
    for bi in range(n_blocks):
        group = [apply_steps(bi)]
        if bi > 0:
            group.append(post_steps(bi - 1))
        group += [gmlp_steps(bi), _take(next_proj, next_steps_per_block)]
        _interleave(group)
    _interleave([post_steps(n_blocks - 1), next_proj])

    p_scr[0:top, :] = p_scr[tile:tile + top, :]


def _wide_masks():
    t = np.arange(SCAN_L)[:, None]
    s = (np.arange(MXU_COLS) % SCAN_L)[None, :]
    same = lambda n: (t // n) == (s // n)
    masks = [t == s, s < t, s <= t, same(8),
             same(16) & ~same(8), same(32) & ~same(16), same(64) & ~same(32)]
    return jnp.asarray(np.stack(masks), F32)


def _const_spec(shape):
    zeros = (0,) * len(shape)
    return pl.BlockSpec(shape, lambda b, t: zeros)


def kernel(x, c, w_ada, b_ada, w_in, mu_b, ln_v_g, ln_v_b, w_spatial, b_spatial, w0, w_up, a0,
           a_up, k_k, k_a, r_k, gn_g, gn_b, w_out, ln_g, ln_b):
    batch, seq, d_model = x.shape
    assert d_model == D_MODEL and w_in.shape == (D_MODEL, N_COLS)
    assert seq % TILE_T == 0 and TILE_T % CHUNK_A == 0 and N_CHUNKS % PREP_CHUNKS == 0
    assert N_CHUNKS // PREP_CHUNKS >= 2

    mod = pl.pallas_call(
        _ada_kernel,
        out_shape=jax.ShapeDtypeStruct((batch, 3 * d_model), F32),
        name="adaln_mod",
    )(c, w_ada, b_ada.reshape(1, -1))
    mod = mod.reshape(batch, 3, d_model)

    named = dict(mu_b=mu_b, ln_v_g=ln_v_g, ln_v_b=ln_v_b, w0=w0, a0=a0, k_k=k_k, k_a=k_a, r_k=r_k,
                 gn_g=gn_g, gn_b=gn_b, ln_g=ln_g, ln_b=ln_b)
    rows = jnp.concatenate([named[name].reshape(-1).astype(F32) for name, _ in ROW_PARAMS])
    rows = rows.reshape(1, -1)
    tok = np.arange(SCAN_L)
    tri = jnp.asarray(tok[None, :] <= tok[:, None], BF16)
    zeros_lr = jnp.zeros((LOW_RANK, D_B), F32)
    w_lr = jnp.concatenate([jnp.concatenate([w_up, zeros_lr], axis=1),
                            jnp.concatenate([zeros_lr, a_up], axis=1)], axis=0).astype(BF16)
    bias_a = jnp.repeat(b_spatial.T, HEAD, axis=1)

    tiles = seq // TILE_T

    def next_block(b, t):
        s = jnp.minimum(b * tiles + t + 1, batch * tiles - 1)
        return s // tiles, (s % tiles) * (TILE_T // PREP_ROWS), 0

    operands = [
        (x, pl.BlockSpec((None, TILE_T, d_model), lambda b, t: (b, t, 0))),
        (mod, pl.BlockSpec((None, 3, d_model), lambda b, t: (b, 0, 0))),
        (x, pl.BlockSpec((None, PREP_ROWS, d_model), next_block)),
        (mod, pl.BlockSpec((None, 3, d_model), lambda b, t: (next_block(b, t)[0], 0, 0))),
        (w_in.astype(BF16), _const_spec((d_model, N_COLS))),
        (rows, _const_spec(rows.shape)),
        (w_spatial, _const_spec(w_spatial.shape)),
        (bias_a, _const_spec((CHUNK_A, D_A))),
        (w_lr, _const_spec((2 * LOW_RANK, 2 * D_B))),
        (w_out.astype(BF16), _const_spec((d_model, d_model))),
        (tri, _const_spec((SCAN_L, SCAN_L))),
        (_wide_masks(), _const_spec((7, SCAN_L, MXU_COLS))),
    ]
    n_slots = N_CHUNKS * N_PAIRS
    return pl.pallas_call(
        _layer_kernel,
        grid=(batch, seq // TILE_T),
        in_specs=[spec for _, spec in operands],
        out_specs=pl.BlockSpec((None, TILE_T, d_model), lambda b, t: (b, t, 0)),
        out_shape=jax.ShapeDtypeStruct(x.shape, x.dtype),
        scratch_shapes=[
            pltpu.VMEM((TILE_T, d_model), BF16),
            pltpu.VMEM((TILE_T + SUBLANES, N_COLS), F32),
            pltpu.VMEM((PREP_ROWS, N_COLS_B), F32),
            pltpu.VMEM((TILE_T, d_model), BF16),
            pltpu.VMEM((N_PAIRS, LANES, LANES), F32),
            pltpu.VMEM((n_slots, 2 * SCAN_L, LANES), BF16),
            pltpu.VMEM((n_slots, SCAN_L, LANES), F32),
            pltpu.VMEM((n_slots, SCAN_L, 2 * PAIR_ROWS), BF16),
            pltpu.VMEM((n_slots, PAIR_ROWS, LANES), BF16),
            pltpu.VMEM((n_slots, LANES, 2 * PAIR_ROWS), BF16),
            pltpu.VMEM((n_slots, LANES, LANES), F32),
            pltpu.VMEM((TILE_T, D_B), F32),
            pltpu.VMEM((TILE_T, D_B), F32),
            pltpu.VMEM((TILE_T, D_B), F32),
            pltpu.VMEM((D_A // HEAD, CHUNK_A, CHUNK_A), BF16),
        ],
        compiler_params=pltpu.CompilerParams(
            dimension_semantics=("arbitrary", "arbitrary"),
            vmem_limit_bytes=VMEM_LIMIT_BYTES),
        name="hybrid_layer",
    )(*[arr for arr, _ in operands])
```

```python
import math

import jax
import jax.numpy as jnp
import numpy as np
from jax import lax
from jax.experimental import pallas as pl
from jax.experimental.pallas import tpu as pltpu

F32 = jnp.float32
BF16 = jnp.bfloat16

D_MODEL = 1024
D_A = 512
D_B = 512
HEAD = 64
CHUNK_A = 128
LOW_RANK = 64
N_COLS_A = 3 * D_A
N_COLS_B = 4 * D_B + 2 * LOW_RANK
N_COLS = N_COLS_A + N_COLS_B
LN_EPS = 1e-5
GN_EPS = 64e-5
ALPHA = 2.0 ** 0.25
LOG_DECAY_SCALE = -math.exp(-0.5)

LANES = 128
SUBLANES = 8
MXU_COLS = 256
N_PAIRS = D_B // LANES
N_GROUPS = D_B // MXU_COLS
HEADS_PER_GROUP = MXU_COLS // HEAD
SCAN_L = 64
PAIR_ROWS = 2 * SCAN_L
TILE_T = 512
N_CHUNKS = TILE_T // SCAN_L
PREP_CHUNKS = 4
PREP_ROWS = PREP_CHUNKS * SCAN_L
CHUNKS_PER_BLOCK = CHUNK_A // SCAN_L
VMEM_LIMIT_BYTES = 56 * 1024 * 1024

ROW_PARAMS = (("mu_b", N_COLS_B), ("ln_v_g", D_A), ("ln_v_b", D_A), ("w0", D_B), ("a0", D_B),
              ("k_k", D_B), ("k_a", D_B), ("r_k", D_B), ("gn_g", D_B), ("gn_b", D_B),
              ("ln_g", D_MODEL), ("ln_b", D_MODEL))
ROW_OFFSETS = {}
for _name, _width in ROW_PARAMS:
    ROW_OFFSETS[_name] = (sum(w for _, w in ROW_PARAMS[:len(ROW_OFFSETS)]), _width)


def _bf(x):
    return x.astype(BF16)


def _mm(a, b):
    return jnp.dot(a, b, preferred_element_type=F32)


def _mm_nt(a, b):
    return lax.dot_general(a, b, (((1,), (1,)), ((), ())), preferred_element_type=F32)


def _interleave(step_generators):
    pending = list(step_generators)
    while pending:
        for gen in list(pending):
            try:
                next(gen)
            except StopIteration:
                pending.remove(gen)


def _take(gen, count):
    for _ in range(count):
        try:
            next(gen)
        except StopIteration:
            return
        yield


def _lock(fn, *lists):
    return [fn(*args) for args in zip(*lists)]


def _split_bf16(x, terms):
    parts = []
    rem = x
    for _ in range(terms):
        hi = _bf(rem)
        parts.append(hi)
        rem = rem - hi.astype(F32)
    return parts


def _group_sum_lanes(x):
    lane = lax.broadcasted_iota(jnp.int32, (x.shape[0], LANES), 1)
    first = lane < HEAD
    outs = []
    for j in range(x.shape[1] // LANES):
        t = x[:, LANES * j:LANES * (j + 1)]
        s_first = jnp.sum(jnp.where(first, t, 0.0), axis=-1, keepdims=True)
        s_second = jnp.sum(jnp.where(first, 0.0, t), axis=-1, keepdims=True)
        outs.append(jnp.where(first, s_first, s_second))
    return jnp.concatenate(outs, axis=1)


def _sigmoid(x, scale=1.0):
    return 0.5 * scale + (0.5 * scale) * jnp.tanh(0.5 * x)


def _silu(x):
    h = 0.5 * x
    return h + h * jnp.tanh(h)


def _gelu_tanh(x):
    c = math.sqrt(2.0 / math.pi)
    h = 0.5 * x
    return h + h * jnp.tanh(x * (c + (c * 0.044715) * (x * x)))


def _block_diag(y, head_masks):
    zero = jnp.zeros((y.shape[0], LANES), BF16)
    rows = []
    for h in range(HEADS_PER_GROUP):
        tile = h // 2
        kept = y[:, LANES * tile:LANES * (tile + 1)] * head_masks[h % 2]
        rows.append(jnp.concatenate([kept if t == tile else zero
                                     for t in range(MXU_COLS // LANES)], axis=1))
    return jnp.concatenate(rows, axis=0)


def _unit_lower_inverse_steps(a_abs, eye, m8, off16, off32, off64, bd_mask):
    bd = lambda y: _block_diag(y, bd_mask)
    a8 = _lock(lambda a: a * m8, a_abs)
    a8b = _lock(_bf, a8)
    a2b = _lock(lambda a: _bf(_mm(a, bd(a))), a8b)
    yield
    p1 = _lock(lambda a: eye + a, a8)
    both = _lock(lambda a, p: _mm(jnp.concatenate([a, _bf(p)], axis=0), bd(a)), a2b, p1)
    yield
    a4b = _lock(lambda m: _bf(m[0:SCAN_L]), both)
    p2 = _lock(lambda p, m: p + m[SCAN_L:], p1, both)
    t = _lock(lambda p, a: p + _mm(_bf(p), bd(a)), p2, a4b)
    yield
    for off in (off16, off32, off64):
        tb = _lock(_bf, t)
        to = _lock(lambda x, a: _bf(_mm(x, bd(_bf(a * off)))), tb, a_abs)
        yield
        t = _lock(lambda x, xo, xb: x + _mm(xo, bd(xb)), t, to, tb)
        yield
    return t


def _scan_prepare_steps(r, k, v, kk, bb, g, lw, consts, bd_mask):
    eye, tri_strict, tri_incl, m8, off16, off32, off64 = consts
    bd = lambda y: _block_diag(y, bd_mask)

    def factors(r, k, v, kk, bb, g, lw):
        e_g = jnp.exp(g)
        e_ge = jnp.exp(g - lw)
        e_ng = jnp.exp(-g)
        e_gl = jnp.exp(g[SCAN_L - 1:SCAN_L, :] - g)
        return (_bf(-(kk * e_ge)), _bf(r * e_g), _bf(v), _bf(bb * e_ng), _bf(k * e_ng),
                _bf(bb * e_gl), _bf(k * e_gl))

    xa, xr, vb, bh, kh, bbar, kbar = zip(*_lock(factors, r, k, v, kk, bb, g, lw))
    lhs = _lock(lambda a, b: jnp.concatenate([a, b], axis=0), xa, xr)
    def pair_rows(y, sl):
        yp = y[:, sl]
        return jnp.concatenate([yp * bd_mask[0], yp * bd_mask[1]], axis=0)

    pairs = [slice(LANES * p, LANES * (p + 1)) for p in range(MXU_COLS // LANES)]
    aa = _lock(lambda x, yb, yk: [_mm_nt(x[:, sl], jnp.concatenate([pair_rows(yb, sl),
                                                                     pair_rows(yk, sl)], axis=0))
                                  for sl in pairs], lhs, bh, kh)
    yield
    aa_b = _lock(lambda parts: jnp.concatenate([m[:, 0:LANES] for m in parts], axis=1), aa)
    aa_k = _lock(lambda parts: jnp.concatenate([m[:, LANES:] for m in parts], axis=1), aa)
    a_ab = _lock(lambda m: jnp.where(tri_strict != 0.0, m[0:SCAN_L], 0.0), aa_b)
    a_rb = _lock(lambda m: _bf(jnp.where(tri_incl != 0.0, m[SCAN_L:], 0.0)), aa_b)
    a_ak = _lock(lambda m: _bf(jnp.where(tri_strict != 0.0, m[0:SCAN_L], 0.0)), aa_k)
    a_rk = _lock(lambda m: _bf(jnp.where(tri_incl != 0.0, m[SCAN_L:], 0.0)), aa_k)
    t_inv = yield from _unit_lower_inverse_steps(a_ab, eye, m8, off16, off32, off64, bd_mask)
    t_inv = _lock(_bf, t_inv)
    akv = _lock(lambda a, y: _bf(_mm(a, bd(y))), a_ak, vb)
    yield
    def pair_weights(y, sl):
        yp = y[:, sl]
        return jnp.concatenate([yp * bd_mask[0], yp * bd_mask[1]], axis=0)

    pair_slices = [slice(LANES * p, LANES * (p + 1)) for p in range(MXU_COLS // LANES)]
    qz = _lock(lambda t, a, b: [_mm(t[:, sl], jnp.concatenate([pair_weights(a, sl),
                                                                pair_weights(b, sl)], axis=1))
                                for sl in pair_slices], t_inv, xa, akv)
    yield
    q = _lock(lambda parts: _bf(jnp.concatenate([m[:, 0:LANES] for m in parts], axis=1)), qz)
    z = _lock(lambda parts: jnp.concatenate([m[:, LANES:] for m in parts], axis=1), qz)
    return q, list(xr), z, a_rb, a_rk, list(vb), list(bbar), list(kbar)


def _ada_kernel(c_ref, w_ref, b_ref, o_ref):
    s_hi, s_lo = _split_bf16(_silu(c_ref[...]), 2)
    w_hi, w_lo = _split_bf16(w_ref[...], 2)
    acc = _mm(s_hi, w_hi) + _mm(s_hi, w_lo) + _mm(s_lo, w_hi)
    o_ref[...] = acc + b_ref[...]


def _layer_kernel(x_ref, mod_ref, xn_ref, modn_ref,
                  win_ref, rows_ref, ws_ref, bs_ref, wlr_ref, wout_ref,
                  tri_ref, cm_ref,
                  out_ref,
                  h_scr, p_scr, pnext_scr, cat_scr, state_scr,
                  qx_scr, z_scr, ar_scr, vp_scr, bkt_scr,
                  dec_scr, y_scr, bonus_scr, sgate_scr, wm_scr):
    t_idx = pl.program_id(1)
    tile = x_ref.shape[0]
    top = SUBLANES

    @pl.when(t_idx == 0)
    def _():
        state_scr[...] = jnp.zeros_like(state_scr)
        p_scr[0:top, :] = jnp.zeros((top, N_COLS), F32)

    def row(name):
        offset, width = ROW_OFFSETS[name]
        return rows_ref[:, offset:offset + width]

    shift = mod_ref[0:1, :]
    scale = mod_ref[1:2, :]
    gate_over_alpha = mod_ref[2:3, :] * (1.0 / ALPHA)

    h_scr[...] = _bf(x_ref[...] * (1.0 + scale) + shift)

    @pl.when((pl.program_id(0) == 0) & (t_idx == 0))
    def _():
        pnext_scr[...] = _mm(h_scr[0:PREP_ROWS, :], win_ref[:, N_COLS_A:])

    def inproj_steps(row_blocks, col0, col1):
        for row0, nrows in row_blocks:
            hb = h_scr[row0:row0 + nrows, :]
            for i, c in enumerate(range(col0, col1, MXU_COLS)):
                ce = min(c + MXU_COLS, col1)
                p_scr[top + row0:top + row0 + nrows, c:ce] = _mm(hb, win_ref[:, c:ce])
                if i % 4 == 3:
                    yield
            yield

    row_a = lax.broadcasted_iota(jnp.int32, (CHUNK_A, CHUNK_A), 0)
    col_a = lax.broadcasted_iota(jnp.int32, (CHUNK_A, CHUNK_A), 1)
    causal = col_a <= row_a
    lane_a = lax.broadcasted_iota(jnp.int32, (CHUNK_A, LANES), 1)
    first_a = lane_a < HEAD

    for gidx in range(ws_ref.shape[0]):
        wm_scr[gidx] = _bf(jnp.where(causal, ws_ref[gidx], 0.0))

    def gmlp_steps(ci):
        rows = slice(top + ci * CHUNK_A, top + (ci + 1) * CHUNK_A)
        v = _gelu_tanh(p_scr[rows, D_A:2 * D_A])
        mean = _group_sum_lanes(v) * (1.0 / HEAD)
        yield
        vc = v - mean
        var = _group_sum_lanes(vc * vc) * (1.0 / HEAD)
        yield
        v = vc * lax.rsqrt(var + LN_EPS) * row("ln_v_g") + row("ln_v_b")
        mixed = []
        for j in range(D_A // LANES):
            vj = _bf(v[:, LANES * j:LANES * (j + 1)])
            mixed.append(jnp.where(first_a, _mm(wm_scr[2 * j], vj), _mm(wm_scr[2 * j + 1], vj)))
        yield
        mix = jnp.concatenate(mixed, axis=1) + bs_ref[...]
        u = _gelu_tanh(p_scr[rows, 0:D_A])
        out_a = u * mix * _silu(p_scr[rows, 2 * D_A:3 * D_A])
        cat_scr[ci * CHUNK_A:(ci + 1) * CHUNK_A, 0:D_A] = _bf(out_a)

    consts = tuple(cm_ref[i] for i in range(7))
    row_b = lax.broadcasted_iota(jnp.int32, (PREP_ROWS, N_COLS_B), 0)
    lane_b = lax.broadcasted_iota(jnp.int32, (PREP_ROWS, LANES), 1)
    lane_s = lax.broadcasted_iota(jnp.int32, (SCAN_L, LANES), 1)
    first_f = lane_s < HEAD
    first_b = jnp.where(first_f, 1.0, 0.0).astype(BF16)
    second_b = jnp.where(first_f, 0.0, 1.0).astype(BF16)
    bd_mask = (first_b, second_b)
    cb = N_COLS_A

    def stack2b(zz):
        return jnp.concatenate([zz * first_b, zz * second_b], axis=0)

    def prepare_steps(bi):
        r0 = top + bi * PREP_ROWS
        if bi == 0:
            cur = pnext_scr[...]
            before = p_scr[0:top, cb:]
        else:
            cur = p_scr[r0:r0 + PREP_ROWS, cb:]
            before = (pnext_scr[PREP_ROWS - SUBLANES:, :] if bi == 1
                      else p_scr[r0 - SUBLANES:r0, cb:])
        prev = jnp.where(row_b == 0, before[SUBLANES - 1:SUBLANES, :], pltpu.roll(cur, 1, axis=0))
        ps = cur + row("mu_b") * (prev - cur)
        r = ps[:, 0:D_B]
        k = ps[:, D_B:2 * D_B]
        v = ps[:, 2 * D_B:3 * D_B]
        gate_b = ps[:, 3 * D_B:4 * D_B]
        low = ps[:, 4 * D_B:]
        low = jnp.where(lane_b < LOW_RANK, jnp.tanh(low), low)
        up = _mm(_bf(low), wlr_ref[...])
        yield
        lw = _sigmoid(row("w0") + up[:, 0:D_B], LOG_DECAY_SCALE)
        ah = _sigmoid(row("a0") + up[:, D_B:])
        lw_terms = jnp.concatenate(_split_bf16(lw, 2), axis=1)
        g = jnp.concatenate([_mm(tri_ref[...], lw_terms[SCAN_L * c:SCAN_L * (c + 1), :])
                             for c in range(PREP_CHUNKS)], axis=0)
        g = g[:, 0:D_B] + g[:, D_B:]
        kk = k * row("k_k")
        kk = kk * lax.rsqrt(_group_sum_lanes(kk * kk) + 1e-12)
        yield
        k = k * (1.0 + (ah - 1.0) * row("k_a"))
        bb = kk * ah

        o0 = bi * PREP_ROWS
        bonus_scr[o0:o0 + PREP_ROWS, :] = _group_sum_lanes(r * k * row("r_k")) * v
        sgate_scr[o0:o0 + PREP_ROWS, :] = _silu(gate_b)
        yield

        problems = [(c, gi) for c in range(PREP_CHUNKS) for gi in range(N_GROUPS)]
        cut = lambda zz: [zz[SCAN_L * c:SCAN_L * (c + 1), MXU_COLS * gi:MXU_COLS * (gi + 1)]
                          for c, gi in problems]
        for c in range(PREP_CHUNKS):
            last = jnp.exp(g[SCAN_L * (c + 1) - 1:SCAN_L * (c + 1), :])
            for j in range(N_PAIRS):
                rowwise = jnp.broadcast_to(last[:, LANES * j:LANES * (j + 1)], (LANES, LANES))
                dec_scr[(bi * PREP_CHUNKS + c) * N_PAIRS + j] = rowwise.T

        q, xr, z, a_rb, a_rk, vb, bbar, kbar = yield from _scan_prepare_steps(
            cut(r), cut(k), cut(v), cut(kk), cut(bb), cut(g), cut(lw), consts, bd_mask)
        for i, (c, gi) in enumerate(problems):
            for pp in range(MXU_COLS // LANES):
                slot = (bi * PREP_CHUNKS + c) * N_PAIRS + gi * (MXU_COLS // LANES) + pp
                sl = slice(LANES * pp, LANES * (pp + 1))
                qx_scr[slot] = jnp.concatenate([q[i][:, sl], xr[i][:, sl]], axis=0)
                z_scr[slot] = z[i][:, sl]
                ar_scr[slot] = jnp.concatenate([a_rb[i][:, sl], a_rk[i][:, sl]], axis=1)
                vp_scr[slot] = stack2b(vb[i][:, sl])
                bkt_scr[slot] = jnp.concatenate([stack2b(bbar[i][:, sl]), stack2b(kbar[i][:, sl])],
                                                axis=0).T

    n_prep = N_CHUNKS // PREP_CHUNKS
    row_block = lambda i: (i * PREP_ROWS, PREP_ROWS)
    for bi in range(n_prep):
        if bi + 1 < n_prep:
            fill = inproj_steps([row_block(bi + 1)], cb, N_COLS)
        else:
            fill = inproj_steps([row_block(i) for i in range(n_prep)], 0, cb)
        _interleave([prepare_steps(bi), fill])

    def apply_steps(bi):
        states = [state_scr[j] for j in range(N_PAIRS)]
        for c in range(CHUNKS_PER_BLOCK):
            ci = bi * CHUNKS_PER_BLOCK + c
            load = lambda ref: [ref[ci * N_PAIRS + j] for j in range(N_PAIRS)]
            xs = _lock(lambda a, s: _mm(a, _bf(s)), load(qx_scr), states)
            yield
            uv = _lock(lambda x, zz, w: jnp.concatenate([stack2b(_bf(x[0:SCAN_L] + zz)), w], axis=0),
                       xs, load(z_scr), load(vp_scr))
            both = _lock(lambda b, a, m: _mm(jnp.concatenate([b, a], axis=0), m),
                         load(bkt_scr), load(ar_scr), uv)
            yield
            states = _lock(lambda s, d, m: s * d + m[0:LANES], states, load(dec_scr), both)
            ys = _lock(lambda x, m: x[SCAN_L:] + m[LANES:], xs, both)
            y_scr[ci * SCAN_L:(ci + 1) * SCAN_L, :] = jnp.concatenate(ys, axis=1)
        for j in range(N_PAIRS):
            state_scr[j] = states[j]

    def post_steps(bi):
        rows = slice(bi * CHUNK_A, (bi + 1) * CHUNK_A)
        y = y_scr[rows, :]
        mean = _group_sum_lanes(y) * (1.0 / HEAD)
        yield
        yc = y - mean
        var = _group_sum_lanes(yc * yc) * (1.0 / HEAD)
        yield
        y = yc * lax.rsqrt(var + GN_EPS) * row("gn_g") + row("gn_b")
        cat_scr[rows, D_A:] = _bf((y + bonus_scr[rows, :]) * sgate_scr[rows, :])
        o = _mm(cat_scr[rows, :], wout_ref[...])
        yield
        res = x_ref[rows, :] + gate_over_alpha * o
        mean = jnp.mean(res, axis=-1, keepdims=True)
        rc = res - mean
        var = jnp.mean(rc * rc, axis=-1, keepdims=True)
        out_ref[rows, :] = rc * lax.rsqrt(var + LN_EPS / ALPHA ** 2) * row("ln_g") + row("ln_b")

    def next_inproj_steps():
        hn = _bf(xn_ref[...] * (1.0 + modn_ref[1:2, :]) + modn_ref[0:1, :])
        for c in range(0, N_COLS_B, MXU_COLS):
            ce = min(c + MXU_COLS, N_COLS_B)
            pnext_scr[:, c:ce] = _mm(hn, win_ref[:, cb + c:cb + ce])
            yield

    n_blocks = tile // CHUNK_A
    next_proj = next_inproj_steps()
    next_steps_per_block = -(-len(range(0, N_COLS_B, MXU_COLS)) // n_blocks)
    for bi in range(n_blocks):
        group = [apply_steps(bi)]
        if bi > 0:
            group.append(post_steps(bi - 1))
        group += [gmlp_steps(bi), _take(next_proj, next_steps_per_block)]
        _interleave(group)
    _interleave([post_steps(n_blocks - 1), next_proj])

    p_scr[0:top, :] = p_scr[tile:tile + top, :]


def _wide_masks():
    t = np.arange(SCAN_L)[:, None]
    s = (np.arange(MXU_COLS) % SCAN_L)[None, :]
    same = lambda n: (t // n) == (s // n)
    masks = [t == s, s < t, s <= t, same(8),
             same(16) & ~same(8), same(32) & ~same(16), same(64) & ~same(32)]
    return jnp.asarray(np.stack(masks), F32)


def _const_spec(shape):
    zeros = (0,) * len(shape)
    return pl.BlockSpec(shape, lambda b, t: zeros)


def kernel(x, c, w_ada, b_ada, w_in, mu_b, ln_v_g, ln_v_b, w_spatial, b_spatial, w0, w_up, a0,
           a_up, k_k, k_a, r_k, gn_g, gn_b, w_out, ln_g, ln_b):
    batch, seq, d_model = x.shape
    assert d_model == D_MODEL and w_in.shape == (D_MODEL, N_COLS)
    assert seq % TILE_T == 0 and TILE_T % CHUNK_A == 0 and N_CHUNKS % PREP_CHUNKS == 0
    assert N_CHUNKS // PREP_CHUNKS >= 2

    mod = pl.pallas_call(
        _ada_kernel,
        out_shape=jax.ShapeDtypeStruct((batch, 3 * d_model), F32),
        name="adaln_mod",
    )(c, w_ada, b_ada.reshape(1, -1))
    mod = mod.reshape(batch, 3, d_model)

    named = dict(mu_b=mu_b, ln_v_g=ln_v_g, ln_v_b=ln_v_b, w0=w0, a0=a0, k_k=k_k, k_a=k_a, r_k=r_k,
                 gn_g=gn_g, gn_b=gn_b, ln_g=ln_g, ln_b=ln_b)
    rows = jnp.concatenate([named[name].reshape(-1).astype(F32) for name, _ in ROW_PARAMS])
    rows = rows.reshape(1, -1)
    tok = np.arange(SCAN_L)
    tri = jnp.asarray(tok[None, :] <= tok[:, None], BF16)
    zeros_lr = jnp.zeros((LOW_RANK, D_B), F32)
    w_lr = jnp.concatenate([jnp.concatenate([w_up, zeros_lr], axis=1),
                            jnp.concatenate([zeros_lr, a_up], axis=1)], axis=0).astype(BF16)
    bias_a = jnp.repeat(b_spatial.T, HEAD, axis=1)

    tiles = seq // TILE_T

    def next_block(b, t):
        s = jnp.minimum(b * tiles + t + 1, batch * tiles - 1)
        return s // tiles, (s % tiles) * (TILE_T // PREP_ROWS), 0

    operands = [
        (x, pl.BlockSpec((None, TILE_T, d_model), lambda b, t: (b, t, 0))),
        (mod, pl.BlockSpec((None, 3, d_model), lambda b, t: (b, 0, 0))),
        (x, pl.BlockSpec((None, PREP_ROWS, d_model), next_block)),
        (mod, pl.BlockSpec((None, 3, d_model), lambda b, t: (next_block(b, t)[0], 0, 0))),
        (w_in.astype(BF16), _const_spec((d_model, N_COLS))),
        (rows, _const_spec(rows.shape)),
        (w_spatial, _const_spec(w_spatial.shape)),
        (bias_a, _const_spec((CHUNK_A, D_A))),
        (w_lr, _const_spec((2 * LOW_RANK, 2 * D_B))),
        (w_out.astype(BF16), _const_spec((d_model, d_model))),
        (tri, _const_spec((SCAN_L, SCAN_L))),
        (_wide_masks(), _const_spec((7, SCAN_L, MXU_COLS))),
    ]
    n_slots = N_CHUNKS * N_PAIRS
    return pl.pallas_call(
        _layer_kernel,
        grid=(batch, seq // TILE_T),
        in_specs=[spec for _, spec in operands],
        out_specs=pl.BlockSpec((None, TILE_T, d_model), lambda b, t: (b, t, 0)),
        out_shape=jax.ShapeDtypeStruct(x.shape, x.dtype),
        scratch_shapes=[
            pltpu.VMEM((TILE_T, d_model), BF16),
            pltpu.VMEM((TILE_T + SUBLANES, N_COLS), F32),
            pltpu.VMEM((PREP_ROWS, N_COLS_B), F32),
            pltpu.VMEM((TILE_T, d_model), BF16),
            pltpu.VMEM((N_PAIRS, LANES, LANES), F32),
            pltpu.VMEM((n_slots, 2 * SCAN_L, LANES), BF16),
            pltpu.VMEM((n_slots, SCAN_L, LANES), F32),
            pltpu.VMEM((n_slots, SCAN_L, 2 * PAIR_ROWS), BF16),
            pltpu.VMEM((n_slots, PAIR_ROWS, LANES), BF16),
            pltpu.VMEM((n_slots, LANES, 2 * PAIR_ROWS), BF16),
            pltpu.VMEM((n_slots, LANES, LANES), F32),
            pltpu.VMEM((TILE_T, D_B), F32),
            pltpu.VMEM((TILE_T, D_B), F32),
            pltpu.VMEM((TILE_T, D_B), F32),
            pltpu.VMEM((D_A // HEAD, CHUNK_A, CHUNK_A), BF16),
        ],
        compiler_params=pltpu.CompilerParams(
            dimension_semantics=("arbitrary", "arbitrary"),
            vmem_limit_bytes=VMEM_LIMIT_BYTES),
        name="hybrid_layer",
    )(*[arr for arr, _ in operands])
```

```python
import math

import jax
import jax.numpy as jnp
import numpy as np
from jax import lax
from jax.experimental import pallas as pl
from jax.experimental.pallas import tpu as pltpu

F32 = jnp.float32
BF16 = jnp.bfloat16

D_MODEL = 1024
D_A = 512
D_B = 512
HEAD = 64
CHUNK_A = 128
LOW_RANK = 64
N_COLS_A = 3 * D_A
N_COLS_B = 4 * D_B + 2 * LOW_RANK
N_COLS = N_COLS_A + N_COLS_B
LN_EPS = 1e-5
GN_EPS = 64e-5
ALPHA = 2.0 ** 0.25
LOG_DECAY_SCALE = -math.exp(-0.5)

LANES = 128
SUBLANES = 8
MXU_COLS = 256
N_PAIRS = D_B // LANES
N_GROUPS = D_B // MXU_COLS
HEADS_PER_GROUP = MXU_COLS // HEAD
SCAN_L = 64
PAIR_ROWS = 2 * SCAN_L
TILE_T = 512
N_CHUNKS = TILE_T // SCAN_L
PREP_CHUNKS = 4
PREP_ROWS = PREP_CHUNKS * SCAN_L
CHUNKS_PER_BLOCK = CHUNK_A // SCAN_L
VMEM_LIMIT_BYTES = 56 * 1024 * 1024

ROW_PARAMS = (("mu_b", N_COLS_B), ("ln_v_g", D_A), ("ln_v_b", D_A), ("w0", D_B), ("a0", D_B),
              ("k_k", D_B), ("k_a", D_B), ("r_k", D_B), ("gn_g", D_B), ("gn_b", D_B),
              ("ln_g", D_MODEL), ("ln_b", D_MODEL))
ROW_OFFSETS = {}
for _name, _width in ROW_PARAMS:
    ROW_OFFSETS[_name] = (sum(w for _, w in ROW_PARAMS[:len(ROW_OFFSETS)]), _width)


def _bf(x):
    return x.astype(BF16)


def _mm(a, b):
    return jnp.dot(a, b, preferred_element_type=F32)


def _mm_nt(a, b):
    return lax.dot_general(a, b, (((1,), (1,)), ((), ())), preferred_element_type=F32)


def _interleave(step_generators):
    pending = list(step_generators)
    while pending:
        for gen in list(pending):
            try:
                next(gen)
            except StopIteration:
                pending.remove(gen)


def _take(gen, count):
    for _ in range(count):
        try:
            next(gen)
        except StopIteration:
            return
        yield


def _lock(fn, *lists):
    return [fn(*args) for args in zip(*lists)]


def _split_bf16(x, terms):
    parts = []
    rem = x
    for _ in range(terms):
        hi = _bf(rem)
        parts.append(hi)
        rem = rem - hi.astype(F32)
    return parts


def _group_sum_lanes(x):
    lane = lax.broadcasted_iota(jnp.int32, (x.shape[0], LANES), 1)
    first = lane < HEAD
    outs = []
    for j in range(x.shape[1] // LANES):
        t = x[:, LANES * j:LANES * (j + 1)]
        s_first = jnp.sum(jnp.where(first, t, 0.0), axis=-1, keepdims=True)
        s_second = jnp.sum(jnp.where(first, 0.0, t), axis=-1, keepdims=True)
        outs.append(jnp.where(first, s_first, s_second))
    return jnp.concatenate(outs, axis=1)


def _sigmoid(x, scale=1.0):
    return 0.5 * scale + (0.5 * scale) * jnp.tanh(0.5 * x)


def _silu(x):
    h = 0.5 * x
    return h + h * jnp.tanh(h)


def _gelu_tanh(x):
    c = math.sqrt(2.0 / math.pi)
    h = 0.5 * x
    return h + h * jnp.tanh(x * (c + (c * 0.044715) * (x * x)))


def _block_diag(y, head_masks):
    zero = jnp.zeros((y.shape[0], LANES), BF16)
    rows = []
    for h in range(HEADS_PER_GROUP):
        tile = h // 2
        kept = y[:, LANES * tile:LANES * (tile + 1)] * head_masks[h % 2]
        rows.append(jnp.concatenate([kept if t == tile else zero
                                     for t in range(MXU_COLS // LANES)], axis=1))
    return jnp.concatenate(rows, axis=0)


def _unit_lower_inverse_steps(a_abs, eye, m8, off16, off32, off64, bd_mask):
    bd = lambda y: _block_diag(y, bd_mask)
    a8 = _lock(lambda a: a * m8, a_abs)
    a8b = _lock(_bf, a8)
    a2b = _lock(lambda a: _bf(_mm(a, bd(a))), a8b)
    yield
    p1 = _lock(lambda a: eye + a, a8)
    both = _lock(lambda a, p: _mm(jnp.concatenate([a, _bf(p)], axis=0), bd(a)), a2b, p1)
    yield
    a4b = _lock(lambda m: _bf(m[0:SCAN_L]), both)
    p2 = _lock(lambda p, m: p + m[SCAN_L:], p1, both)
    t = _lock(lambda p, a: p + _mm(_bf(p), bd(a)), p2, a4b)
    yield
    for off in (off16, off32, off64):
        tb = _lock(_bf, t)
        to = _lock(lambda x, a: _bf(_mm(x, bd(_bf(a * off)))), tb, a_abs)
        yield
        t = _lock(lambda x, xo, xb: x + _mm(xo, bd(xb)), t, to, tb)
        yield
    return t


def _scan_prepare_steps(r, k, v, kk, bb, g, lw, consts, bd_mask):
    eye, tri_strict, tri_incl, m8, off16, off32, off64 = consts
    bd = lambda y: _block_diag(y, bd_mask)

    def factors(r, k, v, kk, bb, g, lw):
        e_g = jnp.exp(g)
        e_ge = jnp.exp(g - lw)
        e_ng = jnp.exp(-g)
        e_gl = jnp.exp(g[SCAN_L - 1:SCAN_L, :] - g)
        return (_bf(-(kk * e_ge)), _bf(r * e_g), _bf(v), _bf(bb * e_ng), _bf(k * e_ng),
                _bf(bb * e_gl), _bf(k * e_gl))

    xa, xr, vb, bh, kh, bbar, kbar = zip(*_lock(factors, r, k, v, kk, bb, g, lw))
    lhs = _lock(lambda a, b: jnp.concatenate([a, b], axis=0), xa, xr)
    def pair_rows(y, sl):
        yp = y[:, sl]
        return jnp.concatenate([yp * bd_mask[0], yp * bd_mask[1]], axis=0)

    pairs = [slice(LANES * p, LANES * (p + 1)) for p in range(MXU_COLS // LANES)]
    aa = _lock(lambda x, yb, yk: [_mm_nt(x[:, sl], jnp.concatenate([pair_rows(yb, sl),
                                                                     pair_rows(yk, sl)], axis=0))
                                  for sl in pairs], lhs, bh, kh)
    yield
    aa_b = _lock(lambda parts: jnp.concatenate([m[:, 0:LANES] for m in parts], axis=1), aa)
    aa_k = _lock(lambda parts: jnp.concatenate([m[:, LANES:] for m in parts], axis=1), aa)
    a_ab = _lock(lambda m: jnp.where(tri_strict != 0.0, m[0:SCAN_L], 0.0), aa_b)
    a_rb = _lock(lambda m: _bf(jnp.where(tri_incl != 0.0, m[SCAN_L:], 0.0)), aa_b)
    a_ak = _lock(lambda m: _bf(jnp.where(tri_strict != 0.0, m[0:SCAN_L], 0.0)), aa_k)
    a_rk = _lock(lambda m: _bf(jnp.where(tri_incl != 0.0, m[SCAN_L:], 0.0)), aa_k)
    t_inv = yield from _unit_lower_inverse_steps(a_ab, eye, m8, off16, off32, off64, bd_mask)
    t_inv = _lock(_bf, t_inv)
    akv = _lock(lambda a, y: _bf(_mm(a, bd(y))), a_ak, vb)
    yield
    def pair_weights(y, sl):
        yp = y[:, sl]
        return jnp.concatenate([yp * bd_mask[0], yp * bd_mask[1]], axis=0)

    pair_slices = [slice(LANES * p, LANES * (p + 1)) for p in range(MXU_COLS // LANES)]
    qz = _lock(lambda t, a, b: [_mm(t[:, sl], jnp.concatenate([pair_weights(a, sl),
                                                                pair_weights(b, sl)], axis=1))
                                for sl in pair_slices], t_inv, xa, akv)
    yield
    q = _lock(lambda parts: _bf(jnp.concatenate([m[:, 0:LANES] for m in parts], axis=1)), qz)
    z = _lock(lambda parts: jnp.concatenate([m[:, LANES:] for m in parts], axis=1), qz)
    return q, list(xr), z, a_rb, a_rk, list(vb), list(bbar), list(kbar)


def _ada_kernel(c_ref, w_ref, b_ref, o_ref):
    s_hi, s_lo = _split_bf16(_silu(c_ref[...]), 2)
    w_hi, w_lo = _split_bf16(w_ref[...], 2)
    acc = _mm(s_hi, w_hi) + _mm(s_hi, w_lo) + _mm(s_lo, w_hi)
    o_ref[...] = acc + b_ref[...]


def _layer_kernel(x_ref, mod_ref, xn_ref, modn_ref,
                  win_ref, rows_ref, ws_ref, bs_ref, wlr_ref, wout_ref,
                  tri_ref, cm_ref,
                  out_ref,
                  h_scr, p_scr, pnext_scr, cat_scr, state_scr,
                  qx_scr, z_scr, ar_scr, vp_scr, bkt_scr,
                  dec_scr, y_scr, bonus_scr, sgate_scr, wm_scr):
    t_idx = pl.program_id(1)
    tile = x_ref.shape[0]
    top = SUBLANES

    @pl.when(t_idx == 0)
    def _():
        state_scr[...] = jnp.zeros_like(state_scr)
        p_scr[0:top, :] = jnp.zeros((top, N_COLS), F32)

    def row(name):
        offset, width = ROW_OFFSETS[name]
        return rows_ref[:, offset:offset + width]

    shift = mod_ref[0:1, :]
    scale = mod_ref[1:2, :]
    gate_over_alpha = mod_ref[2:3, :] * (1.0 / ALPHA)

    h_scr[...] = _bf(x_ref[...] * (1.0 + scale) + shift)

    @pl.when((pl.program_id(0) == 0) & (t_idx == 0))
    def _():
        pnext_scr[...] = _mm(h_scr[0:PREP_ROWS, :], win_ref[:, N_COLS_A:])

    def inproj_steps(row_blocks, col0, col1):
        for row0, nrows in row_blocks:
            hb = h_scr[row0:row0 + nrows, :]
            for i, c in enumerate(range(col0, col1, MXU_COLS)):
                ce = min(c + MXU_COLS, col1)
                p_scr[top + row0:top + row0 + nrows, c:ce] = _mm(hb, win_ref[:, c:ce])
                if i % 2 == 1:
                    yield
            yield

    row_a = lax.broadcasted_iota(jnp.int32, (CHUNK_A, CHUNK_A), 0)
    col_a = lax.broadcasted_iota(jnp.int32, (CHUNK_A, CHUNK_A), 1)
    causal = col_a <= row_a
    lane_a = lax.broadcasted_iota(jnp.int32, (CHUNK_A, LANES), 1)
    first_a = lane_a < HEAD

    for gidx in range(ws_ref.shape[0]):
        wm_scr[gidx] = _bf(jnp.where(causal, ws_ref[gidx], 0.0))

    def gmlp_steps(ci):
        rows = slice(top + ci * CHUNK_A, top + (ci + 1) * CHUNK_A)
        v = _gelu_tanh(p_scr[rows, D_A:2 * D_A])
        mean = _group_sum_lanes(v) * (1.0 / HEAD)
        yield
        vc = v - mean
        var = _group_sum_lanes(vc * vc) * (1.0 / HEAD)
        yield
        v = vc * lax.rsqrt(var + LN_EPS) * row("ln_v_g") + row("ln_v_b")
        mixed = []
        for j in range(D_A // LANES):
            vj = _bf(v[:, LANES * j:LANES * (j + 1)])
            mixed.append(jnp.where(first_a, _mm(wm_scr[2 * j], vj), _mm(wm_scr[2 * j + 1], vj)))
        yield
        mix = jnp.concatenate(mixed, axis=1) + bs_ref[...]
        u = _gelu_tanh(p_scr[rows, 0:D_A])
        out_a = u * mix * _silu(p_scr[rows, 2 * D_A:3 * D_A])
        cat_scr[ci * CHUNK_A:(ci + 1) * CHUNK_A, 0:D_A] = _bf(out_a)

    consts = tuple(cm_ref[i] for i in range(7))
    row_b = lax.broadcasted_iota(jnp.int32, (PREP_ROWS, N_COLS_B), 0)
    lane_b = lax.broadcasted_iota(jnp.int32, (PREP_ROWS, LANES), 1)
    lane_s = lax.broadcasted_iota(jnp.int32, (SCAN_L, LANES), 1)
    first_f = lane_s < HEAD
    first_b = jnp.where(first_f, 1.0, 0.0).astype(BF16)
    second_b = jnp.where(first_f, 0.0, 1.0).astype(BF16)
    bd_mask = (first_b, second_b)
    cb = N_COLS_A

    def stack2b(zz):
        return jnp.concatenate([zz * first_b, zz * second_b], axis=0)

    def prepare_steps(bi):
        r0 = top + bi * PREP_ROWS
        if bi == 0:
            cur = pnext_scr[...]
            before = p_scr[0:top, cb:]
        else:
            cur = p_scr[r0:r0 + PREP_ROWS, cb:]
            before = (pnext_scr[PREP_ROWS - SUBLANES:, :] if bi == 1
                      else p_scr[r0 - SUBLANES:r0, cb:])
        prev = jnp.where(row_b == 0, before[SUBLANES - 1:SUBLANES, :], pltpu.roll(cur, 1, axis=0))
        ps = cur + row("mu_b") * (prev - cur)
        r = ps[:, 0:D_B]
        k = ps[:, D_B:2 * D_B]
        v = ps[:, 2 * D_B:3 * D_B]
        gate_b = ps[:, 3 * D_B:4 * D_B]
        low = ps[:, 4 * D_B:]
        low = jnp.where(lane_b < LOW_RANK, jnp.tanh(low), low)
        up = _mm(_bf(low), wlr_ref[...])
        yield
        lw = _sigmoid(row("w0") + up[:, 0:D_B], LOG_DECAY_SCALE)
        ah = _sigmoid(row("a0") + up[:, D_B:])
        lw_terms = jnp.concatenate(_split_bf16(lw, 2), axis=1)
        g = jnp.concatenate([_mm(tri_ref[...], lw_terms[SCAN_L * c:SCAN_L * (c + 1), :])
                             for c in range(PREP_CHUNKS)], axis=0)
        g = g[:, 0:D_B] + g[:, D_B:]
        kk = k * row("k_k")
        kk = kk * lax.rsqrt(_group_sum_lanes(kk * kk) + 1e-12)
        yield
        k = k * (1.0 + (ah - 1.0) * row("k_a"))
        bb = kk * ah

        o0 = bi * PREP_ROWS
        bonus_scr[o0:o0 + PREP_ROWS, :] = _group_sum_lanes(r * k * row("r_k")) * v
        sgate_scr[o0:o0 + PREP_ROWS, :] = _silu(gate_b)
        yield

        problems = [(c, gi) for c in range(PREP_CHUNKS) for gi in range(N_GROUPS)]
        cut = lambda zz: [zz[SCAN_L * c:SCAN_L * (c + 1), MXU_COLS * gi:MXU_COLS * (gi + 1)]
                          for c, gi in problems]
        for c in range(PREP_CHUNKS):
            last = jnp.exp(g[SCAN_L * (c + 1) - 1:SCAN_L * (c + 1), :])
            for j in range(N_PAIRS):
                rowwise = jnp.broadcast_to(last[:, LANES * j:LANES * (j + 1)], (LANES, LANES))
                dec_scr[(bi * PREP_CHUNKS + c) * N_PAIRS + j] = rowwise.T

        q, xr, z, a_rb, a_rk, vb, bbar, kbar = yield from _scan_prepare_steps(
            cut(r), cut(k), cut(v), cut(kk), cut(bb), cut(g), cut(lw), consts, bd_mask)
        for i, (c, gi) in enumerate(problems):
            for pp in range(MXU_COLS // LANES):
                slot = (bi * PREP_CHUNKS + c) * N_PAIRS + gi * (MXU_COLS // LANES) + pp
                sl = slice(LANES * pp, LANES * (pp + 1))
                qx_scr[slot] = jnp.concatenate([q[i][:, sl], xr[i][:, sl]], axis=0)
                z_scr[slot] = z[i][:, sl]
                ar_scr[slot] = jnp.concatenate([a_rb[i][:, sl], a_rk[i][:, sl]], axis=1)
                vp_scr[slot] = stack2b(vb[i][:, sl])
                bkt_scr[slot] = jnp.concatenate([stack2b(bbar[i][:, sl]), stack2b(kbar[i][:, sl])],
                                                axis=0).T

    n_prep = N_CHUNKS // PREP_CHUNKS
    row_block = lambda i: (i * PREP_ROWS, PREP_ROWS)
    for bi in range(n_prep):
        if bi + 1 < n_prep:
            fill = inproj_steps([row_block(bi + 1)], cb, N_COLS)
        else:
            fill = inproj_steps([row_block(i) for i in range(n_prep)], 0, cb)
        _interleave([prepare_steps(bi), fill])

    def apply_steps(bi):
        states = [state_scr[j] for j in range(N_PAIRS)]
        for c in range(CHUNKS_PER_BLOCK):
            ci = bi * CHUNKS_PER_BLOCK + c
            load = lambda ref: [ref[ci * N_PAIRS + j] for j in range(N_PAIRS)]
            xs = _lock(lambda a, s: _mm(a, _bf(s)), load(qx_scr), states)
            yield
            uv = _lock(lambda x, zz, w: jnp.concatenate([stack2b(_bf(x[0:SCAN_L] + zz)), w], axis=0),
                       xs, load(z_scr), load(vp_scr))
            both = _lock(lambda b, a, m: _mm(jnp.concatenate([b, a], axis=0), m),
                         load(bkt_scr), load(ar_scr), uv)
            yield
            states = _lock(lambda s, d, m: s * d + m[0:LANES], states, load(dec_scr), both)
            ys = _lock(lambda x, m: x[SCAN_L:] + m[LANES:], xs, both)
            y_scr[ci * SCAN_L:(ci + 1) * SCAN_L, :] = jnp.concatenate(ys, axis=1)
        for j in range(N_PAIRS):
            state_scr[j] = states[j]

    def post_steps(bi):
        rows = slice(bi * CHUNK_A, (bi + 1) * CHUNK_A)
        y = y_scr[rows, :]
        mean = _group_sum_lanes(y) * (1.0 / HEAD)
        yield
        yc = y - mean
        var = _group_sum_lanes(yc * yc) * (1.0 / HEAD)
        yield
        y = yc * lax.rsqrt(var + GN_EPS) * row("gn_g") + row("gn_b")
        cat_scr[rows, D_A:] = _bf((y + bonus_scr[rows, :]) * sgate_scr[rows, :])
        o_parts = []
        for c in range(0, D_MODEL, 2 * MXU_COLS):
            o_parts.append(_mm(cat_scr[rows, :], wout_ref[:, c:c + 2 * MXU_COLS]))
            yield
        o = jnp.concatenate(o_parts, axis=1)
        res = x_ref[rows, :] + gate_over_alpha * o
        mean = jnp.mean(res, axis=-1, keepdims=True)
        rc = res - mean
        var = jnp.mean(rc * rc, axis=-1, keepdims=True)
        out_ref[rows, :] = rc * lax.rsqrt(var + LN_EPS / ALPHA ** 2) * row("ln_g") + row("ln_b")

    def next_inproj_steps():
        hn = _bf(xn_ref[...] * (1.0 + modn_ref[1:2, :]) + modn_ref[0:1, :])
        for c in range(0, N_COLS_B, MXU_COLS):
            ce = min(c + MXU_COLS, N_COLS_B)
            pnext_scr[:, c:ce] = _mm(hn, win_ref[:, cb + c:cb + ce])
            yield

    n_blocks = tile // CHUNK_A
    next_proj = next_inproj_steps()
    next_steps_per_block = -(-len(range(0, N_COLS_B, MXU_COLS)) // n_blocks)
    for bi in range(n_blocks):
        group = [apply_steps(bi)]
        if bi > 0:
            group.append(post_steps(bi - 1))
        group += [gmlp_steps(bi), _take(next_proj, next_steps_per_block)]
        _interleave(group)
    _interleave([post_steps(n_blocks - 1), next_proj])

    p_scr[0:top, :] = p_scr[tile:tile + top, :]


def _wide_masks():
    t = np.arange(SCAN_L)[:, None]
    s = (np.arange(MXU_COLS) % SCAN_L)[None, :]
    same = lambda n: (t // n) == (s // n)
    masks = [t == s, s < t, s <= t, same(8),
             same(16) & ~same(8), same(32) & ~same(16), same(64) & ~same(32)]
    return jnp.asarray(np.stack(masks), F32)


def _const_spec(shape):
    zeros = (0,) * len(shape)
    return pl.BlockSpec(shape, lambda b, t: zeros)


def kernel(x, c, w_ada, b_ada, w_in, mu_b, ln_v_g, ln_v_b, w_spatial, b_spatial, w0, w_up, a0,
           a_up, k_k, k_a, r_k, gn_g, gn_b, w_out, ln_g, ln_b):
    batch, seq, d_model = x.shape
    assert d_model == D_MODEL and w_in.shape == (D_MODEL, N_COLS)
    assert seq % TILE_T == 0 and TILE_T % CHUNK_A == 0 and N_CHUNKS % PREP_CHUNKS == 0
    assert N_CHUNKS // PREP_CHUNKS >= 2

    mod = pl.pallas_call(
        _ada_kernel,
        out_shape=jax.ShapeDtypeStruct((batch, 3 * d_model), F32),
        name="adaln_mod",
    )(c, w_ada, b_ada.reshape(1, -1))
    mod = mod.reshape(batch, 3, d_model)

    named = dict(mu_b=mu_b, ln_v_g=ln_v_g, ln_v_b=ln_v_b, w0=w0, a0=a0, k_k=k_k, k_a=k_a, r_k=r_k,
                 gn_g=gn_g, gn_b=gn_b, ln_g=ln_g, ln_b=ln_b)
    rows = jnp.concatenate([named[name].reshape(-1).astype(F32) for name, _ in ROW_PARAMS])
    rows = rows.reshape(1, -1)
    tok = np.arange(SCAN_L)
    tri = jnp.asarray(tok[None, :] <= tok[:, None], BF16)
    zeros_lr = jnp.zeros((LOW_RANK, D_B), F32)
    w_lr = jnp.concatenate([jnp.concatenate([w_up, zeros_lr], axis=1),
                            jnp.concatenate([zeros_lr, a_up], axis=1)], axis=0).astype(BF16)
    bias_a = jnp.repeat(b_spatial.T, HEAD, axis=1)

    tiles = seq // TILE_T

    def next_block(b, t):
        s = jnp.minimum(b * tiles + t + 1, batch * tiles - 1)
        return s // tiles, (s % tiles) * (TILE_T // PREP_ROWS), 0

    operands = [
        (x, pl.BlockSpec((None, TILE_T, d_model), lambda b, t: (b, t, 0))),
        (mod, pl.BlockSpec((None, 3, d_model), lambda b, t: (b, 0, 0))),
        (x, pl.BlockSpec((None, PREP_ROWS, d_model), next_block)),
        (mod, pl.BlockSpec((None, 3, d_model), lambda b, t: (next_block(b, t)[0], 0, 0))),
        (w_in.astype(BF16), _const_spec((d_model, N_COLS))),
        (rows, _const_spec(rows.shape)),
        (w_spatial, _const_spec(w_spatial.shape)),
        (bias_a, _const_spec((CHUNK_A, D_A))),
        (w_lr, _const_spec((2 * LOW_RANK, 2 * D_B))),
        (w_out.astype(BF16), _const_spec((d_model, d_model))),
        (tri, _const_spec((SCAN_L, SCAN_L))),
        (_wide_masks(), _const_spec((7, SCAN_L, MXU_COLS))),
    ]
    n_slots = N_CHUNKS * N_PAIRS
    return pl.pallas_call(
        _layer_kernel,
        grid=(batch, seq // TILE_T),
        in_specs=[spec for _, spec in operands],
        out_specs=pl.BlockSpec((None, TILE_T, d_model), lambda b, t: (b, t, 0)),
        out_shape=jax.ShapeDtypeStruct(x.shape, x.dtype),
        scratch_shapes=[
            pltpu.VMEM((TILE_T, d_model), BF16),
            pltpu.VMEM((TILE_T + SUBLANES, N_COLS), F32),
            pltpu.VMEM((PREP_ROWS, N_COLS_B), F32),
            pltpu.VMEM((TILE_T, d_model), BF16),
            pltpu.VMEM((N_PAIRS, LANES, LANES), F32),
            pltpu.VMEM((n_slots, 2 * SCAN_L, LANES), BF16),
            pltpu.VMEM((n_slots, SCAN_L, LANES), F32),
            pltpu.VMEM((n_slots, SCAN_L, 2 * PAIR_ROWS), BF16),
            pltpu.VMEM((n_slots, PAIR_ROWS, LANES), BF16),
            pltpu.VMEM((n_slots, LANES, 2 * PAIR_ROWS), BF16),
            pltpu.VMEM((n_slots, LANES, LANES), F32),
            pltpu.VMEM((TILE_T, D_B), F32),
            pltpu.VMEM((TILE_T, D_B), F32),
            pltpu.VMEM((TILE_T, D_B), F32),
            pltpu.VMEM((D_A // HEAD, CHUNK_A, CHUNK_A), BF16),
        ],
        compiler_params=pltpu.CompilerParams(
            dimension_semantics=("arbitrary", "arbitrary"),
            vmem_limit_bytes=VMEM_LIMIT_BYTES),
        name="hybrid_layer",
    )(*[arr for arr, _ in operands])
```

```python
import math

import jax
import jax.numpy as jnp
import numpy as np
from jax import lax
from jax.experimental import pallas as pl
from jax.experimental.pallas import tpu as pltpu

F32 = jnp.float32
BF16 = jnp.bfloat16

D_MODEL = 1024
D_A = 512
D_B = 512
HEAD = 64
CHUNK_A = 128
LOW_RANK = 64
N_COLS_A = 3 * D_A
N_COLS_B = 4 * D_B + 2 * LOW_RANK
N_COLS = N_COLS_A + N_COLS_B
LN_EPS = 1e-5
GN_EPS = 64e-5
ALPHA = 2.0 ** 0.25
LOG_DECAY_SCALE = -math.exp(-0.5)

LANES = 128
SUBLANES = 8
MXU_COLS = 256
N_PAIRS = D_B // LANES
N_GROUPS = D_B // MXU_COLS
HEADS_PER_GROUP = MXU_COLS // HEAD
SCAN_L = 64
PAIR_ROWS = 2 * SCAN_L
TILE_T = 512
N_CHUNKS = TILE_T // SCAN_L
PREP_CHUNKS = 4
PREP_ROWS = PREP_CHUNKS * SCAN_L
CHUNKS_PER_BLOCK = CHUNK_A // SCAN_L
VMEM_LIMIT_BYTES = 56 * 1024 * 1024

ROW_PARAMS = (("mu_b", N_COLS_B), ("ln_v_g", D_A), ("ln_v_b", D_A), ("w0", D_B), ("a0", D_B),
              ("k_k", D_B), ("k_a", D_B), ("r_k", D_B), ("gn_g", D_B), ("gn_b", D_B),
              ("ln_g", D_MODEL), ("ln_b", D_MODEL))
ROW_OFFSETS = {}
for _name, _width in ROW_PARAMS:
    ROW_OFFSETS[_name] = (sum(w for _, w in ROW_PARAMS[:len(ROW_OFFSETS)]), _width)


def _bf(x):
    return x.astype(BF16)


def _mm(a, b):
    return jnp.dot(a, b, preferred_element_type=F32)


def _mm_nt(a, b):
    return lax.dot_general(a, b, (((1,), (1,)), ((), ())), preferred_element_type=F32)


def _interleave(step_generators):
    pending = list(step_generators)
    while pending:
        for gen in list(pending):
            try:
                next(gen)
            except StopIteration:
                pending.remove(gen)


def _take(gen, count):
    for _ in range(count):
        try:
            next(gen)
        except StopIteration:
            return
        yield


def _lock(fn, *lists):
    return [fn(*args) for args in zip(*lists)]


def _split_bf16(x, terms):
    parts = []
    rem = x
    for _ in range(terms):
        hi = _bf(rem)
        parts.append(hi)
        rem = rem - hi.astype(F32)
    return parts


def _group_sum_lanes(x):
    lane = lax.broadcasted_iota(jnp.int32, (x.shape[0], LANES), 1)
    first = lane < HEAD
    outs = []
    for j in range(x.shape[1] // LANES):
        t = x[:, LANES * j:LANES * (j + 1)]
        s_first = jnp.sum(jnp.where(first, t, 0.0), axis=-1, keepdims=True)
        s_second = jnp.sum(jnp.where(first, 0.0, t), axis=-1, keepdims=True)
        outs.append(jnp.where(first, s_first, s_second))
    return jnp.concatenate(outs, axis=1)


def _sigmoid(x, scale=1.0):
    return 0.5 * scale + (0.5 * scale) * jnp.tanh(0.5 * x)


def _silu(x):
    h = 0.5 * x
    return h + h * jnp.tanh(h)


def _gelu_tanh(x):
    c = math.sqrt(2.0 / math.pi)
    h = 0.5 * x
    return h + h * jnp.tanh(x * (c + (c * 0.044715) * (x * x)))


def _block_diag(y, head_masks):
    zero = jnp.zeros((y.shape[0], LANES), BF16)
    rows = []
    for h in range(HEADS_PER_GROUP):
        tile = h // 2
        kept = y[:, LANES * tile:LANES * (tile + 1)] * head_masks[h % 2]
        rows.append(jnp.concatenate([kept if t == tile else zero
                                     for t in range(MXU_COLS // LANES)], axis=1))
    return jnp.concatenate(rows, axis=0)


def _unit_lower_inverse_steps(a_abs, eye, m8, off16, off32, off64, bd_mask):
    bd = lambda y: _block_diag(y, bd_mask)
    a8 = _lock(lambda a: a * m8, a_abs)
    a8b = _lock(_bf, a8)
    a2b = _lock(lambda a: _bf(_mm(a, bd(a))), a8b)
    yield
    p1 = _lock(lambda a: eye + a, a8)
    both = _lock(lambda a, p: _mm(jnp.concatenate([a, _bf(p)], axis=0), bd(a)), a2b, p1)
    yield
    a4b = _lock(lambda m: _bf(m[0:SCAN_L]), both)
    p2 = _lock(lambda p, m: p + m[SCAN_L:], p1, both)
    t = _lock(lambda p, a: p + _mm(_bf(p), bd(a)), p2, a4b)
    yield
    for off in (off16, off32, off64):
        tb = _lock(_bf, t)
        to = _lock(lambda x, a: _bf(_mm(x, bd(_bf(a * off)))), tb, a_abs)
        yield
        t = _lock(lambda x, xo, xb: x + _mm(xo, bd(xb)), t, to, tb)
        yield
    return t


def _scan_prepare_steps(r, k, v, kk, bb, g, lw, consts, bd_mask):
    eye, tri_strict, tri_incl, m8, off16, off32, off64 = consts
    bd = lambda y: _block_diag(y, bd_mask)

    def factors(r, k, v, kk, bb, g, lw):
        e_g = jnp.exp(g)
        e_ge = jnp.exp(g - lw)
        e_ng = jnp.exp(-g)
        e_gl = jnp.exp(g[SCAN_L - 1:SCAN_L, :] - g)
        return (_bf(-(kk * e_ge)), _bf(r * e_g), _bf(v), _bf(bb * e_ng), _bf(k * e_ng),
                _bf(bb * e_gl), _bf(k * e_gl))

    xa, xr, vb, bh, kh, bbar, kbar = zip(*_lock(factors, r, k, v, kk, bb, g, lw))
    lhs = _lock(lambda a, b: jnp.concatenate([a, b], axis=0), xa, xr)
    def pair_rows(y, sl):
        yp = y[:, sl]
        return jnp.concatenate([yp * bd_mask[0], yp * bd_mask[1]], axis=0)

    pairs = [slice(LANES * p, LANES * (p + 1)) for p in range(MXU_COLS // LANES)]
    aa = _lock(lambda x, yb, yk: [_mm_nt(x[:, sl], jnp.concatenate([pair_rows(yb, sl),
                                                                     pair_rows(yk, sl)], axis=0))
                                  for sl in pairs], lhs, bh, kh)
    yield
    aa_b = _lock(lambda parts: jnp.concatenate([m[:, 0:LANES] for m in parts], axis=1), aa)
    aa_k = _lock(lambda parts: jnp.concatenate([m[:, LANES:] for m in parts], axis=1), aa)
    a_ab = _lock(lambda m: jnp.where(tri_strict != 0.0, m[0:SCAN_L], 0.0), aa_b)
    a_rb = _lock(lambda m: _bf(jnp.where(tri_incl != 0.0, m[SCAN_L:], 0.0)), aa_b)
    a_ak = _lock(lambda m: _bf(jnp.where(tri_strict != 0.0, m[0:SCAN_L], 0.0)), aa_k)
    a_rk = _lock(lambda m: _bf(jnp.where(tri_incl != 0.0, m[SCAN_L:], 0.0)), aa_k)
    t_inv = yield from _unit_lower_inverse_steps(a_ab, eye, m8, off16, off32, off64, bd_mask)
    t_inv = _lock(_bf, t_inv)
    akv = _lock(lambda a, y: _bf(_mm(a, bd(y))), a_ak, vb)
    yield
    def pair_weights(y, sl):
        yp = y[:, sl]
        return jnp.concatenate([yp * bd_mask[0], yp * bd_mask[1]], axis=0)

    pair_slices = [slice(LANES * p, LANES * (p + 1)) for p in range(MXU_COLS // LANES)]
    qz = _lock(lambda t, a, b: [_mm(t[:, sl], jnp.concatenate([pair_weights(a, sl),
                                                                pair_weights(b, sl)], axis=1))
                                for sl in pair_slices], t_inv, xa, akv)
    yield
    q = _lock(lambda parts: _bf(jnp.concatenate([m[:, 0:LANES] for m in parts], axis=1)), qz)
    z = _lock(lambda parts: jnp.concatenate([m[:, LANES:] for m in parts], axis=1), qz)
    return q, list(xr), z, a_rb, a_rk, list(vb), list(bbar), list(kbar)


def _ada_kernel(c_ref, w_ref, b_ref, o_ref):
    s_hi, s_lo = _split_bf16(_silu(c_ref[...]), 2)
    w_hi, w_lo = _split_bf16(w_ref[...], 2)
    acc = _mm(s_hi, w_hi) + _mm(s_hi, w_lo) + _mm(s_lo, w_hi)
    o_ref[...] = acc + b_ref[...]


def _layer_kernel(x_ref, mod_ref, xn_ref, modn_ref,
                  win_ref, rows_ref, ws_ref, bs_ref, wlr_ref, wout_ref,
                  tri_ref, cm_ref,
                  out_ref,
                  h_scr, p_scr, pnext_scr, cat_scr, state_scr,
                  qx_scr, z_scr, ar_scr, vp_scr, bkt_scr,
                  dec_scr, y_scr, bonus_scr, sgate_scr, wm_scr):
    t_idx = pl.program_id(1)
    tile = x_ref.shape[0]
    top = SUBLANES

    @pl.when(t_idx == 0)
    def _():
        state_scr[...] = jnp.zeros_like(state_scr)
        p_scr[0:top, :] = jnp.zeros((top, N_COLS), F32)

    def row(name):
        offset, width = ROW_OFFSETS[name]
        return rows_ref[:, offset:offset + width]

    shift = mod_ref[0:1, :]
    scale = mod_ref[1:2, :]
    gate_over_alpha = mod_ref[2:3, :] * (1.0 / ALPHA)

    h_scr[...] = _bf(x_ref[...] * (1.0 + scale) + shift)

    @pl.when((pl.program_id(0) == 0) & (t_idx == 0))
    def _():
        pnext_scr[...] = _mm(h_scr[0:PREP_ROWS, :], win_ref[:, N_COLS_A:])

    def inproj_steps(row_blocks, col0, col1):
        for row0, nrows in row_blocks:
            hb = h_scr[row0:row0 + nrows, :]
            for i, c in enumerate(range(col0, col1, MXU_COLS)):
                ce = min(c + MXU_COLS, col1)
                p_scr[top + row0:top + row0 + nrows, c:ce] = _mm(hb, win_ref[:, c:ce])
                if i % 2 == 1:
                    yield
            yield

    row_a = lax.broadcasted_iota(jnp.int32, (CHUNK_A, CHUNK_A), 0)
    col_a = lax.broadcasted_iota(jnp.int32, (CHUNK_A, CHUNK_A), 1)
    causal = col_a <= row_a
    lane_a = lax.broadcasted_iota(jnp.int32, (CHUNK_A, LANES), 1)
    first_a = lane_a < HEAD

    for gidx in range(ws_ref.shape[0]):
        wm_scr[gidx] = _bf(jnp.where(causal, ws_ref[gidx], 0.0))

    def gmlp_steps(ci):
        rows = slice(top + ci * CHUNK_A, top + (ci + 1) * CHUNK_A)
        v = _gelu_tanh(p_scr[rows, D_A:2 * D_A])
        mean = _group_sum_lanes(v) * (1.0 / HEAD)
        yield
        vc = v - mean
        var = _group_sum_lanes(vc * vc) * (1.0 / HEAD)
        yield
        v = vc * lax.rsqrt(var + LN_EPS) * row("ln_v_g") + row("ln_v_b")
        mixed = []
        for j in range(D_A // LANES):
            vj = _bf(v[:, LANES * j:LANES * (j + 1)])
            mixed.append(jnp.where(first_a, _mm(wm_scr[2 * j], vj), _mm(wm_scr[2 * j + 1], vj)))
        yield
        mix = jnp.concatenate(mixed, axis=1) + bs_ref[...]
        u = _gelu_tanh(p_scr[rows, 0:D_A])
        out_a = u * mix * _silu(p_scr[rows, 2 * D_A:3 * D_A])
        cat_scr[ci * CHUNK_A:(ci + 1) * CHUNK_A, 0:D_A] = _bf(out_a)

    consts = tuple(cm_ref[i] for i in range(7))
    row_b = lax.broadcasted_iota(jnp.int32, (PREP_ROWS, N_COLS_B), 0)
    lane_b = lax.broadcasted_iota(jnp.int32, (PREP_ROWS, LANES), 1)
    lane_s = lax.broadcasted_iota(jnp.int32, (SCAN_L, LANES), 1)
    first_f = lane_s < HEAD
    first_b = jnp.where(first_f, 1.0, 0.0).astype(BF16)
    second_b = jnp.where(first_f, 0.0, 1.0).astype(BF16)
    bd_mask = (first_b, second_b)
    cb = N_COLS_A

    def stack2b(zz):
        return jnp.concatenate([zz * first_b, zz * second_b], axis=0)

    def prepare_steps(bi):
        r0 = top + bi * PREP_ROWS
        if bi == 0:
            cur = pnext_scr[...]
            before = p_scr[0:top, cb:]
        else:
            cur = p_scr[r0:r0 + PREP_ROWS, cb:]
            before = (pnext_scr[PREP_ROWS - SUBLANES:, :] if bi == 1
                      else p_scr[r0 - SUBLANES:r0, cb:])
        prev = jnp.where(row_b == 0, before[SUBLANES - 1:SUBLANES, :], pltpu.roll(cur, 1, axis=0))
        ps = cur + row("mu_b") * (prev - cur)
        r = ps[:, 0:D_B]
        k = ps[:, D_B:2 * D_B]
        v = ps[:, 2 * D_B:3 * D_B]
        gate_b = ps[:, 3 * D_B:4 * D_B]
        low = ps[:, 4 * D_B:]
        low = jnp.where(lane_b < LOW_RANK, jnp.tanh(low), low)
        up = _mm(_bf(low), wlr_ref[...])
        yield
        lw = _sigmoid(row("w0") + up[:, 0:D_B], LOG_DECAY_SCALE)
        ah = _sigmoid(row("a0") + up[:, D_B:])
        lw_terms = jnp.concatenate(_split_bf16(lw, 2), axis=1)
        g = jnp.concatenate([_mm(tri_ref[...], lw_terms[SCAN_L * c:SCAN_L * (c + 1), :])
                             for c in range(PREP_CHUNKS)], axis=0)
        g = g[:, 0:D_B] + g[:, D_B:]
        kk = k * row("k_k")
        kk = kk * lax.rsqrt(_group_sum_lanes(kk * kk) + 1e-12)
        yield
        k = k * (1.0 + (ah - 1.0) * row("k_a"))
        bb = kk * ah

        o0 = bi * PREP_ROWS
        bonus_scr[o0:o0 + PREP_ROWS, :] = _group_sum_lanes(r * k * row("r_k")) * v
        sgate_scr[o0:o0 + PREP_ROWS, :] = _silu(gate_b)
        yield

        problems = [(c, gi) for c in range(PREP_CHUNKS) for gi in range(N_GROUPS)]
        cut = lambda zz: [zz[SCAN_L * c:SCAN_L * (c + 1), MXU_COLS * gi:MXU_COLS * (gi + 1)]
                          for c, gi in problems]
        for c in range(PREP_CHUNKS):
            last = jnp.exp(g[SCAN_L * (c + 1) - 1:SCAN_L * (c + 1), :])
            for j in range(N_PAIRS):
                rowwise = jnp.broadcast_to(last[:, LANES * j:LANES * (j + 1)], (LANES, LANES))
                dec_scr[(bi * PREP_CHUNKS + c) * N_PAIRS + j] = rowwise.T

        q, xr, z, a_rb, a_rk, vb, bbar, kbar = yield from _scan_prepare_steps(
            cut(r), cut(k), cut(v), cut(kk), cut(bb), cut(g), cut(lw), consts, bd_mask)
        for i, (c, gi) in enumerate(problems):
            for pp in range(MXU_COLS // LANES):
                slot = (bi * PREP_CHUNKS + c) * N_PAIRS + gi * (MXU_COLS // LANES) + pp
                sl = slice(LANES * pp, LANES * (pp + 1))
                qx_scr[slot] = jnp.concatenate([q[i][:, sl], xr[i][:, sl]], axis=0)
                z_scr[slot] = z[i][:, sl]
                ar_scr[slot] = jnp.concatenate([a_rb[i][:, sl], a_rk[i][:, sl]], axis=1)
                vp_scr[slot] = stack2b(vb[i][:, sl])
                bkt_scr[slot] = jnp.concatenate([stack2b(bbar[i][:, sl]), stack2b(kbar[i][:, sl])],
                                                axis=0).T

    n_prep = N_CHUNKS // PREP_CHUNKS
    row_block = lambda i: (i * PREP_ROWS, PREP_ROWS)
    for bi in range(n_prep):
        if bi + 1 < n_prep:
            fill = inproj_steps([row_block(bi + 1)], cb, N_COLS)
        else:
            fill = inproj_steps([row_block(i) for i in range(n_prep)], 0, cb)
        _interleave([prepare_steps(bi), fill])

    def apply_steps(bi):
        states = [state_scr[j] for j in range(N_PAIRS)]
        for c in range(CHUNKS_PER_BLOCK):
            ci = bi * CHUNKS_PER_BLOCK + c
            load = lambda ref: [ref[ci * N_PAIRS + j] for j in range(N_PAIRS)]
            xs = _lock(lambda a, s: _mm(a, _bf(s)), load(qx_scr), states)
            yield
            uv = _lock(lambda x, zz, w: jnp.concatenate([stack2b(_bf(x[0:SCAN_L] + zz)), w], axis=0),
                       xs, load(z_scr), load(vp_scr))
            both = _lock(lambda b, a, m: _mm(jnp.concatenate([b, a], axis=0), m),
                         load(bkt_scr), load(ar_scr), uv)
            yield
            states = _lock(lambda s, d, m: s * d + m[0:LANES], states, load(dec_scr), both)
            ys = _lock(lambda x, m: x[SCAN_L:] + m[LANES:], xs, both)
            y_scr[ci * SCAN_L:(ci + 1) * SCAN_L, :] = jnp.concatenate(ys, axis=1)
        for j in range(N_PAIRS):
            state_scr[j] = states[j]

    def post_steps(bi):
        rows = slice(bi * CHUNK_A, (bi + 1) * CHUNK_A)
        y = y_scr[rows, :]
        mean = _group_sum_lanes(y) * (1.0 / HEAD)
        yield
        yc = y - mean
        var = _group_sum_lanes(yc * yc) * (1.0 / HEAD)
        yield
        y = yc * lax.rsqrt(var + GN_EPS) * row("gn_g") + row("gn_b")
        cat_scr[rows, D_A:] = _bf((y + bonus_scr[rows, :]) * sgate_scr[rows, :])
        o_parts = []
        for c in range(0, D_MODEL, MXU_COLS):
            o_parts.append(_mm(cat_scr[rows, :], wout_ref[:, c:c + MXU_COLS]))
            yield
        o = jnp.concatenate(o_parts, axis=1)
        res = x_ref[rows, :] + gate_over_alpha * o
        mean = jnp.mean(res, axis=-1, keepdims=True)
        rc = res - mean
        var = jnp.mean(rc * rc, axis=-1, keepdims=True)
        out_ref[rows, :] = rc * lax.rsqrt(var + LN_EPS / ALPHA ** 2) * row("ln_g") + row("ln_b")

    def next_inproj_steps():
        hn = _bf(xn_ref[...] * (1.0 + modn_ref[1:2, :]) + modn_ref[0:1, :])
        for c in range(0, N_COLS_B, MXU_COLS):
            ce = min(c + MXU_COLS, N_COLS_B)
            pnext_scr[:, c:ce] = _mm(hn, win_ref[:, cb + c:cb + ce])
            yield

    n_blocks = tile // CHUNK_A
    next_proj = next_inproj_steps()
    next_steps_per_block = -(-len(range(0, N_COLS_B, MXU_COLS)) // n_blocks)
    for bi in range(n_blocks):
        group = [apply_steps(bi)]
        if bi > 0:
            group.append(post_steps(bi - 1))
        group += [gmlp_steps(bi), _take(next_proj, next_steps_per_block)]
        _interleave(group)
    _interleave([post_steps(n_blocks - 1), next_proj])

    p_scr[0:top, :] = p_scr[tile:tile + top, :]


def _wide_masks():
    t = np.arange(SCAN_L)[:, None]
    s = (np.arange(MXU_COLS) % SCAN_L)[None, :]
    same = lambda n: (t // n) == (s // n)
    masks = [t == s, s < t, s <= t, same(8),
             same(16) & ~same(8), same(32) & ~same(16), same(64) & ~same(32)]
    return jnp.asarray(np.stack(masks), F32)


def _const_spec(shape):
    zeros = (0,) * len(shape)
    return pl.BlockSpec(shape, lambda b, t: zeros)


def kernel(x, c, w_ada, b_ada, w_in, mu_b, ln_v_g, ln_v_b, w_spatial, b_spatial, w0, w_up, a0,
           a_up, k_k, k_a, r_k, gn_g, gn_b, w_out, ln_g, ln_b):
    batch, seq, d_model = x.shape
    assert d_model == D_MODEL and w_in.shape == (D_MODEL, N_COLS)
    assert seq % TILE_T == 0 and TILE_T % CHUNK_A == 0 and N_CHUNKS % PREP_CHUNKS == 0
    assert N_CHUNKS // PREP_CHUNKS >= 2

    mod = pl.pallas_call(
        _ada_kernel,
        out_shape=jax.ShapeDtypeStruct((batch, 3 * d_model), F32),
        name="adaln_mod",
    )(c, w_ada, b_ada.reshape(1, -1))
    mod = mod.reshape(batch, 3, d_model)

    named = dict(mu_b=mu_b, ln_v_g=ln_v_g, ln_v_b=ln_v_b, w0=w0, a0=a0, k_k=k_k, k_a=k_a, r_k=r_k,
                 gn_g=gn_g, gn_b=gn_b, ln_g=ln_g, ln_b=ln_b)
    rows = jnp.concatenate([named[name].reshape(-1).astype(F32) for name, _ in ROW_PARAMS])
    rows = rows.reshape(1, -1)
    tok = np.arange(SCAN_L)
    tri = jnp.asarray(tok[None, :] <= tok[:, None], BF16)
    zeros_lr = jnp.zeros((LOW_RANK, D_B), F32)
    w_lr = jnp.concatenate([jnp.concatenate([w_up, zeros_lr], axis=1),
                            jnp.concatenate([zeros_lr, a_up], axis=1)], axis=0).astype(BF16)
    bias_a = jnp.repeat(b_spatial.T, HEAD, axis=1)

    tiles = seq // TILE_T

    def next_block(b, t):
        s = jnp.minimum(b * tiles + t + 1, batch * tiles - 1)
        return s // tiles, (s % tiles) * (TILE_T // PREP_ROWS), 0

    operands = [
        (x, pl.BlockSpec((None, TILE_T, d_model), lambda b, t: (b, t, 0))),
        (mod, pl.BlockSpec((None, 3, d_model), lambda b, t: (b, 0, 0))),
        (x, pl.BlockSpec((None, PREP_ROWS, d_model), next_block)),
        (mod, pl.BlockSpec((None, 3, d_model), lambda b, t: (next_block(b, t)[0], 0, 0))),
        (w_in.astype(BF16), _const_spec((d_model, N_COLS))),
        (rows, _const_spec(rows.shape)),
        (w_spatial, _const_spec(w_spatial.shape)),
        (bias_a, _const_spec((CHUNK_A, D_A))),
        (w_lr, _const_spec((2 * LOW_RANK, 2 * D_B))),
        (w_out.astype(BF16), _const_spec((d_model, d_model))),
        (tri, _const_spec((SCAN_L, SCAN_L))),
        (_wide_masks(), _const_spec((7, SCAN_L, MXU_COLS))),
    ]
    n_slots = N_CHUNKS * N_PAIRS
    return pl.pallas_call(
        _layer_kernel,
        grid=(batch, seq // TILE_T),
        in_specs=[spec for _, spec in operands],
        out_specs=pl.BlockSpec((None, TILE_T, d_model), lambda b, t: (b, t, 0)),
        out_shape=jax.ShapeDtypeStruct(x.shape, x.dtype),
        scratch_shapes=[
            pltpu.VMEM((TILE_T, d_model), BF16),
            pltpu.VMEM((TILE_T + SUBLANES, N_COLS), F32),
            pltpu.VMEM((PREP_ROWS, N_COLS_B), F32),
            pltpu.VMEM((TILE_T, d_model), BF16),
            pltpu.VMEM((N_PAIRS, LANES, LANES), F32),
            pltpu.VMEM((n_slots, 2 * SCAN_L, LANES), BF16),
            pltpu.VMEM((n_slots, SCAN_L, LANES), F32),
            pltpu.VMEM((n_slots, SCAN_L, 2 * PAIR_ROWS), BF16),
            pltpu.VMEM((n_slots, PAIR_ROWS, LANES), BF16),
            pltpu.VMEM((n_slots, LANES, 2 * PAIR_ROWS), BF16),
            pltpu.VMEM((n_slots, LANES, LANES), F32),
            pltpu.VMEM((TILE_T, D_B), F32),
            pltpu.VMEM((TILE_T, D_B), F32),
            pltpu.VMEM((TILE_T, D_B), F32),
            pltpu.VMEM((D_A // HEAD, CHUNK_A, CHUNK_A), BF16),
        ],
        compiler_params=pltpu.CompilerParams(
            dimension_semantics=("arbitrary", "arbitrary"),
            vmem_limit_bytes=VMEM_LIMIT_BYTES),
        name="hybrid_layer",
    )(*[arr for arr, _ in operands])
```

```python
import math

import jax
import jax.numpy as jnp
import numpy as np
from jax import lax
from jax.experimental import pallas as pl
from jax.experimental.pallas import tpu as pltpu

F32 = jnp.float32
BF16 = jnp.bfloat16

D_MODEL = 1024
D_A = 512
D_B = 512
HEAD = 64
CHUNK_A = 128
LOW_RANK = 64
N_COLS_A = 3 * D_A
N_COLS_B = 4 * D_B + 2 * LOW_RANK
N_COLS = N_COLS_A + N_COLS_B
LN_EPS = 1e-5
GN_EPS = 64e-5
ALPHA = 2.0 ** 0.25
LOG_DECAY_SCALE = -math.exp(-0.5)

LANES = 128
SUBLANES = 8
MXU_COLS = 256
N_PAIRS = D_B // LANES
N_GROUPS = D_B // MXU_COLS
HEADS_PER_GROUP = MXU_COLS // HEAD
SCAN_L = 64
PAIR_ROWS = 2 * SCAN_L
TILE_T = 512
N_CHUNKS = TILE_T // SCAN_L
PREP_CHUNKS = 4
PREP_ROWS = PREP_CHUNKS * SCAN_L
CHUNKS_PER_BLOCK = CHUNK_A // SCAN_L
VMEM_LIMIT_BYTES = 56 * 1024 * 1024

ROW_PARAMS = (("mu_b", N_COLS_B), ("ln_v_g", D_A), ("ln_v_b", D_A), ("w0", D_B), ("a0", D_B),
              ("k_k", D_B), ("k_a", D_B), ("r_k", D_B), ("gn_g", D_B), ("gn_b", D_B),
              ("ln_g", D_MODEL), ("ln_b", D_MODEL))
ROW_OFFSETS = {}
for _name, _width in ROW_PARAMS:
    ROW_OFFSETS[_name] = (sum(w for _, w in ROW_PARAMS[:len(ROW_OFFSETS)]), _width)


def _bf(x):
    return x.astype(BF16)


def _mm(a, b):
    return jnp.dot(a, b, preferred_element_type=F32)


def _mm_nt(a, b):
    return lax.dot_general(a, b, (((1,), (1,)), ((), ())), preferred_element_type=F32)


def _interleave(step_generators):
    pending = list(step_generators)
    while pending:
        for gen in list(pending):
            try:
                next(gen)
            except StopIteration:
                pending.remove(gen)


def _take(gen, count):
    for _ in range(count):
        try:
            next(gen)
        except StopIteration:
            return
        yield


def _lock(fn, *lists):
    return [fn(*args) for args in zip(*lists)]


def _split_bf16(x, terms):
    parts = []
    rem = x
    for _ in range(terms):
        hi = _bf(rem)
        parts.append(hi)
        rem = rem - hi.astype(F32)
    return parts


def _group_sum_lanes(x):
    lane = lax.broadcasted_iota(jnp.int32, (x.shape[0], LANES), 1)
    first = lane < HEAD
    outs = []
    for j in range(x.shape[1] // LANES):
        t = x[:, LANES * j:LANES * (j + 1)]
        s_first = jnp.sum(jnp.where(first, t, 0.0), axis=-1, keepdims=True)
        s_second = jnp.sum(jnp.where(first, 0.0, t), axis=-1, keepdims=True)
        outs.append(jnp.where(first, s_first, s_second))
    return jnp.concatenate(outs, axis=1)


def _sigmoid(x, scale=1.0):
    return 0.5 * scale + (0.5 * scale) * jnp.tanh(0.5 * x)


def _silu(x):
    h = 0.5 * x
    return h + h * jnp.tanh(h)


def _gelu_tanh(x):
    c = math.sqrt(2.0 / math.pi)
    h = 0.5 * x
    return h + h * jnp.tanh(x * (c + (c * 0.044715) * (x * x)))


def _block_diag(y, head_masks):
    zero = jnp.zeros((y.shape[0], LANES), BF16)
    rows = []
    for h in range(HEADS_PER_GROUP):
        tile = h // 2
        kept = y[:, LANES * tile:LANES * (tile + 1)] * head_masks[h % 2]
        rows.append(jnp.concatenate([kept if t == tile else zero
                                     for t in range(MXU_COLS // LANES)], axis=1))
    return jnp.concatenate(rows, axis=0)


def _unit_lower_inverse_steps(a_abs, eye, m8, off16, off32, off64, bd_mask):
    bd = lambda y: _block_diag(y, bd_mask)
    a8 = _lock(lambda a: a * m8, a_abs)
    a8b = _lock(_bf, a8)
    a2b = _lock(lambda a: _bf(_mm(a, bd(a))), a8b)
    yield
    p1 = _lock(lambda a: eye + a, a8)
    both = _lock(lambda a, p: _mm(jnp.concatenate([a, _bf(p)], axis=0), bd(a)), a2b, p1)
    yield
    a4b = _lock(lambda m: _bf(m[0:SCAN_L]), both)
    p2 = _lock(lambda p, m: p + m[SCAN_L:], p1, both)
    t = _lock(lambda p, a: p + _mm(_bf(p), bd(a)), p2, a4b)
    yield
    for off in (off16, off32, off64):
        tb = _lock(_bf, t)
        to = _lock(lambda x, a: _bf(_mm(x, bd(_bf(a * off)))), tb, a_abs)
        yield
        t = _lock(lambda x, xo, xb: x + _mm(xo, bd(xb)), t, to, tb)
        yield
    return t


def _scan_prepare_steps(r, k, v, kk, bb, g, lw, consts, bd_mask):
    eye, tri_strict, tri_incl, m8, off16, off32, off64 = consts
    bd = lambda y: _block_diag(y, bd_mask)

    def factors(r, k, v, kk, bb, g, lw):
        e_g = jnp.exp(g)
        e_ge = jnp.exp(g - lw)
        e_ng = jnp.exp(-g)
        e_gl = jnp.exp(g[SCAN_L - 1:SCAN_L, :] - g)
        return (_bf(-(kk * e_ge)), _bf(r * e_g), _bf(v), _bf(bb * e_ng), _bf(k * e_ng),
                _bf(bb * e_gl), _bf(k * e_gl))

    xa, xr, vb, bh, kh, bbar, kbar = zip(*_lock(factors, r, k, v, kk, bb, g, lw))
    lhs = _lock(lambda a, b: jnp.concatenate([a, b], axis=0), xa, xr)
    def pair_rows(y, sl):
        yp = y[:, sl]
        return jnp.concatenate([yp * bd_mask[0], yp * bd_mask[1]], axis=0)

    pairs = [slice(LANES * p, LANES * (p + 1)) for p in range(MXU_COLS // LANES)]
    aa = _lock(lambda x, yb, yk: [_mm_nt(x[:, sl], jnp.concatenate([pair_rows(yb, sl),
                                                                     pair_rows(yk, sl)], axis=0))
                                  for sl in pairs], lhs, bh, kh)
    yield
    aa_b = _lock(lambda parts: jnp.concatenate([m[:, 0:LANES] for m in parts], axis=1), aa)
    aa_k = _lock(lambda parts: jnp.concatenate([m[:, LANES:] for m in parts], axis=1), aa)
    a_ab = _lock(lambda m: jnp.where(tri_strict != 0.0, m[0:SCAN_L], 0.0), aa_b)
    a_rb = _lock(lambda m: _bf(jnp.where(tri_incl != 0.0, m[SCAN_L:], 0.0)), aa_b)
    a_ak = _lock(lambda m: _bf(jnp.where(tri_strict != 0.0, m[0:SCAN_L], 0.0)), aa_k)
    a_rk = _lock(lambda m: _bf(jnp.where(tri_incl != 0.0, m[SCAN_L:], 0.0)), aa_k)
    t_inv = yield from _unit_lower_inverse_steps(a_ab, eye, m8, off16, off32, off64, bd_mask)
    t_inv = _lock(_bf, t_inv)
    akv = _lock(lambda a, y: _bf(_mm(a, bd(y))), a_ak, vb)
    yield
    def pair_weights(y, sl):
        yp = y[:, sl]
        return jnp.concatenate([yp * bd_mask[0], yp * bd_mask[1]], axis=0)

    pair_slices = [slice(LANES * p, LANES * (p + 1)) for p in range(MXU_COLS // LANES)]
    qz = _lock(lambda t, a, b: [_mm(t[:, sl], jnp.concatenate([pair_weights(a, sl),
                                                                pair_weights(b, sl)], axis=1))
                                for sl in pair_slices], t_inv, xa, akv)
    yield
    q = _lock(lambda parts: _bf(jnp.concatenate([m[:, 0:LANES] for m in parts], axis=1)), qz)
    z = _lock(lambda parts: jnp.concatenate([m[:, LANES:] for m in parts], axis=1), qz)
    return q, list(xr), z, a_rb, a_rk, list(vb), list(bbar), list(kbar)


def _ada_kernel(c_ref, w_ref, b_ref, o_ref):
    s_hi, s_lo = _split_bf16(_silu(c_ref[...]), 2)
    w_hi, w_lo = _split_bf16(w_ref[...], 2)
    acc = _mm(s_hi, w_hi) + _mm(s_hi, w_lo) + _mm(s_lo, w_hi)
    o_ref[...] = acc + b_ref[...]


def _out_kernel(x_ref, cat_ref, mod_ref, wout_ref, ln_ref, out_ref):
    o = _mm(cat_ref[...], wout_ref[...])
    gate_over_alpha = mod_ref[2:3, :] * (1.0 / ALPHA)
    res = x_ref[...] + gate_over_alpha * o
    mean = jnp.mean(res, axis=-1, keepdims=True)
    rc = res - mean
    var = jnp.mean(rc * rc, axis=-1, keepdims=True)
    out_ref[...] = rc * lax.rsqrt(var + LN_EPS / ALPHA ** 2) * ln_ref[0:1, :] + ln_ref[1:2, :]


def _layer_kernel(x_ref, mod_ref, xn_ref, modn_ref,
                  win_ref, rows_ref, ws_ref, bs_ref, wlr_ref,
                  tri_ref, cm_ref,
                  out_ref,
                  h_scr, p_scr, pnext_scr, cat_scr, state_scr,
                  qx_scr, z_scr, ar_scr, vp_scr, bkt_scr,
                  dec_scr, y_scr, bonus_scr, sgate_scr, wm_scr):
    t_idx = pl.program_id(1)
    tile = x_ref.shape[0]
    top = SUBLANES

    @pl.when(t_idx == 0)
    def _():
        state_scr[...] = jnp.zeros_like(state_scr)
        p_scr[0:top, :] = jnp.zeros((top, N_COLS), F32)

    def row(name):
        offset, width = ROW_OFFSETS[name]
        return rows_ref[:, offset:offset + width]

    shift = mod_ref[0:1, :]
    scale = mod_ref[1:2, :]

    h_scr[...] = _bf(x_ref[...] * (1.0 + scale) + shift)

    @pl.when((pl.program_id(0) == 0) & (t_idx == 0))
    def _():
        pnext_scr[...] = _mm(h_scr[0:PREP_ROWS, :], win_ref[:, N_COLS_A:])

    def inproj_steps(row_blocks, col0, col1):
        for row0, nrows in row_blocks:
            hb = h_scr[row0:row0 + nrows, :]
            for i, c in enumerate(range(col0, col1, MXU_COLS)):
                ce = min(c + MXU_COLS, col1)
                p_scr[top + row0:top + row0 + nrows, c:ce] = _mm(hb, win_ref[:, c:ce])
                if i % 2 == 1:
                    yield
            yield

    row_a = lax.broadcasted_iota(jnp.int32, (CHUNK_A, CHUNK_A), 0)
    col_a = lax.broadcasted_iota(jnp.int32, (CHUNK_A, CHUNK_A), 1)
    causal = col_a <= row_a
    lane_a = lax.broadcasted_iota(jnp.int32, (CHUNK_A, LANES), 1)
    first_a = lane_a < HEAD

    for gidx in range(ws_ref.shape[0]):
        wm_scr[gidx] = _bf(jnp.where(causal, ws_ref[gidx], 0.0))

    def gmlp_steps(ci):
        rows = slice(top + ci * CHUNK_A, top + (ci + 1) * CHUNK_A)
        v = _gelu_tanh(p_scr[rows, D_A:2 * D_A])
        mean = _group_sum_lanes(v) * (1.0 / HEAD)
        yield
        vc = v - mean
        var = _group_sum_lanes(vc * vc) * (1.0 / HEAD)
        yield
        v = vc * lax.rsqrt(var + LN_EPS) * row("ln_v_g") + row("ln_v_b")
        mixed = []
        for j in range(D_A // LANES):
            vj = _bf(v[:, LANES * j:LANES * (j + 1)])
            mixed.append(jnp.where(first_a, _mm(wm_scr[2 * j], vj), _mm(wm_scr[2 * j + 1], vj)))
        yield
        mix = jnp.concatenate(mixed, axis=1) + bs_ref[...]
        u = _gelu_tanh(p_scr[rows, 0:D_A])
        out_a = u * mix * _silu(p_scr[rows, 2 * D_A:3 * D_A])
        cat_scr[ci * CHUNK_A:(ci + 1) * CHUNK_A, 0:D_A] = _bf(out_a)

    consts = tuple(cm_ref[i] for i in range(7))
    row_b = lax.broadcasted_iota(jnp.int32, (PREP_ROWS, N_COLS_B), 0)
    lane_b = lax.broadcasted_iota(jnp.int32, (PREP_ROWS, LANES), 1)
    lane_s = lax.broadcasted_iota(jnp.int32, (SCAN_L, LANES), 1)
    first_f = lane_s < HEAD
    first_b = jnp.where(first_f, 1.0, 0.0).astype(BF16)
    second_b = jnp.where(first_f, 0.0, 1.0).astype(BF16)
    bd_mask = (first_b, second_b)
    cb = N_COLS_A

    def stack2b(zz):
        return jnp.concatenate([zz * first_b, zz * second_b], axis=0)

    def prepare_steps(bi):
        r0 = top + bi * PREP_ROWS
        if bi == 0:
            cur = pnext_scr[...]
            before = p_scr[0:top, cb:]
        else:
            cur = p_scr[r0:r0 + PREP_ROWS, cb:]
            before = (pnext_scr[PREP_ROWS - SUBLANES:, :] if bi == 1
                      else p_scr[r0 - SUBLANES:r0, cb:])
        prev = jnp.where(row_b == 0, before[SUBLANES - 1:SUBLANES, :], pltpu.roll(cur, 1, axis=0))
        ps = cur + row("mu_b") * (prev - cur)
        r = ps[:, 0:D_B]
        k = ps[:, D_B:2 * D_B]
        v = ps[:, 2 * D_B:3 * D_B]
        gate_b = ps[:, 3 * D_B:4 * D_B]
        low = ps[:, 4 * D_B:]
        low = jnp.where(lane_b < LOW_RANK, jnp.tanh(low), low)
        up = _mm(_bf(low), wlr_ref[...])
        yield
        lw = _sigmoid(row("w0") + up[:, 0:D_B], LOG_DECAY_SCALE)
        ah = _sigmoid(row("a0") + up[:, D_B:])
        lw_terms = jnp.concatenate(_split_bf16(lw, 2), axis=1)
        g = jnp.concatenate([_mm(tri_ref[...], lw_terms[SCAN_L * c:SCAN_L * (c + 1), :])
                             for c in range(PREP_CHUNKS)], axis=0)
        g = g[:, 0:D_B] + g[:, D_B:]
        kk = k * row("k_k")
        kk = kk * lax.rsqrt(_group_sum_lanes(kk * kk) + 1e-12)
        yield
        k = k * (1.0 + (ah - 1.0) * row("k_a"))
        bb = kk * ah

        o0 = bi * PREP_ROWS
        bonus_scr[o0:o0 + PREP_ROWS, :] = _group_sum_lanes(r * k * row("r_k")) * v
        sgate_scr[o0:o0 + PREP_ROWS, :] = _silu(gate_b)
        yield

        problems = [(c, gi) for c in range(PREP_CHUNKS) for gi in range(N_GROUPS)]
        cut = lambda zz: [zz[SCAN_L * c:SCAN_L * (c + 1), MXU_COLS * gi:MXU_COLS * (gi + 1)]
                          for c, gi in problems]
        for c in range(PREP_CHUNKS):
            last = jnp.exp(g[SCAN_L * (c + 1) - 1:SCAN_L * (c + 1), :])
            for j in range(N_PAIRS):
                rowwise = jnp.broadcast_to(last[:, LANES * j:LANES * (j + 1)], (LANES, LANES))
                dec_scr[(bi * PREP_CHUNKS + c) * N_PAIRS + j] = rowwise.T

        q, xr, z, a_rb, a_rk, vb, bbar, kbar = yield from _scan_prepare_steps(
            cut(r), cut(k), cut(v), cut(kk), cut(bb), cut(g), cut(lw), consts, bd_mask)
        for i, (c, gi) in enumerate(problems):
            for pp in range(MXU_COLS // LANES):
                slot = (bi * PREP_CHUNKS + c) * N_PAIRS + gi * (MXU_COLS // LANES) + pp
                sl = slice(LANES * pp, LANES * (pp + 1))
                qx_scr[slot] = jnp.concatenate([q[i][:, sl], xr[i][:, sl]], axis=0)
                z_scr[slot] = z[i][:, sl]
                ar_scr[slot] = jnp.concatenate([a_rb[i][:, sl], a_rk[i][:, sl]], axis=1)
                vp_scr[slot] = stack2b(vb[i][:, sl])
                bkt_scr[slot] = jnp.concatenate([stack2b(bbar[i][:, sl]), stack2b(kbar[i][:, sl])],
                                                axis=0).T

    n_prep = N_CHUNKS // PREP_CHUNKS
    row_block = lambda i: (i * PREP_ROWS, PREP_ROWS)
    for bi in range(n_prep):
        if bi + 1 < n_prep:
            fill = inproj_steps([row_block(bi + 1)], cb, N_COLS)
        else:
            fill = inproj_steps([row_block(i) for i in range(n_prep)], 0, cb)
        _interleave([prepare_steps(bi), fill])

    def apply_steps(bi):
        states = [state_scr[j] for j in range(N_PAIRS)]
        for c in range(CHUNKS_PER_BLOCK):
            ci = bi * CHUNKS_PER_BLOCK + c
            load = lambda ref: [ref[ci * N_PAIRS + j] for j in range(N_PAIRS)]
            xs = _lock(lambda a, s: _mm(a, _bf(s)), load(qx_scr), states)
            yield
            uv = _lock(lambda x, zz, w: jnp.concatenate([stack2b(_bf(x[0:SCAN_L] + zz)), w], axis=0),
                       xs, load(z_scr), load(vp_scr))
            both = _lock(lambda b, a, m: _mm(jnp.concatenate([b, a], axis=0), m),
                         load(bkt_scr), load(ar_scr), uv)
            yield
            states = _lock(lambda s, d, m: s * d + m[0:LANES], states, load(dec_scr), both)
            ys = _lock(lambda x, m: x[SCAN_L:] + m[LANES:], xs, both)
            y_scr[ci * SCAN_L:(ci + 1) * SCAN_L, :] = jnp.concatenate(ys, axis=1)
        for j in range(N_PAIRS):
            state_scr[j] = states[j]

    def post_steps(bi):
        rows = slice(bi * CHUNK_A, (bi + 1) * CHUNK_A)
        y = y_scr[rows, :]
        mean = _group_sum_lanes(y) * (1.0 / HEAD)
        yield
        yc = y - mean
        var = _group_sum_lanes(yc * yc) * (1.0 / HEAD)
        yield
        y = yc * lax.rsqrt(var + GN_EPS) * row("gn_g") + row("gn_b")
        cat_scr[rows, D_A:] = _bf((y + bonus_scr[rows, :]) * sgate_scr[rows, :])
        yield
        out_ref[rows, :] = cat_scr[rows, :]

    def next_inproj_steps():
        hn = _bf(xn_ref[...] * (1.0 + modn_ref[1:2, :]) + modn_ref[0:1, :])
        for c in range(0, N_COLS_B, MXU_COLS):
            ce = min(c + MXU_COLS, N_COLS_B)
            pnext_scr[:, c:ce] = _mm(hn, win_ref[:, cb + c:cb + ce])
            yield

    n_blocks = tile // CHUNK_A
    next_proj = next_inproj_steps()
    next_steps_per_block = -(-len(range(0, N_COLS_B, MXU_COLS)) // n_blocks)
    for bi in range(n_blocks):
        group = [apply_steps(bi)]
        if bi > 0:
            group.append(post_steps(bi - 1))
        group += [gmlp_steps(bi), _take(next_proj, next_steps_per_block)]
        _interleave(group)
    _interleave([post_steps(n_blocks - 1), next_proj])

    p_scr[0:top, :] = p_scr[tile:tile + top, :]


def _wide_masks():
    t = np.arange(SCAN_L)[:, None]
    s = (np.arange(MXU_COLS) % SCAN_L)[None, :]
    same = lambda n: (t // n) == (s // n)
    masks = [t == s, s < t, s <= t, same(8),
             same(16) & ~same(8), same(32) & ~same(16), same(64) & ~same(32)]
    return jnp.asarray(np.stack(masks), F32)


def _const_spec(shape):
    zeros = (0,) * len(shape)
    return pl.BlockSpec(shape, lambda b, t: zeros)


def kernel(x, c, w_ada, b_ada, w_in, mu_b, ln_v_g, ln_v_b, w_spatial, b_spatial, w0, w_up, a0,
           a_up, k_k, k_a, r_k, gn_g, gn_b, w_out, ln_g, ln_b):
    batch, seq, d_model = x.shape
    assert d_model == D_MODEL and w_in.shape == (D_MODEL, N_COLS)
    assert seq % TILE_T == 0 and TILE_T % CHUNK_A == 0 and N_CHUNKS % PREP_CHUNKS == 0
    assert N_CHUNKS // PREP_CHUNKS >= 2

    mod = pl.pallas_call(
        _ada_kernel,
        out_shape=jax.ShapeDtypeStruct((batch, 3 * d_model), F32),
        name="adaln_mod",
    )(c, w_ada, b_ada.reshape(1, -1))
    mod = mod.reshape(batch, 3, d_model)

    named = dict(mu_b=mu_b, ln_v_g=ln_v_g, ln_v_b=ln_v_b, w0=w0, a0=a0, k_k=k_k, k_a=k_a, r_k=r_k,
                 gn_g=gn_g, gn_b=gn_b, ln_g=ln_g, ln_b=ln_b)
    rows = jnp.concatenate([named[name].reshape(-1).astype(F32) for name, _ in ROW_PARAMS])
    rows = rows.reshape(1, -1)
    tok = np.arange(SCAN_L)
    tri = jnp.asarray(tok[None, :] <= tok[:, None], BF16)
    zeros_lr = jnp.zeros((LOW_RANK, D_B), F32)
    w_lr = jnp.concatenate([jnp.concatenate([w_up, zeros_lr], axis=1),
                            jnp.concatenate([zeros_lr, a_up], axis=1)], axis=0).astype(BF16)
    bias_a = jnp.repeat(b_spatial.T, HEAD, axis=1)

    tiles = seq // TILE_T

    def next_block(b, t):
        s = jnp.minimum(b * tiles + t + 1, batch * tiles - 1)
        return s // tiles, (s % tiles) * (TILE_T // PREP_ROWS), 0

    operands = [
        (x, pl.BlockSpec((None, TILE_T, d_model), lambda b, t: (b, t, 0))),
        (mod, pl.BlockSpec((None, 3, d_model), lambda b, t: (b, 0, 0))),
        (x, pl.BlockSpec((None, PREP_ROWS, d_model), next_block)),
        (mod, pl.BlockSpec((None, 3, d_model), lambda b, t: (next_block(b, t)[0], 0, 0))),
        (w_in.astype(BF16), _const_spec((d_model, N_COLS))),
        (rows, _const_spec(rows.shape)),
        (w_spatial, _const_spec(w_spatial.shape)),
        (bias_a, _const_spec((CHUNK_A, D_A))),
        (w_lr, _const_spec((2 * LOW_RANK, 2 * D_B))),
        (tri, _const_spec((SCAN_L, SCAN_L))),
        (_wide_masks(), _const_spec((7, SCAN_L, MXU_COLS))),
    ]
    n_slots = N_CHUNKS * N_PAIRS
    tile_spec = pl.BlockSpec((None, TILE_T, d_model), lambda b, t: (b, t, 0))
    cat = pl.pallas_call(
        _layer_kernel,
        grid=(batch, seq // TILE_T),
        in_specs=[spec for _, spec in operands],
        out_specs=tile_spec,
        out_shape=jax.ShapeDtypeStruct(x.shape, BF16),
        scratch_shapes=[
            pltpu.VMEM((TILE_T, d_model), BF16),
            pltpu.VMEM((TILE_T + SUBLANES, N_COLS), F32),
            pltpu.VMEM((PREP_ROWS, N_COLS_B), F32),
            pltpu.VMEM((TILE_T, d_model), BF16),
            pltpu.VMEM((N_PAIRS, LANES, LANES), F32),
            pltpu.VMEM((n_slots, 2 * SCAN_L, LANES), BF16),
            pltpu.VMEM((n_slots, SCAN_L, LANES), F32),
            pltpu.VMEM((n_slots, SCAN_L, 2 * PAIR_ROWS), BF16),
            pltpu.VMEM((n_slots, PAIR_ROWS, LANES), BF16),
            pltpu.VMEM((n_slots, LANES, 2 * PAIR_ROWS), BF16),
            pltpu.VMEM((n_slots, LANES, LANES), F32),
            pltpu.VMEM((TILE_T, D_B), F32),
            pltpu.VMEM((TILE_T, D_B), F32),
            pltpu.VMEM((TILE_T, D_B), F32),
            pltpu.VMEM((D_A // HEAD, CHUNK_A, CHUNK_A), BF16),
        ],
        compiler_params=pltpu.CompilerParams(
            dimension_semantics=("arbitrary", "arbitrary"),
            vmem_limit_bytes=VMEM_LIMIT_BYTES),
        name="hybrid_layer",
    )(*[arr for arr, _ in operands])
    ln = jnp.stack([ln_g.reshape(-1), ln_b.reshape(-1)]).astype(F32)
    return pl.pallas_call(
        _out_kernel,
        grid=(batch, seq // TILE_T),
        in_specs=[tile_spec, tile_spec,
                  pl.BlockSpec((None, 3, d_model), lambda b, t: (b, 0, 0)),
                  _const_spec((d_model, d_model)), _const_spec((2, d_model))],
        out_specs=tile_spec,
        out_shape=jax.ShapeDtypeStruct(x.shape, x.dtype),
        compiler_params=pltpu.CompilerParams(
            dimension_semantics=("arbitrary", "arbitrary"),
            vmem_limit_bytes=VMEM_LIMIT_BYTES),
        name="out_proj_norm",
    )(x, cat, mod, w_out.astype(BF16), ln)
```
